```python
import jax, jax.numpy as jnp
from jax import lax
import numpy as np

D_MODEL = 1024
BATCH = 2
SEQ = 8192
DEPTH = 1

RET_HEADS = 4
RET_DK = 128
RET_DV = 256
RET_CHUNK = 128
ROPE_THETA = 10000.0
CONV_WIDTH = 1024
CONV_K = 3
N_EXPERTS = 256
TOP_K = 8
N_GROUPS = 8
TOPK_GROUPS = 4
EXPERT_HIDDEN = 256
SHARED_HIDDEN = 256
ROUTED_SCALE = 2.5
EXPERT_BLOCK = 128
NORM_EPS = 1e-6

Q_W = RET_HEADS * RET_DK
V_W = RET_HEADS * RET_DV
IN_SIZES = (Q_W, Q_W, V_W, V_W, CONV_WIDTH, CONV_WIDTH, CONV_WIDTH, D_MODEL, D_MODEL)
IN_WIDTH = Q_W * 2 + V_W * 2 + CONV_WIDTH * 3 + D_MODEL * 2
IN_OFFSETS = (Q_W, 2 * Q_W, 2 * Q_W + V_W, 2 * Q_W + 2 * V_W, 2 * Q_W + 2 * V_W + CONV_WIDTH,
              2 * Q_W + 2 * V_W + 2 * CONV_WIDTH, 2 * Q_W + 2 * V_W + 3 * CONV_WIDTH,
              2 * Q_W + 2 * V_W + 3 * CONV_WIDTH + D_MODEL)

kernel_name = 'hybrid_retention_shortconv_moe_block'


def rmsnorm(x, g):
    xf = x.astype(jnp.float32)
    y = xf * lax.rsqrt(jnp.mean(xf * xf, axis=-1, keepdims=True) + NORM_EPS)
    return (y * g.astype(jnp.float32)).astype(x.dtype)


def rope(t, positions):
    inv_freq = ROPE_THETA ** (-jnp.arange(0, RET_DK, 2, dtype=jnp.float32) / RET_DK)
    ang = positions.astype(jnp.float32)[..., None] * inv_freq
    cos = jnp.cos(ang)[:, :, None, :]
    sin = jnp.sin(ang)[:, :, None, :]
    tf = t.astype(jnp.float32)
    t1, t2 = tf[..., :RET_DK // 2], tf[..., RET_DK // 2:]
    return jnp.concatenate([t1 * cos - t2 * sin, t2 * cos + t1 * sin], axis=-1).astype(t.dtype)


def retention_chunkwise(q, k, v):
    B, S = q.shape[0], q.shape[1]
    C = RET_CHUNK
    N = S // C
    dt = q.dtype

    def chunks(t):
        return t.reshape(B, N, C, RET_HEADS, -1).transpose(0, 3, 1, 2, 4)

    q, k, v = chunks(q), chunks(k), chunks(v)
    log_g = jnp.log1p(-(2.0 ** (-5.0 - jnp.arange(RET_HEADS, dtype=jnp.float32))))
    idx = jnp.arange(C, dtype=jnp.float32)
    diff = idx[:, None] - idx[None, :]
    intra = jnp.where(diff >= 0, jnp.exp(log_g[:, None, None] * jnp.maximum(diff, 0.0)), 0.0)
    k_decay = jnp.exp(log_g[:, None] * (C - 1 - idx))
    q_decay = jnp.exp(log_g[:, None] * (idx + 1.0))
    chunk_decay = jnp.exp(log_g * C)

    scores = jnp.einsum('bhnid,bhnjd->bhnij', q, k) * intra[None, :, None].astype(dt)
    inner = jnp.einsum('bhnij,bhnjv->bhniv', scores, v)
    kv = jnp.einsum('bhncd,bhncv->nbhdv', k * k_decay[None, :, None, :, None].astype(dt), v)

    def step(state, kv_n):
        return state * chunk_decay[None, :, None, None].astype(dt) + kv_n, state

    _, s_prev = lax.scan(step, jnp.zeros((B, RET_HEADS, RET_DK, RET_DV), dt), kv)
    cross = jnp.einsum('bhncd,nbhdv->bhncv', q * q_decay[None, :, None, :, None].astype(dt), s_prev)
    return (inner + cross).transpose(0, 2, 3, 1, 4).reshape(B, S, RET_HEADS, RET_DV)


def causal_dwconv(z, w):
    return lax.conv_general_dilated(z, w[:, None, :].astype(z.dtype), window_strides=(1,),
                                    padding=[(CONV_K - 1, 0)], dimension_numbers=('NWC', 'WIO', 'NWC'),
                                    feature_group_count=z.shape[-1])


def token_mixer(h, positions, w_in, ret_norm_g, conv_w, w_br_ret, w_br_conv, w_out):
    B, S, _ = h.shape
    q, k, v, g_ret, u, gate_c, gate_b, m_ret, m_conv = jnp.split(h @ w_in, IN_OFFSETS, axis=-1)
    q = rope(q.reshape(B, S, RET_HEADS, RET_DK), positions) * (RET_DK ** -0.5)
    k = rope(k.reshape(B, S, RET_HEADS, RET_DK), positions)
    ret = retention_chunkwise(q, k, v.reshape(B, S, RET_HEADS, RET_DV))
    ret = rmsnorm(ret, ret_norm_g.reshape(RET_HEADS, RET_DV)).reshape(B, S, V_W)
    y_ret = (jax.nn.silu(g_ret) * ret) @ w_br_ret
    y_conv = (gate_b * causal_dwconv(gate_c * u, conv_w)) @ w_br_conv
    mix = jax.nn.sigmoid(m_ret) * y_ret + jax.nn.sigmoid(m_conv) * y_conv
    return mix @ w_out


def swiglu(x, wg, wu, wd):
    return (jax.nn.silu(x @ wg) * (x @ wu)) @ wd


def moe(h, w_router, router_bias, w_exp_gate, w_exp_up, w_exp_down, w_sh_gate, w_sh_up, w_sh_down):
    T, D = h.shape
    dt = h.dtype
    scores = jax.nn.sigmoid(h.astype(jnp.float32) @ w_router.astype(jnp.float32))
    choice = scores + router_bias.astype(jnp.float32)
    grp_score = lax.top_k(choice.reshape(T, N_GROUPS, N_EXPERTS // N_GROUPS), 2)[0].sum(-1)
    _, gidx = lax.top_k(grp_score, TOPK_GROUPS)
    gmask = jax.nn.one_hot(gidx, N_GROUPS, dtype=jnp.float32).sum(1) > 0
    masked = jnp.where(jnp.repeat(gmask, N_EXPERTS // N_GROUPS, axis=-1), choice, -jnp.inf)
    _, eidx = lax.top_k(masked, TOP_K)
    wts = jnp.take_along_axis(scores, eidx, axis=-1)
    wts = wts / jnp.sum(wts, axis=-1, keepdims=True) * ROUTED_SCALE

    A = T * TOP_K
    flat_e = eidx.reshape(-1)
    flat_tok = jnp.repeat(jnp.arange(T, dtype=jnp.int32), TOP_K)
    flat_w = wts.reshape(-1)
    order = jnp.argsort(flat_e)
    se, stok, sw = flat_e[order], flat_tok[order], flat_w[order]
    counts = jnp.bincount(flat_e, length=N_EXPERTS)
    start = jnp.cumsum(counts) - counts
    pcounts = (counts + EXPERT_BLOCK - 1) // EXPERT_BLOCK * EXPERT_BLOCK
    pend = jnp.cumsum(pcounts)
    pstart = pend - pcounts
    dest = pstart[se] + (jnp.arange(A, dtype=jnp.int32) - start[se])
    n_blocks = -(-A // EXPERT_BLOCK) + N_EXPERTS
    P = n_blocks * EXPERT_BLOCK
    buf_tok = jnp.full((P,), T, jnp.int32).at[dest].set(stok)
    buf_w = jnp.zeros((P,), jnp.float32).at[dest].set(sw)
    block_e = jnp.minimum(jnp.searchsorted(pend, jnp.arange(n_blocks) * EXPERT_BLOCK, side='right'),
                          N_EXPERTS - 1)
    h_pad = jnp.concatenate([h, jnp.zeros((1, D), dt)], axis=0)

    def expert_block(args):
        tok, w_rows, e = args
        out = swiglu(h_pad[tok], w_exp_gate[e], w_exp_up[e], w_exp_down[e])
        return out * w_rows[:, None].astype(dt)

    out_buf = lax.map(expert_block, (buf_tok.reshape(n_blocks, EXPERT_BLOCK),
                                     buf_w.reshape(n_blocks, EXPERT_BLOCK), block_e))
    routed = jnp.zeros((T + 1, D), dt).at[buf_tok].add(out_buf.reshape(P, D))[:T]
    return routed + swiglu(h, w_sh_gate, w_sh_up, w_sh_down)


def setup_inputs(seed: int = 0) -> dict:
    key = jax.random.key(seed)
    ks = jax.random.split(key, 24)
    L, D, E, F = DEPTH, D_MODEL, N_EXPERTS, EXPERT_HIDDEN

    def nrm(k, shape, fan_in, scale=1.0):
        return jax.random.normal(k, shape, jnp.float32) * (scale * fan_in ** -0.5)

    def gain(k, shape):
        return 1.0 + 0.05 * jax.random.normal(k, shape, jnp.float32)

    return {
        'x': jax.random.normal(ks[0], (BATCH, SEQ, D), jnp.float32),
        'c': jax.random.normal(ks[1], (BATCH, D), jnp.float32),
        'positions': jnp.broadcast_to(jnp.arange(SEQ, dtype=jnp.int32), (BATCH, SEQ)),
        'w_ada': nrm(ks[2], (L, D, 6 * D), D, 0.5),
        'b_ada': 0.02 * jax.random.normal(ks[3], (L, 6 * D), jnp.float32),
        'norm_mix_g': gain(ks[4], (L, D)),
        'w_in': nrm(ks[5], (L, D, IN_WIDTH), D),
        'ret_norm_g': gain(ks[6], (L, V_W)),
        'conv_w': nrm(ks[7], (L, CONV_K, CONV_WIDTH), CONV_K),
        'w_br_ret': nrm(ks[8], (L, V_W, D), V_W),
        'w_br_conv': nrm(ks[9], (L, CONV_WIDTH, D), CONV_WIDTH),
        'w_out': nrm(ks[10], (L, D, D), D),
        'norm_ffn_g': gain(ks[11], (L, D)),
        'w_router': nrm(ks[12], (L, D, E), D),
        'router_bias': 0.01 * jax.random.normal(ks[13], (L, E), jnp.float32),
        'w_exp_gate': nrm(ks[14], (L, E, D, F), D),
        'w_exp_up': nrm(ks[15], (L, E, D, F), D),
        'w_exp_down': nrm(ks[16], (L, E, F, D), F),
        'w_sh_gate': nrm(ks[17], (L, D, SHARED_HIDDEN), D),
        'w_sh_up': nrm(ks[18], (L, D, SHARED_HIDDEN), D),
        'w_sh_down': nrm(ks[19], (L, SHARED_HIDDEN, D), SHARED_HIDDEN),
        'norm_final_g': gain(ks[20], (D,)),
    }


def reference(x, c, positions, w_ada, b_ada, norm_mix_g, w_in, ret_norm_g, conv_w, w_br_ret, w_br_conv,
              w_out, norm_ffn_g, w_router, router_bias, w_exp_gate, w_exp_up, w_exp_down,
              w_sh_gate, w_sh_up, w_sh_down, norm_final_g):
    B, S, D = x.shape
    for l in range(DEPTH):
        mod = jax.nn.silu(c) @ w_ada[l] + b_ada[l]
        sh1, sc1, g1, sh2, sc2, g2 = [m[:, None, :] for m in jnp.split(mod, 6, axis=-1)]
        h = rmsnorm(x, norm_mix_g[l]) * (1.0 + sc1) + sh1
        x = x + g1 * token_mixer(h, positions, w_in[l], ret_norm_g[l], conv_w[l],
                                 w_br_ret[l], w_br_conv[l], w_out[l])
        h = rmsnorm(x, norm_ffn_g[l]) * (1.0 + sc2) + sh2
        y = moe(h.reshape(B * S, D), w_router[l], router_bias[l], w_exp_gate[l], w_exp_up[l],
                w_exp_down[l], w_sh_gate[l], w_sh_up[l], w_sh_down[l])
        x = x + g2 * y.reshape(B, S, D)
    return rmsnorm(x, norm_final_g)
```

```python
import numpy as np
import jax
import jax.numpy as jnp
from jax import lax
from jax.experimental import pallas as pl
from jax.experimental.pallas import tpu as pltpu

RET_HEADS = 4
RET_DK = 128
RET_DV = 256
RET_CHUNK = 128
ROPE_THETA = 10000.0
CONV_K = 3
N_EXPERTS = 256
TOP_K = 8
N_GROUPS = 8
TOPK_GROUPS = 4
GROUP_SIZE = N_EXPERTS // N_GROUPS
ROUTED_SCALE = 2.5
NORM_EPS = 1e-6

MIXER_TOKENS = 256
ROUTE_TOKENS = 256
DISPATCH_TOKENS = 256
EXPERT_ROWS = 256
COMBINE_TOKENS = 128
DMA_RING = 32

VMEM_LIMIT_BYTES = 56 * 1024 * 1024

F32 = jnp.float32
BF16 = jnp.bfloat16
HIGHEST = lax.Precision.HIGHEST


def _sigmoid(v):
    return 1.0 / (1.0 + jnp.exp(-v))


def _silu(v):
    return v * _sigmoid(v)


def _rms(v):
    return v * lax.rsqrt(jnp.mean(v * v, axis=-1, keepdims=True) + NORM_EPS)


def _resident(shape):
    nd = len(shape)
    return pl.BlockSpec(shape, lambda *_: (0,) * nd, pipeline_mode=pl.Buffered(1))


def _ada_kernel(c_ref, w_ref, b_ref, o_ref):
    c = c_ref[...]
    o_ref[...] = jnp.dot(_silu(c), w_ref[...], precision=HIGHEST, preferred_element_type=F32) + b_ref[...]


def _ada(c8, w, b):
    d, n = w.shape
    tn = 1024
    return pl.pallas_call(
        _ada_kernel,
        grid=(n // tn,),
        in_specs=[pl.BlockSpec((8, d), lambda j: (0, 0)),
                  pl.BlockSpec((d, tn), lambda j: (0, j)),
                  pl.BlockSpec((1, tn), lambda j: (0, j))],
        out_specs=pl.BlockSpec((8, tn), lambda j: (0, j)),
        out_shape=jax.ShapeDtypeStruct((8, n), F32),
        name="ada",
    )(c8, w, b)


def _retention_constants():
    c = RET_CHUNK
    log_g = jnp.log1p(-(2.0 ** (-5.0 - jnp.arange(RET_HEADS, dtype=F32))))
    idx = jnp.arange(c, dtype=F32)
    diff = idx[:, None] - idx[None, :]
    intra = jnp.where(diff >= 0, jnp.exp(log_g[:, None, None] * jnp.maximum(diff, 0.0)), 0.0)
    k_decay = jnp.exp(log_g[:, None] * (c - 1 - idx))
    q_decay = jnp.exp(log_g[:, None] * (idx + 1.0))
    chunk_decay = jnp.exp(log_g * c)
    kd = jnp.broadcast_to(k_decay[:, :, None], (RET_HEADS, c, RET_DK))
    qd = jnp.broadcast_to(q_decay[:, :, None], (RET_HEADS, c, RET_DK))
    cd = jnp.broadcast_to(chunk_decay[:, None, None], (RET_HEADS, 1, RET_DV))
    inv_freq = ROPE_THETA ** (-jnp.arange(0, RET_DK, 2, dtype=F32) / RET_DK)
    inv_freq = jnp.concatenate([inv_freq, inv_freq])[None, :]
    sign = jnp.concatenate([-jnp.ones((RET_DK // 2,), F32), jnp.ones((RET_DK // 2,), F32)])[None, :]
    return intra, qd, kd, cd, inv_freq, sign


def _pack_bf16_pairs(v):
    w = v.shape[1] // 2
    lo = lax.bitcast_convert_type(v[:, :w].astype(BF16).astype(F32), jnp.uint32)
    hi = lax.bitcast_convert_type(v[:, w:].astype(BF16).astype(F32), jnp.uint32)
    return (lo >> 16) | (hi & jnp.uint32(0xFFFF0000))


def _unpack_bf16_pairs(word):
    lo = lax.bitcast_convert_type(word << 16, F32).astype(BF16)
    hi = lax.bitcast_convert_type(word & jnp.uint32(0xFFFF0000), F32).astype(BF16)
    return lo, hi


def _mixer_kernel(x_ref, pos_ref, mod_ref, gmix_ref, win_ref, invf_ref, sign_ref, intra_ref, qd_ref, kd_ref, cd_ref,
                  retg_ref, convw_ref, wbr_ref, wbc_ref, wout_ref, gffn_ref, wrt_ref, wshgu_ref, wshd_ref,
                  x2_ref, h2p_ref, lgt_ref,
                  state_ref, carry_ref, q_ref, k_ref, v_ref, ret_ref):
    tm, d = x_ref.shape[1], x_ref.shape[2]
    q_w = RET_HEADS * RET_DK
    v_w = RET_HEADS * RET_DV
    offs = np.cumsum([0, q_w, q_w, v_w, v_w, d, d, d, d, d])

    @pl.when(pl.program_id(1) == 0)
    def _():
        state_ref[...] = jnp.zeros_like(state_ref)
        carry_ref[...] = jnp.zeros_like(carry_ref)

    x = x_ref[0]
    mod = mod_ref[0]
    sh1, sc1, g1, sh2, sc2, g2 = [mod[i:i + 1] for i in range(6)]
    hb = ((_rms(x) * gmix_ref[...]) * (1.0 + sc1) + sh1).astype(BF16)

    def proj(i):
        return jnp.dot(hb, win_ref[:, offs[i]:offs[i + 1]], preferred_element_type=F32)

    ang = pos_ref[0] * invf_ref[...]
    cosv = jnp.cos(ang)
    sinv = jnp.sin(ang) * sign_ref[...]

    def rope(t):
        return jnp.concatenate(
            [t[:, h * RET_DK:(h + 1) * RET_DK] * cosv
             + pltpu.roll(t[:, h * RET_DK:(h + 1) * RET_DK], RET_DK // 2, 1) * sinv
             for h in range(RET_HEADS)], axis=1)

    q_ref[...] = rope(proj(0)) * (RET_DK ** -0.5)
    k_ref[...] = rope(proj(1))
    v_ref[...] = proj(2).astype(BF16)

    for c in range(tm // RET_CHUNK):
        rows = pl.ds(c * RET_CHUNK, RET_CHUNK)
        for h in range(RET_HEADS):
            qh = q_ref[rows, h * RET_DK:(h + 1) * RET_DK]
            kh = k_ref[rows, h * RET_DK:(h + 1) * RET_DK]
            vh = v_ref[rows, h * RET_DV:(h + 1) * RET_DV]
            scores = lax.dot_general(qh.astype(BF16), kh.astype(BF16), (((1,), (1,)), ((), ())),
                                     preferred_element_type=F32) * intra_ref[h]
            inner = jnp.dot(scores.astype(BF16), vh, preferred_element_type=F32)
            st = state_ref[h]
            cross = jnp.dot((qh * qd_ref[h]).astype(BF16), st.astype(BF16), preferred_element_type=F32)
            kv = lax.dot_general((kh * kd_ref[h]).astype(BF16), vh, (((0,), (0,)), ((), ())),
                                 preferred_element_type=F32)
            state_ref[h] = st * cd_ref[h] + kv
            ret_ref[rows, h * RET_DV:(h + 1) * RET_DV] = _rms(inner + cross)

    y_ret = jnp.dot((_silu(proj(3)) * (ret_ref[...] * retg_ref[...])).astype(BF16), wbr_ref[...],
                    preferred_element_type=F32)

    z = proj(5) * proj(4)
    row = lax.broadcasted_iota(jnp.int32, z.shape, 0)
    prev1 = carry_ref[7:8, :]
    prev2 = carry_ref[6:7, :]
    z1 = jnp.where(row == 0, prev1, pltpu.roll(z, 1, 0))
    z2 = jnp.where(row == 0, prev2, jnp.where(row == 1, prev1, pltpu.roll(z, 2, 0)))
    conv = convw_ref[0:1, :] * z2 + convw_ref[1:2, :] * z1 + convw_ref[2:3, :] * z
    carry_ref[...] = z[tm - 8:tm, :]
    y_conv = jnp.dot((proj(6) * conv).astype(BF16), wbc_ref[...], preferred_element_type=F32)

    mix = _sigmoid(proj(7)) * y_ret + _sigmoid(proj(8)) * y_conv
    x1 = x + g1 * jnp.dot(mix.astype(BF16), wout_ref[...], preferred_element_type=F32)

    h2 = (_rms(x1) * gffn_ref[...]) * (1.0 + sc2) + sh2
    lgt_ref[...] = lax.dot_general(wrt_ref[...], h2, (((1,), (1,)), ((), ())), precision=HIGHEST,
                                   preferred_element_type=F32)
    gu = jnp.dot(h2.astype(BF16), wshgu_ref[...], preferred_element_type=F32)
    f = gu.shape[1] // 2
    shared = jnp.dot((_silu(gu[:, :f]) * gu[:, f:]).astype(BF16), wshd_ref[...], preferred_element_type=F32)
    x2_ref[0] = x1 + g2 * shared
    h2p_ref[...] = h2


def _mixer(x, posf, mod, gmix, win, retg, convw, wbr, wbc, wout, gffn, wrt, wshgu, wshd):
    b, s, d = x.shape
    tm = MIXER_TOKENS
    nt = s // tm
    intra, qd, kd, cd, invf, sign = _retention_constants()
    q_w, v_w = RET_HEADS * RET_DK, RET_HEADS * RET_DV
    weights = (gmix, win, invf, sign, intra, qd, kd, cd, retg, convw, wbr, wbc, wout, gffn, wrt, wshgu, wshd)
    return pl.pallas_call(
        _mixer_kernel,
        grid=(b, nt),
        in_specs=[pl.BlockSpec((1, tm, d), lambda i, j: (i, j, 0)),
                  pl.BlockSpec((1, tm, 1), lambda i, j: (i, j, 0)),
                  pl.BlockSpec((1, 6, d), lambda i, j: (i, 0, 0))]
                 + [_resident(w.shape) for w in weights],
        out_specs=[pl.BlockSpec((1, tm, d), lambda i, j: (i, j, 0)),
                   pl.BlockSpec((tm, d), lambda i, j: (i * nt + j, 0)),
                   pl.BlockSpec((N_EXPERTS, tm), lambda i, j: (0, i * nt + j))],
        out_shape=[jax.ShapeDtypeStruct((b, s, d), F32),
                   jax.ShapeDtypeStruct((b * s, d), F32),
                   jax.ShapeDtypeStruct((N_EXPERTS, b * s), F32)],
        scratch_shapes=[pltpu.VMEM((RET_HEADS, RET_DK, RET_DV), F32),
                        pltpu.VMEM((8, d), F32),
                        pltpu.VMEM((tm, q_w), F32),
                        pltpu.VMEM((tm, q_w), F32),
                        pltpu.VMEM((tm, v_w), BF16),
                        pltpu.VMEM((tm, v_w), F32)],
        compiler_params=pltpu.CompilerParams(dimension_semantics=("arbitrary", "arbitrary"),
                                             vmem_limit_bytes=VMEM_LIMIT_BYTES),
        name="mixer",
    )(x, posf, mod, *weights)


def _first_argmax(v, idx, n):
    m = jnp.max(v, axis=0, keepdims=True)
    return m, jnp.min(jnp.where(v == m, idx, n), axis=0, keepdims=True)


def _route_kernel(lg_ref, bias_ref, tri_ref, eidx_ref, w_ref, rank_ref, cnt_ref, carry_ref):
    @pl.when(pl.program_id(0) == 0)
    def _():
        carry_ref[...] = jnp.zeros_like(carry_ref)

    tn = lg_ref.shape[1]
    neg = F32(-jnp.inf)
    score = _sigmoid(lg_ref[...])
    choice = score + bias_ref[...]

    grow = lax.broadcasted_iota(jnp.int32, (GROUP_SIZE, tn), 0)
    gscores = []
    for g in range(N_GROUPS):
        cg = choice[g * GROUP_SIZE:(g + 1) * GROUP_SIZE]
        m1, i1 = _first_argmax(cg, grow, GROUP_SIZE)
        m2 = jnp.max(jnp.where(grow == i1, neg, cg), axis=0, keepdims=True)
        gscores.append(m1 + m2)
    cur = jnp.concatenate(gscores, axis=0)
    gidx = lax.broadcasted_iota(jnp.int32, (N_GROUPS, tn), 0)
    keep = jnp.zeros((N_GROUPS, tn), F32)
    for _ in range(TOPK_GROUPS):
        _, ig = _first_argmax(cur, gidx, N_GROUPS)
        hit = gidx == ig
        keep = jnp.where(hit, 1.0, keep)
        cur = jnp.where(hit, neg, cur)
    cur = jnp.concatenate(
        [jnp.where(keep[g:g + 1] > 0.0, choice[g * GROUP_SIZE:(g + 1) * GROUP_SIZE], neg)
         for g in range(N_GROUPS)], axis=0)

    erow = lax.broadcasted_iota(jnp.int32, (N_EXPERTS, tn), 0)
    eidx, wts = [], []
    member = jnp.zeros((N_EXPERTS, tn), F32)
    for _ in range(TOP_K):
        _, ie = _first_argmax(cur, erow, N_EXPERTS)
        hit = erow == ie
        eidx.append(ie)
        wts.append(jnp.sum(jnp.where(hit, score, 0.0), axis=0, keepdims=True))
        member = member + hit.astype(F32)
        cur = jnp.where(hit, neg, cur)
    wsum = wts[0]
    for k in range(1, TOP_K):
        wsum = wsum + wts[k]

    before = jnp.dot(member.astype(BF16), tri_ref[...], preferred_element_type=F32) + carry_ref[...]
    ranks = [jnp.sum(jnp.where(erow == eidx[k], before, 0.0), axis=0, keepdims=True) for k in range(TOP_K)]
    carry_ref[...] = carry_ref[...] + jnp.sum(member, axis=1, keepdims=True)

    eidx_ref[...] = jnp.concatenate(eidx, axis=0)
    w_ref[...] = jnp.concatenate([w / wsum * ROUTED_SCALE for w in wts], axis=0)
    rank_ref[...] = jnp.concatenate(ranks, axis=0).astype(jnp.int32)
    cnt_ref[...] = carry_ref[...]


def _route(lgt, bias):
    e, t = lgt.shape
    tn = ROUTE_TOKENS
    tri = jnp.asarray(np.triu(np.ones((tn, tn), np.float32), 1), BF16)
    return pl.pallas_call(
        _route_kernel,
        grid=(t // tn,),
        in_specs=[pl.BlockSpec((e, tn), lambda i: (0, i)),
                  pl.BlockSpec((e, 1), lambda i: (0, 0)),
                  pl.BlockSpec((tn, tn), lambda i: (0, 0))],
        out_specs=[pl.BlockSpec((TOP_K, tn), lambda i: (0, i)),
                   pl.BlockSpec((TOP_K, tn), lambda i: (0, i)),
                   pl.BlockSpec((TOP_K, tn), lambda i: (0, i)),
                   pl.BlockSpec((e, 1), lambda i: (0, 0))],
        out_shape=[jax.ShapeDtypeStruct((TOP_K, t), jnp.int32),
                   jax.ShapeDtypeStruct((TOP_K, t), F32),
                   jax.ShapeDtypeStruct((TOP_K, t), jnp.int32),
                   jax.ShapeDtypeStruct((e, 1), F32)],
        scratch_shapes=[pltpu.VMEM((e, 1), F32)],
        compiler_params=pltpu.CompilerParams(dimension_semantics=("arbitrary",)),
        name="route",
    )(lgt, bias, tri)


def _dest_kernel(eidx_ref, rank_ref, pstart_ref, dest_ref):
    tn = eidx_ref.shape[1]
    erow = lax.broadcasted_iota(jnp.int32, (N_EXPERTS, tn), 0)
    start = pstart_ref[...]
    dest_ref[...] = rank_ref[...] + jnp.concatenate(
        [jnp.sum(jnp.where(erow == eidx_ref[k:k + 1, :], start, 0), axis=0, keepdims=True) for k in range(TOP_K)],
        axis=0)


def _dest(eidx, rank, pstart):
    k, t = eidx.shape
    tn = ROUTE_TOKENS
    return pl.pallas_call(
        _dest_kernel,
        grid=(t // tn,),
        in_specs=[pl.BlockSpec((k, tn), lambda i: (0, i)),
                  pl.BlockSpec((k, tn), lambda i: (0, i)),
                  pl.BlockSpec((N_EXPERTS, 1), lambda i: (0, 0))],
        out_specs=pl.BlockSpec((k, tn), lambda i: (0, i)),
        out_shape=jax.ShapeDtypeStruct((k, t), jnp.int32),
        name="dest",
    )(eidx, rank, pstart)


def _row_copy_ring(n, make_copy, sems):
    ring = sems.shape[0]

    def body(j, carry):
        slot = j % ring

        @pl.when(j >= ring)
        def _():
            make_copy(j - ring, slot).wait()

        make_copy(j, slot).start()
        return carry

    lax.fori_loop(0, n, body, 0)

    def drain(j, carry):
        make_copy(j, j % ring).wait()
        return carry

    lax.fori_loop(n - ring, n, drain, 0)


def _dispatch_kernel(dest_ref, h_ref, zero_ref, xs_ref, sems):
    del zero_ref
    td = h_ref.shape[0]

    def make_copy(j, slot):
        t = j // TOP_K
        k = j % TOP_K
        return pltpu.make_async_copy(h_ref.at[pl.ds(t, 1)], xs_ref.at[pl.ds(dest_ref[k, t], 1)], sems.at[slot])

    _row_copy_ring(td * TOP_K, make_copy, sems)


def _dispatch(dest, h2p, n_rows):
    t, w = h2p.shape
    td = DISPATCH_TOKENS
    zeros = jnp.zeros((n_rows, w), h2p.dtype)
    return pl.pallas_call(
        _dispatch_kernel,
        grid=(t // td,),
        in_specs=[pl.BlockSpec((TOP_K, td), lambda i: (0, i), memory_space=pltpu.SMEM),
                  pl.BlockSpec((td, w), lambda i: (i, 0)),
                  pl.BlockSpec(memory_space=pl.ANY)],
        out_specs=pl.BlockSpec(memory_space=pl.ANY),
        out_shape=jax.ShapeDtypeStruct((n_rows, w), h2p.dtype),
        scratch_shapes=[pltpu.SemaphoreType.DMA((DMA_RING,))],
        input_output_aliases={2: 0},
        compiler_params=pltpu.CompilerParams(dimension_semantics=("arbitrary",), has_side_effects=True),
        name="dispatch",
    )(dest, h2p, zeros)


def _experts_kernel(be_ref, nu_ref, x_ref, wg_ref, wu_ref, wd_ref, o_ref, wgb_ref, wub_ref, wdb_ref):
    i = pl.program_id(0)
    active = i < nu_ref[0]
    new_expert = jnp.logical_or(i == 0, be_ref[i] != be_ref[jnp.maximum(i - 1, 0)])

    @pl.when(jnp.logical_and(active, new_expert))
    def _():
        wgb_ref[...] = wg_ref[0].astype(BF16)
        wub_ref[...] = wu_ref[0].astype(BF16)
        wdb_ref[...] = wd_ref[0].astype(BF16)

    @pl.when(active)
    def _():
        xb = x_ref[...].astype(BF16)
        g = jnp.dot(xb, wgb_ref[...], preferred_element_type=F32)
        u = jnp.dot(xb, wub_ref[...], preferred_element_type=F32)
        o_ref[...] = jnp.dot((_silu(g) * u).astype(BF16), wdb_ref[...], preferred_element_type=F32)

    @pl.when(jnp.logical_not(active))
    def _():
        o_ref[...] = jnp.zeros_like(o_ref)


def _experts(block_e, n_used, xs, wg, wu, wd):
    p, w = xs.shape
    e, d, f = wg.shape
    bm = EXPERT_ROWS
    nb = p // bm

    def row_map(i, be, nu):
        return (jnp.minimum(i, nu[0] - 1), 0)

    def w_map(i, be, nu):
        return (be[jnp.minimum(i, nu[0] - 1)], 0, 0)

    return pl.pallas_call(
        _experts_kernel,
        grid_spec=pltpu.PrefetchScalarGridSpec(
            num_scalar_prefetch=2,
            grid=(nb,),
            in_specs=[pl.BlockSpec((bm, w), row_map),
                      pl.BlockSpec((1, d, f), w_map),
                      pl.BlockSpec((1, d, f), w_map),
                      pl.BlockSpec((1, f, d), w_map)],
            out_specs=pl.BlockSpec((bm, d), lambda i, be, nu: (i, 0)),
            scratch_shapes=[pltpu.VMEM((d, f), BF16), pltpu.VMEM((d, f), BF16), pltpu.VMEM((f, d), BF16)]),
        out_shape=jax.ShapeDtypeStruct((p, d), F32),
        compiler_params=pltpu.CompilerParams(dimension_semantics=("arbitrary",),
                                             vmem_limit_bytes=VMEM_LIMIT_BYTES),
        name="experts",
    )(block_e, n_used, xs, wg, wu, wd)


def _combine_kernel(dest_ref, w_ref, x2_ref, g2_ref, gfin_ref, y_ref, o_ref, rows_ref, sems):
    tc = x2_ref.shape[1]

    def make_copy(j, slot):
        t = j // TOP_K
        k = j % TOP_K
        return pltpu.make_async_copy(y_ref.at[pl.ds(dest_ref[k, t], 1)], rows_ref.at[k, pl.ds(t, 1)],
                                     sems.at[slot])

    _row_copy_ring(tc * TOP_K, make_copy, sems)

    w = w_ref[...]
    routed = rows_ref[0] * w[:, 0:1]
    for k in range(1, TOP_K):
        routed = routed + rows_ref[k] * w[:, k:k + 1]
    o_ref[0] = _rms(x2_ref[0] + g2_ref[0] * routed) * gfin_ref[...]


def _combine(dest, wtok, x2, g2, gfin, ybuf):
    b, s, d = x2.shape
    tc = COMBINE_TOKENS
    nt = s // tc
    return pl.pallas_call(
        _combine_kernel,
        grid=(b, nt),
        in_specs=[pl.BlockSpec((TOP_K, tc), lambda i, j: (0, i * nt + j), memory_space=pltpu.SMEM),
                  pl.BlockSpec((tc, TOP_K), lambda i, j: (i * nt + j, 0)),
                  pl.BlockSpec((1, tc, d), lambda i, j: (i, j, 0)),
                  pl.BlockSpec((1, 1, d), lambda i, j: (i, 0, 0)),
                  pl.BlockSpec((1, d), lambda i, j: (0, 0)),
                  pl.BlockSpec(memory_space=pl.ANY)],
        out_specs=pl.BlockSpec((1, tc, d), lambda i, j: (i, j, 0)),
        out_shape=jax.ShapeDtypeStruct((b, s, d), F32),
        scratch_shapes=[pltpu.VMEM((TOP_K, tc, d), F32), pltpu.SemaphoreType.DMA((DMA_RING,))],
        compiler_params=pltpu.CompilerParams(dimension_semantics=("arbitrary", "arbitrary")),
        name="combine",
    )(dest, wtok, x2, g2, gfin, ybuf)


def kernel(x, c, positions, w_ada, b_ada, norm_mix_g, w_in, ret_norm_g, conv_w, w_br_ret, w_br_conv, w_out,
           norm_ffn_g, w_router, router_bias, w_exp_gate, w_exp_up, w_exp_down, w_sh_gate, w_sh_up, w_sh_down,
           norm_final_g):
    b, s, d = x.shape
    t = b * s
    depth = w_in.shape[0]
    assert depth == 1, "the combine kernel applies the final norm, so exactly one layer is supported"
    posf = positions.astype(F32)[:, :, None]
    c8 = jnp.zeros((8, d), F32).at[:b].set(c)
    bm = EXPERT_ROWS
    n_blocks = t * TOP_K // bm + N_EXPERTS
    n_rows = n_blocks * bm

    for l in range(depth):
        mod = _ada(c8, w_ada[l], b_ada[l][None, :])[:b].reshape(b, 6, d)
        x2, h2p, lgt = _mixer(
            x, posf, mod, norm_mix_g[l][None, :], w_in[l].astype(BF16), ret_norm_g[l][None, :], conv_w[l],
            w_br_ret[l].astype(BF16), w_br_conv[l].astype(BF16), w_out[l].astype(BF16), norm_ffn_g[l][None, :],
            w_router[l].T, jnp.concatenate([w_sh_gate[l], w_sh_up[l]], axis=1).astype(BF16),
            w_sh_down[l].astype(BF16))
        eidx, wts, rank, counts = _route(lgt, router_bias[l][:, None])

        cnt = counts[:, 0].astype(jnp.int32)
        pcnt = (cnt + bm - 1) // bm * bm
        pend = jnp.cumsum(pcnt)
        pstart = pend - pcnt
        n_used = jnp.maximum(pend[-1] // bm, 1).astype(jnp.int32)[None]
        block_start = jnp.arange(n_blocks, dtype=jnp.int32) * bm
        block_e = jnp.minimum(jnp.sum((pend[None, :] <= block_start[:, None]).astype(jnp.int32), axis=1),
                              N_EXPERTS - 1)

        dest = _dest(eidx, rank, pstart[:, None])
        xs = _dispatch(dest, h2p, n_rows)
        ybuf = _experts(block_e, n_used, xs, w_exp_gate[l], w_exp_up[l], w_exp_down[l])
        x = _combine(dest, wts.T, x2, mod[:, 5:6, :], norm_final_g[None, :], ybuf)
    return x
```

```python
import numpy as np
import jax
import jax.numpy as jnp
from jax import lax
from jax.experimental import pallas as pl
from jax.experimental.pallas import tpu as pltpu
from jax.experimental.pallas import tpu_sc as plsc

RET_HEADS = 4
RET_DK = 128
RET_DV = 256
RET_CHUNK = 128
ROPE_THETA = 10000.0
CONV_K = 3
N_EXPERTS = 256
TOP_K = 8
N_GROUPS = 8
TOPK_GROUPS = 4
GROUP_SIZE = N_EXPERTS // N_GROUPS
ROUTED_SCALE = 2.5
NORM_EPS = 1e-6

MIXER_TOKENS = 256
ROUTE_TOKENS = 256
EXPERT_ROWS = 256
COMBINE_TOKENS = 256
LANES = 128
GATHER_BUFFERS = 4

VMEM_LIMIT_BYTES = 56 * 1024 * 1024

F32 = jnp.float32
BF16 = jnp.bfloat16
HIGHEST = lax.Precision.HIGHEST


def _sigmoid(v):
    return 1.0 / (1.0 + jnp.exp(-v))


def _silu(v):
    return v * _sigmoid(v)


def _rms(v):
    return v * lax.rsqrt(jnp.mean(v * v, axis=-1, keepdims=True) + NORM_EPS)


def _resident(shape):
    nd = len(shape)
    return pl.BlockSpec(shape, lambda *_: (0,) * nd, pipeline_mode=pl.Buffered(1))


def _ada_kernel(c_ref, w_ref, b_ref, o_ref):
    c = c_ref[...]
    o_ref[...] = jnp.dot(_silu(c), w_ref[...], precision=HIGHEST, preferred_element_type=F32) + b_ref[...]


def _ada(c8, w, b):
    d, n = w.shape
    tn = 1024
    return pl.pallas_call(
        _ada_kernel,
        grid=(n // tn,),
        in_specs=[pl.BlockSpec((8, d), lambda j: (0, 0)),
                  pl.BlockSpec((d, tn), lambda j: (0, j)),
                  pl.BlockSpec((1, tn), lambda j: (0, j))],
        out_specs=pl.BlockSpec((8, tn), lambda j: (0, j)),
        out_shape=jax.ShapeDtypeStruct((8, n), F32),
        name="ada",
    )(c8, w, b)


def _retention_constants():
    c = RET_CHUNK
    log_g = jnp.log1p(-(2.0 ** (-5.0 - jnp.arange(RET_HEADS, dtype=F32))))
    idx = jnp.arange(c, dtype=F32)
    diff = idx[:, None] - idx[None, :]
    intra = jnp.where(diff >= 0, jnp.exp(log_g[:, None, None] * jnp.maximum(diff, 0.0)), 0.0)
    k_decay = jnp.exp(log_g[:, None] * (c - 1 - idx))
    q_decay = jnp.exp(log_g[:, None] * (idx + 1.0))
    chunk_decay = jnp.exp(log_g * c)
    kd = jnp.broadcast_to(k_decay[:, :, None], (RET_HEADS, c, RET_DK))
    qd = jnp.broadcast_to(q_decay[:, :, None], (RET_HEADS, c, RET_DK))
    cd = jnp.broadcast_to(chunk_decay[:, None, None], (RET_HEADS, 1, RET_DV))
    inv_freq = ROPE_THETA ** (-jnp.arange(0, RET_DK, 2, dtype=F32) / RET_DK)
    inv_freq = jnp.concatenate([inv_freq, inv_freq])[None, :]
    sign = jnp.concatenate([-jnp.ones((RET_DK // 2,), F32), jnp.ones((RET_DK // 2,), F32)])[None, :]
    return intra, qd, kd, cd, inv_freq, sign


def _pack_bf16_pairs(v):
    w = v.shape[1] // 2
    lo = lax.bitcast_convert_type(v[:, :w].astype(BF16).astype(F32), jnp.uint32)
    hi = lax.bitcast_convert_type(v[:, w:].astype(BF16).astype(F32), jnp.uint32)
    return (lo >> 16) | (hi & jnp.uint32(0xFFFF0000))


def _unpack_bf16_pairs(word):
    lo = lax.bitcast_convert_type(word << 16, F32).astype(BF16)
    hi = lax.bitcast_convert_type(word & jnp.uint32(0xFFFF0000), F32).astype(BF16)
    return lo, hi


def _mixer_kernel(x_ref, pos_ref, mod_ref, gmix_ref, win_ref, invf_ref, sign_ref, intra_ref, qd_ref, kd_ref, cd_ref,
                  retg_ref, convw_ref, wbr_ref, wbc_ref, wout_ref, gffn_ref, wrt_ref, wshgu_ref, wshd_ref,
                  x2_ref, h2p_ref, lgt_ref,
                  state_ref, carry_ref, q_ref, k_ref, v_ref, ret_ref):
    tm, d = x_ref.shape[1], x_ref.shape[2]
    q_w = RET_HEADS * RET_DK
    v_w = RET_HEADS * RET_DV
    offs = np.cumsum([0, q_w, q_w, v_w, v_w, d, d, d, d, d])

    @pl.when(pl.program_id(1) == 0)
    def _():
        state_ref[...] = jnp.zeros_like(state_ref)
        carry_ref[...] = jnp.zeros_like(carry_ref)

    x = x_ref[0]
    mod = mod_ref[0]
    sh1, sc1, g1, sh2, sc2, g2 = [mod[i:i + 1] for i in range(6)]
    hb = ((_rms(x) * gmix_ref[...]) * (1.0 + sc1) + sh1).astype(BF16)

    def proj(i):
        return jnp.dot(hb, win_ref[:, offs[i]:offs[i + 1]], preferred_element_type=F32)

    ang = pos_ref[0] * invf_ref[...]
    cosv = jnp.cos(ang)
    sinv = jnp.sin(ang) * sign_ref[...]

    def rope(t):
        return jnp.concatenate(
            [t[:, h * RET_DK:(h + 1) * RET_DK] * cosv
             + pltpu.roll(t[:, h * RET_DK:(h + 1) * RET_DK], RET_DK // 2, 1) * sinv
             for h in range(RET_HEADS)], axis=1)

    q_ref[...] = rope(proj(0)) * (RET_DK ** -0.5)
    k_ref[...] = rope(proj(1))
    v_ref[...] = proj(2).astype(BF16)

    for c in range(tm // RET_CHUNK):
        rows = pl.ds(c * RET_CHUNK, RET_CHUNK)
        for h in range(RET_HEADS):
            qh = q_ref[rows, h * RET_DK:(h + 1) * RET_DK]
            kh = k_ref[rows, h * RET_DK:(h + 1) * RET_DK]
            vh = v_ref[rows, h * RET_DV:(h + 1) * RET_DV]
            scores = lax.dot_general(qh.astype(BF16), kh.astype(BF16), (((1,), (1,)), ((), ())),
                                     preferred_element_type=F32) * intra_ref[h]
            inner = jnp.dot(scores.astype(BF16), vh, preferred_element_type=F32)
            st = state_ref[h]
            cross = jnp.dot((qh * qd_ref[h]).astype(BF16), st.astype(BF16), preferred_element_type=F32)
            kv = lax.dot_general((kh * kd_ref[h]).astype(BF16), vh, (((0,), (0,)), ((), ())),
                                 preferred_element_type=F32)
            state_ref[h] = st * cd_ref[h] + kv
            ret_ref[rows, h * RET_DV:(h + 1) * RET_DV] = _rms(inner + cross)

    y_ret = jnp.dot((_silu(proj(3)) * (ret_ref[...] * retg_ref[...])).astype(BF16), wbr_ref[...],
                    preferred_element_type=F32)

    z = proj(5) * proj(4)
    row = lax.broadcasted_iota(jnp.int32, z.shape, 0)
    prev1 = carry_ref[7:8, :]
    prev2 = carry_ref[6:7, :]
    z1 = jnp.where(row == 0, prev1, pltpu.roll(z, 1, 0))
    z2 = jnp.where(row == 0, prev2, jnp.where(row == 1, prev1, pltpu.roll(z, 2, 0)))
    conv = convw_ref[0:1, :] * z2 + convw_ref[1:2, :] * z1 + convw_ref[2:3, :] * z
    carry_ref[...] = z[tm - 8:tm, :]
    y_conv = jnp.dot((proj(6) * conv).astype(BF16), wbc_ref[...], preferred_element_type=F32)

    mix = _sigmoid(proj(7)) * y_ret + _sigmoid(proj(8)) * y_conv
    x1 = x + g1 * jnp.dot(mix.astype(BF16), wout_ref[...], preferred_element_type=F32)

    h2 = (_rms(x1) * gffn_ref[...]) * (1.0 + sc2) + sh2
    lgt_ref[...] = lax.dot_general(wrt_ref[...], h2, (((1,), (1,)), ((), ())), precision=HIGHEST,
                                   preferred_element_type=F32)
    gu = jnp.dot(h2.astype(BF16), wshgu_ref[...], preferred_element_type=F32)
    f = gu.shape[1] // 2
    shared = jnp.dot((_silu(gu[:, :f]) * gu[:, f:]).astype(BF16), wshd_ref[...], preferred_element_type=F32)
    x2_ref[0] = x1 + g2 * shared
    for j in range(h2p_ref.shape[0]):
        h2p_ref[j] = _pack_bf16_pairs(h2[:, 2 * LANES * j:2 * LANES * (j + 1)])


def _mixer(x, posf, mod, gmix, win, retg, convw, wbr, wbc, wout, gffn, wrt, wshgu, wshd):
    b, s, d = x.shape
    tm = MIXER_TOKENS
    nt = s // tm
    intra, qd, kd, cd, invf, sign = _retention_constants()
    q_w, v_w = RET_HEADS * RET_DK, RET_HEADS * RET_DV
    weights = (gmix, win, invf, sign, intra, qd, kd, cd, retg, convw, wbr, wbc, wout, gffn, wrt, wshgu, wshd)
    return pl.pallas_call(
        _mixer_kernel,
        grid=(b, nt),
        in_specs=[pl.BlockSpec((1, tm, d), lambda i, j: (i, j, 0)),
                  pl.BlockSpec((1, tm, 1), lambda i, j: (i, j, 0)),
                  pl.BlockSpec((1, 6, d), lambda i, j: (i, 0, 0))]
                 + [_resident(w.shape) for w in weights],
        out_specs=[pl.BlockSpec((1, tm, d), lambda i, j: (i, j, 0)),
                   pl.BlockSpec((d // (2 * LANES), tm, LANES), lambda i, j: (0, i * nt + j, 0)),
                   pl.BlockSpec((N_EXPERTS, tm), lambda i, j: (0, i * nt + j))],
        out_shape=[jax.ShapeDtypeStruct((b, s, d), F32),
                   jax.ShapeDtypeStruct((d // (2 * LANES), b * s, LANES), jnp.uint32),
                   jax.ShapeDtypeStruct((N_EXPERTS, b * s), F32)],
        scratch_shapes=[pltpu.VMEM((RET_HEADS, RET_DK, RET_DV), F32),
                        pltpu.VMEM((8, d), F32),
                        pltpu.VMEM((tm, q_w), F32),
                        pltpu.VMEM((tm, q_w), F32),
                        pltpu.VMEM((tm, v_w), BF16),
                        pltpu.VMEM((tm, v_w), F32)],
        compiler_params=pltpu.CompilerParams(dimension_semantics=("arbitrary", "arbitrary"),
                                             vmem_limit_bytes=VMEM_LIMIT_BYTES),
        name="mixer",
    )(x, posf, mod, *weights)


def _first_argmax(v, idx, n):
    m = jnp.max(v, axis=0, keepdims=True)
    return m, jnp.min(jnp.where(v == m, idx, n), axis=0, keepdims=True)


def _route_kernel(lg_ref, bias_ref, tri_ref, eidx_ref, w_ref, rank_ref, cnt_ref, carry_ref):
    @pl.when(pl.program_id(0) == 0)
    def _():
        carry_ref[...] = jnp.zeros_like(carry_ref)

    tn = lg_ref.shape[1]
    neg = F32(-jnp.inf)
    score = _sigmoid(lg_ref[...])
    choice = score + bias_ref[...]

    grow = lax.broadcasted_iota(jnp.int32, (GROUP_SIZE, tn), 0)
    gscores = []
    for g in range(N_GROUPS):
        cg = choice[g * GROUP_SIZE:(g + 1) * GROUP_SIZE]
        m1, i1 = _first_argmax(cg, grow, GROUP_SIZE)
        m2 = jnp.max(jnp.where(grow == i1, neg, cg), axis=0, keepdims=True)
        gscores.append(m1 + m2)
    cur = jnp.concatenate(gscores, axis=0)
    gidx = lax.broadcasted_iota(jnp.int32, (N_GROUPS, tn), 0)
    keep = jnp.zeros((N_GROUPS, tn), F32)
    for _ in range(TOPK_GROUPS):
        _, ig = _first_argmax(cur, gidx, N_GROUPS)
        hit = gidx == ig
        keep = jnp.where(hit, 1.0, keep)
        cur = jnp.where(hit, neg, cur)
    cur = jnp.concatenate(
        [jnp.where(keep[g:g + 1] > 0.0, choice[g * GROUP_SIZE:(g + 1) * GROUP_SIZE], neg)
         for g in range(N_GROUPS)], axis=0)

    erow = lax.broadcasted_iota(jnp.int32, (N_EXPERTS, tn), 0)
    eidx, wts = [], []
    member = jnp.zeros((N_EXPERTS, tn), F32)
    for _ in range(TOP_K):
        _, ie = _first_argmax(cur, erow, N_EXPERTS)
        hit = erow == ie
        eidx.append(ie)
        wts.append(jnp.sum(jnp.where(hit, score, 0.0), axis=0, keepdims=True))
        member = member + hit.astype(F32)
        cur = jnp.where(hit, neg, cur)
    wsum = wts[0]
    for k in range(1, TOP_K):
        wsum = wsum + wts[k]

    before = jnp.dot(member.astype(BF16), tri_ref[...], preferred_element_type=F32) + carry_ref[...]
    ranks = [jnp.sum(jnp.where(erow == eidx[k], before, 0.0), axis=0, keepdims=True) for k in range(TOP_K)]
    carry_ref[...] = carry_ref[...] + jnp.sum(member, axis=1, keepdims=True)

    eidx_ref[...] = jnp.concatenate(eidx, axis=0)
    w_ref[...] = jnp.concatenate([w / wsum * ROUTED_SCALE for w in wts], axis=0)
    rank_ref[...] = jnp.concatenate(ranks, axis=0).astype(jnp.int32)
    cnt_ref[...] = carry_ref[...]


def _route(lgt, bias):
    e, t = lgt.shape
    tn = ROUTE_TOKENS
    tri = jnp.asarray(np.triu(np.ones((tn, tn), np.float32), 1), BF16)
    return pl.pallas_call(
        _route_kernel,
        grid=(t // tn,),
        in_specs=[pl.BlockSpec((e, tn), lambda i: (0, i)),
                  pl.BlockSpec((e, 1), lambda i: (0, 0)),
                  pl.BlockSpec((tn, tn), lambda i: (0, 0))],
        out_specs=[pl.BlockSpec((TOP_K, tn), lambda i: (0, i)),
                   pl.BlockSpec((TOP_K, tn), lambda i: (0, i)),
                   pl.BlockSpec((TOP_K, tn), lambda i: (0, i)),
                   pl.BlockSpec((e, 1), lambda i: (0, 0))],
        out_shape=[jax.ShapeDtypeStruct((TOP_K, t), jnp.int32),
                   jax.ShapeDtypeStruct((TOP_K, t), F32),
                   jax.ShapeDtypeStruct((TOP_K, t), jnp.int32),
                   jax.ShapeDtypeStruct((e, 1), F32)],
        scratch_shapes=[pltpu.VMEM((e, 1), F32)],
        compiler_params=pltpu.CompilerParams(dimension_semantics=("arbitrary",)),
        name="route",
    )(lgt, bias, tri)


def _dest_kernel(eidx_ref, rank_ref, pstart_ref, dest_ref):
    tn = eidx_ref.shape[1]
    erow = lax.broadcasted_iota(jnp.int32, (N_EXPERTS, tn), 0)
    start = pstart_ref[...]
    dest_ref[...] = rank_ref[...] + jnp.concatenate(
        [jnp.sum(jnp.where(erow == eidx_ref[k:k + 1, :], start, 0), axis=0, keepdims=True) for k in range(TOP_K)],
        axis=0)


def _dest(eidx, rank, pstart):
    k, t = eidx.shape
    tn = ROUTE_TOKENS
    return pl.pallas_call(
        _dest_kernel,
        grid=(t // tn,),
        in_specs=[pl.BlockSpec((k, tn), lambda i: (0, i)),
                  pl.BlockSpec((k, tn), lambda i: (0, i)),
                  pl.BlockSpec((N_EXPERTS, 1), lambda i: (0, 0))],
        out_specs=pl.BlockSpec((k, tn), lambda i: (0, i)),
        out_shape=jax.ShapeDtypeStruct((k, t), jnp.int32),
        name="dest",
    )(eidx, rank, pstart)


def _sc_mesh_and_workers():
    mesh = plsc.VectorSubcoreMesh(core_axis_name="c", subcore_axis_name="s")
    return mesh, mesh.num_cores, mesh.num_cores * mesh.num_subcores


def _worker_id(num_cores):
    return lax.axis_index("s") * num_cores + lax.axis_index("c")


def _dispatch(h2p, dest3, n_rows):
    planes, t, lanes = h2p.shape
    n_chunks = dest3.shape[0]
    mesh, num_cores, workers = _sc_mesh_and_workers()
    chunks_per_worker = n_chunks // workers
    assert chunks_per_worker * workers == n_chunks and n_chunks * lanes == t

    def body(h_hbm, d_hbm, xs_hbm, idx_v, rows_v, load_sem, scatter_sem):
        wid = _worker_id(num_cores)

        @pl.loop(0, chunks_per_worker)
        def _(c):
            chunk = wid * chunks_per_worker + c
            tok = pl.ds(chunk * lanes, lanes)
            pltpu.sync_copy(d_hbm.at[chunk], idx_v)
            loads = [None] * planes
            scatters = [None] * planes
            loads[0] = pltpu.async_copy(h_hbm.at[0, tok], rows_v.at[0], load_sem.at[0])
            for j in range(planes):
                loads[j].wait()
                if j + 1 < planes:
                    if j >= 1:
                        for cp in scatters[j - 1]:
                            cp.wait()
                    loads[j + 1] = pltpu.async_copy(h_hbm.at[j + 1, tok], rows_v.at[(j + 1) % 2],
                                                    load_sem.at[(j + 1) % 2])
                scatters[j] = [pltpu.async_copy(rows_v.at[j % 2], xs_hbm.at[j].at[idx_v.at[k]],
                                                scatter_sem.at[j % 2]) for k in range(TOP_K)]
            for j in range(max(planes - 2, 0), planes):
                for cp in scatters[j]:
                    cp.wait()

    return pl.kernel(
        body,
        out_type=jax.ShapeDtypeStruct((planes, n_rows, lanes), h2p.dtype),
        mesh=mesh,
        scratch_types=[pltpu.VMEM((TOP_K, lanes), jnp.int32),
                       pltpu.VMEM((2, lanes, lanes), h2p.dtype),
                       pltpu.SemaphoreType.DMA((2,)),
                       pltpu.SemaphoreType.DMA((2,))],
        name="dispatch",
    )(h2p, dest3)


def _gather(ybuf, dest3, t):
    planes, _, lanes = ybuf.shape
    n_chunks = dest3.shape[0]
    mesh, num_cores, workers = _sc_mesh_and_workers()
    chunks_per_worker = n_chunks // workers
    nbuf = GATHER_BUFFERS
    lag = nbuf // 2
    assert chunks_per_worker * workers == n_chunks and n_chunks * lanes == t

    def body(y_hbm, d_hbm, yg_hbm, idx_v, rows_v, gather_sem, store_sem):
        wid = _worker_id(num_cores)

        @pl.loop(0, chunks_per_worker)
        def _(c):
            chunk = wid * chunks_per_worker + c
            tok = pl.ds(chunk * lanes, lanes)
            pltpu.sync_copy(d_hbm.at[chunk], idx_v)

            @pl.loop(0, TOP_K)
            def _(k):
                gathers = [None] * planes
                stores = [None] * planes

                def store(j):
                    gathers[j].wait()
                    stores[j] = pltpu.async_copy(rows_v.at[j % nbuf], yg_hbm.at[k, j, tok], store_sem.at[j % nbuf])

                for j in range(planes):
                    if j >= nbuf:
                        stores[j - nbuf].wait()
                    gathers[j] = pltpu.async_copy(y_hbm.at[j].at[idx_v.at[k]], rows_v.at[j % nbuf],
                                                  gather_sem.at[j % nbuf])
                    if j >= lag:
                        store(j - lag)
                for j in range(max(planes - lag, 0), planes):
                    store(j)
                for j in range(max(planes - nbuf, 0), planes):
                    stores[j].wait()

    return pl.kernel(
        body,
        out_type=jax.ShapeDtypeStruct((TOP_K, planes, t, lanes), ybuf.dtype),
        mesh=mesh,
        scratch_types=[pltpu.VMEM((TOP_K, lanes), jnp.int32),
                       pltpu.VMEM((nbuf, lanes, lanes), ybuf.dtype),
                       pltpu.SemaphoreType.DMA((nbuf,)),
                       pltpu.SemaphoreType.DMA((nbuf,))],
        name="gather",
    )(ybuf, dest3)


def _experts_kernel(be_ref, nv_ref, nu_ref, x_ref, wg_ref, wu_ref, wd_ref, o_ref, wgb_ref, wub_ref, wdb_ref):
    i = pl.program_id(0)
    active = i < nu_ref[0]
    new_expert = jnp.logical_or(i == 0, be_ref[i] != be_ref[jnp.maximum(i - 1, 0)])

    @pl.when(jnp.logical_and(active, new_expert))
    def _():
        wgb_ref[...] = wg_ref[0].astype(BF16)
        wub_ref[...] = wu_ref[0].astype(BF16)
        wdb_ref[...] = wd_ref[0].astype(BF16)

    @pl.when(active)
    def _():
        planes, bm, lanes = x_ref.shape
        valid = lax.broadcasted_iota(jnp.int32, (bm, lanes), 0) < nv_ref[i]
        g = None
        u = None
        for j in range(planes):
            lo, hi = _unpack_bf16_pairs(jnp.where(valid, x_ref[j], jnp.uint32(0)))
            xj = jnp.concatenate([lo, hi], axis=1)
            rows = pl.ds(2 * lanes * j, 2 * lanes)
            gj = jnp.dot(xj, wgb_ref[rows, :], preferred_element_type=F32)
            uj = jnp.dot(xj, wub_ref[rows, :], preferred_element_type=F32)
            g = gj if g is None else g + gj
            u = uj if u is None else u + uj
        out = jnp.dot((_silu(g) * u).astype(BF16), wdb_ref[...], preferred_element_type=F32)
        for j in range(o_ref.shape[0]):
            o_ref[j] = out[:, lanes * j:lanes * (j + 1)]

    @pl.when(jnp.logical_not(active))
    def _():
        o_ref[...] = jnp.zeros_like(o_ref)


def _experts(block_e, n_valid, n_used, xs, wg, wu, wd):
    planes, p, lanes = xs.shape
    e, d, f = wg.shape
    bm = EXPERT_ROWS
    nb = p // bm

    def row_map(i, be, nv, nu):
        return (0, jnp.minimum(i, nu[0] - 1), 0)

    def w_map(i, be, nv, nu):
        return (be[jnp.minimum(i, nu[0] - 1)], 0, 0)

    return pl.pallas_call(
        _experts_kernel,
        grid_spec=pltpu.PrefetchScalarGridSpec(
            num_scalar_prefetch=3,
            grid=(nb,),
            in_specs=[pl.BlockSpec((planes, bm, lanes), row_map),
                      pl.BlockSpec((1, d, f), w_map),
                      pl.BlockSpec((1, d, f), w_map),
                      pl.BlockSpec((1, f, d), w_map)],
            out_specs=pl.BlockSpec((d // lanes, bm, lanes), lambda i, be, nv, nu: (0, i, 0)),
            scratch_shapes=[pltpu.VMEM((d, f), BF16), pltpu.VMEM((d, f), BF16), pltpu.VMEM((f, d), BF16)]),
        out_shape=jax.ShapeDtypeStruct((d // lanes, p, lanes), F32),
        compiler_params=pltpu.CompilerParams(dimension_semantics=("arbitrary",),
                                             vmem_limit_bytes=VMEM_LIMIT_BYTES),
        name="experts",
    )(block_e, n_valid, n_used, xs, wg, wu, wd)


def _combine_kernel(w_ref, x2_ref, g2_ref, gfin_ref, yg_ref, o_ref):
    w = w_ref[...]
    planes = yg_ref.shape[1]
    routed = None
    for k in range(TOP_K):
        rows = jnp.concatenate([yg_ref[k, j] for j in range(planes)], axis=1) * w[:, k:k + 1]
        routed = rows if routed is None else routed + rows
    o_ref[0] = _rms(x2_ref[0] + g2_ref[0] * routed) * gfin_ref[...]


def _combine(wtok, x2, g2, gfin, yg):
    b, s, d = x2.shape
    _, planes, _, lanes = yg.shape
    tc = COMBINE_TOKENS
    nt = s // tc
    return pl.pallas_call(
        _combine_kernel,
        grid=(b, nt),
        in_specs=[pl.BlockSpec((tc, TOP_K), lambda i, j: (i * nt + j, 0)),
                  pl.BlockSpec((1, tc, d), lambda i, j: (i, j, 0)),
                  pl.BlockSpec((1, 1, d), lambda i, j: (i, 0, 0)),
                  pl.BlockSpec((1, d), lambda i, j: (0, 0)),
                  pl.BlockSpec((TOP_K, planes, tc, lanes), lambda i, j: (0, 0, i * nt + j, 0))],
        out_specs=pl.BlockSpec((1, tc, d), lambda i, j: (i, j, 0)),
        out_shape=jax.ShapeDtypeStruct((b, s, d), F32),
        compiler_params=pltpu.CompilerParams(dimension_semantics=("arbitrary", "arbitrary"),
                                             vmem_limit_bytes=VMEM_LIMIT_BYTES),
        name="combine",
    )(wtok, x2, g2, gfin, yg)


def kernel(x, c, positions, w_ada, b_ada, norm_mix_g, w_in, ret_norm_g, conv_w, w_br_ret, w_br_conv, w_out,
           norm_ffn_g, w_router, router_bias, w_exp_gate, w_exp_up, w_exp_down, w_sh_gate, w_sh_up, w_sh_down,
           norm_final_g):
    b, s, d = x.shape
    t = b * s
    depth = w_in.shape[0]
    assert depth == 1, "the combine kernel applies the final norm, so exactly one layer is supported"
    posf = positions.astype(F32)[:, :, None]
    c8 = jnp.zeros((8, d), F32).at[:b].set(c)
    bm = EXPERT_ROWS
    n_blocks = t * TOP_K // bm + N_EXPERTS
    n_rows = n_blocks * bm

    for l in range(depth):
        mod = _ada(c8, w_ada[l], b_ada[l][None, :])[:b].reshape(b, 6, d)
        x2, h2p, lgt = _mixer(
            x, posf, mod, norm_mix_g[l][None, :], w_in[l].astype(BF16), ret_norm_g[l][None, :], conv_w[l],
            w_br_ret[l].astype(BF16), w_br_conv[l].astype(BF16), w_out[l].astype(BF16), norm_ffn_g[l][None, :],
            w_router[l].T, jnp.concatenate([w_sh_gate[l], w_sh_up[l]], axis=1).astype(BF16),
            w_sh_down[l].astype(BF16))
        eidx, wts, rank, counts = _route(lgt, router_bias[l][:, None])

        cnt = counts[:, 0].astype(jnp.int32)
        pcnt = (cnt + bm - 1) // bm * bm
        pend = jnp.cumsum(pcnt)
        pstart = pend - pcnt
        n_used = jnp.maximum(pend[-1] // bm, 1).astype(jnp.int32)[None]
        block_start = jnp.arange(n_blocks, dtype=jnp.int32) * bm
        block_e = jnp.minimum(jnp.sum((pend[None, :] <= block_start[:, None]).astype(jnp.int32), axis=1),
                              N_EXPERTS - 1)
        n_valid = jnp.clip(pstart[block_e] + cnt[block_e] - block_start, 0, bm).astype(jnp.int32)

        dest = _dest(eidx, rank, pstart[:, None])
        dest3 = dest.reshape(TOP_K, t // LANES, LANES).transpose(1, 0, 2)
        xs = _dispatch(h2p, dest3, n_rows)
        ybuf = _experts(block_e, n_valid, n_used, xs, w_exp_gate[l], w_exp_up[l], w_exp_down[l])
        yg = _gather(ybuf, dest3, t)
        x = _combine(wts.T, x2, mod[:, 5:6, :], norm_final_g[None, :], yg)
    return x
```

```python
import numpy as np
import jax
import jax.numpy as jnp
from jax import lax
from jax.experimental import pallas as pl
from jax.experimental.pallas import tpu as pltpu
from jax.experimental.pallas import tpu_sc as plsc

RET_HEADS = 4
RET_DK = 128
RET_DV = 256
RET_CHUNK = 128
ROPE_THETA = 10000.0
CONV_K = 3
N_EXPERTS = 256
TOP_K = 8
N_GROUPS = 8
TOPK_GROUPS = 4
GROUP_SIZE = N_EXPERTS // N_GROUPS
ROUTED_SCALE = 2.5
NORM_EPS = 1e-6

MIXER_TOKENS = 512
ROUTE_TOKENS = 256
EXPERT_ROWS = 256
COMBINE_TOKENS = 256
LANES = 128
GATHER_BUFFERS = 4

VMEM_LIMIT_BYTES = 56 * 1024 * 1024

F32 = jnp.float32
BF16 = jnp.bfloat16
HIGHEST = lax.Precision.HIGHEST


def _sigmoid(v):
    return 1.0 / (1.0 + jnp.exp(-v))


def _silu(v):
    return v * _sigmoid(v)


def _rms(v):
    return v * lax.rsqrt(jnp.mean(v * v, axis=-1, keepdims=True) + NORM_EPS)


def _resident(shape):
    nd = len(shape)
    return pl.BlockSpec(shape, lambda *_: (0,) * nd, pipeline_mode=pl.Buffered(1))


def _ada_kernel(c_ref, w_ref, b_ref, o_ref):
    c = c_ref[...]
    o_ref[...] = jnp.dot(_silu(c), w_ref[...], precision=HIGHEST, preferred_element_type=F32) + b_ref[...]


def _ada(c8, w, b):
    d, n = w.shape
    tn = 1024
    return pl.pallas_call(
        _ada_kernel,
        grid=(n // tn,),
        in_specs=[pl.BlockSpec((8, d), lambda j: (0, 0)),
                  pl.BlockSpec((d, tn), lambda j: (0, j)),
                  pl.BlockSpec((1, tn), lambda j: (0, j))],
        out_specs=pl.BlockSpec((8, tn), lambda j: (0, j)),
        out_shape=jax.ShapeDtypeStruct((8, n), F32),
        name="ada",
    )(c8, w, b)


def _retention_constants():
    c = RET_CHUNK
    log_g = jnp.log1p(-(2.0 ** (-5.0 - jnp.arange(RET_HEADS, dtype=F32))))
    idx = jnp.arange(c, dtype=F32)
    diff = idx[:, None] - idx[None, :]
    intra = jnp.where(diff >= 0, jnp.exp(log_g[:, None, None] * jnp.maximum(diff, 0.0)), 0.0)
    k_decay = jnp.exp(log_g[:, None] * (c - 1 - idx))
    q_decay = jnp.exp(log_g[:, None] * (idx + 1.0))
    chunk_decay = jnp.exp(log_g * c)
    kd = jnp.broadcast_to(k_decay[:, :, None], (RET_HEADS, c, RET_DK))
    qd = jnp.broadcast_to(q_decay[:, :, None], (RET_HEADS, c, RET_DK))
    cd = jnp.broadcast_to(chunk_decay[:, None, None], (RET_HEADS, 1, RET_DV))
    inv_freq = ROPE_THETA ** (-jnp.arange(0, RET_DK, 2, dtype=F32) / RET_DK)
    inv_freq = jnp.concatenate([inv_freq, inv_freq])[None, :]
    sign = jnp.concatenate([-jnp.ones((RET_DK // 2,), F32), jnp.ones((RET_DK // 2,), F32)])[None, :]
    return intra, qd, kd, cd, inv_freq, sign


def _pack_bf16_pairs(v):
    w = v.shape[1] // 2
    lo = lax.bitcast_convert_type(v[:, :w].astype(BF16).astype(F32), jnp.uint32)
    hi = lax.bitcast_convert_type(v[:, w:].astype(BF16).astype(F32), jnp.uint32)
    return (lo >> 16) | (hi & jnp.uint32(0xFFFF0000))


def _unpack_pairs_f32(word):
    lo = lax.bitcast_convert_type(word << 16, F32)
    hi = lax.bitcast_convert_type(word & jnp.uint32(0xFFFF0000), F32)
    return jnp.concatenate([lo, hi], axis=1)


def _mixer_kernel(x_ref, pos_ref, mod_ref, gmix_ref, win_ref, invf_ref, sign_ref, intra_ref, qd_ref, kd_ref, cd_ref,
                  retg_ref, convw_ref, wbr_ref, wbc_ref, wout_ref, gffn_ref, wrt_ref, wshgu_ref, wshd_ref,
                  x2_ref, h2p_ref, lgt_ref,
                  state_ref, carry_ref, q_ref, k_ref, v_ref, ret_ref):
    tm, d = x_ref.shape[1], x_ref.shape[2]
    q_w = RET_HEADS * RET_DK
    v_w = RET_HEADS * RET_DV
    offs = np.cumsum([0, q_w, q_w, v_w, v_w, d, d, d, d, d])

    @pl.when(pl.program_id(1) == 0)
    def _():
        state_ref[...] = jnp.zeros_like(state_ref)
        carry_ref[...] = jnp.zeros_like(carry_ref)

    x = x_ref[0]
    mod = mod_ref[0]
    sh1, sc1, g1, sh2, sc2, g2 = [mod[i:i + 1] for i in range(6)]
    hb = ((_rms(x) * gmix_ref[...]) * (1.0 + sc1) + sh1).astype(BF16)

    def proj(i):
        return jnp.dot(hb, win_ref[:, offs[i]:offs[i + 1]], preferred_element_type=F32)

    ang = pos_ref[0] * invf_ref[...]
    cosv = jnp.cos(ang)
    sinv = jnp.sin(ang) * sign_ref[...]

    def rope(t):
        return jnp.concatenate(
            [t[:, h * RET_DK:(h + 1) * RET_DK] * cosv
             + pltpu.roll(t[:, h * RET_DK:(h + 1) * RET_DK], RET_DK // 2, 1) * sinv
             for h in range(RET_HEADS)], axis=1)

    q_ref[...] = rope(proj(0)) * (RET_DK ** -0.5)
    k_ref[...] = rope(proj(1))
    v_ref[...] = proj(2).astype(BF16)

    for c in range(tm // RET_CHUNK):
        rows = pl.ds(c * RET_CHUNK, RET_CHUNK)
        for h in range(RET_HEADS):
            qh = q_ref[rows, h * RET_DK:(h + 1) * RET_DK]
            kh = k_ref[rows, h * RET_DK:(h + 1) * RET_DK]
            vh = v_ref[rows, h * RET_DV:(h + 1) * RET_DV]
            scores = lax.dot_general(qh.astype(BF16), kh.astype(BF16), (((1,), (1,)), ((), ())),
                                     preferred_element_type=F32) * intra_ref[h]
            inner = jnp.dot(scores.astype(BF16), vh, preferred_element_type=F32)
            st = state_ref[h]
            cross = jnp.dot((qh * qd_ref[h]).astype(BF16), st.astype(BF16), preferred_element_type=F32)
            kv = lax.dot_general((kh * kd_ref[h]).astype(BF16), vh, (((0,), (0,)), ((), ())),
                                 preferred_element_type=F32)
            state_ref[h] = st * cd_ref[h] + kv
            ret_ref[rows, h * RET_DV:(h + 1) * RET_DV] = _rms(inner + cross)

    y_ret = jnp.dot((_silu(proj(3)) * (ret_ref[...] * retg_ref[...])).astype(BF16), wbr_ref[...],
                    preferred_element_type=F32)

    z = proj(5) * proj(4)
    row = lax.broadcasted_iota(jnp.int32, z.shape, 0)
    prev1 = carry_ref[7:8, :]
    prev2 = carry_ref[6:7, :]
    z1 = jnp.where(row == 0, prev1, pltpu.roll(z, 1, 0))
    z2 = jnp.where(row == 0, prev2, jnp.where(row == 1, prev1, pltpu.roll(z, 2, 0)))
    conv = convw_ref[0:1, :] * z2 + convw_ref[1:2, :] * z1 + convw_ref[2:3, :] * z
    carry_ref[...] = z[tm - 8:tm, :]
    y_conv = jnp.dot((proj(6) * conv).astype(BF16), wbc_ref[...], preferred_element_type=F32)

    mix = _sigmoid(proj(7)) * y_ret + _sigmoid(proj(8)) * y_conv
    x1 = x + g1 * jnp.dot(mix.astype(BF16), wout_ref[...], preferred_element_type=F32)

    h2 = (_rms(x1) * gffn_ref[...]) * (1.0 + sc2) + sh2
    lgt_ref[...] = lax.dot_general(wrt_ref[...], h2, (((1,), (1,)), ((), ())), precision=HIGHEST,
                                   preferred_element_type=F32)
    gu = jnp.dot(h2.astype(BF16), wshgu_ref[...], preferred_element_type=F32)
    f = gu.shape[1] // 2
    shared = jnp.dot((_silu(gu[:, :f]) * gu[:, f:]).astype(BF16), wshd_ref[...], preferred_element_type=F32)
    x2_ref[0] = x1 + g2 * shared
    for j in range(h2p_ref.shape[0]):
        h2p_ref[j] = _pack_bf16_pairs(h2[:, 2 * LANES * j:2 * LANES * (j + 1)])


def _mixer(x, posf, mod, gmix, win, retg, convw, wbr, wbc, wout, gffn, wrt, wshgu, wshd):
    b, s, d = x.shape
    tm = MIXER_TOKENS
    nt = s // tm
    intra, qd, kd, cd, invf, sign = _retention_constants()
    q_w, v_w = RET_HEADS * RET_DK, RET_HEADS * RET_DV
    weights = (gmix, win, invf, sign, intra, qd, kd, cd, retg, convw, wbr, wbc, wout, gffn, wrt, wshgu, wshd)
    return pl.pallas_call(
        _mixer_kernel,
        grid=(b, nt),
        in_specs=[pl.BlockSpec((1, tm, d), lambda i, j: (i, j, 0)),
                  pl.BlockSpec((1, tm, 1), lambda i, j: (i, j, 0)),
                  pl.BlockSpec((1, 6, d), lambda i, j: (i, 0, 0))]
                 + [_resident(w.shape) for w in weights],
        out_specs=[pl.BlockSpec((1, tm, d), lambda i, j: (i, j, 0)),
                   pl.BlockSpec((d // (2 * LANES), tm, LANES), lambda i, j: (0, i * nt + j, 0)),
                   pl.BlockSpec((N_EXPERTS, tm), lambda i, j: (0, i * nt + j))],
        out_shape=[jax.ShapeDtypeStruct((b, s, d), F32),
                   jax.ShapeDtypeStruct((d // (2 * LANES), b * s, LANES), jnp.uint32),
                   jax.ShapeDtypeStruct((N_EXPERTS, b * s), F32)],
        scratch_shapes=[pltpu.VMEM((RET_HEADS, RET_DK, RET_DV), F32),
                        pltpu.VMEM((8, d), F32),
                        pltpu.VMEM((tm, q_w), F32),
                        pltpu.VMEM((tm, q_w), F32),
                        pltpu.VMEM((tm, v_w), BF16),
                        pltpu.VMEM((tm, v_w), F32)],
        compiler_params=pltpu.CompilerParams(dimension_semantics=("arbitrary", "arbitrary"),
                                             vmem_limit_bytes=VMEM_LIMIT_BYTES),
        name="mixer",
    )(x, posf, mod, *weights)


def _first_argmax(v, idx, n):
    m = jnp.max(v, axis=0, keepdims=True)
    return m, jnp.min(jnp.where(v == m, idx, n), axis=0, keepdims=True)


def _route_kernel(lg_ref, bias_ref, tri_ref, eidx_ref, w_ref, rank_ref, cnt_ref, carry_ref):
    @pl.when(pl.program_id(0) == 0)
    def _():
        carry_ref[...] = jnp.zeros_like(carry_ref)

    tn = lg_ref.shape[1]
    neg = F32(-jnp.inf)
    score = _sigmoid(lg_ref[...])
    choice = score + bias_ref[...]

    grow = lax.broadcasted_iota(jnp.int32, (GROUP_SIZE, tn), 0)
    gscores = []
    for g in range(N_GROUPS):
        cg = choice[g * GROUP_SIZE:(g + 1) * GROUP_SIZE]
        m1, i1 = _first_argmax(cg, grow, GROUP_SIZE)
        m2 = jnp.max(jnp.where(grow == i1, neg, cg), axis=0, keepdims=True)
        gscores.append(m1 + m2)
    cur = jnp.concatenate(gscores, axis=0)
    gidx = lax.broadcasted_iota(jnp.int32, (N_GROUPS, tn), 0)
    keep = jnp.zeros((N_GROUPS, tn), F32)
    for _ in range(TOPK_GROUPS):
        _, ig = _first_argmax(cur, gidx, N_GROUPS)
        hit = gidx == ig
        keep = jnp.where(hit, 1.0, keep)
        cur = jnp.where(hit, neg, cur)
    cur = jnp.concatenate(
        [jnp.where(keep[g:g + 1] > 0.0, choice[g * GROUP_SIZE:(g + 1) * GROUP_SIZE], neg)
         for g in range(N_GROUPS)], axis=0)

    erow = lax.broadcasted_iota(jnp.int32, (N_EXPERTS, tn), 0)
    eidx, wts = [], []
    member = jnp.zeros((N_EXPERTS, tn), F32)
    for _ in range(TOP_K):
        _, ie = _first_argmax(cur, erow, N_EXPERTS)
        hit = erow == ie
        eidx.append(ie)
        wts.append(jnp.sum(jnp.where(hit, score, 0.0), axis=0, keepdims=True))
        member = member + hit.astype(F32)
        cur = jnp.where(hit, neg, cur)
    wsum = wts[0]
    for k in range(1, TOP_K):
        wsum = wsum + wts[k]

    before = jnp.dot(member.astype(BF16), tri_ref[...], preferred_element_type=F32) + carry_ref[...]
    ranks = [jnp.sum(jnp.where(erow == eidx[k], before, 0.0), axis=0, keepdims=True) for k in range(TOP_K)]
    carry_ref[...] = carry_ref[...] + jnp.sum(member, axis=1, keepdims=True)

    eidx_ref[...] = jnp.concatenate(eidx, axis=0)
    w_ref[...] = jnp.concatenate([w / wsum * ROUTED_SCALE for w in wts], axis=0)
    rank_ref[...] = jnp.concatenate(ranks, axis=0).astype(jnp.int32)
    cnt_ref[...] = carry_ref[...]


def _route(lgt, bias):
    e, t = lgt.shape
    tn = ROUTE_TOKENS
    tri = jnp.asarray(np.triu(np.ones((tn, tn), np.float32), 1), BF16)
    return pl.pallas_call(
        _route_kernel,
        grid=(t // tn,),
        in_specs=[pl.BlockSpec((e, tn), lambda i: (0, i)),
                  pl.BlockSpec((e, 1), lambda i: (0, 0)),
                  pl.BlockSpec((tn, tn), lambda i: (0, 0))],
        out_specs=[pl.BlockSpec((TOP_K, tn), lambda i: (0, i)),
                   pl.BlockSpec((TOP_K, tn), lambda i: (0, i)),
                   pl.BlockSpec((TOP_K, tn), lambda i: (0, i)),
                   pl.BlockSpec((e, 1), lambda i: (0, 0))],
        out_shape=[jax.ShapeDtypeStruct((TOP_K, t), jnp.int32),
                   jax.ShapeDtypeStruct((TOP_K, t), F32),
                   jax.ShapeDtypeStruct((TOP_K, t), jnp.int32),
                   jax.ShapeDtypeStruct((e, 1), F32)],
        scratch_shapes=[pltpu.VMEM((e, 1), F32)],
        compiler_params=pltpu.CompilerParams(dimension_semantics=("arbitrary",)),
        name="route",
    )(lgt, bias, tri)


def _dest_kernel(eidx_ref, rank_ref, pstart_ref, dest_ref):
    tn = eidx_ref.shape[1]
    erow = lax.broadcasted_iota(jnp.int32, (N_EXPERTS, tn), 0)
    start = pstart_ref[...]
    dest_ref[...] = rank_ref[...] + jnp.concatenate(
        [jnp.sum(jnp.where(erow == eidx_ref[k:k + 1, :], start, 0), axis=0, keepdims=True) for k in range(TOP_K)],
        axis=0)


def _dest(eidx, rank, pstart):
    k, t = eidx.shape
    tn = ROUTE_TOKENS
    return pl.pallas_call(
        _dest_kernel,
        grid=(t // tn,),
        in_specs=[pl.BlockSpec((k, tn), lambda i: (0, i)),
                  pl.BlockSpec((k, tn), lambda i: (0, i)),
                  pl.BlockSpec((N_EXPERTS, 1), lambda i: (0, 0))],
        out_specs=pl.BlockSpec((k, tn), lambda i: (0, i)),
        out_shape=jax.ShapeDtypeStruct((k, t), jnp.int32),
        name="dest",
    )(eidx, rank, pstart)


def _sc_mesh_and_workers():
    mesh = plsc.VectorSubcoreMesh(core_axis_name="c", subcore_axis_name="s")
    return mesh, mesh.num_cores, mesh.num_cores * mesh.num_subcores


def _worker_id(num_cores):
    return lax.axis_index("s") * num_cores + lax.axis_index("c")


def _dispatch(h2p, dest3, n_rows):
    planes, t, lanes = h2p.shape
    n_chunks = dest3.shape[0]
    mesh, num_cores, workers = _sc_mesh_and_workers()
    chunks_per_worker = n_chunks // workers
    assert chunks_per_worker * workers == n_chunks and n_chunks * lanes == t

    def body(h_hbm, d_hbm, xs_hbm, idx_v, rows_v, load_sem, scatter_sem):
        wid = _worker_id(num_cores)

        @pl.loop(0, chunks_per_worker)
        def _(c):
            chunk = wid * chunks_per_worker + c
            tok = pl.ds(chunk * lanes, lanes)
            pltpu.sync_copy(d_hbm.at[chunk], idx_v)
            loads = [None] * planes
            scatters = [None] * planes
            loads[0] = pltpu.async_copy(h_hbm.at[0, tok], rows_v.at[0], load_sem.at[0])
            for j in range(planes):
                loads[j].wait()
                if j + 1 < planes:
                    if j >= 1:
                        for cp in scatters[j - 1]:
                            cp.wait()
                    loads[j + 1] = pltpu.async_copy(h_hbm.at[j + 1, tok], rows_v.at[(j + 1) % 2],
                                                    load_sem.at[(j + 1) % 2])
                scatters[j] = [pltpu.async_copy(rows_v.at[j % 2], xs_hbm.at[j].at[idx_v.at[k]],
                                                scatter_sem.at[j % 2]) for k in range(TOP_K)]
            for j in range(max(planes - 2, 0), planes):
                for cp in scatters[j]:
                    cp.wait()

    return pl.kernel(
        body,
        out_type=jax.ShapeDtypeStruct((planes, n_rows, lanes), h2p.dtype),
        mesh=mesh,
        scratch_types=[pltpu.VMEM((TOP_K, lanes), jnp.int32),
                       pltpu.VMEM((2, lanes, lanes), h2p.dtype),
                       pltpu.SemaphoreType.DMA((2,)),
                       pltpu.SemaphoreType.DMA((2,))],
        name="dispatch",
    )(h2p, dest3)


def _gather(ybuf, dest3, t):
    planes, _, lanes = ybuf.shape
    n_chunks = dest3.shape[0]
    mesh, num_cores, workers = _sc_mesh_and_workers()
    chunks_per_worker = n_chunks // workers
    nbuf = GATHER_BUFFERS
    lag = nbuf // 2
    assert chunks_per_worker * workers == n_chunks and n_chunks * lanes == t

    def body(y_hbm, d_hbm, yg_hbm, idx_v, rows_v, gather_sem, store_sem):
        wid = _worker_id(num_cores)

        @pl.loop(0, chunks_per_worker)
        def _(c):
            chunk = wid * chunks_per_worker + c
            tok = pl.ds(chunk * lanes, lanes)
            pltpu.sync_copy(d_hbm.at[chunk], idx_v)

            @pl.loop(0, TOP_K)
            def _(k):
                gathers = [None] * planes
                stores = [None] * planes

                def store(j):
                    gathers[j].wait()
                    stores[j] = pltpu.async_copy(rows_v.at[j % nbuf], yg_hbm.at[k, j, tok], store_sem.at[j % nbuf])

                for j in range(planes):
                    if j >= nbuf:
                        stores[j - nbuf].wait()
                    gathers[j] = pltpu.async_copy(y_hbm.at[j].at[idx_v.at[k]], rows_v.at[j % nbuf],
                                                  gather_sem.at[j % nbuf])
                    if j >= lag:
                        store(j - lag)
                for j in range(max(planes - lag, 0), planes):
                    store(j)
                for j in range(max(planes - nbuf, 0), planes):
                    stores[j].wait()

    return pl.kernel(
        body,
        out_type=jax.ShapeDtypeStruct((TOP_K, planes, t, lanes), ybuf.dtype),
        mesh=mesh,
        scratch_types=[pltpu.VMEM((TOP_K, lanes), jnp.int32),
                       pltpu.VMEM((nbuf, lanes, lanes), ybuf.dtype),
                       pltpu.SemaphoreType.DMA((nbuf,)),
                       pltpu.SemaphoreType.DMA((nbuf,))],
        name="gather",
    )(ybuf, dest3)


def _experts_kernel(be_ref, nv_ref, nu_ref, x_ref, wg_ref, wu_ref, wd_ref, o_ref, wgb_ref, wub_ref, wdb_ref):
    i = pl.program_id(0)
    active = i < nu_ref[0]
    new_expert = jnp.logical_or(i == 0, be_ref[i] != be_ref[jnp.maximum(i - 1, 0)])

    @pl.when(jnp.logical_and(active, new_expert))
    def _():
        wgb_ref[...] = wg_ref[0].astype(BF16)
        wub_ref[...] = wu_ref[0].astype(BF16)
        wdb_ref[...] = wd_ref[0].astype(BF16)

    @pl.when(active)
    def _():
        planes, bm, lanes = x_ref.shape
        valid = lax.broadcasted_iota(jnp.int32, (bm, lanes), 0) < nv_ref[i]
        g = None
        u = None
        for j in range(planes):
            xj = _unpack_pairs_f32(jnp.where(valid, x_ref[j], jnp.uint32(0))).astype(BF16)
            rows = pl.ds(2 * lanes * j, 2 * lanes)
            gj = jnp.dot(xj, wgb_ref[rows, :], preferred_element_type=F32)
            uj = jnp.dot(xj, wub_ref[rows, :], preferred_element_type=F32)
            g = gj if g is None else g + gj
            u = uj if u is None else u + uj
        out = jnp.dot((_silu(g) * u).astype(BF16), wdb_ref[...], preferred_element_type=F32)
        for j in range(o_ref.shape[0]):
            o_ref[j] = _pack_bf16_pairs(out[:, 2 * lanes * j:2 * lanes * (j + 1)])


def _experts(block_e, n_valid, n_used, xs, wg, wu, wd):
    planes, p, lanes = xs.shape
    e, d, f = wg.shape
    bm = EXPERT_ROWS
    nb = p // bm

    def row_map(i, be, nv, nu):
        return (0, jnp.minimum(i, nu[0] - 1), 0)

    def w_map(i, be, nv, nu):
        return (be[jnp.minimum(i, nu[0] - 1)], 0, 0)

    return pl.pallas_call(
        _experts_kernel,
        grid_spec=pltpu.PrefetchScalarGridSpec(
            num_scalar_prefetch=3,
            grid=(nb,),
            in_specs=[pl.BlockSpec((planes, bm, lanes), row_map),
                      pl.BlockSpec((1, d, f), w_map),
                      pl.BlockSpec((1, d, f), w_map),
                      pl.BlockSpec((1, f, d), w_map)],
            out_specs=pl.BlockSpec((planes, bm, lanes), row_map),
            scratch_shapes=[pltpu.VMEM((d, f), BF16), pltpu.VMEM((d, f), BF16), pltpu.VMEM((f, d), BF16)]),
        out_shape=jax.ShapeDtypeStruct((planes, p, lanes), jnp.uint32),
        compiler_params=pltpu.CompilerParams(dimension_semantics=("arbitrary",),
                                             vmem_limit_bytes=VMEM_LIMIT_BYTES),
        name="experts",
    )(block_e, n_valid, n_used, xs, wg, wu, wd)


def _combine_kernel(w_ref, x2_ref, g2_ref, gfin_ref, yg_ref, o_ref):
    w = w_ref[...]
    planes = yg_ref.shape[1]
    routed = None
    for k in range(TOP_K):
        rows = jnp.concatenate([_unpack_pairs_f32(yg_ref[k, j]) for j in range(planes)], axis=1) * w[:, k:k + 1]
        routed = rows if routed is None else routed + rows
    o_ref[0] = _rms(x2_ref[0] + g2_ref[0] * routed) * gfin_ref[...]


def _combine(wtok, x2, g2, gfin, yg):
    b, s, d = x2.shape
    _, planes, _, lanes = yg.shape
    tc = COMBINE_TOKENS
    nt = s // tc
    return pl.pallas_call(
        _combine_kernel,
        grid=(b, nt),
        in_specs=[pl.BlockSpec((tc, TOP_K), lambda i, j: (i * nt + j, 0)),
                  pl.BlockSpec((1, tc, d), lambda i, j: (i, j, 0)),
                  pl.BlockSpec((1, 1, d), lambda i, j: (i, 0, 0)),
                  pl.BlockSpec((1, d), lambda i, j: (0, 0)),
                  pl.BlockSpec((TOP_K, planes, tc, lanes), lambda i, j: (0, 0, i * nt + j, 0))],
        out_specs=pl.BlockSpec((1, tc, d), lambda i, j: (i, j, 0)),
        out_shape=jax.ShapeDtypeStruct((b, s, d), F32),
        compiler_params=pltpu.CompilerParams(dimension_semantics=("arbitrary", "arbitrary"),
                                             vmem_limit_bytes=VMEM_LIMIT_BYTES),
        name="combine",
    )(wtok, x2, g2, gfin, yg)


def kernel(x, c, positions, w_ada, b_ada, norm_mix_g, w_in, ret_norm_g, conv_w, w_br_ret, w_br_conv, w_out,
           norm_ffn_g, w_router, router_bias, w_exp_gate, w_exp_up, w_exp_down, w_sh_gate, w_sh_up, w_sh_down,
           norm_final_g):
    b, s, d = x.shape
    t = b * s
    depth = w_in.shape[0]
    assert depth == 1, "the combine kernel applies the final norm, so exactly one layer is supported"
    posf = positions.astype(F32)[:, :, None]
    c8 = jnp.zeros((8, d), F32).at[:b].set(c)
    bm = EXPERT_ROWS
    n_blocks = t * TOP_K // bm + N_EXPERTS
    n_rows = n_blocks * bm

    for l in range(depth):
        mod = _ada(c8, w_ada[l], b_ada[l][None, :])[:b].reshape(b, 6, d)
        x2, h2p, lgt = _mixer(
            x, posf, mod, norm_mix_g[l][None, :], w_in[l].astype(BF16), ret_norm_g[l][None, :], conv_w[l],
            w_br_ret[l].astype(BF16), w_br_conv[l].astype(BF16), w_out[l].astype(BF16), norm_ffn_g[l][None, :],
            w_router[l].T, jnp.concatenate([w_sh_gate[l], w_sh_up[l]], axis=1).astype(BF16),
            w_sh_down[l].astype(BF16))
        eidx, wts, rank, counts = _route(lgt, router_bias[l][:, None])

        cnt = counts[:, 0].astype(jnp.int32)
        pcnt = (cnt + bm - 1) // bm * bm
        pend = jnp.cumsum(pcnt)
        pstart = pend - pcnt
        n_used = jnp.maximum(pend[-1] // bm, 1).astype(jnp.int32)[None]
        block_start = jnp.arange(n_blocks, dtype=jnp.int32) * bm
        block_e = jnp.minimum(jnp.sum((pend[None, :] <= block_start[:, None]).astype(jnp.int32), axis=1),
                              N_EXPERTS - 1)
        n_valid = jnp.clip(pstart[block_e] + cnt[block_e] - block_start, 0, bm).astype(jnp.int32)

        dest = _dest(eidx, rank, pstart[:, None])
        dest3 = dest.reshape(TOP_K, t // LANES, LANES).transpose(1, 0, 2)
        xs = _dispatch(h2p, dest3, n_rows)
        ybuf = _experts(block_e, n_valid, n_used, xs, w_exp_gate[l], w_exp_up[l], w_exp_down[l])
        yg = _gather(ybuf, dest3, t)
        x = _combine(wts.T, x2, mod[:, 5:6, :], norm_final_g[None, :], yg)
    return x
```

```python
import numpy as np
import jax
import jax.numpy as jnp
from jax import lax
from jax.experimental import pallas as pl
from jax.experimental.pallas import tpu as pltpu
from jax.experimental.pallas import tpu_sc as plsc

RET_HEADS = 4
RET_DK = 128
RET_DV = 256
RET_CHUNK = 128
ROPE_THETA = 10000.0
CONV_K = 3
N_EXPERTS = 256
TOP_K = 8
N_GROUPS = 8
TOPK_GROUPS = 4
GROUP_SIZE = N_EXPERTS // N_GROUPS
ROUTED_SCALE = 2.5
NORM_EPS = 1e-6

MIXER_TOKENS = 512
ROUTE_TOKENS = 256
EXPERT_ROWS = 256
COMBINE_TOKENS = 256
LANES = 128
GATHER_BUFFERS = 4

VMEM_LIMIT_BYTES = 56 * 1024 * 1024

F32 = jnp.float32
BF16 = jnp.bfloat16
HIGHEST = lax.Precision.HIGHEST


def _sigmoid(v):
    return 1.0 / (1.0 + jnp.exp(-v))


def _silu(v):
    return v * _sigmoid(v)


def _rms(v):
    return v * lax.rsqrt(jnp.mean(v * v, axis=-1, keepdims=True) + NORM_EPS)


def _resident(shape):
    nd = len(shape)
    return pl.BlockSpec(shape, lambda *_: (0,) * nd, pipeline_mode=pl.Buffered(1))


def _ada_kernel(c_ref, w_ref, b_ref, o_ref):
    c = c_ref[...]
    o_ref[...] = jnp.dot(_silu(c), w_ref[...], precision=HIGHEST, preferred_element_type=F32) + b_ref[...]


def _ada(c8, w, b):
    d, n = w.shape
    tn = 1024
    return pl.pallas_call(
        _ada_kernel,
        grid=(n // tn,),
        in_specs=[pl.BlockSpec((8, d), lambda j: (0, 0)),
                  pl.BlockSpec((d, tn), lambda j: (0, j)),
                  pl.BlockSpec((1, tn), lambda j: (0, j))],
        out_specs=pl.BlockSpec((8, tn), lambda j: (0, j)),
        out_shape=jax.ShapeDtypeStruct((8, n), F32),
        name="ada",
    )(c8, w, b)


def _retention_constants():
    c = RET_CHUNK
    log_g = jnp.log1p(-(2.0 ** (-5.0 - jnp.arange(RET_HEADS, dtype=F32))))
    idx = jnp.arange(c, dtype=F32)
    diff = idx[:, None] - idx[None, :]
    intra = jnp.where(diff >= 0, jnp.exp(log_g[:, None, None] * jnp.maximum(diff, 0.0)), 0.0)
    k_decay = jnp.exp(log_g[:, None] * (c - 1 - idx))
    q_decay = jnp.exp(log_g[:, None] * (idx + 1.0))
    chunk_decay = jnp.exp(log_g * c)
    kd = jnp.broadcast_to(k_decay[:, :, None], (RET_HEADS, c, RET_DK))
    qd = jnp.broadcast_to(q_decay[:, :, None], (RET_HEADS, c, RET_DK))
    cd = jnp.broadcast_to(chunk_decay[:, None, None], (RET_HEADS, 1, RET_DV))
    inv_freq = ROPE_THETA ** (-jnp.arange(0, RET_DK, 2, dtype=F32) / RET_DK)
    inv_freq = jnp.concatenate([inv_freq, inv_freq])[None, :]
    sign = jnp.concatenate([-jnp.ones((RET_DK // 2,), F32), jnp.ones((RET_DK // 2,), F32)])[None, :]
    return intra, qd, kd, cd, inv_freq, sign


def _pack_bf16_pairs(v):
    w = v.shape[1] // 2
    lo = lax.bitcast_convert_type(v[:, :w].astype(BF16).astype(F32), jnp.uint32)
    hi = lax.bitcast_convert_type(v[:, w:].astype(BF16).astype(F32), jnp.uint32)
    return (lo >> 16) | (hi & jnp.uint32(0xFFFF0000))


def _unpack_pairs_f32(word):
    lo = lax.bitcast_convert_type(word << 16, F32)
    hi = lax.bitcast_convert_type(word & jnp.uint32(0xFFFF0000), F32)
    return jnp.concatenate([lo, hi], axis=1)


def _mixer_kernel(x_ref, pos_ref, mod_ref, gmix_ref, win_ref, invf_ref, sign_ref, intra_ref, qd_ref, kd_ref, cd_ref,
                  retg_ref, convw_ref, wbr_ref, wbc_ref, wout_ref, gffn_ref, wrt_ref, wshgu_ref, wshd_ref,
                  x2_ref, h2p_ref, lgt_ref,
                  state_ref, carry_ref, q_ref, k_ref, v_ref, ret_ref):
    tm, d = x_ref.shape[1], x_ref.shape[2]
    q_w = RET_HEADS * RET_DK
    v_w = RET_HEADS * RET_DV
    offs = np.cumsum([0, q_w, q_w, v_w, v_w, d, d, d, d, d])

    @pl.when(pl.program_id(1) == 0)
    def _():
        state_ref[...] = jnp.zeros_like(state_ref)
        carry_ref[...] = jnp.zeros_like(carry_ref)

    x = x_ref[0]
    mod = mod_ref[0]
    sh1, sc1, g1, sh2, sc2, g2 = [mod[i:i + 1] for i in range(6)]
    hb = ((_rms(x) * gmix_ref[...]) * (1.0 + sc1) + sh1).astype(BF16)

    def proj(i):
        return jnp.dot(hb, win_ref[:, offs[i]:offs[i + 1]], preferred_element_type=F32)

    ang = pos_ref[0] * invf_ref[...]
    cosv = jnp.cos(ang)
    sinv = jnp.sin(ang) * sign_ref[...]

    def rope(t):
        return jnp.concatenate(
            [t[:, h * RET_DK:(h + 1) * RET_DK] * cosv
             + pltpu.roll(t[:, h * RET_DK:(h + 1) * RET_DK], RET_DK // 2, 1) * sinv
             for h in range(RET_HEADS)], axis=1)

    q_ref[...] = rope(proj(0)) * (RET_DK ** -0.5)
    k_ref[...] = rope(proj(1))
    v_ref[...] = proj(2).astype(BF16)

    for c in range(tm // RET_CHUNK):
        rows = pl.ds(c * RET_CHUNK, RET_CHUNK)
        for h in range(RET_HEADS):
            qh = q_ref[rows, h * RET_DK:(h + 1) * RET_DK]
            kh = k_ref[rows, h * RET_DK:(h + 1) * RET_DK]
            vh = v_ref[rows, h * RET_DV:(h + 1) * RET_DV]
            scores = lax.dot_general(qh.astype(BF16), kh.astype(BF16), (((1,), (1,)), ((), ())),
                                     preferred_element_type=F32) * intra_ref[h]
            inner = jnp.dot(scores.astype(BF16), vh, preferred_element_type=F32)
            st = state_ref[h]
            cross = jnp.dot((qh * qd_ref[h]).astype(BF16), st.astype(BF16), preferred_element_type=F32)
            kv = lax.dot_general((kh * kd_ref[h]).astype(BF16), vh, (((0,), (0,)), ((), ())),
                                 preferred_element_type=F32)
            state_ref[h] = st * cd_ref[h] + kv
            ret_ref[rows, h * RET_DV:(h + 1) * RET_DV] = _rms(inner + cross)

    y_ret = jnp.dot((_silu(proj(3)) * (ret_ref[...] * retg_ref[...])).astype(BF16), wbr_ref[...],
                    preferred_element_type=F32)

    z = proj(5) * proj(4)
    row = lax.broadcasted_iota(jnp.int32, z.shape, 0)
    prev1 = carry_ref[7:8, :]
    prev2 = carry_ref[6:7, :]
    z1 = jnp.where(row == 0, prev1, pltpu.roll(z, 1, 0))
    z2 = jnp.where(row == 0, prev2, jnp.where(row == 1, prev1, pltpu.roll(z, 2, 0)))
    conv = convw_ref[0:1, :] * z2 + convw_ref[1:2, :] * z1 + convw_ref[2:3, :] * z
    carry_ref[...] = z[tm - 8:tm, :]
    y_conv = jnp.dot((proj(6) * conv).astype(BF16), wbc_ref[...], preferred_element_type=F32)

    mix = _sigmoid(proj(7)) * y_ret + _sigmoid(proj(8)) * y_conv
    x1 = x + g1 * jnp.dot(mix.astype(BF16), wout_ref[...], preferred_element_type=F32)

    h2 = (_rms(x1) * gffn_ref[...]) * (1.0 + sc2) + sh2
    lgt_ref[...] = lax.dot_general(wrt_ref[...], h2, (((1,), (1,)), ((), ())), precision=HIGHEST,
                                   preferred_element_type=F32)
    gu = jnp.dot(h2.astype(BF16), wshgu_ref[...], preferred_element_type=F32)
    f = gu.shape[1] // 2
    shared = jnp.dot((_silu(gu[:, :f]) * gu[:, f:]).astype(BF16), wshd_ref[...], preferred_element_type=F32)
    x2_ref[0] = x1 + g2 * shared
    for j in range(h2p_ref.shape[0]):
        h2p_ref[j] = _pack_bf16_pairs(h2[:, 2 * LANES * j:2 * LANES * (j + 1)])


def _mixer(x, posf, mod, gmix, win, retg, convw, wbr, wbc, wout, gffn, wrt, wshgu, wshd):
    b, s, d = x.shape
    tm = MIXER_TOKENS
    nt = s // tm
    intra, qd, kd, cd, invf, sign = _retention_constants()
    q_w, v_w = RET_HEADS * RET_DK, RET_HEADS * RET_DV
    weights = (gmix, win, invf, sign, intra, qd, kd, cd, retg, convw, wbr, wbc, wout, gffn, wrt, wshgu, wshd)
    return pl.pallas_call(
        _mixer_kernel,
        grid=(b, nt),
        in_specs=[pl.BlockSpec((1, tm, d), lambda i, j: (i, j, 0)),
                  pl.BlockSpec((1, tm, 1), lambda i, j: (i, j, 0)),
                  pl.BlockSpec((1, 6, d), lambda i, j: (i, 0, 0))]
                 + [_resident(w.shape) for w in weights],
        out_specs=[pl.BlockSpec((1, tm, d), lambda i, j: (i, j, 0)),
                   pl.BlockSpec((d // (2 * LANES), tm, LANES), lambda i, j: (0, i * nt + j, 0)),
                   pl.BlockSpec((N_EXPERTS, tm), lambda i, j: (0, i * nt + j))],
        out_shape=[jax.ShapeDtypeStruct((b, s, d), F32),
                   jax.ShapeDtypeStruct((d // (2 * LANES), b * s, LANES), jnp.uint32),
                   jax.ShapeDtypeStruct((N_EXPERTS, b * s), F32)],
        scratch_shapes=[pltpu.VMEM((RET_HEADS, RET_DK, RET_DV), F32),
                        pltpu.VMEM((8, d), F32),
                        pltpu.VMEM((tm, q_w), F32),
                        pltpu.VMEM((tm, q_w), F32),
                        pltpu.VMEM((tm, v_w), BF16),
                        pltpu.VMEM((tm, v_w), F32)],
        compiler_params=pltpu.CompilerParams(dimension_semantics=("arbitrary", "arbitrary"),
                                             vmem_limit_bytes=VMEM_LIMIT_BYTES),
        name="mixer",
    )(x, posf, mod, *weights)


def _first_argmax(v, idx, n):
    m = jnp.max(v, axis=0, keepdims=True)
    return m, jnp.min(jnp.where(v == m, idx, n), axis=0, keepdims=True)


def _route_kernel(lg_ref, bias_ref, tri_ref, eidx_ref, w_ref, rank_ref, cnt_ref, carry_ref):
    @pl.when(pl.program_id(0) == 0)
    def _():
        carry_ref[...] = jnp.zeros_like(carry_ref)

    tn = lg_ref.shape[1]
    neg = F32(-jnp.inf)
    score = _sigmoid(lg_ref[...])
    choice = score + bias_ref[...]

    grow = lax.broadcasted_iota(jnp.int32, (GROUP_SIZE, tn), 0)
    gscores = []
    for g in range(N_GROUPS):
        cg = choice[g * GROUP_SIZE:(g + 1) * GROUP_SIZE]
        m1, i1 = _first_argmax(cg, grow, GROUP_SIZE)
        m2 = jnp.max(jnp.where(grow == i1, neg, cg), axis=0, keepdims=True)
        gscores.append(m1 + m2)
    cur = jnp.concatenate(gscores, axis=0)
    gidx = lax.broadcasted_iota(jnp.int32, (N_GROUPS, tn), 0)
    keep = jnp.zeros((N_GROUPS, tn), F32)
    for _ in range(TOPK_GROUPS):
        _, ig = _first_argmax(cur, gidx, N_GROUPS)
        hit = gidx == ig
        keep = jnp.where(hit, 1.0, keep)
        cur = jnp.where(hit, neg, cur)
    cur = jnp.concatenate(
        [jnp.where(keep[g:g + 1] > 0.0, choice[g * GROUP_SIZE:(g + 1) * GROUP_SIZE], neg)
         for g in range(N_GROUPS)], axis=0)

    erow = lax.broadcasted_iota(jnp.int32, (N_EXPERTS, tn), 0)
    eidx, wts = [], []
    member = jnp.zeros((N_EXPERTS, tn), F32)
    for _ in range(TOP_K):
        _, ie = _first_argmax(cur, erow, N_EXPERTS)
        hit = erow == ie
        eidx.append(ie)
        wts.append(jnp.sum(jnp.where(hit, score, 0.0), axis=0, keepdims=True))
        member = member + hit.astype(F32)
        cur = jnp.where(hit, neg, cur)
    wsum = wts[0]
    for k in range(1, TOP_K):
        wsum = wsum + wts[k]

    before = jnp.dot(member.astype(BF16), tri_ref[...], preferred_element_type=F32) + carry_ref[...]
    ranks = [jnp.sum(jnp.where(erow == eidx[k], before, 0.0), axis=0, keepdims=True) for k in range(TOP_K)]
    carry_ref[...] = carry_ref[...] + jnp.sum(member, axis=1, keepdims=True)

    eidx_ref[...] = jnp.concatenate(eidx, axis=0)
    w_ref[...] = jnp.concatenate([w / wsum * ROUTED_SCALE for w in wts], axis=0)
    rank_ref[...] = jnp.concatenate(ranks, axis=0).astype(jnp.int32)
    cnt_ref[...] = carry_ref[...]


def _route(lgt, bias):
    e, t = lgt.shape
    tn = ROUTE_TOKENS
    tri = jnp.asarray(np.triu(np.ones((tn, tn), np.float32), 1), BF16)
    return pl.pallas_call(
        _route_kernel,
        grid=(t // tn,),
        in_specs=[pl.BlockSpec((e, tn), lambda i: (0, i)),
                  pl.BlockSpec((e, 1), lambda i: (0, 0)),
                  pl.BlockSpec((tn, tn), lambda i: (0, 0))],
        out_specs=[pl.BlockSpec((TOP_K, tn), lambda i: (0, i)),
                   pl.BlockSpec((TOP_K, tn), lambda i: (0, i)),
                   pl.BlockSpec((TOP_K, tn), lambda i: (0, i)),
                   pl.BlockSpec((e, 1), lambda i: (0, 0))],
        out_shape=[jax.ShapeDtypeStruct((TOP_K, t), jnp.int32),
                   jax.ShapeDtypeStruct((TOP_K, t), F32),
                   jax.ShapeDtypeStruct((TOP_K, t), jnp.int32),
                   jax.ShapeDtypeStruct((e, 1), F32)],
        scratch_shapes=[pltpu.VMEM((e, 1), F32)],
        compiler_params=pltpu.CompilerParams(dimension_semantics=("arbitrary",)),
        name="route",
    )(lgt, bias, tri)


def _dest_kernel(eidx_ref, rank_ref, pstart_ref, dest_ref):
    tn = eidx_ref.shape[1]
    erow = lax.broadcasted_iota(jnp.int32, (N_EXPERTS, tn), 0)
    start = pstart_ref[...]
    dest_ref[...] = rank_ref[...] + jnp.concatenate(
        [jnp.sum(jnp.where(erow == eidx_ref[k:k + 1, :], start, 0), axis=0, keepdims=True) for k in range(TOP_K)],
        axis=0)


def _dest(eidx, rank, pstart):
    k, t = eidx.shape
    tn = ROUTE_TOKENS
    return pl.pallas_call(
        _dest_kernel,
        grid=(t // tn,),
        in_specs=[pl.BlockSpec((k, tn), lambda i: (0, i)),
                  pl.BlockSpec((k, tn), lambda i: (0, i)),
                  pl.BlockSpec((N_EXPERTS, 1), lambda i: (0, 0))],
        out_specs=pl.BlockSpec((k, tn), lambda i: (0, i)),
        out_shape=jax.ShapeDtypeStruct((k, t), jnp.int32),
        name="dest",
    )(eidx, rank, pstart)


def _sc_mesh_and_workers():
    mesh = plsc.VectorSubcoreMesh(core_axis_name="c", subcore_axis_name="s")
    return mesh, mesh.num_cores, mesh.num_cores * mesh.num_subcores


def _worker_id(num_cores):
    return lax.axis_index("s") * num_cores + lax.axis_index("c")


def _dispatch(h2p, dest3, n_rows):
    planes, t, lanes = h2p.shape
    n_chunks = dest3.shape[0]
    mesh, num_cores, workers = _sc_mesh_and_workers()
    chunks_per_worker = n_chunks // workers
    assert chunks_per_worker * workers == n_chunks and n_chunks * lanes == t

    def body(h_hbm, d_hbm, xs_hbm, idx_v, rows_v, load_sem, scatter_sem):
        wid = _worker_id(num_cores)

        @pl.loop(0, chunks_per_worker)
        def _(c):
            chunk = wid * chunks_per_worker + c
            tok = pl.ds(chunk * lanes, lanes)
            pltpu.sync_copy(d_hbm.at[chunk], idx_v)
            loads = [None] * planes
            scatters = [None] * planes
            loads[0] = pltpu.async_copy(h_hbm.at[0, tok], rows_v.at[0], load_sem.at[0])
            for j in range(planes):
                loads[j].wait()
                if j + 1 < planes:
                    if j >= 1:
                        for cp in scatters[j - 1]:
                            cp.wait()
                    loads[j + 1] = pltpu.async_copy(h_hbm.at[j + 1, tok], rows_v.at[(j + 1) % 2],
                                                    load_sem.at[(j + 1) % 2])
                scatters[j] = [pltpu.async_copy(rows_v.at[j % 2], xs_hbm.at[j].at[idx_v.at[k]],
                                                scatter_sem.at[j % 2]) for k in range(TOP_K)]
            for j in range(max(planes - 2, 0), planes):
                for cp in scatters[j]:
                    cp.wait()

    return pl.kernel(
        body,
        out_type=jax.ShapeDtypeStruct((planes, n_rows, lanes), h2p.dtype),
        mesh=mesh,
        scratch_types=[pltpu.VMEM((TOP_K, lanes), jnp.int32),
                       pltpu.VMEM((2, lanes, lanes), h2p.dtype),
                       pltpu.SemaphoreType.DMA((2,)),
                       pltpu.SemaphoreType.DMA((2,))],
        name="dispatch",
    )(h2p, dest3)


def _gather(ybuf, dest3, t):
    planes, _, lanes = ybuf.shape
    n_chunks = dest3.shape[0]
    mesh, num_cores, workers = _sc_mesh_and_workers()
    chunks_per_worker = n_chunks // workers
    nbuf = GATHER_BUFFERS
    lag = nbuf // 2
    assert chunks_per_worker * workers == n_chunks and n_chunks * lanes == t

    def body(y_hbm, d_hbm, yg_hbm, idx_v, rows_v, gather_sem, store_sem):
        wid = _worker_id(num_cores)

        @pl.loop(0, chunks_per_worker)
        def _(c):
            chunk = wid * chunks_per_worker + c
            tok = pl.ds(chunk * lanes, lanes)
            pltpu.sync_copy(d_hbm.at[chunk], idx_v)

            @pl.loop(0, TOP_K)
            def _(k):
                gathers = [None] * planes
                stores = [None] * planes

                def store(j):
                    gathers[j].wait()
                    stores[j] = pltpu.async_copy(rows_v.at[j % nbuf], yg_hbm.at[k, j, tok], store_sem.at[j % nbuf])

                for j in range(planes):
                    if j >= nbuf:
                        stores[j - nbuf].wait()
                    gathers[j] = pltpu.async_copy(y_hbm.at[j].at[idx_v.at[k]], rows_v.at[j % nbuf],
                                                  gather_sem.at[j % nbuf])
                    if j >= lag:
                        store(j - lag)
                for j in range(max(planes - lag, 0), planes):
                    store(j)
                for j in range(max(planes - nbuf, 0), planes):
                    stores[j].wait()

    return pl.kernel(
        body,
        out_type=jax.ShapeDtypeStruct((TOP_K, planes, t, lanes), ybuf.dtype),
        mesh=mesh,
        scratch_types=[pltpu.VMEM((TOP_K, lanes), jnp.int32),
                       pltpu.VMEM((nbuf, lanes, lanes), ybuf.dtype),
                       pltpu.SemaphoreType.DMA((nbuf,)),
                       pltpu.SemaphoreType.DMA((nbuf,))],
        name="gather",
    )(ybuf, dest3)


def _experts_kernel(be_ref, nv_ref, nu_ref, nxt_ref, x_ref, wg_hbm, wu_hbm, wd_hbm, o_ref,
                    wgs_ref, wus_ref, wds_ref, wgb_ref, wub_ref, wdb_ref, sems):
    i = pl.program_id(0)
    active = i < nu_ref[0]
    e = be_ref[i]
    new_expert = jnp.logical_or(i == 0, e != be_ref[jnp.maximum(i - 1, 0)])

    def weight_copies(expert):
        return [pltpu.make_async_copy(src.at[expert], dst, sems.at[n])
                for n, (src, dst) in enumerate(((wg_hbm, wgs_ref), (wu_hbm, wus_ref), (wd_hbm, wds_ref)))]

    @pl.when(jnp.logical_and(active, new_expert))
    def _():
        @pl.when(i == 0)
        def _():
            for cp in weight_copies(e):
                cp.start()

        for cp in weight_copies(e):
            cp.wait()
        wgb_ref[...] = wgs_ref[...].astype(BF16)
        wub_ref[...] = wus_ref[...].astype(BF16)
        wdb_ref[...] = wds_ref[...].astype(BF16)
        nxt = nxt_ref[e]

        @pl.when(nxt != e)
        def _():
            for cp in weight_copies(nxt):
                cp.start()

    @pl.when(active)
    def _():
        planes, bm, lanes = x_ref.shape
        valid = lax.broadcasted_iota(jnp.int32, (bm, lanes), 0) < nv_ref[i]
        g = None
        u = None
        for j in range(planes):
            xj = _unpack_pairs_f32(jnp.where(valid, x_ref[j], jnp.uint32(0))).astype(BF16)
            rows = pl.ds(2 * lanes * j, 2 * lanes)
            gj = jnp.dot(xj, wgb_ref[rows, :], preferred_element_type=F32)
            uj = jnp.dot(xj, wub_ref[rows, :], preferred_element_type=F32)
            g = gj if g is None else g + gj
            u = uj if u is None else u + uj
        out = jnp.dot((_silu(g) * u).astype(BF16), wdb_ref[...], preferred_element_type=F32)
        for j in range(o_ref.shape[0]):
            o_ref[j] = _pack_bf16_pairs(out[:, 2 * lanes * j:2 * lanes * (j + 1)])


def _experts(block_e, n_valid, n_used, next_expert, xs, wg, wu, wd):
    planes, p, lanes = xs.shape
    e, d, f = wg.shape
    bm = EXPERT_ROWS
    nb = p // bm

    def row_map(i, be, nv, nu, nxt):
        return (0, jnp.minimum(i, nu[0] - 1), 0)

    return pl.pallas_call(
        _experts_kernel,
        grid_spec=pltpu.PrefetchScalarGridSpec(
            num_scalar_prefetch=4,
            grid=(nb,),
            in_specs=[pl.BlockSpec((planes, bm, lanes), row_map),
                      pl.BlockSpec(memory_space=pl.ANY),
                      pl.BlockSpec(memory_space=pl.ANY),
                      pl.BlockSpec(memory_space=pl.ANY)],
            out_specs=pl.BlockSpec((planes, bm, lanes), row_map),
            scratch_shapes=[pltpu.VMEM((d, f), F32), pltpu.VMEM((d, f), F32), pltpu.VMEM((f, d), F32),
                            pltpu.VMEM((d, f), BF16), pltpu.VMEM((d, f), BF16), pltpu.VMEM((f, d), BF16),
                            pltpu.SemaphoreType.DMA((3,))]),
        out_shape=jax.ShapeDtypeStruct((planes, p, lanes), jnp.uint32),
        compiler_params=pltpu.CompilerParams(dimension_semantics=("arbitrary",),
                                             vmem_limit_bytes=VMEM_LIMIT_BYTES),
        name="experts",
    )(block_e, n_valid, n_used, next_expert, xs, wg, wu, wd)


def _combine_kernel(w_ref, x2_ref, g2_ref, gfin_ref, yg_ref, o_ref):
    w = w_ref[...]
    planes = yg_ref.shape[1]
    routed = None
    for k in range(TOP_K):
        rows = jnp.concatenate([_unpack_pairs_f32(yg_ref[k, j]) for j in range(planes)], axis=1) * w[:, k:k + 1]
        routed = rows if routed is None else routed + rows
    o_ref[0] = _rms(x2_ref[0] + g2_ref[0] * routed) * gfin_ref[...]


def _combine(wtok, x2, g2, gfin, yg):
    b, s, d = x2.shape
    _, planes, _, lanes = yg.shape
    tc = COMBINE_TOKENS
    nt = s // tc
    return pl.pallas_call(
        _combine_kernel,
        grid=(b, nt),
        in_specs=[pl.BlockSpec((tc, TOP_K), lambda i, j: (i * nt + j, 0)),
                  pl.BlockSpec((1, tc, d), lambda i, j: (i, j, 0)),
                  pl.BlockSpec((1, 1, d), lambda i, j: (i, 0, 0)),
                  pl.BlockSpec((1, d), lambda i, j: (0, 0)),
                  pl.BlockSpec((TOP_K, planes, tc, lanes), lambda i, j: (0, 0, i * nt + j, 0))],
        out_specs=pl.BlockSpec((1, tc, d), lambda i, j: (i, j, 0)),
        out_shape=jax.ShapeDtypeStruct((b, s, d), F32),
        compiler_params=pltpu.CompilerParams(dimension_semantics=("arbitrary", "arbitrary"),
                                             vmem_limit_bytes=VMEM_LIMIT_BYTES),
        name="combine",
    )(wtok, x2, g2, gfin, yg)


def kernel(x, c, positions, w_ada, b_ada, norm_mix_g, w_in, ret_norm_g, conv_w, w_br_ret, w_br_conv, w_out,
           norm_ffn_g, w_router, router_bias, w_exp_gate, w_exp_up, w_exp_down, w_sh_gate, w_sh_up, w_sh_down,
           norm_final_g):
    b, s, d = x.shape
    t = b * s
    depth = w_in.shape[0]
    assert depth == 1, "the combine kernel applies the final norm, so exactly one layer is supported"
    posf = positions.astype(F32)[:, :, None]
    c8 = jnp.zeros((8, d), F32).at[:b].set(c)
    bm = EXPERT_ROWS
    n_blocks = t * TOP_K // bm + N_EXPERTS
    n_rows = n_blocks * bm

    for l in range(depth):
        mod = _ada(c8, w_ada[l], b_ada[l][None, :])[:b].reshape(b, 6, d)
        x2, h2p, lgt = _mixer(
            x, posf, mod, norm_mix_g[l][None, :], w_in[l].astype(BF16), ret_norm_g[l][None, :], conv_w[l],
            w_br_ret[l].astype(BF16), w_br_conv[l].astype(BF16), w_out[l].astype(BF16), norm_ffn_g[l][None, :],
            w_router[l].T, jnp.concatenate([w_sh_gate[l], w_sh_up[l]], axis=1).astype(BF16),
            w_sh_down[l].astype(BF16))
        eidx, wts, rank, counts = _route(lgt, router_bias[l][:, None])

        cnt = counts[:, 0].astype(jnp.int32)
        pcnt = (cnt + bm - 1) // bm * bm
        pend = jnp.cumsum(pcnt)
        pstart = pend - pcnt
        n_used = jnp.maximum(pend[-1] // bm, 1).astype(jnp.int32)[None]
        block_start = jnp.arange(n_blocks, dtype=jnp.int32) * bm
        block_e = jnp.minimum(jnp.sum((pend[None, :] <= block_start[:, None]).astype(jnp.int32), axis=1),
                              N_EXPERTS - 1)
        n_valid = jnp.clip(pstart[block_e] + cnt[block_e] - block_start, 0, bm).astype(jnp.int32)
        eid = jnp.arange(N_EXPERTS, dtype=jnp.int32)
        later_used = jnp.logical_and(eid[None, :] > eid[:, None], cnt[None, :] > 0)
        next_expert = jnp.min(jnp.where(later_used, eid[None, :], N_EXPERTS), axis=1)
        next_expert = jnp.where(next_expert == N_EXPERTS, eid, next_expert)

        dest = _dest(eidx, rank, pstart[:, None])
        dest3 = dest.reshape(TOP_K, t // LANES, LANES).transpose(1, 0, 2)
        xs = _dispatch(h2p, dest3, n_rows)
        ybuf = _experts(block_e, n_valid, n_used, next_expert, xs, w_exp_gate[l], w_exp_up[l], w_exp_down[l])
        yg = _gather(ybuf, dest3, t)
        x = _combine(wts.T, x2, mod[:, 5:6, :], norm_final_g[None, :], yg)
    return x
```

```python
import numpy as np
import jax
import jax.numpy as jnp
from jax import lax
from jax.experimental import pallas as pl
from jax.experimental.pallas import tpu as pltpu
from jax.experimental.pallas import tpu_sc as plsc

RET_HEADS = 4
RET_DK = 128
RET_DV = 256
RET_CHUNK = 128
ROPE_THETA = 10000.0
CONV_K = 3
N_EXPERTS = 256
TOP_K = 8
N_GROUPS = 8
TOPK_GROUPS = 4
GROUP_SIZE = N_EXPERTS // N_GROUPS
ROUTED_SCALE = 2.5
NORM_EPS = 1e-6

MIXER_TOKENS = 512
ROUTE_TOKENS = 256
EXPERT_ROWS = 256
EXPERT_BLOCKS_PER_STEP = 4
COMBINE_TOKENS = 256
LANES = 128
GATHER_BUFFERS = 4

VMEM_LIMIT_BYTES = 56 * 1024 * 1024

F32 = jnp.float32
BF16 = jnp.bfloat16
HIGHEST = lax.Precision.HIGHEST


def _sigmoid(v):
    return 1.0 / (1.0 + jnp.exp(-v))


def _silu(v):
    return v * _sigmoid(v)


def _rms(v):
    return v * lax.rsqrt(jnp.mean(v * v, axis=-1, keepdims=True) + NORM_EPS)


def _resident(shape):
    nd = len(shape)
    return pl.BlockSpec(shape, lambda *_: (0,) * nd, pipeline_mode=pl.Buffered(1))


def _ada_kernel(c_ref, w_ref, b_ref, o_ref):
    c = c_ref[...]
    o_ref[...] = jnp.dot(_silu(c), w_ref[...], precision=HIGHEST, preferred_element_type=F32) + b_ref[...]


def _ada(c8, w, b):
    d, n = w.shape
    tn = 1024
    return pl.pallas_call(
        _ada_kernel,
        grid=(n // tn,),
        in_specs=[pl.BlockSpec((8, d), lambda j: (0, 0)),
                  pl.BlockSpec((d, tn), lambda j: (0, j)),
                  pl.BlockSpec((1, tn), lambda j: (0, j))],
        out_specs=pl.BlockSpec((8, tn), lambda j: (0, j)),
        out_shape=jax.ShapeDtypeStruct((8, n), F32),
        name="ada",
    )(c8, w, b)


def _retention_constants():
    c = RET_CHUNK
    log_g = jnp.log1p(-(2.0 ** (-5.0 - jnp.arange(RET_HEADS, dtype=F32))))
    idx = jnp.arange(c, dtype=F32)
    diff = idx[:, None] - idx[None, :]
    intra = jnp.where(diff >= 0, jnp.exp(log_g[:, None, None] * jnp.maximum(diff, 0.0)), 0.0)
    k_decay = jnp.exp(log_g[:, None] * (c - 1 - idx))
    q_decay = jnp.exp(log_g[:, None] * (idx + 1.0))
    chunk_decay = jnp.exp(log_g * c)
    kd = jnp.broadcast_to(k_decay[:, :, None], (RET_HEADS, c, RET_DK))
    qd = jnp.broadcast_to(q_decay[:, :, None], (RET_HEADS, c, RET_DK))
    cd = jnp.broadcast_to(chunk_decay[:, None, None], (RET_HEADS, 1, RET_DV))
    inv_freq = ROPE_THETA ** (-jnp.arange(0, RET_DK, 2, dtype=F32) / RET_DK)
    inv_freq = jnp.concatenate([inv_freq, inv_freq])[None, :]
    sign = jnp.concatenate([-jnp.ones((RET_DK // 2,), F32), jnp.ones((RET_DK // 2,), F32)])[None, :]
    return intra, qd, kd, cd, inv_freq, sign


def _pack_bf16_pairs(v):
    w = v.shape[1] // 2
    lo = lax.bitcast_convert_type(v[:, :w].astype(BF16).astype(F32), jnp.uint32)
    hi = lax.bitcast_convert_type(v[:, w:].astype(BF16).astype(F32), jnp.uint32)
    return (lo >> 16) | (hi & jnp.uint32(0xFFFF0000))


def _unpack_pairs_f32(word):
    lo = lax.bitcast_convert_type(word << 16, F32)
    hi = lax.bitcast_convert_type(word & jnp.uint32(0xFFFF0000), F32)
    return jnp.concatenate([lo, hi], axis=1)


def _mixer_kernel(x_ref, pos_ref, mod_ref, gmix_ref, win_ref, invf_ref, sign_ref, intra_ref, qd_ref, kd_ref, cd_ref,
                  retg_ref, convw_ref, wbr_ref, wbc_ref, wout_ref, gffn_ref, wrt_ref, wshgu_ref, wshd_ref,
                  x2_ref, h2p_ref, lgt_ref,
                  state_ref, carry_ref, q_ref, k_ref, v_ref, ret_ref):
    tm, d = x_ref.shape[1], x_ref.shape[2]
    q_w = RET_HEADS * RET_DK
    v_w = RET_HEADS * RET_DV
    offs = np.cumsum([0, q_w, q_w, v_w, v_w, d, d, d, d, d])

    @pl.when(pl.program_id(1) == 0)
    def _():
        state_ref[...] = jnp.zeros_like(state_ref)
        carry_ref[...] = jnp.zeros_like(carry_ref)

    x = x_ref[0]
    mod = mod_ref[0]
    sh1, sc1, g1, sh2, sc2, g2 = [mod[i:i + 1] for i in range(6)]
    hb = ((_rms(x) * gmix_ref[...]) * (1.0 + sc1) + sh1).astype(BF16)

    def proj(i):
        return jnp.dot(hb, win_ref[:, offs[i]:offs[i + 1]], preferred_element_type=F32)

    ang = pos_ref[0] * invf_ref[...]
    cosv = jnp.cos(ang)
    sinv = jnp.sin(ang) * sign_ref[...]

    def rope(t):
        return jnp.concatenate(
            [t[:, h * RET_DK:(h + 1) * RET_DK] * cosv
             + pltpu.roll(t[:, h * RET_DK:(h + 1) * RET_DK], RET_DK // 2, 1) * sinv
             for h in range(RET_HEADS)], axis=1)

    q_ref[...] = rope(proj(0)) * (RET_DK ** -0.5)
    k_ref[...] = rope(proj(1))
    v_ref[...] = proj(2).astype(BF16)

    for c in range(tm // RET_CHUNK):
        rows = pl.ds(c * RET_CHUNK, RET_CHUNK)
        for h in range(RET_HEADS):
            qh = q_ref[rows, h * RET_DK:(h + 1) * RET_DK]
            kh = k_ref[rows, h * RET_DK:(h + 1) * RET_DK]
            vh = v_ref[rows, h * RET_DV:(h + 1) * RET_DV]
            scores = lax.dot_general(qh.astype(BF16), kh.astype(BF16), (((1,), (1,)), ((), ())),
                                     preferred_element_type=F32) * intra_ref[h]
            inner = jnp.dot(scores.astype(BF16), vh, preferred_element_type=F32)
            st = state_ref[h]
            cross = jnp.dot((qh * qd_ref[h]).astype(BF16), st.astype(BF16), preferred_element_type=F32)
            kv = lax.dot_general((kh * kd_ref[h]).astype(BF16), vh, (((0,), (0,)), ((), ())),
                                 preferred_element_type=F32)
            state_ref[h] = st * cd_ref[h] + kv
            ret_ref[rows, h * RET_DV:(h + 1) * RET_DV] = _rms(inner + cross)

    y_ret = jnp.dot((_silu(proj(3)) * (ret_ref[...] * retg_ref[...])).astype(BF16), wbr_ref[...],
                    preferred_element_type=F32)

    z = proj(5) * proj(4)
    row = lax.broadcasted_iota(jnp.int32, z.shape, 0)
    prev1 = carry_ref[7:8, :]
    prev2 = carry_ref[6:7, :]
    z1 = jnp.where(row == 0, prev1, pltpu.roll(z, 1, 0))
    z2 = jnp.where(row == 0, prev2, jnp.where(row == 1, prev1, pltpu.roll(z, 2, 0)))
    conv = convw_ref[0:1, :] * z2 + convw_ref[1:2, :] * z1 + convw_ref[2:3, :] * z
    carry_ref[...] = z[tm - 8:tm, :]
    y_conv = jnp.dot((proj(6) * conv).astype(BF16), wbc_ref[...], preferred_element_type=F32)

    mix = _sigmoid(proj(7)) * y_ret + _sigmoid(proj(8)) * y_conv
    x1 = x + g1 * jnp.dot(mix.astype(BF16), wout_ref[...], preferred_element_type=F32)

    h2 = (_rms(x1) * gffn_ref[...]) * (1.0 + sc2) + sh2
    lgt_ref[...] = lax.dot_general(wrt_ref[...], h2, (((1,), (1,)), ((), ())), precision=HIGHEST,
                                   preferred_element_type=F32)
    gu = jnp.dot(h2.astype(BF16), wshgu_ref[...], preferred_element_type=F32)
    f = gu.shape[1] // 2
    shared = jnp.dot((_silu(gu[:, :f]) * gu[:, f:]).astype(BF16), wshd_ref[...], preferred_element_type=F32)
    x2_ref[0] = x1 + g2 * shared
    for j in range(h2p_ref.shape[0]):
        h2p_ref[j] = _pack_bf16_pairs(h2[:, 2 * LANES * j:2 * LANES * (j + 1)])


def _mixer(x, posf, mod, gmix, win, retg, convw, wbr, wbc, wout, gffn, wrt, wshgu, wshd):
    b, s, d = x.shape
    tm = MIXER_TOKENS
    nt = s // tm
    intra, qd, kd, cd, invf, sign = _retention_constants()
    q_w, v_w = RET_HEADS * RET_DK, RET_HEADS * RET_DV
    weights = (gmix, win, invf, sign, intra, qd, kd, cd, retg, convw, wbr, wbc, wout, gffn, wrt, wshgu, wshd)
    return pl.pallas_call(
        _mixer_kernel,
        grid=(b, nt),
        in_specs=[pl.BlockSpec((1, tm, d), lambda i, j: (i, j, 0)),
                  pl.BlockSpec((1, tm, 1), lambda i, j: (i, j, 0)),
                  pl.BlockSpec((1, 6, d), lambda i, j: (i, 0, 0))]
                 + [_resident(w.shape) for w in weights],
        out_specs=[pl.BlockSpec((1, tm, d), lambda i, j: (i, j, 0)),
                   pl.BlockSpec((d // (2 * LANES), tm, LANES), lambda i, j: (0, i * nt + j, 0)),
                   pl.BlockSpec((N_EXPERTS, tm), lambda i, j: (0, i * nt + j))],
        out_shape=[jax.ShapeDtypeStruct((b, s, d), F32),
                   jax.ShapeDtypeStruct((d // (2 * LANES), b * s, LANES), jnp.uint32),
                   jax.ShapeDtypeStruct((N_EXPERTS, b * s), F32)],
        scratch_shapes=[pltpu.VMEM((RET_HEADS, RET_DK, RET_DV), F32),
                        pltpu.VMEM((8, d), F32),
                        pltpu.VMEM((tm, q_w), F32),
                        pltpu.VMEM((tm, q_w), F32),
                        pltpu.VMEM((tm, v_w), BF16),
                        pltpu.VMEM((tm, v_w), F32)],
        compiler_params=pltpu.CompilerParams(dimension_semantics=("arbitrary", "arbitrary"),
                                             vmem_limit_bytes=VMEM_LIMIT_BYTES),
        name="mixer",
    )(x, posf, mod, *weights)


def _first_argmax(v, idx, n):
    m = jnp.max(v, axis=0, keepdims=True)
    return m, jnp.min(jnp.where(v == m, idx, n), axis=0, keepdims=True)


def _route_kernel(lg_ref, bias_ref, tri_ref, eidx_ref, w_ref, rank_ref, cnt_ref, carry_ref):
    @pl.when(pl.program_id(0) == 0)
    def _():
        carry_ref[...] = jnp.zeros_like(carry_ref)

    tn = lg_ref.shape[1]
    neg = F32(-jnp.inf)
    score = _sigmoid(lg_ref[...])
    choice = score + bias_ref[...]

    grow = lax.broadcasted_iota(jnp.int32, (GROUP_SIZE, tn), 0)
    gscores = []
    for g in range(N_GROUPS):
        cg = choice[g * GROUP_SIZE:(g + 1) * GROUP_SIZE]
        m1, i1 = _first_argmax(cg, grow, GROUP_SIZE)
        m2 = jnp.max(jnp.where(grow == i1, neg, cg), axis=0, keepdims=True)
        gscores.append(m1 + m2)
    cur = jnp.concatenate(gscores, axis=0)
    gidx = lax.broadcasted_iota(jnp.int32, (N_GROUPS, tn), 0)
    keep = jnp.zeros((N_GROUPS, tn), F32)
    for _ in range(TOPK_GROUPS):
        _, ig = _first_argmax(cur, gidx, N_GROUPS)
        hit = gidx == ig
        keep = jnp.where(hit, 1.0, keep)
        cur = jnp.where(hit, neg, cur)
    cur = jnp.concatenate(
        [jnp.where(keep[g:g + 1] > 0.0, choice[g * GROUP_SIZE:(g + 1) * GROUP_SIZE], neg)
         for g in range(N_GROUPS)], axis=0)

    erow = lax.broadcasted_iota(jnp.int32, (N_EXPERTS, tn), 0)
    eidx, wts = [], []
    member = jnp.zeros((N_EXPERTS, tn), F32)
    for _ in range(TOP_K):
        _, ie = _first_argmax(cur, erow, N_EXPERTS)
        hit = erow == ie
        eidx.append(ie)
        wts.append(jnp.sum(jnp.where(hit, score, 0.0), axis=0, keepdims=True))
        member = member + hit.astype(F32)
        cur = jnp.where(hit, neg, cur)
    wsum = wts[0]
    for k in range(1, TOP_K):
        wsum = wsum + wts[k]

    before = jnp.dot(member.astype(BF16), tri_ref[...], preferred_element_type=F32) + carry_ref[...]
    ranks = [jnp.sum(jnp.where(erow == eidx[k], before, 0.0), axis=0, keepdims=True) for k in range(TOP_K)]
    carry_ref[...] = carry_ref[...] + jnp.sum(member, axis=1, keepdims=True)

    eidx_ref[...] = jnp.concatenate(eidx, axis=0)
    w_ref[...] = jnp.concatenate([w / wsum * ROUTED_SCALE for w in wts], axis=0)
    rank_ref[...] = jnp.concatenate(ranks, axis=0).astype(jnp.int32)
    cnt_ref[...] = carry_ref[...]


def _route(lgt, bias):
    e, t = lgt.shape
    tn = ROUTE_TOKENS
    tri = jnp.asarray(np.triu(np.ones((tn, tn), np.float32), 1), BF16)
    return pl.pallas_call(
        _route_kernel,
        grid=(t // tn,),
        in_specs=[pl.BlockSpec((e, tn), lambda i: (0, i)),
                  pl.BlockSpec((e, 1), lambda i: (0, 0)),
                  pl.BlockSpec((tn, tn), lambda i: (0, 0))],
        out_specs=[pl.BlockSpec((TOP_K, tn), lambda i: (0, i)),
                   pl.BlockSpec((TOP_K, tn), lambda i: (0, i)),
                   pl.BlockSpec((TOP_K, tn), lambda i: (0, i)),
                   pl.BlockSpec((e, 1), lambda i: (0, 0))],
        out_shape=[jax.ShapeDtypeStruct((TOP_K, t), jnp.int32),
                   jax.ShapeDtypeStruct((TOP_K, t), F32),
                   jax.ShapeDtypeStruct((TOP_K, t), jnp.int32),
                   jax.ShapeDtypeStruct((e, 1), F32)],
        scratch_shapes=[pltpu.VMEM((e, 1), F32)],
        compiler_params=pltpu.CompilerParams(dimension_semantics=("arbitrary",)),
        name="route",
    )(lgt, bias, tri)


def _dest_kernel(eidx_ref, rank_ref, pstart_ref, dest_ref):
    tn = eidx_ref.shape[1]
    erow = lax.broadcasted_iota(jnp.int32, (N_EXPERTS, tn), 0)
    start = pstart_ref[...]
    dest_ref[...] = rank_ref[...] + jnp.concatenate(
        [jnp.sum(jnp.where(erow == eidx_ref[k:k + 1, :], start, 0), axis=0, keepdims=True) for k in range(TOP_K)],
        axis=0)


def _dest(eidx, rank, pstart):
    k, t = eidx.shape
    tn = ROUTE_TOKENS
    return pl.pallas_call(
        _dest_kernel,
        grid=(t // tn,),
        in_specs=[pl.BlockSpec((k, tn), lambda i: (0, i)),
                  pl.BlockSpec((k, tn), lambda i: (0, i)),
                  pl.BlockSpec((N_EXPERTS, 1), lambda i: (0, 0))],
        out_specs=pl.BlockSpec((k, tn), lambda i: (0, i)),
        out_shape=jax.ShapeDtypeStruct((k, t), jnp.int32),
        name="dest",
    )(eidx, rank, pstart)


def _sc_mesh_and_workers():
    mesh = plsc.VectorSubcoreMesh(core_axis_name="c", subcore_axis_name="s")
    return mesh, mesh.num_cores, mesh.num_cores * mesh.num_subcores


def _worker_id(num_cores):
    return lax.axis_index("s") * num_cores + lax.axis_index("c")


def _dispatch(h2p, dest3, n_rows):
    planes, t, lanes = h2p.shape
    n_chunks = dest3.shape[0]
    mesh, num_cores, workers = _sc_mesh_and_workers()
    chunks_per_worker = n_chunks // workers
    assert chunks_per_worker * workers == n_chunks and n_chunks * lanes == t

    def body(h_hbm, d_hbm, xs_hbm, idx_v, rows_v, load_sem, scatter_sem):
        wid = _worker_id(num_cores)

        @pl.loop(0, chunks_per_worker)
        def _(c):
            chunk = wid * chunks_per_worker + c
            tok = pl.ds(chunk * lanes, lanes)
            pltpu.sync_copy(d_hbm.at[chunk], idx_v)
            loads = [None] * planes
            scatters = [None] * planes
            loads[0] = pltpu.async_copy(h_hbm.at[0, tok], rows_v.at[0], load_sem.at[0])
            for j in range(planes):
                loads[j].wait()
                if j + 1 < planes:
                    if j >= 1:
                        for cp in scatters[j - 1]:
                            cp.wait()
                    loads[j + 1] = pltpu.async_copy(h_hbm.at[j + 1, tok], rows_v.at[(j + 1) % 2],
                                                    load_sem.at[(j + 1) % 2])
                scatters[j] = [pltpu.async_copy(rows_v.at[j % 2], xs_hbm.at[j].at[idx_v.at[k]],
                                                scatter_sem.at[j % 2]) for k in range(TOP_K)]
            for j in range(max(planes - 2, 0), planes):
                for cp in scatters[j]:
                    cp.wait()

    return pl.kernel(
        body,
        out_type=jax.ShapeDtypeStruct((planes, n_rows, lanes), h2p.dtype),
        mesh=mesh,
        scratch_types=[pltpu.VMEM((TOP_K, lanes), jnp.int32),
                       pltpu.VMEM((2, lanes, lanes), h2p.dtype),
                       pltpu.SemaphoreType.DMA((2,)),
                       pltpu.SemaphoreType.DMA((2,))],
        name="dispatch",
    )(h2p, dest3)


def _gather(ybuf, dest3, t):
    planes, _, lanes = ybuf.shape
    n_chunks = dest3.shape[0]
    mesh, num_cores, workers = _sc_mesh_and_workers()
    chunks_per_worker = n_chunks // workers
    nbuf = GATHER_BUFFERS
    lag = nbuf // 2
    assert chunks_per_worker * workers == n_chunks and n_chunks * lanes == t

    def body(y_hbm, d_hbm, yg_hbm, idx_v, rows_v, gather_sem, store_sem):
        wid = _worker_id(num_cores)

        @pl.loop(0, chunks_per_worker)
        def _(c):
            chunk = wid * chunks_per_worker + c
            tok = pl.ds(chunk * lanes, lanes)
            pltpu.sync_copy(d_hbm.at[chunk], idx_v)

            @pl.loop(0, TOP_K)
            def _(k):
                gathers = [None] * planes
                stores = [None] * planes

                def store(j):
                    gathers[j].wait()
                    stores[j] = pltpu.async_copy(rows_v.at[j % nbuf], yg_hbm.at[k, j, tok], store_sem.at[j % nbuf])

                for j in range(planes):
                    if j >= nbuf:
                        stores[j - nbuf].wait()
                    gathers[j] = pltpu.async_copy(y_hbm.at[j].at[idx_v.at[k]], rows_v.at[j % nbuf],
                                                  gather_sem.at[j % nbuf])
                    if j >= lag:
                        store(j - lag)
                for j in range(max(planes - lag, 0), planes):
                    store(j)
                for j in range(max(planes - nbuf, 0), planes):
                    stores[j].wait()

    return pl.kernel(
        body,
        out_type=jax.ShapeDtypeStruct((TOP_K, planes, t, lanes), ybuf.dtype),
        mesh=mesh,
        scratch_types=[pltpu.VMEM((TOP_K, lanes), jnp.int32),
                       pltpu.VMEM((nbuf, lanes, lanes), ybuf.dtype),
                       pltpu.SemaphoreType.DMA((nbuf,)),
                       pltpu.SemaphoreType.DMA((nbuf,))],
        name="gather",
    )(ybuf, dest3)


def _experts_kernel(be_ref, nv_ref, nu_ref, nxt_ref, x_ref, wg_hbm, wu_hbm, wd_hbm, o_ref,
                    wgs_ref, wus_ref, wds_ref, wgb_ref, wub_ref, wdb_ref, sems):
    planes, step_rows, lanes = x_ref.shape
    bm = EXPERT_ROWS
    blocks_per_step = step_rows // bm
    step = pl.program_id(0)
    last_step = (nu_ref[0] - 1) // blocks_per_step

    def weight_copies(expert):
        return [pltpu.make_async_copy(src.at[expert], dst, sems.at[n])
                for n, (src, dst) in enumerate(((wg_hbm, wgs_ref), (wu_hbm, wus_ref), (wd_hbm, wds_ref)))]

    @pl.loop(0, blocks_per_step)
    def _(sb):
        i = step * blocks_per_step + sb
        rows = pl.ds(pl.multiple_of(sb * bm, bm), bm)
        active = i < nu_ref[0]
        e = be_ref[i]
        new_expert = jnp.logical_or(i == 0, e != be_ref[jnp.maximum(i - 1, 0)])

        @pl.when(jnp.logical_and(active, new_expert))
        def _():
            @pl.when(i == 0)
            def _():
                for cp in weight_copies(e):
                    cp.start()

            for cp in weight_copies(e):
                cp.wait()
            wgb_ref[...] = wgs_ref[...].astype(BF16)
            wub_ref[...] = wus_ref[...].astype(BF16)
            wdb_ref[...] = wds_ref[...].astype(BF16)
            nxt = nxt_ref[e]

            @pl.when(nxt != e)
            def _():
                for cp in weight_copies(nxt):
                    cp.start()

        @pl.when(active)
        def _():
            valid = lax.broadcasted_iota(jnp.int32, (bm, lanes), 0) < nv_ref[i]
            g = None
            u = None
            for j in range(planes):
                xj = _unpack_pairs_f32(jnp.where(valid, x_ref[j, rows, :], jnp.uint32(0))).astype(BF16)
                wrows = pl.ds(2 * lanes * j, 2 * lanes)
                gj = jnp.dot(xj, wgb_ref[wrows, :], preferred_element_type=F32)
                uj = jnp.dot(xj, wub_ref[wrows, :], preferred_element_type=F32)
                g = gj if g is None else g + gj
                u = uj if u is None else u + uj
            out = jnp.dot((_silu(g) * u).astype(BF16), wdb_ref[...], preferred_element_type=F32)
            for j in range(planes):
                o_ref[j, rows, :] = _pack_bf16_pairs(out[:, 2 * lanes * j:2 * lanes * (j + 1)])

        @pl.when(jnp.logical_and(jnp.logical_not(active), step == last_step))
        def _():
            for j in range(planes):
                o_ref[j, rows, :] = jnp.zeros((bm, lanes), o_ref.dtype)


def _experts(block_e, n_valid, n_used, next_expert, xs, wg, wu, wd):
    planes, p, lanes = xs.shape
    e, d, f = wg.shape
    step_rows = EXPERT_ROWS * EXPERT_BLOCKS_PER_STEP
    assert p % step_rows == 0

    def row_map(i, be, nv, nu, nxt):
        return (0, jnp.minimum(i, (nu[0] - 1) // EXPERT_BLOCKS_PER_STEP), 0)

    return pl.pallas_call(
        _experts_kernel,
        grid_spec=pltpu.PrefetchScalarGridSpec(
            num_scalar_prefetch=4,
            grid=(p // step_rows,),
            in_specs=[pl.BlockSpec((planes, step_rows, lanes), row_map),
                      pl.BlockSpec(memory_space=pl.ANY),
                      pl.BlockSpec(memory_space=pl.ANY),
                      pl.BlockSpec(memory_space=pl.ANY)],
            out_specs=pl.BlockSpec((planes, step_rows, lanes), row_map),
            scratch_shapes=[pltpu.VMEM((d, f), F32), pltpu.VMEM((d, f), F32), pltpu.VMEM((f, d), F32),
                            pltpu.VMEM((d, f), BF16), pltpu.VMEM((d, f), BF16), pltpu.VMEM((f, d), BF16),
                            pltpu.SemaphoreType.DMA((3,))]),
        out_shape=jax.ShapeDtypeStruct((planes, p, lanes), jnp.uint32),
        compiler_params=pltpu.CompilerParams(dimension_semantics=("arbitrary",),
                                             vmem_limit_bytes=VMEM_LIMIT_BYTES),
        name="experts",
    )(block_e, n_valid, n_used, next_expert, xs, wg, wu, wd)


def _combine_kernel(w_ref, x2_ref, g2_ref, gfin_ref, yg_ref, o_ref):
    w = w_ref[...]
    planes = yg_ref.shape[1]
    routed = None
    for k in range(TOP_K):
        rows = jnp.concatenate([_unpack_pairs_f32(yg_ref[k, j]) for j in range(planes)], axis=1) * w[:, k:k + 1]
        routed = rows if routed is None else routed + rows
    o_ref[0] = _rms(x2_ref[0] + g2_ref[0] * routed) * gfin_ref[...]


def _combine(wtok, x2, g2, gfin, yg):
    b, s, d = x2.shape
    _, planes, _, lanes = yg.shape
    tc = COMBINE_TOKENS
    nt = s // tc
    return pl.pallas_call(
        _combine_kernel,
        grid=(b, nt),
        in_specs=[pl.BlockSpec((tc, TOP_K), lambda i, j: (i * nt + j, 0)),
                  pl.BlockSpec((1, tc, d), lambda i, j: (i, j, 0)),
                  pl.BlockSpec((1, 1, d), lambda i, j: (i, 0, 0)),
                  pl.BlockSpec((1, d), lambda i, j: (0, 0)),
                  pl.BlockSpec((TOP_K, planes, tc, lanes), lambda i, j: (0, 0, i * nt + j, 0))],
        out_specs=pl.BlockSpec((1, tc, d), lambda i, j: (i, j, 0)),
        out_shape=jax.ShapeDtypeStruct((b, s, d), F32),
        compiler_params=pltpu.CompilerParams(dimension_semantics=("arbitrary", "arbitrary"),
                                             vmem_limit_bytes=VMEM_LIMIT_BYTES),
        name="combine",
    )(wtok, x2, g2, gfin, yg)


def kernel(x, c, positions, w_ada, b_ada, norm_mix_g, w_in, ret_norm_g, conv_w, w_br_ret, w_br_conv, w_out,
           norm_ffn_g, w_router, router_bias, w_exp_gate, w_exp_up, w_exp_down, w_sh_gate, w_sh_up, w_sh_down,
           norm_final_g):
    b, s, d = x.shape
    t = b * s
    depth = w_in.shape[0]
    assert depth == 1, "the combine kernel applies the final norm, so exactly one layer is supported"
    posf = positions.astype(F32)[:, :, None]
    c8 = jnp.zeros((8, d), F32).at[:b].set(c)
    bm = EXPERT_ROWS
    n_blocks = t * TOP_K // bm + N_EXPERTS
    n_rows = n_blocks * bm

    for l in range(depth):
        mod = _ada(c8, w_ada[l], b_ada[l][None, :])[:b].reshape(b, 6, d)
        x2, h2p, lgt = _mixer(
            x, posf, mod, norm_mix_g[l][None, :], w_in[l].astype(BF16), ret_norm_g[l][None, :], conv_w[l],
            w_br_ret[l].astype(BF16), w_br_conv[l].astype(BF16), w_out[l].astype(BF16), norm_ffn_g[l][None, :],
            w_router[l].T, jnp.concatenate([w_sh_gate[l], w_sh_up[l]], axis=1).astype(BF16),
            w_sh_down[l].astype(BF16))
        eidx, wts, rank, counts = _route(lgt, router_bias[l][:, None])

        cnt = counts[:, 0].astype(jnp.int32)
        pcnt = (cnt + bm - 1) // bm * bm
        pend = jnp.cumsum(pcnt)
        pstart = pend - pcnt
        n_used = jnp.maximum(pend[-1] // bm, 1).astype(jnp.int32)[None]
        block_start = jnp.arange(n_blocks, dtype=jnp.int32) * bm
        block_e = jnp.minimum(jnp.sum((pend[None, :] <= block_start[:, None]).astype(jnp.int32), axis=1),
                              N_EXPERTS - 1)
        n_valid = jnp.clip(pstart[block_e] + cnt[block_e] - block_start, 0, bm).astype(jnp.int32)
        eid = jnp.arange(N_EXPERTS, dtype=jnp.int32)
        later_used = jnp.logical_and(eid[None, :] > eid[:, None], cnt[None, :] > 0)
        next_expert = jnp.min(jnp.where(later_used, eid[None, :], N_EXPERTS), axis=1)
        next_expert = jnp.where(next_expert == N_EXPERTS, eid, next_expert)

        dest = _dest(eidx, rank, pstart[:, None])
        dest3 = dest.reshape(TOP_K, t // LANES, LANES).transpose(1, 0, 2)
        xs = _dispatch(h2p, dest3, n_rows)
        ybuf = _experts(block_e, n_valid, n_used, next_expert, xs, w_exp_gate[l], w_exp_up[l], w_exp_down[l])
        yg = _gather(ybuf, dest3, t)
        x = _combine(wts.T, x2, mod[:, 5:6, :], norm_final_g[None, :], yg)
    return x
```

```python
import numpy as np
import jax
import jax.numpy as jnp
from jax import lax
from jax.experimental import pallas as pl
from jax.experimental.pallas import tpu as pltpu
from jax.experimental.pallas import tpu_sc as plsc

RET_HEADS = 4
RET_DK = 128
RET_DV = 256
RET_CHUNK = 128
ROPE_THETA = 10000.0
CONV_K = 3
N_EXPERTS = 256
TOP_K = 8
N_GROUPS = 8
TOPK_GROUPS = 4
GROUP_SIZE = N_EXPERTS // N_GROUPS
ROUTED_SCALE = 2.5
NORM_EPS = 1e-6

MIXER_TOKENS = 512
ROUTE_TOKENS = 256
EXPERT_ROWS = 256
EXPERT_BLOCKS_PER_STEP = 4
COMBINE_TOKENS = 256
LANES = 128
GATHER_BUFFERS = 4

VMEM_LIMIT_BYTES = 56 * 1024 * 1024

F32 = jnp.float32
BF16 = jnp.bfloat16
HIGHEST = lax.Precision.HIGHEST


def _sigmoid(v):
    return 1.0 / (1.0 + jnp.exp(-v))


def _silu(v):
    return v * _sigmoid(v)


def _rms(v):
    return v * lax.rsqrt(jnp.mean(v * v, axis=-1, keepdims=True) + NORM_EPS)


def _resident(shape):
    nd = len(shape)
    return pl.BlockSpec(shape, lambda *_: (0,) * nd, pipeline_mode=pl.Buffered(1))


def _ada_kernel(c_ref, w_ref, b_ref, o_ref):
    c = c_ref[...]
    o_ref[...] = jnp.dot(_silu(c), w_ref[...], precision=HIGHEST, preferred_element_type=F32) + b_ref[...]


def _ada(c8, w, b):
    d, n = w.shape
    tn = 1024
    return pl.pallas_call(
        _ada_kernel,
        grid=(n // tn,),
        in_specs=[pl.BlockSpec((8, d), lambda j: (0, 0)),
                  pl.BlockSpec((d, tn), lambda j: (0, j)),
                  pl.BlockSpec((1, tn), lambda j: (0, j))],
        out_specs=pl.BlockSpec((8, tn), lambda j: (0, j)),
        out_shape=jax.ShapeDtypeStruct((8, n), F32),
        name="ada",
    )(c8, w, b)


def _retention_constants():
    c = RET_CHUNK
    log_g = jnp.log1p(-(2.0 ** (-5.0 - jnp.arange(RET_HEADS, dtype=F32))))
    idx = jnp.arange(c, dtype=F32)
    diff = idx[:, None] - idx[None, :]
    intra = jnp.where(diff >= 0, jnp.exp(log_g[:, None, None] * jnp.maximum(diff, 0.0)), 0.0)
    k_decay = jnp.exp(log_g[:, None] * (c - 1 - idx))
    q_decay = jnp.exp(log_g[:, None] * (idx + 1.0))
    chunk_decay = jnp.exp(log_g * c)
    kd = jnp.broadcast_to(k_decay[:, :, None], (RET_HEADS, c, RET_DK))
    qd = jnp.broadcast_to(q_decay[:, :, None], (RET_HEADS, c, RET_DK))
    cd = jnp.broadcast_to(chunk_decay[:, None, None], (RET_HEADS, 1, RET_DV))
    inv_freq = ROPE_THETA ** (-jnp.arange(0, RET_DK, 2, dtype=F32) / RET_DK)
    inv_freq = jnp.concatenate([inv_freq, inv_freq])[None, :]
    sign = jnp.concatenate([-jnp.ones((RET_DK // 2,), F32), jnp.ones((RET_DK // 2,), F32)])[None, :]
    return intra, qd, kd, cd, inv_freq, sign


def _pack_bf16_pairs(v):
    w = v.shape[1] // 2
    lo = lax.bitcast_convert_type(v[:, :w].astype(BF16).astype(F32), jnp.uint32)
    hi = lax.bitcast_convert_type(v[:, w:].astype(BF16).astype(F32), jnp.uint32)
    return (lo >> 16) | (hi & jnp.uint32(0xFFFF0000))


def _unpack_pairs_f32(word):
    lo = lax.bitcast_convert_type(word << 16, F32)
    hi = lax.bitcast_convert_type(word & jnp.uint32(0xFFFF0000), F32)
    return jnp.concatenate([lo, hi], axis=1)


def _mixer_kernel(x_ref, pos_ref, mod_ref, gmix_ref, win_ref, invf_ref, sign_ref, intra_ref, qd_ref, kd_ref, cd_ref,
                  retg_ref, convw_ref, wbr_ref, wbc_ref, wout_ref, gffn_ref, wrt_ref, wshgu_ref, wshd_ref,
                  x2_ref, h2p_ref, lgt_ref,
                  state_ref, carry_ref, q_ref, k_ref, v_ref, ret_ref):
    tm, d = x_ref.shape[1], x_ref.shape[2]
    q_w = RET_HEADS * RET_DK
    v_w = RET_HEADS * RET_DV
    offs = np.cumsum([0, q_w, q_w, v_w, v_w, d, d, d, d, d])

    @pl.when(pl.program_id(1) == 0)
    def _():
        state_ref[...] = jnp.zeros_like(state_ref)
        carry_ref[...] = jnp.zeros_like(carry_ref)

    x = x_ref[0]
    mod = mod_ref[0]
    sh1, sc1, g1, sh2, sc2, g2 = [mod[i:i + 1] for i in range(6)]
    hb = ((_rms(x) * gmix_ref[...]) * (1.0 + sc1) + sh1).astype(BF16)

    def proj(i):
        return jnp.dot(hb, win_ref[:, offs[i]:offs[i + 1]], preferred_element_type=F32)

    ang = pos_ref[0] * invf_ref[...]
    cosv = jnp.cos(ang)
    sinv = jnp.sin(ang) * sign_ref[...]

    def rope(t):
        return jnp.concatenate(
            [t[:, h * RET_DK:(h + 1) * RET_DK] * cosv
             + pltpu.roll(t[:, h * RET_DK:(h + 1) * RET_DK], RET_DK // 2, 1) * sinv
             for h in range(RET_HEADS)], axis=1)

    q_ref[...] = rope(proj(0)) * (RET_DK ** -0.5)
    k_ref[...] = rope(proj(1))
    v_ref[...] = proj(2).astype(BF16)

    for c in range(tm // RET_CHUNK):
        rows = pl.ds(c * RET_CHUNK, RET_CHUNK)
        for h in range(RET_HEADS):
            qh = q_ref[rows, h * RET_DK:(h + 1) * RET_DK]
            kh = k_ref[rows, h * RET_DK:(h + 1) * RET_DK]
            vh = v_ref[rows, h * RET_DV:(h + 1) * RET_DV]
            scores = lax.dot_general(qh.astype(BF16), kh.astype(BF16), (((1,), (1,)), ((), ())),
                                     preferred_element_type=F32) * intra_ref[h]
            inner = jnp.dot(scores.astype(BF16), vh, preferred_element_type=F32)
            st = state_ref[h]
            cross = jnp.dot((qh * qd_ref[h]).astype(BF16), st.astype(BF16), preferred_element_type=F32)
            kv = lax.dot_general((kh * kd_ref[h]).astype(BF16), vh, (((0,), (0,)), ((), ())),
                                 preferred_element_type=F32)
            state_ref[h] = st * cd_ref[h] + kv
            ret_ref[rows, h * RET_DV:(h + 1) * RET_DV] = _rms(inner + cross)

    y_ret = jnp.dot((_silu(proj(3)) * (ret_ref[...] * retg_ref[...])).astype(BF16), wbr_ref[...],
                    preferred_element_type=F32)

    z = proj(5) * proj(4)
    row = lax.broadcasted_iota(jnp.int32, z.shape, 0)
    prev1 = carry_ref[7:8, :]
    prev2 = carry_ref[6:7, :]
    z1 = jnp.where(row == 0, prev1, pltpu.roll(z, 1, 0))
    z2 = jnp.where(row == 0, prev2, jnp.where(row == 1, prev1, pltpu.roll(z, 2, 0)))
    conv = convw_ref[0:1, :] * z2 + convw_ref[1:2, :] * z1 + convw_ref[2:3, :] * z
    carry_ref[...] = z[tm - 8:tm, :]
    y_conv = jnp.dot((proj(6) * conv).astype(BF16), wbc_ref[...], preferred_element_type=F32)

    mix = _sigmoid(proj(7)) * y_ret + _sigmoid(proj(8)) * y_conv
    x1 = x + g1 * jnp.dot(mix.astype(BF16), wout_ref[...], preferred_element_type=F32)

    h2 = (_rms(x1) * gffn_ref[...]) * (1.0 + sc2) + sh2
    lgt_ref[...] = lax.dot_general(wrt_ref[...], h2, (((1,), (1,)), ((), ())), precision=HIGHEST,
                                   preferred_element_type=F32)
    gu = jnp.dot(h2.astype(BF16), wshgu_ref[...], preferred_element_type=F32)
    f = gu.shape[1] // 2
    shared = jnp.dot((_silu(gu[:, :f]) * gu[:, f:]).astype(BF16), wshd_ref[...], preferred_element_type=F32)
    x2_ref[0] = x1 + g2 * shared
    for j in range(h2p_ref.shape[0]):
        h2p_ref[j] = _pack_bf16_pairs(h2[:, 2 * LANES * j:2 * LANES * (j + 1)])


def _mixer(x, posf, mod, gmix, win, retg, convw, wbr, wbc, wout, gffn, wrt, wshgu, wshd):
    b, s, d = x.shape
    tm = MIXER_TOKENS
    nt = s // tm
    intra, qd, kd, cd, invf, sign = _retention_constants()
    q_w, v_w = RET_HEADS * RET_DK, RET_HEADS * RET_DV
    weights = (gmix, win, invf, sign, intra, qd, kd, cd, retg, convw, wbr, wbc, wout, gffn, wrt, wshgu, wshd)
    return pl.pallas_call(
        _mixer_kernel,
        grid=(b, nt),
        in_specs=[pl.BlockSpec((1, tm, d), lambda i, j: (i, j, 0)),
                  pl.BlockSpec((1, tm, 1), lambda i, j: (i, j, 0)),
                  pl.BlockSpec((1, 6, d), lambda i, j: (i, 0, 0))]
                 + [_resident(w.shape) for w in weights],
        out_specs=[pl.BlockSpec((1, tm, d), lambda i, j: (i, j, 0)),
                   pl.BlockSpec((d // (2 * LANES), tm, LANES), lambda i, j: (0, i * nt + j, 0)),
                   pl.BlockSpec((N_EXPERTS, tm), lambda i, j: (0, i * nt + j))],
        out_shape=[jax.ShapeDtypeStruct((b, s, d), F32),
                   jax.ShapeDtypeStruct((d // (2 * LANES), b * s, LANES), jnp.uint32),
                   jax.ShapeDtypeStruct((N_EXPERTS, b * s), F32)],
        scratch_shapes=[pltpu.VMEM((RET_HEADS, RET_DK, RET_DV), F32),
                        pltpu.VMEM((8, d), F32),
                        pltpu.VMEM((tm, q_w), F32),
                        pltpu.VMEM((tm, q_w), F32),
                        pltpu.VMEM((tm, v_w), BF16),
                        pltpu.VMEM((tm, v_w), F32)],
        compiler_params=pltpu.CompilerParams(dimension_semantics=("arbitrary", "arbitrary"),
                                             vmem_limit_bytes=VMEM_LIMIT_BYTES),
        name="mixer",
    )(x, posf, mod, *weights)


def _first_argmax(v, idx, n):
    m = jnp.max(v, axis=0, keepdims=True)
    return m, jnp.min(jnp.where(v == m, idx, n), axis=0, keepdims=True)


def _route_kernel(lg_ref, bias_ref, tri_ref, eidx_ref, w_ref, rank_ref, cnt_ref, carry_ref):
    @pl.when(pl.program_id(0) == 0)
    def _():
        carry_ref[...] = jnp.zeros_like(carry_ref)

    tn = lg_ref.shape[1]
    neg = F32(-jnp.inf)
    score = _sigmoid(lg_ref[...])
    choice = score + bias_ref[...]

    grow = lax.broadcasted_iota(jnp.int32, (GROUP_SIZE, tn), 0)
    gscores = []
    for g in range(N_GROUPS):
        cg = choice[g * GROUP_SIZE:(g + 1) * GROUP_SIZE]
        m1, i1 = _first_argmax(cg, grow, GROUP_SIZE)
        m2 = jnp.max(jnp.where(grow == i1, neg, cg), axis=0, keepdims=True)
        gscores.append(m1 + m2)
    cur = jnp.concatenate(gscores, axis=0)
    gidx = lax.broadcasted_iota(jnp.int32, (N_GROUPS, tn), 0)
    keep = jnp.zeros((N_GROUPS, tn), F32)
    for _ in range(TOPK_GROUPS):
        _, ig = _first_argmax(cur, gidx, N_GROUPS)
        hit = gidx == ig
        keep = jnp.where(hit, 1.0, keep)
        cur = jnp.where(hit, neg, cur)
    cur = jnp.concatenate(
        [jnp.where(keep[g:g + 1] > 0.0, choice[g * GROUP_SIZE:(g + 1) * GROUP_SIZE], neg)
         for g in range(N_GROUPS)], axis=0)

    erow = lax.broadcasted_iota(jnp.int32, (N_EXPERTS, tn), 0)
    eidx, wts = [], []
    member = jnp.zeros((N_EXPERTS, tn), F32)
    for _ in range(TOP_K):
        _, ie = _first_argmax(cur, erow, N_EXPERTS)
        hit = erow == ie
        eidx.append(ie)
        wts.append(jnp.sum(jnp.where(hit, score, 0.0), axis=0, keepdims=True))
        member = member + hit.astype(F32)
        cur = jnp.where(hit, neg, cur)
    wsum = wts[0]
    for k in range(1, TOP_K):
        wsum = wsum + wts[k]

    before = jnp.dot(member.astype(BF16), tri_ref[...], preferred_element_type=F32) + carry_ref[...]
    ranks = [jnp.sum(jnp.where(erow == eidx[k], before, 0.0), axis=0, keepdims=True) for k in range(TOP_K)]
    carry_ref[...] = carry_ref[...] + jnp.sum(member, axis=1, keepdims=True)

    eidx_ref[...] = jnp.concatenate(eidx, axis=0)
    w_ref[...] = jnp.concatenate([w / wsum * ROUTED_SCALE for w in wts], axis=0)
    rank_ref[...] = jnp.concatenate(ranks, axis=0).astype(jnp.int32)
    cnt_ref[...] = carry_ref[...]


def _route(lgt, bias):
    e, t = lgt.shape
    tn = ROUTE_TOKENS
    tri = jnp.asarray(np.triu(np.ones((tn, tn), np.float32), 1), BF16)
    return pl.pallas_call(
        _route_kernel,
        grid=(t // tn,),
        in_specs=[pl.BlockSpec((e, tn), lambda i: (0, i)),
                  pl.BlockSpec((e, 1), lambda i: (0, 0)),
                  pl.BlockSpec((tn, tn), lambda i: (0, 0))],
        out_specs=[pl.BlockSpec((TOP_K, tn), lambda i: (0, i)),
                   pl.BlockSpec((TOP_K, tn), lambda i: (0, i)),
                   pl.BlockSpec((TOP_K, tn), lambda i: (0, i)),
                   pl.BlockSpec((e, 1), lambda i: (0, 0))],
        out_shape=[jax.ShapeDtypeStruct((TOP_K, t), jnp.int32),
                   jax.ShapeDtypeStruct((TOP_K, t), F32),
                   jax.ShapeDtypeStruct((TOP_K, t), jnp.int32),
                   jax.ShapeDtypeStruct((e, 1), F32)],
        scratch_shapes=[pltpu.VMEM((e, 1), F32)],
        compiler_params=pltpu.CompilerParams(dimension_semantics=("arbitrary",)),
        name="route",
    )(lgt, bias, tri)


def _dest_kernel(eidx_ref, rank_ref, pstart_ref, dest_ref):
    tn = eidx_ref.shape[1]
    erow = lax.broadcasted_iota(jnp.int32, (N_EXPERTS, tn), 0)
    start = pstart_ref[...]
    dest_ref[...] = rank_ref[...] + jnp.concatenate(
        [jnp.sum(jnp.where(erow == eidx_ref[k:k + 1, :], start, 0), axis=0, keepdims=True) for k in range(TOP_K)],
        axis=0)


def _dest(eidx, rank, pstart):
    k, t = eidx.shape
    tn = ROUTE_TOKENS
    return pl.pallas_call(
        _dest_kernel,
        grid=(t // tn,),
        in_specs=[pl.BlockSpec((k, tn), lambda i: (0, i)),
                  pl.BlockSpec((k, tn), lambda i: (0, i)),
                  pl.BlockSpec((N_EXPERTS, 1), lambda i: (0, 0))],
        out_specs=pl.BlockSpec((k, tn), lambda i: (0, i)),
        out_shape=jax.ShapeDtypeStruct((k, t), jnp.int32),
        name="dest",
    )(eidx, rank, pstart)


def _sc_mesh_and_workers():
    mesh = plsc.VectorSubcoreMesh(core_axis_name="c", subcore_axis_name="s")
    return mesh, mesh.num_cores, mesh.num_cores * mesh.num_subcores


def _worker_id(num_cores):
    return lax.axis_index("s") * num_cores + lax.axis_index("c")


def _dispatch(h2p, dest3, n_rows):
    planes, t, lanes = h2p.shape
    n_chunks = dest3.shape[0]
    mesh, num_cores, workers = _sc_mesh_and_workers()
    chunks_per_worker = n_chunks // workers
    assert chunks_per_worker * workers == n_chunks and n_chunks * lanes == t

    def body(h_hbm, d_hbm, xs_hbm, idx_v, rows_v, load_sem, scatter_sem):
        wid = _worker_id(num_cores)

        @pl.loop(0, chunks_per_worker)
        def _(c):
            chunk = wid * chunks_per_worker + c
            tok = pl.ds(chunk * lanes, lanes)
            pltpu.sync_copy(d_hbm.at[chunk], idx_v)
            loads = [None] * planes
            scatters = [None] * planes
            loads[0] = pltpu.async_copy(h_hbm.at[0, tok], rows_v.at[0], load_sem.at[0])
            for j in range(planes):
                loads[j].wait()
                if j + 1 < planes:
                    if j >= 1:
                        for cp in scatters[j - 1]:
                            cp.wait()
                    loads[j + 1] = pltpu.async_copy(h_hbm.at[j + 1, tok], rows_v.at[(j + 1) % 2],
                                                    load_sem.at[(j + 1) % 2])
                scatters[j] = [pltpu.async_copy(rows_v.at[j % 2], xs_hbm.at[j].at[idx_v.at[k]],
                                                scatter_sem.at[j % 2]) for k in range(TOP_K)]
            for j in range(max(planes - 2, 0), planes):
                for cp in scatters[j]:
                    cp.wait()

    return pl.kernel(
        body,
        out_type=jax.ShapeDtypeStruct((planes, n_rows, lanes), h2p.dtype),
        mesh=mesh,
        scratch_types=[pltpu.VMEM((TOP_K, lanes), jnp.int32),
                       pltpu.VMEM((2, lanes, lanes), h2p.dtype),
                       pltpu.SemaphoreType.DMA((2,)),
                       pltpu.SemaphoreType.DMA((2,))],
        name="dispatch",
    )(h2p, dest3)


def _gather(ybuf, dest3, t):
    planes, _, lanes = ybuf.shape
    n_chunks = dest3.shape[0]
    mesh, num_cores, workers = _sc_mesh_and_workers()
    chunks_per_worker = n_chunks // workers
    nbuf = GATHER_BUFFERS
    lag = nbuf // 2
    assert chunks_per_worker * workers == n_chunks and n_chunks * lanes == t

    def body(y_hbm, d_hbm, yg_hbm, idx_v, rows_v, gather_sem, store_sem):
        wid = _worker_id(num_cores)

        @pl.loop(0, chunks_per_worker)
        def _(c):
            chunk = wid * chunks_per_worker + c
            tok = pl.ds(chunk * lanes, lanes)
            pltpu.sync_copy(d_hbm.at[chunk], idx_v)

            @pl.loop(0, TOP_K)
            def _(k):
                gathers = [None] * planes
                stores = [None] * planes

                def store(j):
                    gathers[j].wait()
                    stores[j] = pltpu.async_copy(rows_v.at[j % nbuf], yg_hbm.at[k, j, tok], store_sem.at[j % nbuf])

                for j in range(planes):
                    if j >= nbuf:
                        stores[j - nbuf].wait()
                    gathers[j] = pltpu.async_copy(y_hbm.at[j].at[idx_v.at[k]], rows_v.at[j % nbuf],
                                                  gather_sem.at[j % nbuf])
                    if j >= lag:
                        store(j - lag)
                for j in range(max(planes - lag, 0), planes):
                    store(j)
                for j in range(max(planes - nbuf, 0), planes):
                    stores[j].wait()

    return pl.kernel(
        body,
        out_type=jax.ShapeDtypeStruct((TOP_K, planes, t, lanes), ybuf.dtype),
        mesh=mesh,
        scratch_types=[pltpu.VMEM((TOP_K, lanes), jnp.int32),
                       pltpu.VMEM((nbuf, lanes, lanes), ybuf.dtype),
                       pltpu.SemaphoreType.DMA((nbuf,)),
                       pltpu.SemaphoreType.DMA((nbuf,))],
        name="gather",
    )(ybuf, dest3)


def _experts_kernel(be_ref, nv_ref, nu_ref, em_ref, x_ref, wg_hbm, wu_hbm, wd_hbm, o_ref,
                    wgs_ref, wus_ref, wds_ref, wgb_ref, wub_ref, wdb_ref, sems):
    planes, step_rows, lanes = x_ref.shape
    bm = EXPERT_ROWS
    blocks_per_step = step_rows // bm
    step = pl.program_id(0)
    last_step = (nu_ref[0] - 1) // blocks_per_step

    def weight_copies(expert):
        slot = em_ref[2, expert]
        return [pltpu.make_async_copy(src.at[expert], dst.at[slot], sems.at[slot, n])
                for n, (src, dst) in enumerate(((wg_hbm, wgs_ref), (wu_hbm, wus_ref), (wd_hbm, wds_ref)))]

    def start_if_other(expert, current):
        @pl.when(expert != current)
        def _():
            for cp in weight_copies(expert):
                cp.start()

    @pl.loop(0, blocks_per_step)
    def _(sb):
        i = step * blocks_per_step + sb
        rows = pl.ds(pl.multiple_of(sb * bm, bm), bm)
        active = i < nu_ref[0]
        e = be_ref[i]
        new_expert = jnp.logical_or(i == 0, e != be_ref[jnp.maximum(i - 1, 0)])

        @pl.when(jnp.logical_and(active, new_expert))
        def _():
            @pl.when(i == 0)
            def _():
                for cp in weight_copies(e):
                    cp.start()
                start_if_other(em_ref[0, e], e)

            for cp in weight_copies(e):
                cp.wait()
            slot = em_ref[2, e]
            wgb_ref[...] = wgs_ref[slot].astype(BF16)
            wub_ref[...] = wus_ref[slot].astype(BF16)
            wdb_ref[...] = wds_ref[slot].astype(BF16)
            start_if_other(em_ref[1, e], e)

        @pl.when(active)
        def _():
            valid = lax.broadcasted_iota(jnp.int32, (bm, lanes), 0) < nv_ref[i]
            g = None
            u = None
            for j in range(planes):
                xj = _unpack_pairs_f32(jnp.where(valid, x_ref[j, rows, :], jnp.uint32(0))).astype(BF16)
                wrows = pl.ds(2 * lanes * j, 2 * lanes)
                gj = jnp.dot(xj, wgb_ref[wrows, :], preferred_element_type=F32)
                uj = jnp.dot(xj, wub_ref[wrows, :], preferred_element_type=F32)
                g = gj if g is None else g + gj
                u = uj if u is None else u + uj
            out = jnp.dot((_silu(g) * u).astype(BF16), wdb_ref[...], preferred_element_type=F32)
            for j in range(planes):
                o_ref[j, rows, :] = _pack_bf16_pairs(out[:, 2 * lanes * j:2 * lanes * (j + 1)])

        @pl.when(jnp.logical_and(jnp.logical_not(active), step == last_step))
        def _():
            for j in range(planes):
                o_ref[j, rows, :] = jnp.zeros((bm, lanes), o_ref.dtype)


def _experts(block_e, n_valid, n_used, expert_meta, xs, wg, wu, wd):
    planes, p, lanes = xs.shape
    e, d, f = wg.shape
    step_rows = EXPERT_ROWS * EXPERT_BLOCKS_PER_STEP
    assert p % step_rows == 0

    def row_map(i, be, nv, nu, nxt):
        return (0, jnp.minimum(i, (nu[0] - 1) // EXPERT_BLOCKS_PER_STEP), 0)

    return pl.pallas_call(
        _experts_kernel,
        grid_spec=pltpu.PrefetchScalarGridSpec(
            num_scalar_prefetch=4,
            grid=(p // step_rows,),
            in_specs=[pl.BlockSpec((planes, step_rows, lanes), row_map),
                      pl.BlockSpec(memory_space=pl.ANY),
                      pl.BlockSpec(memory_space=pl.ANY),
                      pl.BlockSpec(memory_space=pl.ANY)],
            out_specs=pl.BlockSpec((planes, step_rows, lanes), row_map),
            scratch_shapes=[pltpu.VMEM((2, d, f), F32), pltpu.VMEM((2, d, f), F32), pltpu.VMEM((2, f, d), F32),
                            pltpu.VMEM((d, f), BF16), pltpu.VMEM((d, f), BF16), pltpu.VMEM((f, d), BF16),
                            pltpu.SemaphoreType.DMA((2, 3))]),
        out_shape=jax.ShapeDtypeStruct((planes, p, lanes), jnp.uint32),
        compiler_params=pltpu.CompilerParams(dimension_semantics=("arbitrary",),
                                             vmem_limit_bytes=VMEM_LIMIT_BYTES),
        name="experts",
    )(block_e, n_valid, n_used, expert_meta, xs, wg, wu, wd)


def _combine_kernel(w_ref, x2_ref, g2_ref, gfin_ref, yg_ref, *maybe_prev_and_out):
    o_ref = maybe_prev_and_out[-1]
    w = w_ref[...]
    planes = yg_ref.shape[1]
    routed = None
    for k in range(TOP_K):
        rows = jnp.concatenate([_unpack_pairs_f32(yg_ref[k, j]) for j in range(planes)], axis=1) * w[:, k:k + 1]
        routed = rows if routed is None else routed + rows
    o_ref[0] = _rms(x2_ref[0] + g2_ref[0] * routed) * gfin_ref[...]


def _combine(wtok, x2, g2, gfin, yg, bi, prev_out):
    b, s, d = x2.shape
    _, planes, _, lanes = yg.shape
    tc = COMBINE_TOKENS
    nt = s // tc
    in_specs = [pl.BlockSpec((tc, TOP_K), lambda j: (bi * nt + j, 0)),
                pl.BlockSpec((1, tc, d), lambda j: (bi, j, 0)),
                pl.BlockSpec((1, 1, d), lambda j: (bi, 0, 0)),
                pl.BlockSpec((1, d), lambda j: (0, 0)),
                pl.BlockSpec((TOP_K, planes, tc, lanes), lambda j: (0, 0, j, 0))]
    args = [wtok, x2, g2, gfin, yg]
    aliases = {}
    if prev_out is not None:
        in_specs.append(pl.BlockSpec(memory_space=pl.ANY))
        args.append(prev_out)
        aliases = {len(args) - 1: 0}
    return pl.pallas_call(
        _combine_kernel,
        grid=(nt,),
        in_specs=in_specs,
        out_specs=pl.BlockSpec((1, tc, d), lambda j: (bi, j, 0)),
        out_shape=jax.ShapeDtypeStruct((b, s, d), F32),
        input_output_aliases=aliases,
        compiler_params=pltpu.CompilerParams(dimension_semantics=("arbitrary",),
                                             vmem_limit_bytes=VMEM_LIMIT_BYTES),
        name="combine",
    )(*args)


def kernel(x, c, positions, w_ada, b_ada, norm_mix_g, w_in, ret_norm_g, conv_w, w_br_ret, w_br_conv, w_out,
           norm_ffn_g, w_router, router_bias, w_exp_gate, w_exp_up, w_exp_down, w_sh_gate, w_sh_up, w_sh_down,
           norm_final_g):
    b, s, d = x.shape
    t = b * s
    depth = w_in.shape[0]
    assert depth == 1, "the combine kernel applies the final norm, so exactly one layer is supported"
    posf = positions.astype(F32)[:, :, None]
    c8 = jnp.zeros((8, d), F32).at[:b].set(c)
    bm = EXPERT_ROWS
    n_blocks = t * TOP_K // bm + N_EXPERTS
    n_rows = n_blocks * bm

    for l in range(depth):
        mod = _ada(c8, w_ada[l], b_ada[l][None, :])[:b].reshape(b, 6, d)
        x2, h2p, lgt = _mixer(
            x, posf, mod, norm_mix_g[l][None, :], w_in[l].astype(BF16), ret_norm_g[l][None, :], conv_w[l],
            w_br_ret[l].astype(BF16), w_br_conv[l].astype(BF16), w_out[l].astype(BF16), norm_ffn_g[l][None, :],
            w_router[l].T, jnp.concatenate([w_sh_gate[l], w_sh_up[l]], axis=1).astype(BF16),
            w_sh_down[l].astype(BF16))
        eidx, wts, rank, counts = _route(lgt, router_bias[l][:, None])

        cnt = counts[:, 0].astype(jnp.int32)
        pcnt = (cnt + bm - 1) // bm * bm
        pend = jnp.cumsum(pcnt)
        pstart = pend - pcnt
        n_used = jnp.maximum(pend[-1] // bm, 1).astype(jnp.int32)[None]
        block_start = jnp.arange(n_blocks, dtype=jnp.int32) * bm
        block_e = jnp.minimum(jnp.sum((pend[None, :] <= block_start[:, None]).astype(jnp.int32), axis=1),
                              N_EXPERTS - 1)
        n_valid = jnp.clip(pstart[block_e] + cnt[block_e] - block_start, 0, bm).astype(jnp.int32)
        eid = jnp.arange(N_EXPERTS, dtype=jnp.int32)
        later_used = jnp.logical_and(eid[None, :] > eid[:, None], cnt[None, :] > 0)
        next_expert = jnp.min(jnp.where(later_used, eid[None, :], N_EXPERTS), axis=1)
        next_expert = jnp.where(next_expert == N_EXPERTS, eid, next_expert)
        after_next = next_expert[next_expert]
        after_next = jnp.where(after_next == next_expert, eid, after_next)
        stage_slot = (jnp.cumsum((cnt > 0).astype(jnp.int32)) - 1) % 2
        expert_meta = jnp.stack([next_expert, after_next, stage_slot]).astype(jnp.int32)

        dest = _dest(eidx, rank, pstart[:, None])
        dest3 = dest.reshape(TOP_K, t // LANES, LANES).transpose(1, 0, 2)
        xs = _dispatch(h2p, dest3, n_rows)
        ybuf = _experts(block_e, n_valid, n_used, expert_meta, xs, w_exp_gate[l], w_exp_up[l], w_exp_down[l])
        chunks = s // LANES
        out = None
        for bi in range(b):
            yg = _gather(ybuf, dest3[bi * chunks:(bi + 1) * chunks], s)
            out = _combine(wts.T, x2, mod[:, 5:6, :], norm_final_g[None, :], yg, bi, out)
        x = out
    return x
```

```python
import numpy as np
import jax
import jax.numpy as jnp
from jax import lax
from jax.experimental import pallas as pl
from jax.experimental.pallas import tpu as pltpu
from jax.experimental.pallas import tpu_sc as plsc

RET_HEADS = 4
RET_DK = 128
RET_DV = 256
RET_CHUNK = 128
ROPE_THETA = 10000.0
CONV_K = 3
N_EXPERTS = 256
TOP_K = 8
N_GROUPS = 8
TOPK_GROUPS = 4
GROUP_SIZE = N_EXPERTS // N_GROUPS
ROUTED_SCALE = 2.5
NORM_EPS = 1e-6

MIXER_TOKENS = 512
ROUTE_TOKENS = 512
EXPERT_ROWS = 256
EXPERT_BLOCKS_PER_STEP = 8
COMBINE_TOKENS = 256
LANES = 128
GATHER_BUFFERS = 4

VMEM_LIMIT_BYTES = 56 * 1024 * 1024

F32 = jnp.float32
BF16 = jnp.bfloat16
HIGHEST = lax.Precision.HIGHEST


def _sigmoid(v):
    return 1.0 / (1.0 + jnp.exp(-v))


def _silu(v):
    return v * _sigmoid(v)


def _rms(v):
    return v * lax.rsqrt(jnp.mean(v * v, axis=-1, keepdims=True) + NORM_EPS)


def _resident(shape):
    nd = len(shape)
    return pl.BlockSpec(shape, lambda *_: (0,) * nd, pipeline_mode=pl.Buffered(1))


def _ada_kernel(c_ref, w_ref, b_ref, o_ref):
    c = c_ref[...]
    o_ref[...] = jnp.dot(_silu(c), w_ref[...], precision=HIGHEST, preferred_element_type=F32) + b_ref[...]


def _ada(c8, w, b):
    d, n = w.shape
    tn = 1024
    return pl.pallas_call(
        _ada_kernel,
        grid=(n // tn,),
        in_specs=[pl.BlockSpec((8, d), lambda j: (0, 0)),
                  pl.BlockSpec((d, tn), lambda j: (0, j)),
                  pl.BlockSpec((1, tn), lambda j: (0, j))],
        out_specs=pl.BlockSpec((8, tn), lambda j: (0, j)),
        out_shape=jax.ShapeDtypeStruct((8, n), F32),
        name="ada",
    )(c8, w, b)


def _retention_constants():
    c = RET_CHUNK
    log_g = jnp.log1p(-(2.0 ** (-5.0 - jnp.arange(RET_HEADS, dtype=F32))))
    idx = jnp.arange(c, dtype=F32)
    diff = idx[:, None] - idx[None, :]
    intra = jnp.where(diff >= 0, jnp.exp(log_g[:, None, None] * jnp.maximum(diff, 0.0)), 0.0)
    k_decay = jnp.exp(log_g[:, None] * (c - 1 - idx))
    q_decay = jnp.exp(log_g[:, None] * (idx + 1.0))
    chunk_decay = jnp.exp(log_g * c)
    kd = jnp.broadcast_to(k_decay[:, :, None], (RET_HEADS, c, RET_DK))
    qd = jnp.broadcast_to(q_decay[:, :, None], (RET_HEADS, c, RET_DK))
    cd = jnp.broadcast_to(chunk_decay[:, None, None], (RET_HEADS, 1, RET_DV))
    inv_freq = ROPE_THETA ** (-jnp.arange(0, RET_DK, 2, dtype=F32) / RET_DK)
    inv_freq = jnp.concatenate([inv_freq, inv_freq])[None, :]
    sign = jnp.concatenate([-jnp.ones((RET_DK // 2,), F32), jnp.ones((RET_DK // 2,), F32)])[None, :]
    return intra, qd, kd, cd, inv_freq, sign


def _pack_bf16_pairs(v):
    w = v.shape[1] // 2
    lo = lax.bitcast_convert_type(v[:, :w].astype(BF16).astype(F32), jnp.uint32)
    hi = lax.bitcast_convert_type(v[:, w:].astype(BF16).astype(F32), jnp.uint32)
    return (lo >> 16) | (hi & jnp.uint32(0xFFFF0000))


def _unpack_pairs_f32(word):
    lo = lax.bitcast_convert_type(word << 16, F32)
    hi = lax.bitcast_convert_type(word & jnp.uint32(0xFFFF0000), F32)
    return jnp.concatenate([lo, hi], axis=1)


def _mixer_kernel(x_ref, pos_ref, mod_ref, gmix_ref, win_ref, invf_ref, sign_ref, intra_ref, qd_ref, kd_ref, cd_ref,
                  retg_ref, convw_ref, wbr_ref, wbc_ref, wout_ref, gffn_ref, wrt_ref, wshgu_ref, wshd_ref,
                  x2_ref, h2p_ref, lgt_ref,
                  state_ref, carry_ref, q_ref, k_ref, v_ref, ret_ref):
    tm, d = x_ref.shape[1], x_ref.shape[2]
    q_w = RET_HEADS * RET_DK
    v_w = RET_HEADS * RET_DV
    offs = np.cumsum([0, q_w, q_w, v_w, v_w, d, d, d, d, d])

    @pl.when(pl.program_id(1) == 0)
    def _():
        state_ref[...] = jnp.zeros_like(state_ref)
        carry_ref[...] = jnp.zeros_like(carry_ref)

    x = x_ref[0]
    mod = mod_ref[0]
    sh1, sc1, g1, sh2, sc2, g2 = [mod[i:i + 1] for i in range(6)]
    hb = ((_rms(x) * gmix_ref[...]) * (1.0 + sc1) + sh1).astype(BF16)

    def proj(i):
        return jnp.dot(hb, win_ref[:, offs[i]:offs[i + 1]], preferred_element_type=F32)

    ang = pos_ref[0] * invf_ref[...]
    cosv = jnp.cos(ang)
    sinv = jnp.sin(ang) * sign_ref[...]

    def rope(t):
        return jnp.concatenate(
            [t[:, h * RET_DK:(h + 1) * RET_DK] * cosv
             + pltpu.roll(t[:, h * RET_DK:(h + 1) * RET_DK], RET_DK // 2, 1) * sinv
             for h in range(RET_HEADS)], axis=1)

    q_ref[...] = rope(proj(0)) * (RET_DK ** -0.5)
    k_ref[...] = rope(proj(1))
    v_ref[...] = proj(2).astype(BF16)

    for c in range(tm // RET_CHUNK):
        rows = pl.ds(c * RET_CHUNK, RET_CHUNK)
        for h in range(RET_HEADS):
            qh = q_ref[rows, h * RET_DK:(h + 1) * RET_DK]
            kh = k_ref[rows, h * RET_DK:(h + 1) * RET_DK]
            vh = v_ref[rows, h * RET_DV:(h + 1) * RET_DV]
            scores = lax.dot_general(qh.astype(BF16), kh.astype(BF16), (((1,), (1,)), ((), ())),
                                     preferred_element_type=F32) * intra_ref[h]
            inner = jnp.dot(scores.astype(BF16), vh, preferred_element_type=F32)
            st = state_ref[h]
            cross = jnp.dot((qh * qd_ref[h]).astype(BF16), st.astype(BF16), preferred_element_type=F32)
            kv = lax.dot_general((kh * kd_ref[h]).astype(BF16), vh, (((0,), (0,)), ((), ())),
                                 preferred_element_type=F32)
            state_ref[h] = st * cd_ref[h] + kv
            ret_ref[rows, h * RET_DV:(h + 1) * RET_DV] = _rms(inner + cross)

    y_ret = jnp.dot((_silu(proj(3)) * (ret_ref[...] * retg_ref[...])).astype(BF16), wbr_ref[...],
                    preferred_element_type=F32)

    z = proj(5) * proj(4)
    row = lax.broadcasted_iota(jnp.int32, z.shape, 0)
    prev1 = carry_ref[7:8, :]
    prev2 = carry_ref[6:7, :]
    z1 = jnp.where(row == 0, prev1, pltpu.roll(z, 1, 0))
    z2 = jnp.where(row == 0, prev2, jnp.where(row == 1, prev1, pltpu.roll(z, 2, 0)))
    conv = convw_ref[0:1, :] * z2 + convw_ref[1:2, :] * z1 + convw_ref[2:3, :] * z
    carry_ref[...] = z[tm - 8:tm, :]
    y_conv = jnp.dot((proj(6) * conv).astype(BF16), wbc_ref[...], preferred_element_type=F32)

    mix = _sigmoid(proj(7)) * y_ret + _sigmoid(proj(8)) * y_conv
    x1 = x + g1 * jnp.dot(mix.astype(BF16), wout_ref[...], preferred_element_type=F32)

    h2 = (_rms(x1) * gffn_ref[...]) * (1.0 + sc2) + sh2
    lgt_ref[...] = lax.dot_general(wrt_ref[...], h2, (((1,), (1,)), ((), ())), precision=HIGHEST,
                                   preferred_element_type=F32)
    gu = jnp.dot(h2.astype(BF16), wshgu_ref[...], preferred_element_type=F32)
    f = gu.shape[1] // 2
    shared = jnp.dot((_silu(gu[:, :f]) * gu[:, f:]).astype(BF16), wshd_ref[...], preferred_element_type=F32)
    x2_ref[0] = x1 + g2 * shared
    for j in range(h2p_ref.shape[0]):
        h2p_ref[j] = _pack_bf16_pairs(h2[:, 2 * LANES * j:2 * LANES * (j + 1)])


def _mixer(x, posf, mod, gmix, win, retg, convw, wbr, wbc, wout, gffn, wrt, wshgu, wshd):
    b, s, d = x.shape
    tm = MIXER_TOKENS
    nt = s // tm
    intra, qd, kd, cd, invf, sign = _retention_constants()
    q_w, v_w = RET_HEADS * RET_DK, RET_HEADS * RET_DV
    weights = (gmix, win, invf, sign, intra, qd, kd, cd, retg, convw, wbr, wbc, wout, gffn, wrt, wshgu, wshd)
    return pl.pallas_call(
        _mixer_kernel,
        grid=(b, nt),
        in_specs=[pl.BlockSpec((1, tm, d), lambda i, j: (i, j, 0)),
                  pl.BlockSpec((1, tm, 1), lambda i, j: (i, j, 0)),
                  pl.BlockSpec((1, 6, d), lambda i, j: (i, 0, 0))]
                 + [_resident(w.shape) for w in weights],
        out_specs=[pl.BlockSpec((1, tm, d), lambda i, j: (i, j, 0)),
                   pl.BlockSpec((d // (2 * LANES), tm, LANES), lambda i, j: (0, i * nt + j, 0)),
                   pl.BlockSpec((N_EXPERTS, tm), lambda i, j: (0, i * nt + j))],
        out_shape=[jax.ShapeDtypeStruct((b, s, d), F32),
                   jax.ShapeDtypeStruct((d // (2 * LANES), b * s, LANES), jnp.uint32),
                   jax.ShapeDtypeStruct((N_EXPERTS, b * s), F32)],
        scratch_shapes=[pltpu.VMEM((RET_HEADS, RET_DK, RET_DV), F32),
                        pltpu.VMEM((8, d), F32),
                        pltpu.VMEM((tm, q_w), F32),
                        pltpu.VMEM((tm, q_w), F32),
                        pltpu.VMEM((tm, v_w), BF16),
                        pltpu.VMEM((tm, v_w), F32)],
        compiler_params=pltpu.CompilerParams(dimension_semantics=("arbitrary", "arbitrary"),
                                             vmem_limit_bytes=VMEM_LIMIT_BYTES),
        name="mixer",
    )(x, posf, mod, *weights)


def _first_argmax(v, idx, n):
    m = jnp.max(v, axis=0, keepdims=True)
    return m, jnp.min(jnp.where(v == m, idx, n), axis=0, keepdims=True)


def _route_kernel(lg_ref, bias_ref, tri_ref, eidx_ref, w_ref, rank_ref, cnt_ref, carry_ref):
    @pl.when(pl.program_id(0) == 0)
    def _():
        carry_ref[...] = jnp.zeros_like(carry_ref)

    tn = lg_ref.shape[1]
    neg = F32(-jnp.inf)
    score = _sigmoid(lg_ref[...])
    choice = score + bias_ref[...]

    grow = lax.broadcasted_iota(jnp.int32, (GROUP_SIZE, tn), 0)
    gscores = []
    for g in range(N_GROUPS):
        cg = choice[g * GROUP_SIZE:(g + 1) * GROUP_SIZE]
        m1, i1 = _first_argmax(cg, grow, GROUP_SIZE)
        m2 = jnp.max(jnp.where(grow == i1, neg, cg), axis=0, keepdims=True)
        gscores.append(m1 + m2)
    cur = jnp.concatenate(gscores, axis=0)
    gidx = lax.broadcasted_iota(jnp.int32, (N_GROUPS, tn), 0)
    keep = jnp.zeros((N_GROUPS, tn), F32)
    for _ in range(TOPK_GROUPS):
        _, ig = _first_argmax(cur, gidx, N_GROUPS)
        hit = gidx == ig
        keep = jnp.where(hit, 1.0, keep)
        cur = jnp.where(hit, neg, cur)
    cur = jnp.concatenate(
        [jnp.where(keep[g:g + 1] > 0.0, choice[g * GROUP_SIZE:(g + 1) * GROUP_SIZE], neg)
         for g in range(N_GROUPS)], axis=0)

    erow = lax.broadcasted_iota(jnp.int32, (N_EXPERTS, tn), 0)
    eidx, wts = [], []
    member = jnp.zeros((N_EXPERTS, tn), F32)
    for _ in range(TOP_K):
        _, ie = _first_argmax(cur, erow, N_EXPERTS)
        hit = erow == ie
        eidx.append(ie)
        wts.append(jnp.sum(jnp.where(hit, score, 0.0), axis=0, keepdims=True))
        member = member + hit.astype(F32)
        cur = jnp.where(hit, neg, cur)
    wsum = wts[0]
    for k in range(1, TOP_K):
        wsum = wsum + wts[k]

    before = jnp.dot(member.astype(BF16), tri_ref[...], preferred_element_type=F32) + carry_ref[...]
    ranks = [jnp.sum(jnp.where(erow == eidx[k], before, 0.0), axis=0, keepdims=True) for k in range(TOP_K)]
    carry_ref[...] = carry_ref[...] + jnp.sum(member, axis=1, keepdims=True)

    eidx_ref[...] = jnp.concatenate(eidx, axis=0)
    w_ref[...] = jnp.concatenate([w / wsum * ROUTED_SCALE for w in wts], axis=0)
    rank_ref[...] = jnp.concatenate(ranks, axis=0).astype(jnp.int32)
    cnt_ref[...] = carry_ref[...]


def _route(lgt, bias):
    e, t = lgt.shape
    tn = ROUTE_TOKENS
    tri = jnp.asarray(np.triu(np.ones((tn, tn), np.float32), 1), BF16)
    return pl.pallas_call(
        _route_kernel,
        grid=(t // tn,),
        in_specs=[pl.BlockSpec((e, tn), lambda i: (0, i)),
                  pl.BlockSpec((e, 1), lambda i: (0, 0)),
                  pl.BlockSpec((tn, tn), lambda i: (0, 0))],
        out_specs=[pl.BlockSpec((TOP_K, tn), lambda i: (0, i)),
                   pl.BlockSpec((TOP_K, tn), lambda i: (0, i)),
                   pl.BlockSpec((TOP_K, tn), lambda i: (0, i)),
                   pl.BlockSpec((e, 1), lambda i: (0, 0))],
        out_shape=[jax.ShapeDtypeStruct((TOP_K, t), jnp.int32),
                   jax.ShapeDtypeStruct((TOP_K, t), F32),
                   jax.ShapeDtypeStruct((TOP_K, t), jnp.int32),
                   jax.ShapeDtypeStruct((e, 1), F32)],
        scratch_shapes=[pltpu.VMEM((e, 1), F32)],
        compiler_params=pltpu.CompilerParams(dimension_semantics=("arbitrary",)),
        name="route",
    )(lgt, bias, tri)


def _dest_kernel(eidx_ref, rank_ref, pstart_ref, dest_ref):
    tn = eidx_ref.shape[1]
    erow = lax.broadcasted_iota(jnp.int32, (N_EXPERTS, tn), 0)
    start = pstart_ref[...]
    dest_ref[...] = rank_ref[...] + jnp.concatenate(
        [jnp.sum(jnp.where(erow == eidx_ref[k:k + 1, :], start, 0), axis=0, keepdims=True) for k in range(TOP_K)],
        axis=0)


def _dest(eidx, rank, pstart):
    k, t = eidx.shape
    tn = ROUTE_TOKENS
    return pl.pallas_call(
        _dest_kernel,
        grid=(t // tn,),
        in_specs=[pl.BlockSpec((k, tn), lambda i: (0, i)),
                  pl.BlockSpec((k, tn), lambda i: (0, i)),
                  pl.BlockSpec((N_EXPERTS, 1), lambda i: (0, 0))],
        out_specs=pl.BlockSpec((k, tn), lambda i: (0, i)),
        out_shape=jax.ShapeDtypeStruct((k, t), jnp.int32),
        name="dest",
    )(eidx, rank, pstart)


def _sc_mesh_and_workers():
    mesh = plsc.VectorSubcoreMesh(core_axis_name="c", subcore_axis_name="s")
    return mesh, mesh.num_cores, mesh.num_cores * mesh.num_subcores


def _worker_id(num_cores):
    return lax.axis_index("s") * num_cores + lax.axis_index("c")


def _dispatch(h2p, dest3, n_rows):
    planes, t, lanes = h2p.shape
    n_chunks = dest3.shape[0]
    mesh, num_cores, workers = _sc_mesh_and_workers()
    chunks_per_worker = n_chunks // workers
    assert chunks_per_worker * workers == n_chunks and n_chunks * lanes == t

    def body(h_hbm, d_hbm, xs_hbm, idx_v, rows_v, load_sem, scatter_sem):
        wid = _worker_id(num_cores)

        @pl.loop(0, chunks_per_worker)
        def _(c):
            chunk = wid * chunks_per_worker + c
            tok = pl.ds(chunk * lanes, lanes)
            pltpu.sync_copy(d_hbm.at[chunk], idx_v)
            loads = [None] * planes
            scatters = [None] * planes
            loads[0] = pltpu.async_copy(h_hbm.at[0, tok], rows_v.at[0], load_sem.at[0])
            for j in range(planes):
                loads[j].wait()
                if j + 1 < planes:
                    if j >= 1:
                        for cp in scatters[j - 1]:
                            cp.wait()
                    loads[j + 1] = pltpu.async_copy(h_hbm.at[j + 1, tok], rows_v.at[(j + 1) % 2],
                                                    load_sem.at[(j + 1) % 2])
                scatters[j] = [pltpu.async_copy(rows_v.at[j % 2], xs_hbm.at[j].at[idx_v.at[k]],
                                                scatter_sem.at[j % 2]) for k in range(TOP_K)]
            for j in range(max(planes - 2, 0), planes):
                for cp in scatters[j]:
                    cp.wait()

    return pl.kernel(
        body,
        out_type=jax.ShapeDtypeStruct((planes, n_rows, lanes), h2p.dtype),
        mesh=mesh,
        scratch_types=[pltpu.VMEM((TOP_K, lanes), jnp.int32),
                       pltpu.VMEM((2, lanes, lanes), h2p.dtype),
                       pltpu.SemaphoreType.DMA((2,)),
                       pltpu.SemaphoreType.DMA((2,))],
        name="dispatch",
    )(h2p, dest3)


def _gather(ybuf, dest3, t):
    planes, _, lanes = ybuf.shape
    n_chunks = dest3.shape[0]
    mesh, num_cores, workers = _sc_mesh_and_workers()
    chunks_per_worker = n_chunks // workers
    nbuf = GATHER_BUFFERS
    lag = nbuf // 2
    assert chunks_per_worker * workers == n_chunks and n_chunks * lanes == t

    def body(y_hbm, d_hbm, yg_hbm, idx_v, rows_v, gather_sem, store_sem):
        wid = _worker_id(num_cores)

        @pl.loop(0, chunks_per_worker)
        def _(c):
            chunk = wid * chunks_per_worker + c
            tok = pl.ds(chunk * lanes, lanes)
            pltpu.sync_copy(d_hbm.at[chunk], idx_v)

            @pl.loop(0, TOP_K)
            def _(k):
                gathers = [None] * planes
                stores = [None] * planes

                def store(j):
                    gathers[j].wait()
                    stores[j] = pltpu.async_copy(rows_v.at[j % nbuf], yg_hbm.at[k, j, tok], store_sem.at[j % nbuf])

                for j in range(planes):
                    if j >= nbuf:
                        stores[j - nbuf].wait()
                    gathers[j] = pltpu.async_copy(y_hbm.at[j].at[idx_v.at[k]], rows_v.at[j % nbuf],
                                                  gather_sem.at[j % nbuf])
                    if j >= lag:
                        store(j - lag)
                for j in range(max(planes - lag, 0), planes):
                    store(j)
                for j in range(max(planes - nbuf, 0), planes):
                    stores[j].wait()

    return pl.kernel(
        body,
        out_type=jax.ShapeDtypeStruct((TOP_K, planes, t, lanes), ybuf.dtype),
        mesh=mesh,
        scratch_types=[pltpu.VMEM((TOP_K, lanes), jnp.int32),
                       pltpu.VMEM((nbuf, lanes, lanes), ybuf.dtype),
                       pltpu.SemaphoreType.DMA((nbuf,)),
                       pltpu.SemaphoreType.DMA((nbuf,))],
        name="gather",
    )(ybuf, dest3)


def _experts_kernel(be_ref, nv_ref, nu_ref, em_ref, x_ref, wg_hbm, wu_hbm, wd_hbm, o_ref,
                    wgs_ref, wus_ref, wds_ref, wgb_ref, wub_ref, wdb_ref, sems):
    planes, step_rows, lanes = x_ref.shape
    bm = EXPERT_ROWS
    blocks_per_step = step_rows // bm
    step = pl.program_id(0)
    last_step = (nu_ref[0] - 1) // blocks_per_step

    def weight_copies(expert):
        slot = em_ref[2, expert]
        return [pltpu.make_async_copy(src.at[expert], dst.at[slot], sems.at[slot, n])
                for n, (src, dst) in enumerate(((wg_hbm, wgs_ref), (wu_hbm, wus_ref), (wd_hbm, wds_ref)))]

    def start_if_other(expert, current):
        @pl.when(expert != current)
        def _():
            for cp in weight_copies(expert):
                cp.start()

    def swiglu_rows(row0, m, n_valid):
        rows = pl.ds(row0, m)
        valid = lax.broadcasted_iota(jnp.int32, (m, lanes), 0) < n_valid
        g = None
        u = None
        for j in range(planes):
            xj = _unpack_pairs_f32(jnp.where(valid, x_ref[j, rows, :], jnp.uint32(0))).astype(BF16)
            wrows = pl.ds(2 * lanes * j, 2 * lanes)
            gj = jnp.dot(xj, wgb_ref[wrows, :], preferred_element_type=F32)
            uj = jnp.dot(xj, wub_ref[wrows, :], preferred_element_type=F32)
            g = gj if g is None else g + gj
            u = uj if u is None else u + uj
        out = jnp.dot((_silu(g) * u).astype(BF16), wdb_ref[...], preferred_element_type=F32)
        for j in range(planes):
            o_ref[j, rows, :] = _pack_bf16_pairs(out[:, 2 * lanes * j:2 * lanes * (j + 1)])

    def block_step(sb):
        i = step * blocks_per_step + sb
        row0 = pl.multiple_of(sb * bm, bm)
        rows = pl.ds(row0, bm)
        active = i < nu_ref[0]
        e = be_ref[i]
        new_expert = jnp.logical_or(i == 0, e != be_ref[jnp.maximum(i - 1, 0)])
        nxt_i = jnp.minimum(i + 1, be_ref.shape[0] - 1)
        pair = jnp.logical_and(jnp.logical_and(active, sb + 1 < blocks_per_step),
                               jnp.logical_and(i + 1 < nu_ref[0], be_ref[nxt_i] == e))

        @pl.when(jnp.logical_and(active, new_expert))
        def _():
            @pl.when(i == 0)
            def _():
                for cp in weight_copies(e):
                    cp.start()
                start_if_other(em_ref[0, e], e)

            for cp in weight_copies(e):
                cp.wait()
            slot = em_ref[2, e]
            wgb_ref[...] = wgs_ref[slot].astype(BF16)
            wub_ref[...] = wus_ref[slot].astype(BF16)
            wdb_ref[...] = wds_ref[slot].astype(BF16)
            start_if_other(em_ref[1, e], e)

        @pl.when(pair)
        def _():
            swiglu_rows(row0, 2 * bm, bm + nv_ref[nxt_i])

        @pl.when(jnp.logical_and(active, jnp.logical_not(pair)))
        def _():
            swiglu_rows(row0, bm, nv_ref[i])

        @pl.when(jnp.logical_and(jnp.logical_not(active), step == last_step))
        def _():
            for j in range(planes):
                o_ref[j, rows, :] = jnp.zeros((bm, lanes), o_ref.dtype)

        return sb + jnp.where(pair, 2, 1)

    lax.while_loop(lambda sb: sb < blocks_per_step, block_step, jnp.int32(0))


def _experts(block_e, n_valid, n_used, expert_meta, xs, wg, wu, wd):
    planes, p, lanes = xs.shape
    e, d, f = wg.shape
    step_rows = EXPERT_ROWS * EXPERT_BLOCKS_PER_STEP
    assert p % step_rows == 0

    def row_map(i, be, nv, nu, nxt):
        return (0, jnp.minimum(i, (nu[0] - 1) // EXPERT_BLOCKS_PER_STEP), 0)

    return pl.pallas_call(
        _experts_kernel,
        grid_spec=pltpu.PrefetchScalarGridSpec(
            num_scalar_prefetch=4,
            grid=(p // step_rows,),
            in_specs=[pl.BlockSpec((planes, step_rows, lanes), row_map),
                      pl.BlockSpec(memory_space=pl.ANY),
                      pl.BlockSpec(memory_space=pl.ANY),
                      pl.BlockSpec(memory_space=pl.ANY)],
            out_specs=pl.BlockSpec((planes, step_rows, lanes), row_map),
            scratch_shapes=[pltpu.VMEM((2, d, f), F32), pltpu.VMEM((2, d, f), F32), pltpu.VMEM((2, f, d), F32),
                            pltpu.VMEM((d, f), BF16), pltpu.VMEM((d, f), BF16), pltpu.VMEM((f, d), BF16),
                            pltpu.SemaphoreType.DMA((2, 3))]),
        out_shape=jax.ShapeDtypeStruct((planes, p, lanes), jnp.uint32),
        compiler_params=pltpu.CompilerParams(dimension_semantics=("arbitrary",),
                                             vmem_limit_bytes=VMEM_LIMIT_BYTES),
        name="experts",
    )(block_e, n_valid, n_used, expert_meta, xs, wg, wu, wd)


def _combine_kernel(w_ref, x2_ref, g2_ref, gfin_ref, yg_ref, *maybe_prev_and_out):
    o_ref = maybe_prev_and_out[-1]
    w = w_ref[...]
    planes = yg_ref.shape[1]
    routed = None
    for k in range(TOP_K):
        rows = jnp.concatenate([_unpack_pairs_f32(yg_ref[k, j]) for j in range(planes)], axis=1) * w[:, k:k + 1]
        routed = rows if routed is None else routed + rows
    o_ref[0] = _rms(x2_ref[0] + g2_ref[0] * routed) * gfin_ref[...]


def _combine(wtok, x2, g2, gfin, yg, bi, prev_out):
    b, s, d = x2.shape
    _, planes, _, lanes = yg.shape
    tc = COMBINE_TOKENS
    nt = s // tc
    in_specs = [pl.BlockSpec((tc, TOP_K), lambda j: (bi * nt + j, 0)),
                pl.BlockSpec((1, tc, d), lambda j: (bi, j, 0)),
                pl.BlockSpec((1, 1, d), lambda j: (bi, 0, 0)),
                pl.BlockSpec((1, d), lambda j: (0, 0)),
                pl.BlockSpec((TOP_K, planes, tc, lanes), lambda j: (0, 0, j, 0))]
    args = [wtok, x2, g2, gfin, yg]
    aliases = {}
    if prev_out is not None:
        in_specs.append(pl.BlockSpec(memory_space=pl.ANY))
        args.append(prev_out)
        aliases = {len(args) - 1: 0}
    return pl.pallas_call(
        _combine_kernel,
        grid=(nt,),
        in_specs=in_specs,
        out_specs=pl.BlockSpec((1, tc, d), lambda j: (bi, j, 0)),
        out_shape=jax.ShapeDtypeStruct((b, s, d), F32),
        input_output_aliases=aliases,
        compiler_params=pltpu.CompilerParams(dimension_semantics=("arbitrary",),
                                             vmem_limit_bytes=VMEM_LIMIT_BYTES),
        name="combine",
    )(*args)


def kernel(x, c, positions, w_ada, b_ada, norm_mix_g, w_in, ret_norm_g, conv_w, w_br_ret, w_br_conv, w_out,
           norm_ffn_g, w_router, router_bias, w_exp_gate, w_exp_up, w_exp_down, w_sh_gate, w_sh_up, w_sh_down,
           norm_final_g):
    b, s, d = x.shape
    t = b * s
    depth = w_in.shape[0]
    assert depth == 1, "the combine kernel applies the final norm, so exactly one layer is supported"
    posf = positions.astype(F32)[:, :, None]
    c8 = jnp.zeros((8, d), F32).at[:b].set(c)
    bm = EXPERT_ROWS
    n_blocks = t * TOP_K // bm + N_EXPERTS
    n_rows = n_blocks * bm

    for l in range(depth):
        mod = _ada(c8, w_ada[l], b_ada[l][None, :])[:b].reshape(b, 6, d)
        x2, h2p, lgt = _mixer(
            x, posf, mod, norm_mix_g[l][None, :], w_in[l].astype(BF16), ret_norm_g[l][None, :], conv_w[l],
            w_br_ret[l].astype(BF16), w_br_conv[l].astype(BF16), w_out[l].astype(BF16), norm_ffn_g[l][None, :],
            w_router[l].T, jnp.concatenate([w_sh_gate[l], w_sh_up[l]], axis=1).astype(BF16),
            w_sh_down[l].astype(BF16))
        eidx, wts, rank, counts = _route(lgt, router_bias[l][:, None])

        cnt = counts[:, 0].astype(jnp.int32)
        pcnt = (cnt + bm - 1) // bm * bm
        pend = jnp.cumsum(pcnt)
        pstart = pend - pcnt
        n_used = jnp.maximum(pend[-1] // bm, 1).astype(jnp.int32)[None]
        block_start = jnp.arange(n_blocks, dtype=jnp.int32) * bm
        block_e = jnp.minimum(jnp.sum((pend[None, :] <= block_start[:, None]).astype(jnp.int32), axis=1),
                              N_EXPERTS - 1)
        n_valid = jnp.clip(pstart[block_e] + cnt[block_e] - block_start, 0, bm).astype(jnp.int32)
        eid = jnp.arange(N_EXPERTS, dtype=jnp.int32)
        later_used = jnp.logical_and(eid[None, :] > eid[:, None], cnt[None, :] > 0)
        next_expert = jnp.min(jnp.where(later_used, eid[None, :], N_EXPERTS), axis=1)
        next_expert = jnp.where(next_expert == N_EXPERTS, eid, next_expert)
        after_next = next_expert[next_expert]
        after_next = jnp.where(after_next == next_expert, eid, after_next)
        stage_slot = (jnp.cumsum((cnt > 0).astype(jnp.int32)) - 1) % 2
        expert_meta = jnp.stack([next_expert, after_next, stage_slot]).astype(jnp.int32)

        dest = _dest(eidx, rank, pstart[:, None])
        dest3 = dest.reshape(TOP_K, t // LANES, LANES).transpose(1, 0, 2)
        xs = _dispatch(h2p, dest3, n_rows)
        ybuf = _experts(block_e, n_valid, n_used, expert_meta, xs, w_exp_gate[l], w_exp_up[l], w_exp_down[l])
        chunks = s // LANES
        out = None
        for bi in range(b):
            yg = _gather(ybuf, dest3[bi * chunks:(bi + 1) * chunks], s)
            out = _combine(wts.T, x2, mod[:, 5:6, :], norm_final_g[None, :], yg, bi, out)
        x = out
    return x
```

```python
import functools

import numpy as np
import jax
import jax.numpy as jnp
from jax import lax
from jax.experimental import pallas as pl
from jax.experimental.pallas import tpu as pltpu
from jax.experimental.pallas import tpu_sc as plsc

RET_HEADS = 4
RET_DK = 128
RET_DV = 256
RET_CHUNK = 128
ROPE_THETA = 10000.0
CONV_K = 3
N_EXPERTS = 256
TOP_K = 8
N_GROUPS = 8
TOPK_GROUPS = 4
GROUP_SIZE = N_EXPERTS // N_GROUPS
ROUTED_SCALE = 2.5
NORM_EPS = 1e-6

MIXER_TOKENS = 512
DEST_TOKENS = 512
EXPERT_ROWS = 256
EXPERT_BLOCKS_PER_STEP = 8
COMBINE_TOKENS = 256
LANES = 128
GATHER_BUFFERS = 4

VMEM_LIMIT_BYTES = 56 * 1024 * 1024

F32 = jnp.float32
BF16 = jnp.bfloat16
HIGHEST = lax.Precision.HIGHEST


def _sigmoid(v):
    return 1.0 / (1.0 + jnp.exp(-v))


def _silu(v):
    return v * _sigmoid(v)


def _rms(v):
    return v * lax.rsqrt(jnp.mean(v * v, axis=-1, keepdims=True) + NORM_EPS)


def _resident(shape):
    nd = len(shape)
    return pl.BlockSpec(shape, lambda *_: (0,) * nd, pipeline_mode=pl.Buffered(1))


def _ada_kernel(c_ref, w_ref, b_ref, o_ref):
    c = c_ref[...]
    o_ref[...] = jnp.dot(_silu(c), w_ref[...], precision=HIGHEST, preferred_element_type=F32) + b_ref[...]


def _ada(c8, w, b):
    d, n = w.shape
    tn = 1024
    return pl.pallas_call(
        _ada_kernel,
        grid=(n // tn,),
        in_specs=[pl.BlockSpec((8, d), lambda j: (0, 0)),
                  pl.BlockSpec((d, tn), lambda j: (0, j)),
                  pl.BlockSpec((1, tn), lambda j: (0, j))],
        out_specs=pl.BlockSpec((8, tn), lambda j: (0, j)),
        out_shape=jax.ShapeDtypeStruct((8, n), F32),
        name="ada",
    )(c8, w, b)


def _retention_constants():
    c = RET_CHUNK
    log_g = jnp.log1p(-(2.0 ** (-5.0 - jnp.arange(RET_HEADS, dtype=F32))))
    idx = jnp.arange(c, dtype=F32)
    diff = idx[:, None] - idx[None, :]
    intra = jnp.where(diff >= 0, jnp.exp(log_g[:, None, None] * jnp.maximum(diff, 0.0)), 0.0)
    k_decay = jnp.exp(log_g[:, None] * (c - 1 - idx))
    q_decay = jnp.exp(log_g[:, None] * (idx + 1.0))
    chunk_decay = jnp.exp(log_g * c)
    kd = jnp.broadcast_to(k_decay[:, :, None], (RET_HEADS, c, RET_DK))
    qd = jnp.broadcast_to(q_decay[:, :, None], (RET_HEADS, c, RET_DK))
    cd = jnp.broadcast_to(chunk_decay[:, None, None], (RET_HEADS, 1, RET_DV))
    inv_freq = ROPE_THETA ** (-jnp.arange(0, RET_DK, 2, dtype=F32) / RET_DK)
    inv_freq = jnp.concatenate([inv_freq, inv_freq])[None, :]
    sign = jnp.concatenate([-jnp.ones((RET_DK // 2,), F32), jnp.ones((RET_DK // 2,), F32)])[None, :]
    return intra, qd, kd, cd, inv_freq, sign


def _pack_bf16_pairs(v):
    w = v.shape[1] // 2
    lo = lax.bitcast_convert_type(v[:, :w].astype(BF16).astype(F32), jnp.uint32)
    hi = lax.bitcast_convert_type(v[:, w:].astype(BF16).astype(F32), jnp.uint32)
    return (lo >> 16) | (hi & jnp.uint32(0xFFFF0000))


def _unpack_pairs_f32(word):
    lo = lax.bitcast_convert_type(word << 16, F32)
    hi = lax.bitcast_convert_type(word & jnp.uint32(0xFFFF0000), F32)
    return jnp.concatenate([lo, hi], axis=1)


def _mix_tile(x_ref, pos_ref, mod_ref, gmix_ref, win_ref, invf_ref, sign_ref, intra_ref, qd_ref, kd_ref, cd_ref,
              retg_ref, convw_ref, wbr_ref, wbc_ref, wout_ref, gffn_ref, wrh_ref, wrl_ref, wshgu_ref, wshd_ref,
              x2_ref, h2p_ref, lg_ref, state_ref, carry_ref, q_ref, k_ref, v_ref, ret_ref, *, background):
    def tick():
        next(background, None)

    tm, d = x_ref.shape[1], x_ref.shape[2]
    q_w = RET_HEADS * RET_DK
    v_w = RET_HEADS * RET_DV
    offs = np.cumsum([0, q_w, q_w, v_w, v_w, d, d, d, d, d])

    x = x_ref[0]
    mod = mod_ref[0]
    sh1, sc1, g1, sh2, sc2, g2 = [mod[i:i + 1] for i in range(6)]
    hb = ((_rms(x) * gmix_ref[...]) * (1.0 + sc1) + sh1).astype(BF16)

    def proj(i):
        return jnp.dot(hb, win_ref[:, offs[i]:offs[i + 1]], preferred_element_type=F32)

    ang = pos_ref[0] * invf_ref[...]
    cosv = jnp.cos(ang)
    sinv = jnp.sin(ang) * sign_ref[...]

    def rope(t):
        return jnp.concatenate(
            [t[:, h * RET_DK:(h + 1) * RET_DK] * cosv
             + pltpu.roll(t[:, h * RET_DK:(h + 1) * RET_DK], RET_DK // 2, 1) * sinv
             for h in range(RET_HEADS)], axis=1)

    q_ref[...] = rope(proj(0)) * (RET_DK ** -0.5)
    tick()
    k_ref[...] = rope(proj(1))
    tick()
    v_ref[...] = proj(2).astype(BF16)
    tick()

    for c in range(tm // RET_CHUNK):
        rows = pl.ds(c * RET_CHUNK, RET_CHUNK)
        for h in range(RET_HEADS):
            qh = q_ref[rows, h * RET_DK:(h + 1) * RET_DK]
            kh = k_ref[rows, h * RET_DK:(h + 1) * RET_DK]
            vh = v_ref[rows, h * RET_DV:(h + 1) * RET_DV]
            scores = lax.dot_general(qh.astype(BF16), kh.astype(BF16), (((1,), (1,)), ((), ())),
                                     preferred_element_type=F32) * intra_ref[h]
            inner = jnp.dot(scores.astype(BF16), vh, preferred_element_type=F32)
            st = state_ref[h]
            cross = jnp.dot((qh * qd_ref[h]).astype(BF16), st.astype(BF16), preferred_element_type=F32)
            kv = lax.dot_general((kh * kd_ref[h]).astype(BF16), vh, (((0,), (0,)), ((), ())),
                                 preferred_element_type=F32)
            state_ref[h] = st * cd_ref[h] + kv
            ret_ref[rows, h * RET_DV:(h + 1) * RET_DV] = _rms(inner + cross)
        tick()

    y_ret = jnp.dot((_silu(proj(3)) * (ret_ref[...] * retg_ref[...])).astype(BF16), wbr_ref[...],
                    preferred_element_type=F32)
    tick()

    z = proj(5) * proj(4)
    row = lax.broadcasted_iota(jnp.int32, z.shape, 0)
    prev1 = carry_ref[7:8, :]
    prev2 = carry_ref[6:7, :]
    z1 = jnp.where(row == 0, prev1, pltpu.roll(z, 1, 0))
    z2 = jnp.where(row == 0, prev2, jnp.where(row == 1, prev1, pltpu.roll(z, 2, 0)))
    conv = convw_ref[0:1, :] * z2 + convw_ref[1:2, :] * z1 + convw_ref[2:3, :] * z
    carry_ref[...] = z[tm - 8:tm, :]
    tick()
    y_conv = jnp.dot((proj(6) * conv).astype(BF16), wbc_ref[...], preferred_element_type=F32)
    tick()

    mix = _sigmoid(proj(7)) * y_ret + _sigmoid(proj(8)) * y_conv
    tick()
    x1 = x + g1 * jnp.dot(mix.astype(BF16), wout_ref[...], preferred_element_type=F32)
    tick()

    h2 = (_rms(x1) * gffn_ref[...]) * (1.0 + sc2) + sh2
    h2_hi = h2.astype(BF16)
    h2_lo = (h2 - h2_hi.astype(F32)).astype(BF16)

    def nt_dot(a, b):
        return lax.dot_general(a, b, (((1,), (1,)), ((), ())), preferred_element_type=F32)

    lg_ref[...] = nt_dot(wrh_ref[...], h2_hi) + (nt_dot(wrh_ref[...], h2_lo) + nt_dot(wrl_ref[...], h2_hi))
    gu = jnp.dot(h2_hi, wshgu_ref[...], preferred_element_type=F32)
    f = gu.shape[1] // 2
    shared = jnp.dot((_silu(gu[:, :f]) * gu[:, f:]).astype(BF16), wshd_ref[...], preferred_element_type=F32)
    x2_ref[0] = x1 + g2 * shared
    for j in range(h2p_ref.shape[0]):
        h2p_ref[j] = _pack_bf16_pairs(h2[:, 2 * LANES * j:2 * LANES * (j + 1)])
    for _ in background:
        pass


def _mixer_kernel(*refs, tiles_per_seq):
    mix_inputs, (bias_ref, tri_ref) = refs[:21], refs[21:23]
    x2_ref, h2p_ref, eidx_ref, w_ref, rank_ref, cnt_ref = refs[23:29]
    state_ref, carry_ref, q_ref, k_ref, v_ref, ret_ref, lg_ref, rcarry_ref = refs[29:]
    s = pl.program_id(0)
    n_tiles = pl.num_programs(0) - 1

    @pl.when(s == 0)
    def _():
        lg_ref[...] = jnp.zeros_like(lg_ref)
        rcarry_ref[...] = jnp.zeros_like(rcarry_ref)

    @pl.when(s % tiles_per_seq == 0)
    def _():
        state_ref[...] = jnp.zeros_like(state_ref)
        carry_ref[...] = jnp.zeros_like(carry_ref)

    def route_previous_tile():
        return _route_phases(lg_ref.at[(s + 1) % 2], bias_ref, tri_ref, eidx_ref, w_ref, rank_ref, cnt_ref,
                             rcarry_ref, (s > 0).astype(jnp.int32))

    @pl.when(s == n_tiles)
    def _():
        for _ in route_previous_tile():
            pass

    @pl.when(s < n_tiles)
    def _():
        _mix_tile(*mix_inputs, x2_ref, h2p_ref, lg_ref.at[s % 2], state_ref, carry_ref, q_ref, k_ref, v_ref,
                  ret_ref, background=route_previous_tile())


def _mixer(x, posf, mod, gmix, win, retg, convw, wbr, wbc, wout, gffn, wr, wshgu, wshd, bias):
    b, s, d = x.shape
    tm = MIXER_TOKENS
    nt = s // tm
    n_tiles = b * nt
    intra, qd, kd, cd, invf, sign = _retention_constants()
    q_w, v_w = RET_HEADS * RET_DK, RET_HEADS * RET_DV
    wr_hi = wr.astype(BF16)
    wr_lo = (wr - wr_hi.astype(F32)).astype(BF16)
    tri = jnp.asarray(np.triu(np.ones((tm, tm), np.float32), 1), BF16)
    weights = (gmix, win, invf, sign, intra, qd, kd, cd, retg, convw, wbr, wbc, wout, gffn, wr_hi, wr_lo,
               wshgu, wshd, bias, tri)

    def tile(i):
        return jnp.minimum(i, n_tiles - 1)

    def routed(i):
        return (0, jnp.maximum(i - 1, 0))

    return pl.pallas_call(
        functools.partial(_mixer_kernel, tiles_per_seq=nt),
        grid=(n_tiles + 1,),
        in_specs=[pl.BlockSpec((1, tm, d), lambda i: (tile(i) // nt, tile(i) % nt, 0)),
                  pl.BlockSpec((1, tm, 1), lambda i: (tile(i) // nt, tile(i) % nt, 0)),
                  pl.BlockSpec((1, 6, d), lambda i: (tile(i) // nt, 0, 0))]
                 + [_resident(w.shape) for w in weights],
        out_specs=[pl.BlockSpec((1, tm, d), lambda i: (tile(i) // nt, tile(i) % nt, 0)),
                   pl.BlockSpec((d // (2 * LANES), tm, LANES), lambda i: (0, tile(i), 0)),
                   pl.BlockSpec((TOP_K, tm), routed),
                   pl.BlockSpec((TOP_K, tm), routed),
                   pl.BlockSpec((TOP_K, tm), routed),
                   pl.BlockSpec((N_EXPERTS, 1), lambda i: (0, 0))],
        out_shape=[jax.ShapeDtypeStruct((b, s, d), F32),
                   jax.ShapeDtypeStruct((d // (2 * LANES), b * s, LANES), jnp.uint32),
                   jax.ShapeDtypeStruct((TOP_K, b * s), jnp.int32),
                   jax.ShapeDtypeStruct((TOP_K, b * s), F32),
                   jax.ShapeDtypeStruct((TOP_K, b * s), jnp.int32),
                   jax.ShapeDtypeStruct((N_EXPERTS, 1), F32)],
        scratch_shapes=[pltpu.VMEM((RET_HEADS, RET_DK, RET_DV), F32),
                        pltpu.VMEM((8, d), F32),
                        pltpu.VMEM((tm, q_w), F32),
                        pltpu.VMEM((tm, q_w), F32),
                        pltpu.VMEM((tm, v_w), BF16),
                        pltpu.VMEM((tm, v_w), F32),
                        pltpu.VMEM((2, N_EXPERTS, tm), F32),
                        pltpu.VMEM((N_EXPERTS, 1), F32)],
        compiler_params=pltpu.CompilerParams(dimension_semantics=("arbitrary",),
                                             vmem_limit_bytes=VMEM_LIMIT_BYTES),
        name="mixer",
    )(x, posf, mod, *weights)


def _first_argmax(v, idx, n):
    m = jnp.max(v, axis=0, keepdims=True)
    return m, jnp.min(jnp.where(v == m, idx, n), axis=0, keepdims=True)


def _route_phases(logits_ref, bias_ref, tri_ref, eidx_ref, w_ref, rank_ref, cnt_ref, carry_ref, count_it):
    tn = logits_ref.shape[1]
    neg = F32(-jnp.inf)
    score = _sigmoid(logits_ref[...])
    choice = score + bias_ref[...]

    grow = lax.broadcasted_iota(jnp.int32, (GROUP_SIZE, tn), 0)
    gscores = []
    for g in range(N_GROUPS):
        cg = choice[g * GROUP_SIZE:(g + 1) * GROUP_SIZE]
        m1, i1 = _first_argmax(cg, grow, GROUP_SIZE)
        m2 = jnp.max(jnp.where(grow == i1, neg, cg), axis=0, keepdims=True)
        gscores.append(m1 + m2)
    cur = jnp.concatenate(gscores, axis=0)
    yield
    gidx = lax.broadcasted_iota(jnp.int32, (N_GROUPS, tn), 0)
    keep = jnp.zeros((N_GROUPS, tn), F32)
    for _ in range(TOPK_GROUPS):
        _, ig = _first_argmax(cur, gidx, N_GROUPS)
        hit = gidx == ig
        keep = jnp.where(hit, 1.0, keep)
        cur = jnp.where(hit, neg, cur)
    cur = jnp.concatenate(
        [jnp.where(keep[g:g + 1] > 0.0, choice[g * GROUP_SIZE:(g + 1) * GROUP_SIZE], neg)
         for g in range(N_GROUPS)], axis=0)
    yield

    erow = lax.broadcasted_iota(jnp.int32, (N_EXPERTS, tn), 0)
    eidx, wts = [], []
    member = jnp.zeros((N_EXPERTS, tn), F32)
    for _ in range(TOP_K):
        _, ie = _first_argmax(cur, erow, N_EXPERTS)
        hit = erow == ie
        eidx.append(ie)
        wts.append(jnp.sum(jnp.where(hit, score, 0.0), axis=0, keepdims=True))
        member = member + hit.astype(F32)
        cur = jnp.where(hit, neg, cur)
        yield
    wsum = wts[0]
    for k in range(1, TOP_K):
        wsum = wsum + wts[k]

    before = jnp.dot(member.astype(BF16), tri_ref[...], preferred_element_type=F32) + carry_ref[...]
    ranks = []
    for k in range(TOP_K):
        ranks.append(jnp.sum(jnp.where(erow == eidx[k], before, 0.0), axis=0, keepdims=True))
        if k % 2 == 1:
            yield
    carry_ref[...] = carry_ref[...] + count_it.astype(F32) * jnp.sum(member, axis=1, keepdims=True)

    eidx_ref[...] = jnp.concatenate(eidx, axis=0)
    w_ref[...] = jnp.concatenate([w / wsum * ROUTED_SCALE for w in wts], axis=0)
    rank_ref[...] = jnp.concatenate(ranks, axis=0).astype(jnp.int32)
    cnt_ref[...] = carry_ref[...]


def _dest_kernel(eidx_ref, rank_ref, pstart_ref, dest_ref):
    tn = eidx_ref.shape[1]
    erow = lax.broadcasted_iota(jnp.int32, (N_EXPERTS, tn), 0)
    start = pstart_ref[...]
    dest_ref[...] = rank_ref[...] + jnp.concatenate(
        [jnp.sum(jnp.where(erow == eidx_ref[k:k + 1, :], start, 0), axis=0, keepdims=True) for k in range(TOP_K)],
        axis=0)


def _dest(eidx, rank, pstart):
    k, t = eidx.shape
    tn = DEST_TOKENS
    return pl.pallas_call(
        _dest_kernel,
        grid=(t // tn,),
        in_specs=[pl.BlockSpec((k, tn), lambda i: (0, i)),
                  pl.BlockSpec((k, tn), lambda i: (0, i)),
                  pl.BlockSpec((N_EXPERTS, 1), lambda i: (0, 0))],
        out_specs=pl.BlockSpec((k, tn), lambda i: (0, i)),
        out_shape=jax.ShapeDtypeStruct((k, t), jnp.int32),
        name="dest",
    )(eidx, rank, pstart)


def _sc_mesh_and_workers():
    mesh = plsc.VectorSubcoreMesh(core_axis_name="c", subcore_axis_name="s")
    return mesh, mesh.num_cores, mesh.num_cores * mesh.num_subcores


def _worker_id(num_cores):
    return lax.axis_index("s") * num_cores + lax.axis_index("c")


def _dispatch(h2p, dest3, n_rows):
    planes, t, lanes = h2p.shape
    n_chunks = dest3.shape[0]
    mesh, num_cores, workers = _sc_mesh_and_workers()
    chunks_per_worker = n_chunks // workers
    assert chunks_per_worker * workers == n_chunks and n_chunks * lanes == t

    def body(h_hbm, d_hbm, xs_hbm, idx_v, rows_v, load_sem, scatter_sem):
        wid = _worker_id(num_cores)

        @pl.loop(0, chunks_per_worker)
        def _(c):
            chunk = wid * chunks_per_worker + c
            tok = pl.ds(chunk * lanes, lanes)
            pltpu.sync_copy(d_hbm.at[chunk], idx_v)
            loads = [None] * planes
            scatters = [None] * planes
            loads[0] = pltpu.async_copy(h_hbm.at[0, tok], rows_v.at[0], load_sem.at[0])
            for j in range(planes):
                loads[j].wait()
                if j + 1 < planes:
                    if j >= 1:
                        for cp in scatters[j - 1]:
                            cp.wait()
                    loads[j + 1] = pltpu.async_copy(h_hbm.at[j + 1, tok], rows_v.at[(j + 1) % 2],
                                                    load_sem.at[(j + 1) % 2])
                scatters[j] = [pltpu.async_copy(rows_v.at[j % 2], xs_hbm.at[j].at[idx_v.at[k]],
                                                scatter_sem.at[j % 2]) for k in range(TOP_K)]
            for j in range(max(planes - 2, 0), planes):
                for cp in scatters[j]:
                    cp.wait()

    return pl.kernel(
        body,
        out_type=jax.ShapeDtypeStruct((planes, n_rows, lanes), h2p.dtype),
        mesh=mesh,
        scratch_types=[pltpu.VMEM((TOP_K, lanes), jnp.int32),
                       pltpu.VMEM((2, lanes, lanes), h2p.dtype),
                       pltpu.SemaphoreType.DMA((2,)),
                       pltpu.SemaphoreType.DMA((2,))],
        name="dispatch",
    )(h2p, dest3)


def _gather(ybuf, dest3, t):
    planes, _, lanes = ybuf.shape
    n_chunks = dest3.shape[0]
    mesh, num_cores, workers = _sc_mesh_and_workers()
    chunks_per_worker = n_chunks // workers
    nbuf = GATHER_BUFFERS
    lag = nbuf // 2
    assert chunks_per_worker * workers == n_chunks and n_chunks * lanes == t

    def body(y_hbm, d_hbm, yg_hbm, idx_v, rows_v, gather_sem, store_sem):
        wid = _worker_id(num_cores)

        @pl.loop(0, chunks_per_worker)
        def _(c):
            chunk = wid * chunks_per_worker + c
            tok = pl.ds(chunk * lanes, lanes)
            pltpu.sync_copy(d_hbm.at[chunk], idx_v)

            @pl.loop(0, TOP_K)
            def _(k):
                gathers = [None] * planes
                stores = [None] * planes

                def store(j):
                    gathers[j].wait()
                    stores[j] = pltpu.async_copy(rows_v.at[j % nbuf], yg_hbm.at[k, j, tok], store_sem.at[j % nbuf])

                for j in range(planes):
                    if j >= nbuf:
                        stores[j - nbuf].wait()
                    gathers[j] = pltpu.async_copy(y_hbm.at[j].at[idx_v.at[k]], rows_v.at[j % nbuf],
                                                  gather_sem.at[j % nbuf])
                    if j >= lag:
                        store(j - lag)
                for j in range(max(planes - lag, 0), planes):
                    store(j)
                for j in range(max(planes - nbuf, 0), planes):
                    stores[j].wait()

    return pl.kernel(
        body,
        out_type=jax.ShapeDtypeStruct((TOP_K, planes, t, lanes), ybuf.dtype),
        mesh=mesh,
        scratch_types=[pltpu.VMEM((TOP_K, lanes), jnp.int32),
                       pltpu.VMEM((nbuf, lanes, lanes), ybuf.dtype),
                       pltpu.SemaphoreType.DMA((nbuf,)),
                       pltpu.SemaphoreType.DMA((nbuf,))],
        name="gather",
    )(ybuf, dest3)


def _experts_kernel(be_ref, nv_ref, nu_ref, em_ref, x_ref, wg_hbm, wu_hbm, wd_hbm, o_ref,
                    wgs_ref, wus_ref, wds_ref, wgb_ref, wub_ref, wdb_ref, sems):
    planes, step_rows, lanes = x_ref.shape
    bm = EXPERT_ROWS
    blocks_per_step = step_rows // bm
    step = pl.program_id(0)
    last_step = (nu_ref[0] - 1) // blocks_per_step

    def weight_copies(expert):
        slot = em_ref[2, expert]
        return [pltpu.make_async_copy(src.at[expert], dst.at[slot], sems.at[slot, n])
                for n, (src, dst) in enumerate(((wg_hbm, wgs_ref), (wu_hbm, wus_ref), (wd_hbm, wds_ref)))]

    def start_if_other(expert, current):
        @pl.when(expert != current)
        def _():
            for cp in weight_copies(expert):
                cp.start()

    def swiglu_rows(row0, m, n_valid):
        rows = pl.ds(row0, m)
        valid = lax.broadcasted_iota(jnp.int32, (m, lanes), 0) < n_valid
        g = None
        u = None
        for j in range(planes):
            xj = _unpack_pairs_f32(jnp.where(valid, x_ref[j, rows, :], jnp.uint32(0))).astype(BF16)
            wrows = pl.ds(2 * lanes * j, 2 * lanes)
            gj = jnp.dot(xj, wgb_ref[wrows, :], preferred_element_type=F32)
            uj = jnp.dot(xj, wub_ref[wrows, :], preferred_element_type=F32)
            g = gj if g is None else g + gj
            u = uj if u is None else u + uj
        out = jnp.dot((_silu(g) * u).astype(BF16), wdb_ref[...], preferred_element_type=F32)
        for j in range(planes):
            o_ref[j, rows, :] = _pack_bf16_pairs(out[:, 2 * lanes * j:2 * lanes * (j + 1)])

    def block_step(sb):
        i = step * blocks_per_step + sb
        row0 = pl.multiple_of(sb * bm, bm)
        rows = pl.ds(row0, bm)
        active = i < nu_ref[0]
        e = be_ref[i]
        new_expert = jnp.logical_or(i == 0, e != be_ref[jnp.maximum(i - 1, 0)])
        nxt_i = jnp.minimum(i + 1, be_ref.shape[0] - 1)
        pair = jnp.logical_and(jnp.logical_and(active, sb + 1 < blocks_per_step),
                               jnp.logical_and(i + 1 < nu_ref[0], be_ref[nxt_i] == e))

        @pl.when(jnp.logical_and(active, new_expert))
        def _():
            @pl.when(i == 0)
            def _():
                for cp in weight_copies(e):
                    cp.start()
                start_if_other(em_ref[0, e], e)

            for cp in weight_copies(e):
                cp.wait()
            slot = em_ref[2, e]
            wgb_ref[...] = wgs_ref[slot].astype(BF16)
            wub_ref[...] = wus_ref[slot].astype(BF16)
            wdb_ref[...] = wds_ref[slot].astype(BF16)
            start_if_other(em_ref[1, e], e)

        @pl.when(pair)
        def _():
            swiglu_rows(row0, 2 * bm, bm + nv_ref[nxt_i])

        @pl.when(jnp.logical_and(active, jnp.logical_not(pair)))
        def _():
            swiglu_rows(row0, bm, nv_ref[i])

        @pl.when(jnp.logical_and(jnp.logical_not(active), step == last_step))
        def _():
            for j in range(planes):
                o_ref[j, rows, :] = jnp.zeros((bm, lanes), o_ref.dtype)

        return sb + jnp.where(pair, 2, 1)

    lax.while_loop(lambda sb: sb < blocks_per_step, block_step, jnp.int32(0))


def _experts(block_e, n_valid, n_used, expert_meta, xs, wg, wu, wd):
    planes, p, lanes = xs.shape
    e, d, f = wg.shape
    step_rows = EXPERT_ROWS * EXPERT_BLOCKS_PER_STEP
    assert p % step_rows == 0

    def row_map(i, be, nv, nu, nxt):
        return (0, jnp.minimum(i, (nu[0] - 1) // EXPERT_BLOCKS_PER_STEP), 0)

    return pl.pallas_call(
        _experts_kernel,
        grid_spec=pltpu.PrefetchScalarGridSpec(
            num_scalar_prefetch=4,
            grid=(p // step_rows,),
            in_specs=[pl.BlockSpec((planes, step_rows, lanes), row_map),
                      pl.BlockSpec(memory_space=pl.ANY),
                      pl.BlockSpec(memory_space=pl.ANY),
                      pl.BlockSpec(memory_space=pl.ANY)],
            out_specs=pl.BlockSpec((planes, step_rows, lanes), row_map),
            scratch_shapes=[pltpu.VMEM((2, d, f), F32), pltpu.VMEM((2, d, f), F32), pltpu.VMEM((2, f, d), F32),
                            pltpu.VMEM((d, f), BF16), pltpu.VMEM((d, f), BF16), pltpu.VMEM((f, d), BF16),
                            pltpu.SemaphoreType.DMA((2, 3))]),
        out_shape=jax.ShapeDtypeStruct((planes, p, lanes), jnp.uint32),
        compiler_params=pltpu.CompilerParams(dimension_semantics=("arbitrary",),
                                             vmem_limit_bytes=VMEM_LIMIT_BYTES),
        name="experts",
    )(block_e, n_valid, n_used, expert_meta, xs, wg, wu, wd)


def _combine_kernel(w_ref, x2_ref, g2_ref, gfin_ref, yg_ref, *maybe_prev_and_out):
    o_ref = maybe_prev_and_out[-1]
    w = w_ref[...]
    planes = yg_ref.shape[1]
    routed = None
    for k in range(TOP_K):
        rows = jnp.concatenate([_unpack_pairs_f32(yg_ref[k, j]) for j in range(planes)], axis=1) * w[:, k:k + 1]
        routed = rows if routed is None else routed + rows
    o_ref[0] = _rms(x2_ref[0] + g2_ref[0] * routed) * gfin_ref[...]


def _combine(wtok, x2, g2, gfin, yg, bi, prev_out):
    b, s, d = x2.shape
    _, planes, _, lanes = yg.shape
    tc = COMBINE_TOKENS
    nt = s // tc
    in_specs = [pl.BlockSpec((tc, TOP_K), lambda j: (bi * nt + j, 0)),
                pl.BlockSpec((1, tc, d), lambda j: (bi, j, 0)),
                pl.BlockSpec((1, 1, d), lambda j: (bi, 0, 0)),
                pl.BlockSpec((1, d), lambda j: (0, 0)),
                pl.BlockSpec((TOP_K, planes, tc, lanes), lambda j: (0, 0, j, 0))]
    args = [wtok, x2, g2, gfin, yg]
    aliases = {}
    if prev_out is not None:
        in_specs.append(pl.BlockSpec(memory_space=pl.ANY))
        args.append(prev_out)
        aliases = {len(args) - 1: 0}
    return pl.pallas_call(
        _combine_kernel,
        grid=(nt,),
        in_specs=in_specs,
        out_specs=pl.BlockSpec((1, tc, d), lambda j: (bi, j, 0)),
        out_shape=jax.ShapeDtypeStruct((b, s, d), F32),
        input_output_aliases=aliases,
        compiler_params=pltpu.CompilerParams(dimension_semantics=("arbitrary",),
                                             vmem_limit_bytes=VMEM_LIMIT_BYTES),
        name="combine",
    )(*args)


def kernel(x, c, positions, w_ada, b_ada, norm_mix_g, w_in, ret_norm_g, conv_w, w_br_ret, w_br_conv, w_out,
           norm_ffn_g, w_router, router_bias, w_exp_gate, w_exp_up, w_exp_down, w_sh_gate, w_sh_up, w_sh_down,
           norm_final_g):
    b, s, d = x.shape
    t = b * s
    depth = w_in.shape[0]
    assert depth == 1, "the combine kernel applies the final norm, so exactly one layer is supported"
    posf = positions.astype(F32)[:, :, None]
    c8 = jnp.zeros((8, d), F32).at[:b].set(c)
    bm = EXPERT_ROWS
    n_blocks = t * TOP_K // bm + N_EXPERTS
    n_rows = n_blocks * bm

    for l in range(depth):
        mod = _ada(c8, w_ada[l], b_ada[l][None, :])[:b].reshape(b, 6, d)
        x2, h2p, eidx, wts, rank, counts = _mixer(
            x, posf, mod, norm_mix_g[l][None, :], w_in[l].astype(BF16), ret_norm_g[l][None, :], conv_w[l],
            w_br_ret[l].astype(BF16), w_br_conv[l].astype(BF16), w_out[l].astype(BF16), norm_ffn_g[l][None, :],
            w_router[l].T, jnp.concatenate([w_sh_gate[l], w_sh_up[l]], axis=1).astype(BF16),
            w_sh_down[l].astype(BF16), router_bias[l][:, None])

        cnt = counts[:, 0].astype(jnp.int32)
        pcnt = (cnt + bm - 1) // bm * bm
        pend = jnp.cumsum(pcnt)
        pstart = pend - pcnt
        n_used = jnp.maximum(pend[-1] // bm, 1).astype(jnp.int32)[None]
        block_start = jnp.arange(n_blocks, dtype=jnp.int32) * bm
        block_e = jnp.minimum(jnp.sum((pend[None, :] <= block_start[:, None]).astype(jnp.int32), axis=1),
                              N_EXPERTS - 1)
        n_valid = jnp.clip(pstart[block_e] + cnt[block_e] - block_start, 0, bm).astype(jnp.int32)
        eid = jnp.arange(N_EXPERTS, dtype=jnp.int32)
        later_used = jnp.logical_and(eid[None, :] > eid[:, None], cnt[None, :] > 0)
        next_expert = jnp.min(jnp.where(later_used, eid[None, :], N_EXPERTS), axis=1)
        next_expert = jnp.where(next_expert == N_EXPERTS, eid, next_expert)
        after_next = next_expert[next_expert]
        after_next = jnp.where(after_next == next_expert, eid, after_next)
        stage_slot = (jnp.cumsum((cnt > 0).astype(jnp.int32)) - 1) % 2
        expert_meta = jnp.stack([next_expert, after_next, stage_slot]).astype(jnp.int32)

        dest = _dest(eidx, rank, pstart[:, None])
        dest3 = dest.reshape(TOP_K, t // LANES, LANES).transpose(1, 0, 2)
        xs = _dispatch(h2p, dest3, n_rows)
        ybuf = _experts(block_e, n_valid, n_used, expert_meta, xs, w_exp_gate[l], w_exp_up[l], w_exp_down[l])
        chunks = s // LANES
        out = None
        for bi in range(b):
            yg = _gather(ybuf, dest3[bi * chunks:(bi + 1) * chunks], s)
            out = _combine(wts.T, x2, mod[:, 5:6, :], norm_final_g[None, :], yg, bi, out)
        x = out
    return x
```

```python
import functools

import numpy as np
import jax
import jax.numpy as jnp
from jax import lax
from jax.experimental import pallas as pl
from jax.experimental.pallas import tpu as pltpu
from jax.experimental.pallas import tpu_sc as plsc

RET_HEADS = 4
RET_DK = 128
RET_DV = 256
RET_CHUNK = 128
ROPE_THETA = 10000.0
CONV_K = 3
N_EXPERTS = 256
TOP_K = 8
N_GROUPS = 8
TOPK_GROUPS = 4
GROUP_SIZE = N_EXPERTS // N_GROUPS
ROUTED_SCALE = 2.5
NORM_EPS = 1e-6

MIXER_TOKENS = 512
DEST_TOKENS = 512
EXPERT_ROWS = 256
EXPERT_BLOCKS_PER_STEP = 8
COMBINE_TOKENS = 256
COMBINE_SEGMENTS = 4
LANES = 128
GATHER_BUFFERS = 4

VMEM_LIMIT_BYTES = 56 * 1024 * 1024

F32 = jnp.float32
BF16 = jnp.bfloat16
HIGHEST = lax.Precision.HIGHEST


def _sigmoid(v):
    return 1.0 / (1.0 + jnp.exp(-v))


def _silu(v):
    return v * _sigmoid(v)


def _rms(v):
    return v * lax.rsqrt(jnp.mean(v * v, axis=-1, keepdims=True) + NORM_EPS)


def _resident(shape):
    nd = len(shape)
    return pl.BlockSpec(shape, lambda *_: (0,) * nd, pipeline_mode=pl.Buffered(1))


def _ada_kernel(c_ref, w_ref, b_ref, o_ref):
    c = c_ref[...]
    o_ref[...] = jnp.dot(_silu(c), w_ref[...], precision=HIGHEST, preferred_element_type=F32) + b_ref[...]


def _ada(c8, w, b):
    d, n = w.shape
    tn = 1024
    return pl.pallas_call(
        _ada_kernel,
        grid=(n // tn,),
        in_specs=[pl.BlockSpec((8, d), lambda j: (0, 0)),
                  pl.BlockSpec((d, tn), lambda j: (0, j)),
                  pl.BlockSpec((1, tn), lambda j: (0, j))],
        out_specs=pl.BlockSpec((8, tn), lambda j: (0, j)),
        out_shape=jax.ShapeDtypeStruct((8, n), F32),
        name="ada",
    )(c8, w, b)


def _retention_constants():
    c = RET_CHUNK
    log_g = jnp.log1p(-(2.0 ** (-5.0 - jnp.arange(RET_HEADS, dtype=F32))))
    idx = jnp.arange(c, dtype=F32)
    diff = idx[:, None] - idx[None, :]
    intra = jnp.where(diff >= 0, jnp.exp(log_g[:, None, None] * jnp.maximum(diff, 0.0)), 0.0)
    k_decay = jnp.exp(log_g[:, None] * (c - 1 - idx))
    q_decay = jnp.exp(log_g[:, None] * (idx + 1.0))
    chunk_decay = jnp.exp(log_g * c)
    kd = jnp.broadcast_to(k_decay[:, :, None], (RET_HEADS, c, RET_DK))
    qd = jnp.broadcast_to(q_decay[:, :, None], (RET_HEADS, c, RET_DK))
    cd = jnp.broadcast_to(chunk_decay[:, None, None], (RET_HEADS, 1, RET_DV))
    inv_freq = ROPE_THETA ** (-jnp.arange(0, RET_DK, 2, dtype=F32) / RET_DK)
    inv_freq = jnp.concatenate([inv_freq, inv_freq])[None, :]
    sign = jnp.concatenate([-jnp.ones((RET_DK // 2,), F32), jnp.ones((RET_DK // 2,), F32)])[None, :]
    return intra, qd, kd, cd, inv_freq, sign


def _pack_bf16_pairs(v):
    w = v.shape[1] // 2
    lo = lax.bitcast_convert_type(v[:, :w].astype(BF16).astype(F32), jnp.uint32)
    hi = lax.bitcast_convert_type(v[:, w:].astype(BF16).astype(F32), jnp.uint32)
    return (lo >> 16) | (hi & jnp.uint32(0xFFFF0000))


def _unpack_pairs_f32(word):
    lo = lax.bitcast_convert_type(word << 16, F32)
    hi = lax.bitcast_convert_type(word & jnp.uint32(0xFFFF0000), F32)
    return jnp.concatenate([lo, hi], axis=1)


def _mix_tile(x_ref, pos_ref, mod_ref, gmix_ref, win_ref, invf_ref, sign_ref, intra_ref, qd_ref, kd_ref, cd_ref,
              retg_ref, convw_ref, wbr_ref, wbc_ref, wout_ref, gffn_ref, wrh_ref, wrl_ref, wshgu_ref, wshd_ref,
              x2_ref, h2p_ref, lg_ref, state_ref, carry_ref, q_ref, k_ref, v_ref, ret_ref, *, background):
    def tick():
        next(background, None)

    tm, d = x_ref.shape[1], x_ref.shape[2]
    q_w = RET_HEADS * RET_DK
    v_w = RET_HEADS * RET_DV
    offs = np.cumsum([0, q_w, q_w, v_w, v_w, d, d, d, d, d])

    x = x_ref[0]
    mod = mod_ref[0]
    sh1, sc1, g1, sh2, sc2, g2 = [mod[i:i + 1] for i in range(6)]
    hb = ((_rms(x) * gmix_ref[...]) * (1.0 + sc1) + sh1).astype(BF16)

    def proj(i):
        return jnp.dot(hb, win_ref[:, offs[i]:offs[i + 1]], preferred_element_type=F32)

    ang = pos_ref[0] * invf_ref[...]
    cosv = jnp.cos(ang)
    sinv = jnp.sin(ang) * sign_ref[...]

    def rope(t):
        return jnp.concatenate(
            [t[:, h * RET_DK:(h + 1) * RET_DK] * cosv
             + pltpu.roll(t[:, h * RET_DK:(h + 1) * RET_DK], RET_DK // 2, 1) * sinv
             for h in range(RET_HEADS)], axis=1)

    q_ref[...] = rope(proj(0)) * (RET_DK ** -0.5)
    tick()
    k_ref[...] = rope(proj(1))
    tick()
    v_ref[...] = proj(2).astype(BF16)
    tick()

    for c in range(tm // RET_CHUNK):
        rows = pl.ds(c * RET_CHUNK, RET_CHUNK)
        for h in range(RET_HEADS):
            qh = q_ref[rows, h * RET_DK:(h + 1) * RET_DK]
            kh = k_ref[rows, h * RET_DK:(h + 1) * RET_DK]
            vh = v_ref[rows, h * RET_DV:(h + 1) * RET_DV]
            scores = lax.dot_general(qh.astype(BF16), kh.astype(BF16), (((1,), (1,)), ((), ())),
                                     preferred_element_type=F32) * intra_ref[h]
            inner = jnp.dot(scores.astype(BF16), vh, preferred_element_type=F32)
            st = state_ref[h]
            cross = jnp.dot((qh * qd_ref[h]).astype(BF16), st.astype(BF16), preferred_element_type=F32)
            kv = lax.dot_general((kh * kd_ref[h]).astype(BF16), vh, (((0,), (0,)), ((), ())),
                                 preferred_element_type=F32)
            state_ref[h] = st * cd_ref[h] + kv
            ret_ref[rows, h * RET_DV:(h + 1) * RET_DV] = _rms(inner + cross)
        tick()

    y_ret = jnp.dot((_silu(proj(3)) * (ret_ref[...] * retg_ref[...])).astype(BF16), wbr_ref[...],
                    preferred_element_type=F32)
    tick()

    z = proj(5) * proj(4)
    row = lax.broadcasted_iota(jnp.int32, z.shape, 0)
    prev1 = carry_ref[7:8, :]
    prev2 = carry_ref[6:7, :]
    z1 = jnp.where(row == 0, prev1, pltpu.roll(z, 1, 0))
    z2 = jnp.where(row == 0, prev2, jnp.where(row == 1, prev1, pltpu.roll(z, 2, 0)))
    conv = convw_ref[0:1, :] * z2 + convw_ref[1:2, :] * z1 + convw_ref[2:3, :] * z
    carry_ref[...] = z[tm - 8:tm, :]
    tick()
    y_conv = jnp.dot((proj(6) * conv).astype(BF16), wbc_ref[...], preferred_element_type=F32)
    tick()

    mix = _sigmoid(proj(7)) * y_ret + _sigmoid(proj(8)) * y_conv
    tick()
    x1 = x + g1 * jnp.dot(mix.astype(BF16), wout_ref[...], preferred_element_type=F32)
    tick()

    h2 = (_rms(x1) * gffn_ref[...]) * (1.0 + sc2) + sh2
    hi_f32 = lax.bitcast_convert_type(lax.bitcast_convert_type(h2, jnp.uint32) & jnp.uint32(0xFFFF0000), F32)
    h2_hi = hi_f32.astype(BF16)
    h2_lo = (h2 - hi_f32).astype(BF16)

    def nt_dot(a, b):
        return lax.dot_general(a, b, (((1,), (1,)), ((), ())), preferred_element_type=F32)

    lg_ref[...] = nt_dot(wrh_ref[...], h2_hi) + (nt_dot(wrh_ref[...], h2_lo) + nt_dot(wrl_ref[...], h2_hi))
    gu = jnp.dot(h2.astype(BF16), wshgu_ref[...], preferred_element_type=F32)
    f = gu.shape[1] // 2
    shared = jnp.dot((_silu(gu[:, :f]) * gu[:, f:]).astype(BF16), wshd_ref[...], preferred_element_type=F32)
    x2_ref[0] = x1 + g2 * shared
    for j in range(h2p_ref.shape[0]):
        h2p_ref[j] = _pack_bf16_pairs(h2[:, 2 * LANES * j:2 * LANES * (j + 1)])
    for _ in background:
        pass


def _mixer_kernel(*refs, tiles_per_seq):
    mix_inputs, (bias_ref, tri_ref) = refs[:21], refs[21:23]
    x2_ref, h2p_ref, eidx_ref, w_ref, rank_ref, cnt_ref = refs[23:29]
    state_ref, carry_ref, q_ref, k_ref, v_ref, ret_ref, lg_ref, rcarry_ref = refs[29:]
    s = pl.program_id(0)
    n_tiles = pl.num_programs(0) - 1

    @pl.when(s == 0)
    def _():
        lg_ref[...] = jnp.zeros_like(lg_ref)
        rcarry_ref[...] = jnp.zeros_like(rcarry_ref)

    @pl.when(s % tiles_per_seq == 0)
    def _():
        state_ref[...] = jnp.zeros_like(state_ref)
        carry_ref[...] = jnp.zeros_like(carry_ref)

    def route_previous_tile():
        return _route_phases(lg_ref.at[(s + 1) % 2], bias_ref, tri_ref, eidx_ref, w_ref, rank_ref, cnt_ref,
                             rcarry_ref, (s > 0).astype(jnp.int32))

    @pl.when(s == n_tiles)
    def _():
        for _ in route_previous_tile():
            pass

    @pl.when(s < n_tiles)
    def _():
        _mix_tile(*mix_inputs, x2_ref, h2p_ref, lg_ref.at[s % 2], state_ref, carry_ref, q_ref, k_ref, v_ref,
                  ret_ref, background=route_previous_tile())


def _mixer(x, posf, mod, gmix, win, retg, convw, wbr, wbc, wout, gffn, wr, wshgu, wshd, bias):
    b, s, d = x.shape
    tm = MIXER_TOKENS
    nt = s // tm
    n_tiles = b * nt
    intra, qd, kd, cd, invf, sign = _retention_constants()
    q_w, v_w = RET_HEADS * RET_DK, RET_HEADS * RET_DV
    hi_f32 = lax.bitcast_convert_type(lax.bitcast_convert_type(wr, jnp.uint32) & jnp.uint32(0xFFFF0000), F32)
    wr_hi = hi_f32.astype(BF16)
    wr_lo = (wr - hi_f32).astype(BF16)
    tri = jnp.asarray(np.triu(np.ones((tm, tm), np.float32), 1), BF16)
    weights = (gmix, win, invf, sign, intra, qd, kd, cd, retg, convw, wbr, wbc, wout, gffn, wr_hi, wr_lo,
               wshgu, wshd, bias, tri)

    def tile(i):
        return jnp.minimum(i, n_tiles - 1)

    def routed(i):
        return (0, jnp.maximum(i - 1, 0))

    return pl.pallas_call(
        functools.partial(_mixer_kernel, tiles_per_seq=nt),
        grid=(n_tiles + 1,),
        in_specs=[pl.BlockSpec((1, tm, d), lambda i: (tile(i) // nt, tile(i) % nt, 0)),
                  pl.BlockSpec((1, tm, 1), lambda i: (tile(i) // nt, tile(i) % nt, 0)),
                  pl.BlockSpec((1, 6, d), lambda i: (tile(i) // nt, 0, 0))]
                 + [_resident(w.shape) for w in weights],
        out_specs=[pl.BlockSpec((1, tm, d), lambda i: (tile(i) // nt, tile(i) % nt, 0)),
                   pl.BlockSpec((d // (2 * LANES), tm, LANES), lambda i: (0, tile(i), 0)),
                   pl.BlockSpec((TOP_K, tm), routed),
                   pl.BlockSpec((TOP_K, tm), routed),
                   pl.BlockSpec((TOP_K, tm), routed),
                   pl.BlockSpec((N_EXPERTS, 1), lambda i: (0, 0))],
        out_shape=[jax.ShapeDtypeStruct((b, s, d), F32),
                   jax.ShapeDtypeStruct((d // (2 * LANES), b * s, LANES), jnp.uint32),
                   jax.ShapeDtypeStruct((TOP_K, b * s), jnp.int32),
                   jax.ShapeDtypeStruct((TOP_K, b * s), F32),
                   jax.ShapeDtypeStruct((TOP_K, b * s), jnp.int32),
                   jax.ShapeDtypeStruct((N_EXPERTS, 1), F32)],
        scratch_shapes=[pltpu.VMEM((RET_HEADS, RET_DK, RET_DV), F32),
                        pltpu.VMEM((8, d), F32),
                        pltpu.VMEM((tm, q_w), F32),
                        pltpu.VMEM((tm, q_w), F32),
                        pltpu.VMEM((tm, v_w), BF16),
                        pltpu.VMEM((tm, v_w), F32),
                        pltpu.VMEM((2, N_EXPERTS, tm), F32),
                        pltpu.VMEM((N_EXPERTS, 1), F32)],
        compiler_params=pltpu.CompilerParams(dimension_semantics=("arbitrary",),
                                             vmem_limit_bytes=VMEM_LIMIT_BYTES),
        name="mixer",
    )(x, posf, mod, *weights)


def _first_argmax(v, idx, n):
    m = jnp.max(v, axis=0, keepdims=True)
    return m, jnp.min(jnp.where(v == m, idx, n), axis=0, keepdims=True)


def _route_phases(logits_ref, bias_ref, tri_ref, eidx_ref, w_ref, rank_ref, cnt_ref, carry_ref, count_it):
    tn = logits_ref.shape[1]
    neg = F32(-jnp.inf)
    score = _sigmoid(logits_ref[...])
    choice = score + bias_ref[...]

    grow = lax.broadcasted_iota(jnp.int32, (GROUP_SIZE, tn), 0)
    gscores = []
    for g in range(N_GROUPS):
        cg = choice[g * GROUP_SIZE:(g + 1) * GROUP_SIZE]
        m1, i1 = _first_argmax(cg, grow, GROUP_SIZE)
        m2 = jnp.max(jnp.where(grow == i1, neg, cg), axis=0, keepdims=True)
        gscores.append(m1 + m2)
    cur = jnp.concatenate(gscores, axis=0)
    yield
    gidx = lax.broadcasted_iota(jnp.int32, (N_GROUPS, tn), 0)
    keep = jnp.zeros((N_GROUPS, tn), F32)
    for _ in range(TOPK_GROUPS):
        _, ig = _first_argmax(cur, gidx, N_GROUPS)
        hit = gidx == ig
        keep = jnp.where(hit, 1.0, keep)
        cur = jnp.where(hit, neg, cur)
    cur = jnp.concatenate(
        [jnp.where(keep[g:g + 1] > 0.0, choice[g * GROUP_SIZE:(g + 1) * GROUP_SIZE], neg)
         for g in range(N_GROUPS)], axis=0)
    yield

    erow = lax.broadcasted_iota(jnp.int32, (N_EXPERTS, tn), 0)
    eidx, wts = [], []
    member = jnp.zeros((N_EXPERTS, tn), F32)
    for _ in range(TOP_K):
        _, ie = _first_argmax(cur, erow, N_EXPERTS)
        hit = erow == ie
        eidx.append(ie)
        wts.append(jnp.sum(jnp.where(hit, score, 0.0), axis=0, keepdims=True))
        member = member + hit.astype(F32)
        cur = jnp.where(hit, neg, cur)
        yield
    wsum = wts[0]
    for k in range(1, TOP_K):
        wsum = wsum + wts[k]

    before = jnp.dot(member.astype(BF16), tri_ref[...], preferred_element_type=F32) + carry_ref[...]
    ranks = []
    for k in range(TOP_K):
        ranks.append(jnp.sum(jnp.where(erow == eidx[k], before, 0.0), axis=0, keepdims=True))
        if k % 2 == 1:
            yield
    carry_ref[...] = carry_ref[...] + count_it.astype(F32) * jnp.sum(member, axis=1, keepdims=True)

    eidx_ref[...] = jnp.concatenate(eidx, axis=0)
    w_ref[...] = jnp.concatenate([w / wsum * ROUTED_SCALE for w in wts], axis=0)
    rank_ref[...] = jnp.concatenate(ranks, axis=0).astype(jnp.int32)
    cnt_ref[...] = carry_ref[...]


def _dest_kernel(eidx_ref, rank_ref, pstart_ref, dest_ref):
    tn = eidx_ref.shape[1]
    erow = lax.broadcasted_iota(jnp.int32, (N_EXPERTS, tn), 0)
    start = pstart_ref[...]
    dest_ref[...] = rank_ref[...] + jnp.concatenate(
        [jnp.sum(jnp.where(erow == eidx_ref[k:k + 1, :], start, 0), axis=0, keepdims=True) for k in range(TOP_K)],
        axis=0)


def _dest(eidx, rank, pstart):
    k, t = eidx.shape
    tn = DEST_TOKENS
    return pl.pallas_call(
        _dest_kernel,
        grid=(t // tn,),
        in_specs=[pl.BlockSpec((k, tn), lambda i: (0, i)),
                  pl.BlockSpec((k, tn), lambda i: (0, i)),
                  pl.BlockSpec((N_EXPERTS, 1), lambda i: (0, 0))],
        out_specs=pl.BlockSpec((k, tn), lambda i: (0, i)),
        out_shape=jax.ShapeDtypeStruct((k, t), jnp.int32),
        name="dest",
    )(eidx, rank, pstart)


def _sc_mesh_and_workers():
    mesh = plsc.VectorSubcoreMesh(core_axis_name="c", subcore_axis_name="s")
    return mesh, mesh.num_cores, mesh.num_cores * mesh.num_subcores


def _worker_id(num_cores):
    return lax.axis_index("s") * num_cores + lax.axis_index("c")


def _dispatch(h2p, dest3, n_rows):
    planes, t, lanes = h2p.shape
    n_chunks = dest3.shape[0]
    mesh, num_cores, workers = _sc_mesh_and_workers()
    chunks_per_worker = n_chunks // workers
    assert chunks_per_worker * workers == n_chunks and n_chunks * lanes == t

    def body(h_hbm, d_hbm, xs_hbm, idx_v, rows_v, load_sem, scatter_sem):
        wid = _worker_id(num_cores)

        @pl.loop(0, chunks_per_worker)
        def _(c):
            chunk = wid * chunks_per_worker + c
            tok = pl.ds(chunk * lanes, lanes)
            pltpu.sync_copy(d_hbm.at[chunk], idx_v)
            loads = [None] * planes
            scatters = [None] * planes
            loads[0] = pltpu.async_copy(h_hbm.at[0, tok], rows_v.at[0], load_sem.at[0])
            for j in range(planes):
                loads[j].wait()
                if j + 1 < planes:
                    if j >= 1:
                        for cp in scatters[j - 1]:
                            cp.wait()
                    loads[j + 1] = pltpu.async_copy(h_hbm.at[j + 1, tok], rows_v.at[(j + 1) % 2],
                                                    load_sem.at[(j + 1) % 2])
                scatters[j] = [pltpu.async_copy(rows_v.at[j % 2], xs_hbm.at[j].at[idx_v.at[k]],
                                                scatter_sem.at[j % 2]) for k in range(TOP_K)]
            for j in range(max(planes - 2, 0), planes):
                for cp in scatters[j]:
                    cp.wait()

    return pl.kernel(
        body,
        out_type=jax.ShapeDtypeStruct((planes, n_rows, lanes), h2p.dtype),
        mesh=mesh,
        scratch_types=[pltpu.VMEM((TOP_K, lanes), jnp.int32),
                       pltpu.VMEM((2, lanes, lanes), h2p.dtype),
                       pltpu.SemaphoreType.DMA((2,)),
                       pltpu.SemaphoreType.DMA((2,))],
        name="dispatch",
    )(h2p, dest3)


def _gather(ybuf, dest3, t):
    planes, _, lanes = ybuf.shape
    n_chunks = dest3.shape[0]
    mesh, num_cores, workers = _sc_mesh_and_workers()
    chunks_per_worker = n_chunks // workers
    nbuf = GATHER_BUFFERS
    lag = nbuf // 2
    assert chunks_per_worker * workers == n_chunks and n_chunks * lanes == t

    def body(y_hbm, d_hbm, yg_hbm, idx_v, rows_v, gather_sem, store_sem):
        wid = _worker_id(num_cores)

        @pl.loop(0, chunks_per_worker)
        def _(c):
            chunk = wid * chunks_per_worker + c
            tok = pl.ds(chunk * lanes, lanes)
            pltpu.sync_copy(d_hbm.at[chunk], idx_v)

            @pl.loop(0, TOP_K)
            def _(k):
                gathers = [None] * planes
                stores = [None] * planes

                def store(j):
                    gathers[j].wait()
                    stores[j] = pltpu.async_copy(rows_v.at[j % nbuf], yg_hbm.at[k, j, tok], store_sem.at[j % nbuf])

                for j in range(planes):
                    if j >= nbuf:
                        stores[j - nbuf].wait()
                    gathers[j] = pltpu.async_copy(y_hbm.at[j].at[idx_v.at[k]], rows_v.at[j % nbuf],
                                                  gather_sem.at[j % nbuf])
                    if j >= lag:
                        store(j - lag)
                for j in range(max(planes - lag, 0), planes):
                    store(j)
                for j in range(max(planes - nbuf, 0), planes):
                    stores[j].wait()

    return pl.kernel(
        body,
        out_type=jax.ShapeDtypeStruct((TOP_K, planes, t, lanes), ybuf.dtype),
        mesh=mesh,
        scratch_types=[pltpu.VMEM((TOP_K, lanes), jnp.int32),
                       pltpu.VMEM((nbuf, lanes, lanes), ybuf.dtype),
                       pltpu.SemaphoreType.DMA((nbuf,)),
                       pltpu.SemaphoreType.DMA((nbuf,))],
        name="gather",
    )(ybuf, dest3)


def _experts_kernel(be_ref, nv_ref, nu_ref, em_ref, x_ref, wg_hbm, wu_hbm, wd_hbm, o_ref,
                    wgs_ref, wus_ref, wds_ref, wgb_ref, wub_ref, wdb_ref, sems):
    planes, step_rows, lanes = x_ref.shape
    bm = EXPERT_ROWS
    blocks_per_step = step_rows // bm
    step = pl.program_id(0)
    last_step = (nu_ref[0] - 1) // blocks_per_step

    def weight_copies(expert):
        slot = em_ref[2, expert]
        return [pltpu.make_async_copy(src.at[expert], dst.at[slot], sems.at[slot, n])
                for n, (src, dst) in enumerate(((wg_hbm, wgs_ref), (wu_hbm, wus_ref), (wd_hbm, wds_ref)))]

    def start_if_other(expert, current):
        @pl.when(expert != current)
        def _():
            for cp in weight_copies(expert):
                cp.start()

    def swiglu_rows(row0, m, n_valid):
        rows = pl.ds(row0, m)
        valid = lax.broadcasted_iota(jnp.int32, (m, lanes), 0) < n_valid
        g = None
        u = None
        for j in range(planes):
            xj = _unpack_pairs_f32(jnp.where(valid, x_ref[j, rows, :], jnp.uint32(0))).astype(BF16)
            wrows = pl.ds(2 * lanes * j, 2 * lanes)
            gj = jnp.dot(xj, wgb_ref[wrows, :], preferred_element_type=F32)
            uj = jnp.dot(xj, wub_ref[wrows, :], preferred_element_type=F32)
            g = gj if g is None else g + gj
            u = uj if u is None else u + uj
        out = jnp.dot((_silu(g) * u).astype(BF16), wdb_ref[...], preferred_element_type=F32)
        for j in range(planes):
            o_ref[j, rows, :] = _pack_bf16_pairs(out[:, 2 * lanes * j:2 * lanes * (j + 1)])

    def block_step(sb):
        i = step * blocks_per_step + sb
        row0 = pl.multiple_of(sb * bm, bm)
        rows = pl.ds(row0, bm)
        active = i < nu_ref[0]
        e = be_ref[i]
        new_expert = jnp.logical_or(i == 0, e != be_ref[jnp.maximum(i - 1, 0)])
        nxt_i = jnp.minimum(i + 1, be_ref.shape[0] - 1)
        pair = jnp.logical_and(jnp.logical_and(active, sb + 1 < blocks_per_step),
                               jnp.logical_and(i + 1 < nu_ref[0], be_ref[nxt_i] == e))

        @pl.when(jnp.logical_and(active, new_expert))
        def _():
            @pl.when(i == 0)
            def _():
                for cp in weight_copies(e):
                    cp.start()
                start_if_other(em_ref[0, e], e)

            for cp in weight_copies(e):
                cp.wait()
            slot = em_ref[2, e]
            wgb_ref[...] = wgs_ref[slot].astype(BF16)
            wub_ref[...] = wus_ref[slot].astype(BF16)
            wdb_ref[...] = wds_ref[slot].astype(BF16)
            start_if_other(em_ref[1, e], e)

        @pl.when(pair)
        def _():
            swiglu_rows(row0, 2 * bm, bm + nv_ref[nxt_i])

        @pl.when(jnp.logical_and(active, jnp.logical_not(pair)))
        def _():
            swiglu_rows(row0, bm, nv_ref[i])

        @pl.when(jnp.logical_and(jnp.logical_not(active), step == last_step))
        def _():
            for j in range(planes):
                o_ref[j, rows, :] = jnp.zeros((bm, lanes), o_ref.dtype)

        return sb + jnp.where(pair, 2, 1)

    lax.while_loop(lambda sb: sb < blocks_per_step, block_step, jnp.int32(0))


def _experts(block_e, n_valid, n_used, expert_meta, xs, wg, wu, wd):
    planes, p, lanes = xs.shape
    e, d, f = wg.shape
    step_rows = EXPERT_ROWS * EXPERT_BLOCKS_PER_STEP
    assert p % step_rows == 0

    def row_map(i, be, nv, nu, nxt):
        return (0, jnp.minimum(i, (nu[0] - 1) // EXPERT_BLOCKS_PER_STEP), 0)

    return pl.pallas_call(
        _experts_kernel,
        grid_spec=pltpu.PrefetchScalarGridSpec(
            num_scalar_prefetch=4,
            grid=(p // step_rows,),
            in_specs=[pl.BlockSpec((planes, step_rows, lanes), row_map),
                      pl.BlockSpec(memory_space=pl.ANY),
                      pl.BlockSpec(memory_space=pl.ANY),
                      pl.BlockSpec(memory_space=pl.ANY)],
            out_specs=pl.BlockSpec((planes, step_rows, lanes), row_map),
            scratch_shapes=[pltpu.VMEM((2, d, f), F32), pltpu.VMEM((2, d, f), F32), pltpu.VMEM((2, f, d), F32),
                            pltpu.VMEM((d, f), BF16), pltpu.VMEM((d, f), BF16), pltpu.VMEM((f, d), BF16),
                            pltpu.SemaphoreType.DMA((2, 3))]),
        out_shape=jax.ShapeDtypeStruct((planes, p, lanes), jnp.uint32),
        compiler_params=pltpu.CompilerParams(dimension_semantics=("arbitrary",),
                                             vmem_limit_bytes=VMEM_LIMIT_BYTES),
        name="experts",
    )(block_e, n_valid, n_used, expert_meta, xs, wg, wu, wd)


def _combine_kernel(w_ref, x2_ref, g2_ref, gfin_ref, yg_ref, *maybe_prev_and_out):
    o_ref = maybe_prev_and_out[-1]
    w = w_ref[...]
    planes = yg_ref.shape[1]
    routed = None
    for k in range(TOP_K):
        rows = jnp.concatenate([_unpack_pairs_f32(yg_ref[k, j]) for j in range(planes)], axis=1) * w[:, k:k + 1]
        routed = rows if routed is None else routed + rows
    o_ref[0] = _rms(x2_ref[0] + g2_ref[0] * routed) * gfin_ref[...]


def _combine(wtok, x2, g2, gfin, yg, seg, prev_out):
    b, s, d = x2.shape
    _, planes, seg_tokens, lanes = yg.shape
    tc = COMBINE_TOKENS
    nt = seg_tokens // tc
    per_seq = s // seg_tokens
    assert nt * tc == seg_tokens and per_seq * seg_tokens == s
    bi, t0 = seg // per_seq, (seg % per_seq) * nt
    in_specs = [pl.BlockSpec((tc, TOP_K), lambda j: (seg * nt + j, 0)),
                pl.BlockSpec((1, tc, d), lambda j: (bi, t0 + j, 0)),
                pl.BlockSpec((1, 1, d), lambda j: (bi, 0, 0)),
                pl.BlockSpec((1, d), lambda j: (0, 0)),
                pl.BlockSpec((TOP_K, planes, tc, lanes), lambda j: (0, 0, j, 0))]
    args = [wtok, x2, g2, gfin, yg]
    aliases = {}
    if prev_out is not None:
        in_specs.append(pl.BlockSpec(memory_space=pl.ANY))
        args.append(prev_out)
        aliases = {len(args) - 1: 0}
    return pl.pallas_call(
        _combine_kernel,
        grid=(nt,),
        in_specs=in_specs,
        out_specs=pl.BlockSpec((1, tc, d), lambda j: (bi, t0 + j, 0)),
        out_shape=jax.ShapeDtypeStruct((b, s, d), F32),
        input_output_aliases=aliases,
        compiler_params=pltpu.CompilerParams(dimension_semantics=("arbitrary",),
                                             vmem_limit_bytes=VMEM_LIMIT_BYTES),
        name="combine",
    )(*args)


def kernel(x, c, positions, w_ada, b_ada, norm_mix_g, w_in, ret_norm_g, conv_w, w_br_ret, w_br_conv, w_out,
           norm_ffn_g, w_router, router_bias, w_exp_gate, w_exp_up, w_exp_down, w_sh_gate, w_sh_up, w_sh_down,
           norm_final_g):
    b, s, d = x.shape
    t = b * s
    depth = w_in.shape[0]
    assert depth == 1, "the combine kernel applies the final norm, so exactly one layer is supported"
    posf = positions.astype(F32)[:, :, None]
    c8 = jnp.zeros((8, d), F32).at[:b].set(c)
    bm = EXPERT_ROWS
    n_blocks = t * TOP_K // bm + N_EXPERTS
    n_rows = n_blocks * bm

    for l in range(depth):
        mod = _ada(c8, w_ada[l], b_ada[l][None, :])[:b].reshape(b, 6, d)
        x2, h2p, eidx, wts, rank, counts = _mixer(
            x, posf, mod, norm_mix_g[l][None, :], w_in[l].astype(BF16), ret_norm_g[l][None, :], conv_w[l],
            w_br_ret[l].astype(BF16), w_br_conv[l].astype(BF16), w_out[l].astype(BF16), norm_ffn_g[l][None, :],
            w_router[l].T, jnp.concatenate([w_sh_gate[l], w_sh_up[l]], axis=1).astype(BF16),
            w_sh_down[l].astype(BF16), router_bias[l][:, None])

        cnt = counts[:, 0].astype(jnp.int32)
        pcnt = (cnt + bm - 1) // bm * bm
        pend = jnp.cumsum(pcnt)
        pstart = pend - pcnt
        n_used = jnp.maximum(pend[-1] // bm, 1).astype(jnp.int32)[None]
        block_start = jnp.arange(n_blocks, dtype=jnp.int32) * bm
        block_e = jnp.minimum(jnp.sum((pend[None, :] <= block_start[:, None]).astype(jnp.int32), axis=1),
                              N_EXPERTS - 1)
        n_valid = jnp.clip(pstart[block_e] + cnt[block_e] - block_start, 0, bm).astype(jnp.int32)
        eid = jnp.arange(N_EXPERTS, dtype=jnp.int32)
        later_used = jnp.logical_and(eid[None, :] > eid[:, None], cnt[None, :] > 0)
        next_expert = jnp.min(jnp.where(later_used, eid[None, :], N_EXPERTS), axis=1)
        next_expert = jnp.where(next_expert == N_EXPERTS, eid, next_expert)
        after_next = next_expert[next_expert]
        after_next = jnp.where(after_next == next_expert, eid, after_next)
        stage_slot = (jnp.cumsum((cnt > 0).astype(jnp.int32)) - 1) % 2
        expert_meta = jnp.stack([next_expert, after_next, stage_slot]).astype(jnp.int32)

        dest = _dest(eidx, rank, pstart[:, None])
        dest3 = dest.reshape(TOP_K, t // LANES, LANES).transpose(1, 0, 2)
        xs = _dispatch(h2p, dest3, n_rows)
        ybuf = _experts(block_e, n_valid, n_used, expert_meta, xs, w_exp_gate[l], w_exp_up[l], w_exp_down[l])
        chunks = t // LANES // COMBINE_SEGMENTS
        out = None
        for seg in range(COMBINE_SEGMENTS):
            yg = _gather(ybuf, dest3[seg * chunks:(seg + 1) * chunks], chunks * LANES)
            out = _combine(wts.T, x2, mod[:, 5:6, :], norm_final_g[None, :], yg, seg, out)
        x = out
    return x
```

```python
import functools

import numpy as np
import jax
import jax.numpy as jnp
from jax import lax
from jax.experimental import pallas as pl
from jax.experimental.pallas import tpu as pltpu
from jax.experimental.pallas import tpu_sc as plsc

RET_HEADS = 4
RET_DK = 128
RET_DV = 256
RET_CHUNK = 128
ROPE_THETA = 10000.0
CONV_K = 3
N_EXPERTS = 256
TOP_K = 8
N_GROUPS = 8
TOPK_GROUPS = 4
GROUP_SIZE = N_EXPERTS // N_GROUPS
ROUTED_SCALE = 2.5
NORM_EPS = 1e-6

MIXER_TOKENS = 512
DEST_TOKENS = 512
EXPERT_ROWS = 256
EXPERT_BLOCKS_PER_STEP = 8
COMBINE_TOKENS = 256
COMBINE_SEGMENTS = 4
LANES = 128
GATHER_BUFFERS = 4

VMEM_LIMIT_BYTES = 56 * 1024 * 1024

F32 = jnp.float32
BF16 = jnp.bfloat16
HIGHEST = lax.Precision.HIGHEST


def _sigmoid(v):
    return 0.5 * jnp.tanh(0.5 * v) + 0.5


def _silu(v):
    return v * _sigmoid(v)


def _rms(v):
    return v * lax.rsqrt(jnp.mean(v * v, axis=-1, keepdims=True) + NORM_EPS)


def _resident(shape):
    nd = len(shape)
    return pl.BlockSpec(shape, lambda *_: (0,) * nd, pipeline_mode=pl.Buffered(1))


def _ada_kernel(c_ref, w_ref, b_ref, o_ref):
    c = c_ref[...]
    o_ref[...] = jnp.dot(_silu(c), w_ref[...], precision=HIGHEST, preferred_element_type=F32) + b_ref[...]


def _ada(c8, w, b):
    d, n = w.shape
    tn = 1024
    return pl.pallas_call(
        _ada_kernel,
        grid=(n // tn,),
        in_specs=[pl.BlockSpec((8, d), lambda j: (0, 0)),
                  pl.BlockSpec((d, tn), lambda j: (0, j)),
                  pl.BlockSpec((1, tn), lambda j: (0, j))],
        out_specs=pl.BlockSpec((8, tn), lambda j: (0, j)),
        out_shape=jax.ShapeDtypeStruct((8, n), F32),
        name="ada",
    )(c8, w, b)


def _retention_constants():
    c = RET_CHUNK
    log_g = jnp.log1p(-(2.0 ** (-5.0 - jnp.arange(RET_HEADS, dtype=F32))))
    idx = jnp.arange(c, dtype=F32)
    diff = idx[:, None] - idx[None, :]
    intra = jnp.where(diff >= 0, jnp.exp(log_g[:, None, None] * jnp.maximum(diff, 0.0)), 0.0)
    k_decay = jnp.exp(log_g[:, None] * (c - 1 - idx))
    q_decay = jnp.exp(log_g[:, None] * (idx + 1.0))
    chunk_decay = jnp.exp(log_g * c)
    kd = jnp.broadcast_to(k_decay[:, :, None], (RET_HEADS, c, RET_DK))
    qd = jnp.broadcast_to(q_decay[:, :, None], (RET_HEADS, c, RET_DK))
    cd = jnp.broadcast_to(chunk_decay[:, None, None], (RET_HEADS, 1, RET_DV))
    inv_freq = ROPE_THETA ** (-jnp.arange(0, RET_DK, 2, dtype=F32) / RET_DK)
    inv_freq = jnp.concatenate([inv_freq, inv_freq])[None, :]
    sign = jnp.concatenate([-jnp.ones((RET_DK // 2,), F32), jnp.ones((RET_DK // 2,), F32)])[None, :]
    return intra, qd, kd, cd, inv_freq, sign


def _pack_bf16_pairs(v):
    w = v.shape[1] // 2
    lo = lax.bitcast_convert_type(v[:, :w].astype(BF16).astype(F32), jnp.uint32)
    hi = lax.bitcast_convert_type(v[:, w:].astype(BF16).astype(F32), jnp.uint32)
    return (lo >> 16) | (hi & jnp.uint32(0xFFFF0000))


def _unpack_pairs_f32(word):
    lo = lax.bitcast_convert_type(word << 16, F32)
    hi = lax.bitcast_convert_type(word & jnp.uint32(0xFFFF0000), F32)
    return jnp.concatenate([lo, hi], axis=1)


def _mix_tile(x_ref, pos_ref, mod_ref, gmix_ref, win_ref, invf_ref, sign_ref, intra_ref, qd_ref, kd_ref, cd_ref,
              retg_ref, convw_ref, wbr_ref, wbc_ref, wout_ref, gffn_ref, wrh_ref, wrl_ref, wshgu_ref, wshd_ref,
              x2_ref, h2p_ref, lg_ref, state_ref, carry_ref, q_ref, k_ref, v_ref, ret_ref, hb_ref, *, background):
    def tick():
        next(background, None)

    tm, d = x_ref.shape[1], x_ref.shape[2]
    q_w = RET_HEADS * RET_DK
    v_w = RET_HEADS * RET_DV
    offs = np.cumsum([0, q_w, q_w, v_w, v_w, d, d, d, d, d])

    x = x_ref[0]
    mod = mod_ref[0]
    sh1, sc1, g1, sh2, sc2, g2 = [mod[i:i + 1] for i in range(6)]
    hb_ref[...] = ((_rms(x) * gmix_ref[...]) * (1.0 + sc1) + sh1).astype(BF16)

    def proj(i):
        return jnp.dot(hb_ref[...], win_ref[:, offs[i]:offs[i + 1]], preferred_element_type=F32)

    ang = pos_ref[0] * invf_ref[...]
    cosv = jnp.cos(ang)
    sinv = jnp.sin(ang) * sign_ref[...]

    def rope(t):
        return jnp.concatenate(
            [t[:, h * RET_DK:(h + 1) * RET_DK] * cosv
             + pltpu.roll(t[:, h * RET_DK:(h + 1) * RET_DK], RET_DK // 2, 1) * sinv
             for h in range(RET_HEADS)], axis=1)

    q_ref[...] = rope(proj(0)) * (RET_DK ** -0.5)
    tick()
    k_ref[...] = rope(proj(1))
    tick()
    v_ref[...] = proj(2).astype(BF16)
    tick()

    for c in range(tm // RET_CHUNK):
        rows = pl.ds(c * RET_CHUNK, RET_CHUNK)
        for h in range(RET_HEADS):
            qh = q_ref[rows, h * RET_DK:(h + 1) * RET_DK]
            kh = k_ref[rows, h * RET_DK:(h + 1) * RET_DK]
            vh = v_ref[rows, h * RET_DV:(h + 1) * RET_DV]
            scores = lax.dot_general(qh.astype(BF16), kh.astype(BF16), (((1,), (1,)), ((), ())),
                                     preferred_element_type=F32) * intra_ref[h]
            inner = jnp.dot(scores.astype(BF16), vh, preferred_element_type=F32)
            st = state_ref[h]
            cross = jnp.dot((qh * qd_ref[h]).astype(BF16), st.astype(BF16), preferred_element_type=F32)
            kv = lax.dot_general((kh * kd_ref[h]).astype(BF16), vh, (((0,), (0,)), ((), ())),
                                 preferred_element_type=F32)
            state_ref[h] = st * cd_ref[h] + kv
            ret_ref[rows, h * RET_DV:(h + 1) * RET_DV] = _rms(inner + cross)
        tick()

    y_ret = jnp.dot((_silu(proj(3)) * (ret_ref[...] * retg_ref[...])).astype(BF16), wbr_ref[...],
                    preferred_element_type=F32)
    tick()

    z = proj(5) * proj(4)
    row = lax.broadcasted_iota(jnp.int32, z.shape, 0)
    prev1 = carry_ref[7:8, :]
    prev2 = carry_ref[6:7, :]
    z1 = jnp.where(row == 0, prev1, pltpu.roll(z, 1, 0))
    z2 = jnp.where(row == 0, prev2, jnp.where(row == 1, prev1, pltpu.roll(z, 2, 0)))
    conv = convw_ref[0:1, :] * z2 + convw_ref[1:2, :] * z1 + convw_ref[2:3, :] * z
    carry_ref[...] = z[tm - 8:tm, :]
    tick()
    y_conv = jnp.dot((proj(6) * conv).astype(BF16), wbc_ref[...], preferred_element_type=F32)
    tick()

    mix = _sigmoid(proj(7)) * y_ret + _sigmoid(proj(8)) * y_conv
    tick()
    x1 = x + g1 * jnp.dot(mix.astype(BF16), wout_ref[...], preferred_element_type=F32)
    tick()

    h2 = (_rms(x1) * gffn_ref[...]) * (1.0 + sc2) + sh2
    hi_f32 = lax.bitcast_convert_type(lax.bitcast_convert_type(h2, jnp.uint32) & jnp.uint32(0xFFFF0000), F32)
    h2_hi = hi_f32.astype(BF16)
    h2_lo = (h2 - hi_f32).astype(BF16)

    def nt_dot(a, b):
        return lax.dot_general(a, b, (((1,), (1,)), ((), ())), preferred_element_type=F32)

    lg_ref[...] = nt_dot(wrh_ref[...], h2_hi) + (nt_dot(wrh_ref[...], h2_lo) + nt_dot(wrl_ref[...], h2_hi))
    gu = jnp.dot(h2.astype(BF16), wshgu_ref[...], preferred_element_type=F32)
    f = gu.shape[1] // 2
    shared = jnp.dot((_silu(gu[:, :f]) * gu[:, f:]).astype(BF16), wshd_ref[...], preferred_element_type=F32)
    x2_ref[0] = x1 + g2 * shared
    for j in range(h2p_ref.shape[0]):
        h2p_ref[j] = _pack_bf16_pairs(h2[:, 2 * LANES * j:2 * LANES * (j + 1)])
    for _ in background:
        pass


def _mixer_kernel(*refs, tiles_per_seq):
    mix_inputs, (bias_ref, tri_ref) = refs[:21], refs[21:23]
    x2_ref, h2p_ref, eidx_ref, w_ref, rank_ref, cnt_ref = refs[23:29]
    state_ref, carry_ref, q_ref, k_ref, v_ref, ret_ref, hb_ref, lg_ref, rcarry_ref = refs[29:]
    s = pl.program_id(0)
    n_tiles = pl.num_programs(0) - 1

    @pl.when(s == 0)
    def _():
        lg_ref[...] = jnp.zeros_like(lg_ref)
        rcarry_ref[...] = jnp.zeros_like(rcarry_ref)

    @pl.when(s % tiles_per_seq == 0)
    def _():
        state_ref[...] = jnp.zeros_like(state_ref)
        carry_ref[...] = jnp.zeros_like(carry_ref)

    def route_previous_tile():
        return _route_phases(lg_ref.at[(s + 1) % 2], bias_ref, tri_ref, eidx_ref, w_ref, rank_ref, cnt_ref,
                             rcarry_ref, (s > 0).astype(jnp.int32))

    @pl.when(s == n_tiles)
    def _():
        for _ in route_previous_tile():
            pass

    @pl.when(s < n_tiles)
    def _():
        _mix_tile(*mix_inputs, x2_ref, h2p_ref, lg_ref.at[s % 2], state_ref, carry_ref, q_ref, k_ref, v_ref,
                  ret_ref, hb_ref, background=route_previous_tile())


def _mixer(x, posf, mod, gmix, win, retg, convw, wbr, wbc, wout, gffn, wr, wshgu, wshd, bias):
    b, s, d = x.shape
    tm = MIXER_TOKENS
    nt = s // tm
    n_tiles = b * nt
    intra, qd, kd, cd, invf, sign = _retention_constants()
    q_w, v_w = RET_HEADS * RET_DK, RET_HEADS * RET_DV
    hi_f32 = lax.bitcast_convert_type(lax.bitcast_convert_type(wr, jnp.uint32) & jnp.uint32(0xFFFF0000), F32)
    wr_hi = hi_f32.astype(BF16)
    wr_lo = (wr - hi_f32).astype(BF16)
    tri = jnp.asarray(np.triu(np.ones((tm, tm), np.float32), 1), BF16)
    weights = (gmix, win, invf, sign, intra, qd, kd, cd, retg, convw, wbr, wbc, wout, gffn, wr_hi, wr_lo,
               wshgu, wshd, bias, tri)

    def tile(i):
        return jnp.minimum(i, n_tiles - 1)

    def routed(i):
        return (0, jnp.maximum(i - 1, 0))

    return pl.pallas_call(
        functools.partial(_mixer_kernel, tiles_per_seq=nt),
        grid=(n_tiles + 1,),
        in_specs=[pl.BlockSpec((1, tm, d), lambda i: (tile(i) // nt, tile(i) % nt, 0)),
                  pl.BlockSpec((1, tm, 1), lambda i: (tile(i) // nt, tile(i) % nt, 0)),
                  pl.BlockSpec((1, 6, d), lambda i: (tile(i) // nt, 0, 0))]
                 + [_resident(w.shape) for w in weights],
        out_specs=[pl.BlockSpec((1, tm, d), lambda i: (tile(i) // nt, tile(i) % nt, 0)),
                   pl.BlockSpec((d // (2 * LANES), tm, LANES), lambda i: (0, tile(i), 0)),
                   pl.BlockSpec((TOP_K, tm), routed),
                   pl.BlockSpec((TOP_K, tm), routed),
                   pl.BlockSpec((TOP_K, tm), routed),
                   pl.BlockSpec((N_EXPERTS, 1), lambda i: (0, 0))],
        out_shape=[jax.ShapeDtypeStruct((b, s, d), F32),
                   jax.ShapeDtypeStruct((d // (2 * LANES), b * s, LANES), jnp.uint32),
                   jax.ShapeDtypeStruct((TOP_K, b * s), jnp.int32),
                   jax.ShapeDtypeStruct((TOP_K, b * s), F32),
                   jax.ShapeDtypeStruct((TOP_K, b * s), jnp.int32),
                   jax.ShapeDtypeStruct((N_EXPERTS, 1), F32)],
        scratch_shapes=[pltpu.VMEM((RET_HEADS, RET_DK, RET_DV), F32),
                        pltpu.VMEM((8, d), F32),
                        pltpu.VMEM((tm, q_w), F32),
                        pltpu.VMEM((tm, q_w), F32),
                        pltpu.VMEM((tm, v_w), BF16),
                        pltpu.VMEM((tm, v_w), F32),
                        pltpu.VMEM((tm, d), BF16),
                        pltpu.VMEM((2, N_EXPERTS, tm), F32),
                        pltpu.VMEM((N_EXPERTS, 1), F32)],
        compiler_params=pltpu.CompilerParams(dimension_semantics=("arbitrary",),
                                             vmem_limit_bytes=VMEM_LIMIT_BYTES),
        name="mixer",
    )(x, posf, mod, *weights)


def _first_argmax(v, idx, n):
    m = jnp.max(v, axis=0, keepdims=True)
    return m, jnp.min(jnp.where(v == m, idx, n), axis=0, keepdims=True)


def _route_phases(logits_ref, bias_ref, tri_ref, eidx_ref, w_ref, rank_ref, cnt_ref, carry_ref, count_it):
    tn = logits_ref.shape[1]
    neg = F32(-jnp.inf)
    score = _sigmoid(logits_ref[...])
    choice = score + bias_ref[...]

    grow = lax.broadcasted_iota(jnp.int32, (GROUP_SIZE, tn), 0)
    gscores = []
    for g in range(N_GROUPS):
        cg = choice[g * GROUP_SIZE:(g + 1) * GROUP_SIZE]
        m1, i1 = _first_argmax(cg, grow, GROUP_SIZE)
        m2 = jnp.max(jnp.where(grow == i1, neg, cg), axis=0, keepdims=True)
        gscores.append(m1 + m2)
    cur = jnp.concatenate(gscores, axis=0)
    yield
    gidx = lax.broadcasted_iota(jnp.int32, (N_GROUPS, tn), 0)
    keep = jnp.zeros((N_GROUPS, tn), F32)
    for _ in range(TOPK_GROUPS):
        _, ig = _first_argmax(cur, gidx, N_GROUPS)
        hit = gidx == ig
        keep = jnp.where(hit, 1.0, keep)
        cur = jnp.where(hit, neg, cur)
    cur = jnp.concatenate(
        [jnp.where(keep[g:g + 1] > 0.0, choice[g * GROUP_SIZE:(g + 1) * GROUP_SIZE], neg)
         for g in range(N_GROUPS)], axis=0)
    yield

    erow = lax.broadcasted_iota(jnp.int32, (N_EXPERTS, tn), 0)
    eidx, wts = [], []
    member = jnp.zeros((N_EXPERTS, tn), F32)
    for _ in range(TOP_K):
        _, ie = _first_argmax(cur, erow, N_EXPERTS)
        hit = erow == ie
        eidx.append(ie)
        wts.append(jnp.sum(jnp.where(hit, score, 0.0), axis=0, keepdims=True))
        member = member + hit.astype(F32)
        cur = jnp.where(hit, neg, cur)
        yield
    wsum = wts[0]
    for k in range(1, TOP_K):
        wsum = wsum + wts[k]

    before = jnp.dot(member.astype(BF16), tri_ref[...], preferred_element_type=F32) + carry_ref[...]
    ranks = []
    for k in range(TOP_K):
        ranks.append(jnp.sum(jnp.where(erow == eidx[k], before, 0.0), axis=0, keepdims=True))
        if k % 2 == 1:
            yield
    carry_ref[...] = carry_ref[...] + count_it.astype(F32) * jnp.sum(member, axis=1, keepdims=True)

    eidx_ref[...] = jnp.concatenate(eidx, axis=0)
    w_ref[...] = jnp.concatenate([w / wsum * ROUTED_SCALE for w in wts], axis=0)
    rank_ref[...] = jnp.concatenate(ranks, axis=0).astype(jnp.int32)
    cnt_ref[...] = carry_ref[...]


def _dest_kernel(eidx_ref, rank_ref, pstart_ref, dest_ref):
    tn = eidx_ref.shape[1]
    erow = lax.broadcasted_iota(jnp.int32, (N_EXPERTS, tn), 0)
    start = pstart_ref[...]
    dest_ref[...] = rank_ref[...] + jnp.concatenate(
        [jnp.sum(jnp.where(erow == eidx_ref[k:k + 1, :], start, 0), axis=0, keepdims=True) for k in range(TOP_K)],
        axis=0)


def _dest(eidx, rank, pstart):
    k, t = eidx.shape
    tn = DEST_TOKENS
    return pl.pallas_call(
        _dest_kernel,
        grid=(t // tn,),
        in_specs=[pl.BlockSpec((k, tn), lambda i: (0, i)),
                  pl.BlockSpec((k, tn), lambda i: (0, i)),
                  pl.BlockSpec((N_EXPERTS, 1), lambda i: (0, 0))],
        out_specs=pl.BlockSpec((k, tn), lambda i: (0, i)),
        out_shape=jax.ShapeDtypeStruct((k, t), jnp.int32),
        name="dest",
    )(eidx, rank, pstart)


def _sc_mesh_and_workers():
    mesh = plsc.VectorSubcoreMesh(core_axis_name="c", subcore_axis_name="s")
    return mesh, mesh.num_cores, mesh.num_cores * mesh.num_subcores


def _worker_id(num_cores):
    return lax.axis_index("s") * num_cores + lax.axis_index("c")


def _dispatch(h2p, dest3, n_rows):
    planes, t, lanes = h2p.shape
    n_chunks = dest3.shape[0]
    mesh, num_cores, workers = _sc_mesh_and_workers()
    chunks_per_worker = n_chunks // workers
    assert chunks_per_worker * workers == n_chunks and n_chunks * lanes == t

    def body(h_hbm, d_hbm, xs_hbm, idx_v, rows_v, load_sem, scatter_sem):
        wid = _worker_id(num_cores)

        @pl.loop(0, chunks_per_worker)
        def _(c):
            chunk = wid * chunks_per_worker + c
            tok = pl.ds(chunk * lanes, lanes)
            pltpu.sync_copy(d_hbm.at[chunk], idx_v)
            loads = [None] * planes
            scatters = [None] * planes
            loads[0] = pltpu.async_copy(h_hbm.at[0, tok], rows_v.at[0], load_sem.at[0])
            for j in range(planes):
                loads[j].wait()
                if j + 1 < planes:
                    if j >= 1:
                        for cp in scatters[j - 1]:
                            cp.wait()
                    loads[j + 1] = pltpu.async_copy(h_hbm.at[j + 1, tok], rows_v.at[(j + 1) % 2],
                                                    load_sem.at[(j + 1) % 2])
                scatters[j] = [pltpu.async_copy(rows_v.at[j % 2], xs_hbm.at[j].at[idx_v.at[k]],
                                                scatter_sem.at[j % 2]) for k in range(TOP_K)]
            for j in range(max(planes - 2, 0), planes):
                for cp in scatters[j]:
                    cp.wait()

    return pl.kernel(
        body,
        out_type=jax.ShapeDtypeStruct((planes, n_rows, lanes), h2p.dtype),
        mesh=mesh,
        scratch_types=[pltpu.VMEM((TOP_K, lanes), jnp.int32),
                       pltpu.VMEM((2, lanes, lanes), h2p.dtype),
                       pltpu.SemaphoreType.DMA((2,)),
                       pltpu.SemaphoreType.DMA((2,))],
        name="dispatch",
    )(h2p, dest3)


def _gather(ybuf, dest3, t):
    planes, _, lanes = ybuf.shape
    n_chunks = dest3.shape[0]
    mesh, num_cores, workers = _sc_mesh_and_workers()
    chunks_per_worker = n_chunks // workers
    nbuf = GATHER_BUFFERS
    lag = nbuf // 2
    assert chunks_per_worker * workers == n_chunks and n_chunks * lanes == t

    def body(y_hbm, d_hbm, yg_hbm, idx_v, rows_v, gather_sem, store_sem):
        wid = _worker_id(num_cores)

        @pl.loop(0, chunks_per_worker)
        def _(c):
            chunk = wid * chunks_per_worker + c
            tok = pl.ds(chunk * lanes, lanes)
            pltpu.sync_copy(d_hbm.at[chunk], idx_v)

            @pl.loop(0, TOP_K)
            def _(k):
                gathers = [None] * planes
                stores = [None] * planes

                def store(j):
                    gathers[j].wait()
                    stores[j] = pltpu.async_copy(rows_v.at[j % nbuf], yg_hbm.at[k, j, tok], store_sem.at[j % nbuf])

                for j in range(planes):
                    if j >= nbuf:
                        stores[j - nbuf].wait()
                    gathers[j] = pltpu.async_copy(y_hbm.at[j].at[idx_v.at[k]], rows_v.at[j % nbuf],
                                                  gather_sem.at[j % nbuf])
                    if j >= lag:
                        store(j - lag)
                for j in range(max(planes - lag, 0), planes):
                    store(j)
                for j in range(max(planes - nbuf, 0), planes):
                    stores[j].wait()

    return pl.kernel(
        body,
        out_type=jax.ShapeDtypeStruct((TOP_K, planes, t, lanes), ybuf.dtype),
        mesh=mesh,
        scratch_types=[pltpu.VMEM((TOP_K, lanes), jnp.int32),
                       pltpu.VMEM((nbuf, lanes, lanes), ybuf.dtype),
                       pltpu.SemaphoreType.DMA((nbuf,)),
                       pltpu.SemaphoreType.DMA((nbuf,))],
        name="gather",
    )(ybuf, dest3)


def _experts_kernel(be_ref, nv_ref, nu_ref, em_ref, x_ref, wg_hbm, wu_hbm, wd_hbm, o_ref,
                    wgs_ref, wus_ref, wds_ref, wgb_ref, wub_ref, wdb_ref, sems):
    planes, step_rows, lanes = x_ref.shape
    bm = EXPERT_ROWS
    blocks_per_step = step_rows // bm
    step = pl.program_id(0)
    last_step = (nu_ref[0] - 1) // blocks_per_step

    def weight_copies(expert):
        slot = em_ref[2, expert]
        return [pltpu.make_async_copy(src.at[expert], dst.at[slot], sems.at[slot, n])
                for n, (src, dst) in enumerate(((wg_hbm, wgs_ref), (wu_hbm, wus_ref), (wd_hbm, wds_ref)))]

    def start_if_other(expert, current):
        @pl.when(expert != current)
        def _():
            for cp in weight_copies(expert):
                cp.start()

    def swiglu_rows(row0, m, n_valid):
        rows = pl.ds(row0, m)
        valid = lax.broadcasted_iota(jnp.int32, (m, lanes), 0) < n_valid
        g = None
        u = None
        for j in range(planes):
            xj = _unpack_pairs_f32(jnp.where(valid, x_ref[j, rows, :], jnp.uint32(0))).astype(BF16)
            wrows = pl.ds(2 * lanes * j, 2 * lanes)
            gj = jnp.dot(xj, wgb_ref[wrows, :], preferred_element_type=F32)
            uj = jnp.dot(xj, wub_ref[wrows, :], preferred_element_type=F32)
            g = gj if g is None else g + gj
            u = uj if u is None else u + uj
        out = jnp.dot((_silu(g) * u).astype(BF16), wdb_ref[...], preferred_element_type=F32)
        for j in range(planes):
            o_ref[j, rows, :] = _pack_bf16_pairs(out[:, 2 * lanes * j:2 * lanes * (j + 1)])

    def block_step(sb):
        i = step * blocks_per_step + sb
        row0 = pl.multiple_of(sb * bm, bm)
        rows = pl.ds(row0, bm)
        active = i < nu_ref[0]
        e = be_ref[i]
        new_expert = jnp.logical_or(i == 0, e != be_ref[jnp.maximum(i - 1, 0)])
        nxt_i = jnp.minimum(i + 1, be_ref.shape[0] - 1)
        pair = jnp.logical_and(jnp.logical_and(active, sb + 1 < blocks_per_step),
                               jnp.logical_and(i + 1 < nu_ref[0], be_ref[nxt_i] == e))

        @pl.when(jnp.logical_and(active, new_expert))
        def _():
            @pl.when(i == 0)
            def _():
                for cp in weight_copies(e):
                    cp.start()
                start_if_other(em_ref[0, e], e)

            for cp in weight_copies(e):
                cp.wait()
            slot = em_ref[2, e]
            wgb_ref[...] = wgs_ref[slot].astype(BF16)
            wub_ref[...] = wus_ref[slot].astype(BF16)
            wdb_ref[...] = wds_ref[slot].astype(BF16)
            start_if_other(em_ref[1, e], e)

        @pl.when(pair)
        def _():
            swiglu_rows(row0, 2 * bm, bm + nv_ref[nxt_i])

        @pl.when(jnp.logical_and(active, jnp.logical_not(pair)))
        def _():
            swiglu_rows(row0, bm, nv_ref[i])

        @pl.when(jnp.logical_and(jnp.logical_not(active), step == last_step))
        def _():
            for j in range(planes):
                o_ref[j, rows, :] = jnp.zeros((bm, lanes), o_ref.dtype)

        return sb + jnp.where(pair, 2, 1)

    lax.while_loop(lambda sb: sb < blocks_per_step, block_step, jnp.int32(0))


def _experts(block_e, n_valid, n_used, expert_meta, xs, wg, wu, wd):
    planes, p, lanes = xs.shape
    e, d, f = wg.shape
    step_rows = EXPERT_ROWS * EXPERT_BLOCKS_PER_STEP
    assert p % step_rows == 0

    def row_map(i, be, nv, nu, nxt):
        return (0, jnp.minimum(i, (nu[0] - 1) // EXPERT_BLOCKS_PER_STEP), 0)

    return pl.pallas_call(
        _experts_kernel,
        grid_spec=pltpu.PrefetchScalarGridSpec(
            num_scalar_prefetch=4,
            grid=(p // step_rows,),
            in_specs=[pl.BlockSpec((planes, step_rows, lanes), row_map),
                      pl.BlockSpec(memory_space=pl.ANY),
                      pl.BlockSpec(memory_space=pl.ANY),
                      pl.BlockSpec(memory_space=pl.ANY)],
            out_specs=pl.BlockSpec((planes, step_rows, lanes), row_map),
            scratch_shapes=[pltpu.VMEM((2, d, f), F32), pltpu.VMEM((2, d, f), F32), pltpu.VMEM((2, f, d), F32),
                            pltpu.VMEM((d, f), BF16), pltpu.VMEM((d, f), BF16), pltpu.VMEM((f, d), BF16),
                            pltpu.SemaphoreType.DMA((2, 3))]),
        out_shape=jax.ShapeDtypeStruct((planes, p, lanes), jnp.uint32),
        compiler_params=pltpu.CompilerParams(dimension_semantics=("arbitrary",),
                                             vmem_limit_bytes=VMEM_LIMIT_BYTES),
        name="experts",
    )(block_e, n_valid, n_used, expert_meta, xs, wg, wu, wd)


def _combine_kernel(w_ref, x2_ref, g2_ref, gfin_ref, yg_ref, *maybe_prev_and_out):
    o_ref = maybe_prev_and_out[-1]
    w = w_ref[...]
    planes = yg_ref.shape[1]
    routed = None
    for k in range(TOP_K):
        rows = jnp.concatenate([_unpack_pairs_f32(yg_ref[k, j]) for j in range(planes)], axis=1) * w[:, k:k + 1]
        routed = rows if routed is None else routed + rows
    o_ref[0] = _rms(x2_ref[0] + g2_ref[0] * routed) * gfin_ref[...]


def _combine(wtok, x2, g2, gfin, yg, seg, prev_out):
    b, s, d = x2.shape
    _, planes, seg_tokens, lanes = yg.shape
    tc = COMBINE_TOKENS
    nt = seg_tokens // tc
    per_seq = s // seg_tokens
    assert nt * tc == seg_tokens and per_seq * seg_tokens == s
    bi, t0 = seg // per_seq, (seg % per_seq) * nt
    in_specs = [pl.BlockSpec((tc, TOP_K), lambda j: (seg * nt + j, 0)),
                pl.BlockSpec((1, tc, d), lambda j: (bi, t0 + j, 0)),
                pl.BlockSpec((1, 1, d), lambda j: (bi, 0, 0)),
                pl.BlockSpec((1, d), lambda j: (0, 0)),
                pl.BlockSpec((TOP_K, planes, tc, lanes), lambda j: (0, 0, j, 0))]
    args = [wtok, x2, g2, gfin, yg]
    aliases = {}
    if prev_out is not None:
        in_specs.append(pl.BlockSpec(memory_space=pl.ANY))
        args.append(prev_out)
        aliases = {len(args) - 1: 0}
    return pl.pallas_call(
        _combine_kernel,
        grid=(nt,),
        in_specs=in_specs,
        out_specs=pl.BlockSpec((1, tc, d), lambda j: (bi, t0 + j, 0)),
        out_shape=jax.ShapeDtypeStruct((b, s, d), F32),
        input_output_aliases=aliases,
        compiler_params=pltpu.CompilerParams(dimension_semantics=("arbitrary",),
                                             vmem_limit_bytes=VMEM_LIMIT_BYTES),
        name="combine",
    )(*args)


def kernel(x, c, positions, w_ada, b_ada, norm_mix_g, w_in, ret_norm_g, conv_w, w_br_ret, w_br_conv, w_out,
           norm_ffn_g, w_router, router_bias, w_exp_gate, w_exp_up, w_exp_down, w_sh_gate, w_sh_up, w_sh_down,
           norm_final_g):
    b, s, d = x.shape
    t = b * s
    depth = w_in.shape[0]
    assert depth == 1, "the combine kernel applies the final norm, so exactly one layer is supported"
    posf = positions.astype(F32)[:, :, None]
    c8 = jnp.zeros((8, d), F32).at[:b].set(c)
    bm = EXPERT_ROWS
    n_blocks = t * TOP_K // bm + N_EXPERTS
    n_rows = n_blocks * bm

    for l in range(depth):
        mod = _ada(c8, w_ada[l], b_ada[l][None, :])[:b].reshape(b, 6, d)
        x2, h2p, eidx, wts, rank, counts = _mixer(
            x, posf, mod, norm_mix_g[l][None, :], w_in[l].astype(BF16), ret_norm_g[l][None, :], conv_w[l],
            w_br_ret[l].astype(BF16), w_br_conv[l].astype(BF16), w_out[l].astype(BF16), norm_ffn_g[l][None, :],
            w_router[l].T, jnp.concatenate([w_sh_gate[l], w_sh_up[l]], axis=1).astype(BF16),
            w_sh_down[l].astype(BF16), router_bias[l][:, None])

        cnt = counts[:, 0].astype(jnp.int32)
        pcnt = (cnt + bm - 1) // bm * bm
        pend = jnp.cumsum(pcnt)
        pstart = pend - pcnt
        n_used = jnp.maximum(pend[-1] // bm, 1).astype(jnp.int32)[None]
        block_start = jnp.arange(n_blocks, dtype=jnp.int32) * bm
        block_e = jnp.minimum(jnp.sum((pend[None, :] <= block_start[:, None]).astype(jnp.int32), axis=1),
                              N_EXPERTS - 1)
        n_valid = jnp.clip(pstart[block_e] + cnt[block_e] - block_start, 0, bm).astype(jnp.int32)
        eid = jnp.arange(N_EXPERTS, dtype=jnp.int32)
        later_used = jnp.logical_and(eid[None, :] > eid[:, None], cnt[None, :] > 0)
        next_expert = jnp.min(jnp.where(later_used, eid[None, :], N_EXPERTS), axis=1)
        next_expert = jnp.where(next_expert == N_EXPERTS, eid, next_expert)
        after_next = next_expert[next_expert]
        after_next = jnp.where(after_next == next_expert, eid, after_next)
        stage_slot = (jnp.cumsum((cnt > 0).astype(jnp.int32)) - 1) % 2
        expert_meta = jnp.stack([next_expert, after_next, stage_slot]).astype(jnp.int32)

        dest = _dest(eidx, rank, pstart[:, None])
        dest3 = dest.reshape(TOP_K, t // LANES, LANES).transpose(1, 0, 2)
        xs = _dispatch(h2p, dest3, n_rows)
        ybuf = _experts(block_e, n_valid, n_used, expert_meta, xs, w_exp_gate[l], w_exp_up[l], w_exp_down[l])
        chunks = t // LANES // COMBINE_SEGMENTS
        out = None
        for seg in range(COMBINE_SEGMENTS):
            yg = _gather(ybuf, dest3[seg * chunks:(seg + 1) * chunks], chunks * LANES)
            out = _combine(wts.T, x2, mod[:, 5:6, :], norm_final_g[None, :], yg, seg, out)
        x = out
    return x
```

```python
import functools

import numpy as np
import jax
import jax.numpy as jnp
from jax import lax
from jax.experimental import pallas as pl
from jax.experimental.pallas import tpu as pltpu
from jax.experimental.pallas import tpu_sc as plsc

RET_HEADS = 4
RET_DK = 128
RET_DV = 256
RET_CHUNK = 128
ROPE_THETA = 10000.0
CONV_K = 3
N_EXPERTS = 256
TOP_K = 8
N_GROUPS = 8
TOPK_GROUPS = 4
GROUP_SIZE = N_EXPERTS // N_GROUPS
ROUTED_SCALE = 2.5
NORM_EPS = 1e-6

MIXER_TOKENS = 512
DEST_TOKENS = 512
EXPERT_ROWS = 256
EXPERT_BLOCKS_PER_STEP = 16
COMBINE_TOKENS = 256
COMBINE_SEGMENTS = 4
LANES = 128
GATHER_BUFFERS = 4

VMEM_LIMIT_BYTES = 56 * 1024 * 1024

F32 = jnp.float32
BF16 = jnp.bfloat16
HIGHEST = lax.Precision.HIGHEST


def _sigmoid(v):
    return 0.5 * jnp.tanh(0.5 * v) + 0.5


def _silu(v):
    return v * _sigmoid(v)


def _rms(v):
    return v * lax.rsqrt(jnp.mean(v * v, axis=-1, keepdims=True) + NORM_EPS)


def _resident(shape):
    nd = len(shape)
    return pl.BlockSpec(shape, lambda *_: (0,) * nd, pipeline_mode=pl.Buffered(1))


def _ada_kernel(c_ref, w_ref, b_ref, o_ref):
    c = c_ref[...]
    o_ref[...] = jnp.dot(_silu(c), w_ref[...], precision=HIGHEST, preferred_element_type=F32) + b_ref[...]


def _ada(c8, w, b):
    d, n = w.shape
    tn = 1024
    return pl.pallas_call(
        _ada_kernel,
        grid=(n // tn,),
        in_specs=[pl.BlockSpec((8, d), lambda j: (0, 0)),
                  pl.BlockSpec((d, tn), lambda j: (0, j)),
                  pl.BlockSpec((1, tn), lambda j: (0, j))],
        out_specs=pl.BlockSpec((8, tn), lambda j: (0, j)),
        out_shape=jax.ShapeDtypeStruct((8, n), F32),
        name="ada",
    )(c8, w, b)


def _retention_constants():
    f32 = np.float32
    c = RET_CHUNK
    log_g = np.log1p(-(f32(2.0) ** (f32(-5.0) - np.arange(RET_HEADS, dtype=f32)))).astype(f32)
    idx = np.arange(c, dtype=f32)
    diff = idx[:, None] - idx[None, :]
    intra = np.where(diff >= 0, np.exp(log_g[:, None, None] * np.maximum(diff, f32(0.0))), f32(0.0))
    k_decay = np.exp(log_g[:, None] * (f32(c - 1) - idx))
    q_decay = np.exp(log_g[:, None] * (idx + f32(1.0)))
    chunk_decay = np.exp(log_g * f32(c))
    kd = np.broadcast_to(k_decay[:, :, None], (RET_HEADS, c, RET_DK))
    qd = np.broadcast_to(q_decay[:, :, None], (RET_HEADS, c, RET_DK))
    cd = np.broadcast_to(chunk_decay[:, None, None], (RET_HEADS, 1, RET_DV))
    inv_freq = f32(ROPE_THETA) ** (-np.arange(0, RET_DK, 2, dtype=f32) / f32(RET_DK))
    inv_freq = np.concatenate([inv_freq, inv_freq])[None, :]
    sign = np.concatenate([-np.ones((RET_DK // 2,), f32), np.ones((RET_DK // 2,), f32)])[None, :]
    return tuple(jnp.asarray(np.ascontiguousarray(v, dtype=f32)) for v in (intra, qd, kd, cd, inv_freq, sign))


def _pack_bf16_pairs(v):
    w = v.shape[1] // 2
    lo = lax.bitcast_convert_type(v[:, :w].astype(BF16).astype(F32), jnp.uint32)
    hi = lax.bitcast_convert_type(v[:, w:].astype(BF16).astype(F32), jnp.uint32)
    return (lo >> 16) | (hi & jnp.uint32(0xFFFF0000))


def _unpack_pairs_f32(word):
    lo = lax.bitcast_convert_type(word << 16, F32)
    hi = lax.bitcast_convert_type(word & jnp.uint32(0xFFFF0000), F32)
    return jnp.concatenate([lo, hi], axis=1)


def _mix_tile(x_ref, pos_ref, mod_ref, gmix_ref, win_ref, invf_ref, sign_ref, intra_ref, qd_ref, kd_ref, cd_ref,
              retg_ref, convw_ref, wbr_ref, wbc_ref, wout_ref, gffn_ref, wrh_ref, wrl_ref, wshgu_ref, wshd_ref,
              x2_ref, h2p_ref, lg_ref, state_ref, carry_ref, q_ref, k_ref, v_ref, ret_ref, hb_ref, *, background):
    def tick():
        next(background, None)

    tm, d = x_ref.shape[1], x_ref.shape[2]
    q_w = RET_HEADS * RET_DK
    v_w = RET_HEADS * RET_DV
    offs = np.cumsum([0, q_w, q_w, v_w, v_w, d, d, d, d, d])

    x = x_ref[0]
    mod = mod_ref[0]
    sh1, sc1, g1, sh2, sc2, g2 = [mod[i:i + 1] for i in range(6)]
    hb_ref[...] = ((_rms(x) * gmix_ref[...]) * (1.0 + sc1) + sh1).astype(BF16)

    def proj(i):
        return jnp.dot(hb_ref[...], win_ref[:, offs[i]:offs[i + 1]], preferred_element_type=F32)

    ang = pos_ref[0] * invf_ref[...]
    cosv = jnp.cos(ang)
    sinv = jnp.sin(ang) * sign_ref[...]

    def rope(t):
        return jnp.concatenate(
            [t[:, h * RET_DK:(h + 1) * RET_DK] * cosv
             + pltpu.roll(t[:, h * RET_DK:(h + 1) * RET_DK], RET_DK // 2, 1) * sinv
             for h in range(RET_HEADS)], axis=1)

    q_ref[...] = rope(proj(0)) * (RET_DK ** -0.5)
    tick()
    k_ref[...] = rope(proj(1))
    tick()
    v_ref[...] = proj(2).astype(BF16)
    tick()

    for c in range(tm // RET_CHUNK):
        rows = pl.ds(c * RET_CHUNK, RET_CHUNK)
        for h in range(RET_HEADS):
            qh = q_ref[rows, h * RET_DK:(h + 1) * RET_DK]
            kh = k_ref[rows, h * RET_DK:(h + 1) * RET_DK]
            vh = v_ref[rows, h * RET_DV:(h + 1) * RET_DV]
            scores = lax.dot_general(qh.astype(BF16), kh.astype(BF16), (((1,), (1,)), ((), ())),
                                     preferred_element_type=F32) * intra_ref[h]
            inner = jnp.dot(scores.astype(BF16), vh, preferred_element_type=F32)
            st = state_ref[h]
            cross = jnp.dot((qh * qd_ref[h]).astype(BF16), st.astype(BF16), preferred_element_type=F32)
            kv = lax.dot_general((kh * kd_ref[h]).astype(BF16), vh, (((0,), (0,)), ((), ())),
                                 preferred_element_type=F32)
            state_ref[h] = st * cd_ref[h] + kv
            ret_ref[rows, h * RET_DV:(h + 1) * RET_DV] = _rms(inner + cross)
        tick()

    y_ret = jnp.dot((_silu(proj(3)) * (ret_ref[...] * retg_ref[...])).astype(BF16), wbr_ref[...],
                    preferred_element_type=F32)
    tick()

    z = proj(5) * proj(4)
    row = lax.broadcasted_iota(jnp.int32, z.shape, 0)
    prev1 = carry_ref[7:8, :]
    prev2 = carry_ref[6:7, :]
    z1 = jnp.where(row == 0, prev1, pltpu.roll(z, 1, 0))
    z2 = jnp.where(row == 0, prev2, jnp.where(row == 1, prev1, pltpu.roll(z, 2, 0)))
    conv = convw_ref[0:1, :] * z2 + convw_ref[1:2, :] * z1 + convw_ref[2:3, :] * z
    carry_ref[...] = z[tm - 8:tm, :]
    tick()
    y_conv = jnp.dot((proj(6) * conv).astype(BF16), wbc_ref[...], preferred_element_type=F32)
    tick()

    mix = _sigmoid(proj(7)) * y_ret + _sigmoid(proj(8)) * y_conv
    tick()
    x1 = x + g1 * jnp.dot(mix.astype(BF16), wout_ref[...], preferred_element_type=F32)
    tick()

    h2 = (_rms(x1) * gffn_ref[...]) * (1.0 + sc2) + sh2
    hi_f32 = lax.bitcast_convert_type(lax.bitcast_convert_type(h2, jnp.uint32) & jnp.uint32(0xFFFF0000), F32)
    h2_hi = hi_f32.astype(BF16)
    h2_lo = (h2 - hi_f32).astype(BF16)

    def nt_dot(a, b):
        return lax.dot_general(a, b, (((1,), (1,)), ((), ())), preferred_element_type=F32)

    lg_ref[...] = nt_dot(wrh_ref[...], h2_hi) + (nt_dot(wrh_ref[...], h2_lo) + nt_dot(wrl_ref[...], h2_hi))
    gu = jnp.dot(h2.astype(BF16), wshgu_ref[...], preferred_element_type=F32)
    f = gu.shape[1] // 2
    shared = jnp.dot((_silu(gu[:, :f]) * gu[:, f:]).astype(BF16), wshd_ref[...], preferred_element_type=F32)
    x2_ref[0] = x1 + g2 * shared
    for j in range(h2p_ref.shape[0]):
        h2p_ref[j] = _pack_bf16_pairs(h2[:, 2 * LANES * j:2 * LANES * (j + 1)])
    for _ in background:
        pass


def _mixer_kernel(*refs, tiles_per_seq):
    mix_inputs, (bias_ref, tri_ref) = refs[:21], refs[21:23]
    x2_ref, h2p_ref, eidx_ref, w_ref, rank_ref, cnt_ref = refs[23:29]
    state_ref, carry_ref, q_ref, k_ref, v_ref, ret_ref, hb_ref, lg_ref, rcarry_ref = refs[29:]
    s = pl.program_id(0)
    n_tiles = pl.num_programs(0) - 1

    @pl.when(s == 0)
    def _():
        lg_ref[...] = jnp.zeros_like(lg_ref)
        rcarry_ref[...] = jnp.zeros_like(rcarry_ref)

    @pl.when(s % tiles_per_seq == 0)
    def _():
        state_ref[...] = jnp.zeros_like(state_ref)
        carry_ref[...] = jnp.zeros_like(carry_ref)

    def route_previous_tile():
        return _route_phases(lg_ref.at[(s + 1) % 2], bias_ref, tri_ref, eidx_ref, w_ref, rank_ref, cnt_ref,
                             rcarry_ref, (s > 0).astype(jnp.int32))

    @pl.when(s == n_tiles)
    def _():
        for _ in route_previous_tile():
            pass

    @pl.when(s < n_tiles)
    def _():
        _mix_tile(*mix_inputs, x2_ref, h2p_ref, lg_ref.at[s % 2], state_ref, carry_ref, q_ref, k_ref, v_ref,
                  ret_ref, hb_ref, background=route_previous_tile())


def _mixer(x, posf, mod, gmix, win, retg, convw, wbr, wbc, wout, gffn, wr, wshgu, wshd, bias):
    b, s, d = x.shape
    tm = MIXER_TOKENS
    nt = s // tm
    n_tiles = b * nt
    intra, qd, kd, cd, invf, sign = _retention_constants()
    q_w, v_w = RET_HEADS * RET_DK, RET_HEADS * RET_DV
    hi_f32 = lax.bitcast_convert_type(lax.bitcast_convert_type(wr, jnp.uint32) & jnp.uint32(0xFFFF0000), F32)
    wr_hi = hi_f32.astype(BF16)
    wr_lo = (wr - hi_f32).astype(BF16)
    tri = jnp.asarray(np.triu(np.ones((tm, tm), np.float32), 1), BF16)
    weights = (gmix, win, invf, sign, intra, qd, kd, cd, retg, convw, wbr, wbc, wout, gffn, wr_hi, wr_lo,
               wshgu, wshd, bias, tri)

    def tile(i):
        return jnp.minimum(i, n_tiles - 1)

    def routed(i):
        return (0, jnp.maximum(i - 1, 0))

    return pl.pallas_call(
        functools.partial(_mixer_kernel, tiles_per_seq=nt),
        grid=(n_tiles + 1,),
        in_specs=[pl.BlockSpec((1, tm, d), lambda i: (tile(i) // nt, tile(i) % nt, 0)),
                  pl.BlockSpec((1, tm, 1), lambda i: (tile(i) // nt, tile(i) % nt, 0)),
                  pl.BlockSpec((1, 6, d), lambda i: (tile(i) // nt, 0, 0))]
                 + [_resident(w.shape) for w in weights],
        out_specs=[pl.BlockSpec((1, tm, d), lambda i: (tile(i) // nt, tile(i) % nt, 0)),
                   pl.BlockSpec((d // (2 * LANES), tm, LANES), lambda i: (0, tile(i), 0)),
                   pl.BlockSpec((TOP_K, tm), routed),
                   pl.BlockSpec((TOP_K, tm), routed),
                   pl.BlockSpec((TOP_K, tm), routed),
                   pl.BlockSpec((N_EXPERTS, 1), lambda i: (0, 0))],
        out_shape=[jax.ShapeDtypeStruct((b, s, d), F32),
                   jax.ShapeDtypeStruct((d // (2 * LANES), b * s, LANES), jnp.uint32),
                   jax.ShapeDtypeStruct((TOP_K, b * s), jnp.int32),
                   jax.ShapeDtypeStruct((TOP_K, b * s), F32),
                   jax.ShapeDtypeStruct((TOP_K, b * s), jnp.int32),
                   jax.ShapeDtypeStruct((N_EXPERTS, 1), F32)],
        scratch_shapes=[pltpu.VMEM((RET_HEADS, RET_DK, RET_DV), F32),
                        pltpu.VMEM((8, d), F32),
                        pltpu.VMEM((tm, q_w), F32),
                        pltpu.VMEM((tm, q_w), F32),
                        pltpu.VMEM((tm, v_w), BF16),
                        pltpu.VMEM((tm, v_w), F32),
                        pltpu.VMEM((tm, d), BF16),
                        pltpu.VMEM((2, N_EXPERTS, tm), F32),
                        pltpu.VMEM((N_EXPERTS, 1), F32)],
        compiler_params=pltpu.CompilerParams(dimension_semantics=("arbitrary",),
                                             vmem_limit_bytes=VMEM_LIMIT_BYTES),
        name="mixer",
    )(x, posf, mod, *weights)


def _first_argmax(v, idx, n):
    m = jnp.max(v, axis=0, keepdims=True)
    return m, jnp.min(jnp.where(v == m, idx, n), axis=0, keepdims=True)


def _route_phases(logits_ref, bias_ref, tri_ref, eidx_ref, w_ref, rank_ref, cnt_ref, carry_ref, count_it):
    tn = logits_ref.shape[1]
    neg = F32(-jnp.inf)
    score = _sigmoid(logits_ref[...])
    choice = score + bias_ref[...]

    grow = lax.broadcasted_iota(jnp.int32, (GROUP_SIZE, tn), 0)
    gscores = []
    for g in range(N_GROUPS):
        cg = choice[g * GROUP_SIZE:(g + 1) * GROUP_SIZE]
        m1, i1 = _first_argmax(cg, grow, GROUP_SIZE)
        m2 = jnp.max(jnp.where(grow == i1, neg, cg), axis=0, keepdims=True)
        gscores.append(m1 + m2)
    cur = jnp.concatenate(gscores, axis=0)
    yield
    gidx = lax.broadcasted_iota(jnp.int32, (N_GROUPS, tn), 0)
    keep = jnp.zeros((N_GROUPS, tn), F32)
    for _ in range(TOPK_GROUPS):
        _, ig = _first_argmax(cur, gidx, N_GROUPS)
        hit = gidx == ig
        keep = jnp.where(hit, 1.0, keep)
        cur = jnp.where(hit, neg, cur)
    cur = jnp.concatenate(
        [jnp.where(keep[g:g + 1] > 0.0, choice[g * GROUP_SIZE:(g + 1) * GROUP_SIZE], neg)
         for g in range(N_GROUPS)], axis=0)
    yield

    erow = lax.broadcasted_iota(jnp.int32, (N_EXPERTS, tn), 0)
    eidx, wts = [], []
    member = jnp.zeros((N_EXPERTS, tn), F32)
    for _ in range(TOP_K):
        _, ie = _first_argmax(cur, erow, N_EXPERTS)
        hit = erow == ie
        eidx.append(ie)
        wts.append(jnp.sum(jnp.where(hit, score, 0.0), axis=0, keepdims=True))
        member = member + hit.astype(F32)
        cur = jnp.where(hit, neg, cur)
        yield
    wsum = wts[0]
    for k in range(1, TOP_K):
        wsum = wsum + wts[k]

    before = jnp.dot(member.astype(BF16), tri_ref[...], preferred_element_type=F32) + carry_ref[...]
    ranks = []
    for k in range(TOP_K):
        ranks.append(jnp.sum(jnp.where(erow == eidx[k], before, 0.0), axis=0, keepdims=True))
        if k % 2 == 1:
            yield
    carry_ref[...] = carry_ref[...] + count_it.astype(F32) * jnp.sum(member, axis=1, keepdims=True)

    eidx_ref[...] = jnp.concatenate(eidx, axis=0)
    w_ref[...] = jnp.concatenate([w / wsum * ROUTED_SCALE for w in wts], axis=0)
    rank_ref[...] = jnp.concatenate(ranks, axis=0).astype(jnp.int32)
    cnt_ref[...] = carry_ref[...]


def _dest_kernel(eidx_ref, rank_ref, pstart_ref, dest_ref):
    tn = eidx_ref.shape[1]
    erow = lax.broadcasted_iota(jnp.int32, (N_EXPERTS, tn), 0)
    start = pstart_ref[...]
    dest = rank_ref[...] + jnp.concatenate(
        [jnp.sum(jnp.where(erow == eidx_ref[k:k + 1, :], start, 0), axis=0, keepdims=True) for k in range(TOP_K)],
        axis=0)
    for c in range(dest_ref.shape[0]):
        dest_ref[c] = dest[:, c * LANES:(c + 1) * LANES]


def _dest(eidx, rank, pstart):
    k, t = eidx.shape
    tn = DEST_TOKENS
    return pl.pallas_call(
        _dest_kernel,
        grid=(t // tn,),
        in_specs=[pl.BlockSpec((k, tn), lambda i: (0, i)),
                  pl.BlockSpec((k, tn), lambda i: (0, i)),
                  pl.BlockSpec((N_EXPERTS, 1), lambda i: (0, 0))],
        out_specs=pl.BlockSpec((tn // LANES, k, LANES), lambda i: (i, 0, 0)),
        out_shape=jax.ShapeDtypeStruct((t // LANES, k, LANES), jnp.int32),
        name="dest",
    )(eidx, rank, pstart)


def _sc_mesh_and_workers():
    mesh = plsc.VectorSubcoreMesh(core_axis_name="c", subcore_axis_name="s")
    return mesh, mesh.num_cores, mesh.num_cores * mesh.num_subcores


def _worker_id(num_cores):
    return lax.axis_index("s") * num_cores + lax.axis_index("c")


def _dispatch(h2p, dest3, n_rows):
    planes, t, lanes = h2p.shape
    n_chunks = dest3.shape[0]
    mesh, num_cores, workers = _sc_mesh_and_workers()
    chunks_per_worker = n_chunks // workers
    assert chunks_per_worker * workers == n_chunks and n_chunks * lanes == t

    def body(h_hbm, d_hbm, xs_hbm, idx_v, rows_v, load_sem, scatter_sem):
        wid = _worker_id(num_cores)

        @pl.loop(0, chunks_per_worker)
        def _(c):
            chunk = wid * chunks_per_worker + c
            tok = pl.ds(chunk * lanes, lanes)
            pltpu.sync_copy(d_hbm.at[chunk], idx_v)
            loads = [None] * planes
            scatters = [None] * planes
            loads[0] = pltpu.async_copy(h_hbm.at[0, tok], rows_v.at[0], load_sem.at[0])
            for j in range(planes):
                loads[j].wait()
                if j + 1 < planes:
                    if j >= 1:
                        for cp in scatters[j - 1]:
                            cp.wait()
                    loads[j + 1] = pltpu.async_copy(h_hbm.at[j + 1, tok], rows_v.at[(j + 1) % 2],
                                                    load_sem.at[(j + 1) % 2])
                scatters[j] = [pltpu.async_copy(rows_v.at[j % 2], xs_hbm.at[j].at[idx_v.at[k]],
                                                scatter_sem.at[j % 2]) for k in range(TOP_K)]
            for j in range(max(planes - 2, 0), planes):
                for cp in scatters[j]:
                    cp.wait()

    return pl.kernel(
        body,
        out_type=jax.ShapeDtypeStruct((planes, n_rows, lanes), h2p.dtype),
        mesh=mesh,
        scratch_types=[pltpu.VMEM((TOP_K, lanes), jnp.int32),
                       pltpu.VMEM((2, lanes, lanes), h2p.dtype),
                       pltpu.SemaphoreType.DMA((2,)),
                       pltpu.SemaphoreType.DMA((2,))],
        name="dispatch",
    )(h2p, dest3)


def _gather(ybuf, dest3, t):
    planes, _, lanes = ybuf.shape
    n_chunks = dest3.shape[0]
    mesh, num_cores, workers = _sc_mesh_and_workers()
    chunks_per_worker = n_chunks // workers
    nbuf = GATHER_BUFFERS
    lag = nbuf // 2
    assert chunks_per_worker * workers == n_chunks and n_chunks * lanes == t

    def body(y_hbm, d_hbm, yg_hbm, idx_v, rows_v, gather_sem, store_sem):
        wid = _worker_id(num_cores)

        @pl.loop(0, chunks_per_worker)
        def _(c):
            chunk = wid * chunks_per_worker + c
            tok = pl.ds(chunk * lanes, lanes)
            pltpu.sync_copy(d_hbm.at[chunk], idx_v)

            @pl.loop(0, TOP_K)
            def _(k):
                gathers = [None] * planes
                stores = [None] * planes

                def store(j):
                    gathers[j].wait()
                    stores[j] = pltpu.async_copy(rows_v.at[j % nbuf], yg_hbm.at[k, j, tok], store_sem.at[j % nbuf])

                for j in range(planes):
                    if j >= nbuf:
                        stores[j - nbuf].wait()
                    gathers[j] = pltpu.async_copy(y_hbm.at[j].at[idx_v.at[k]], rows_v.at[j % nbuf],
                                                  gather_sem.at[j % nbuf])
                    if j >= lag:
                        store(j - lag)
                for j in range(max(planes - lag, 0), planes):
                    store(j)
                for j in range(max(planes - nbuf, 0), planes):
                    stores[j].wait()

    return pl.kernel(
        body,
        out_type=jax.ShapeDtypeStruct((TOP_K, planes, t, lanes), ybuf.dtype),
        mesh=mesh,
        scratch_types=[pltpu.VMEM((TOP_K, lanes), jnp.int32),
                       pltpu.VMEM((nbuf, lanes, lanes), ybuf.dtype),
                       pltpu.SemaphoreType.DMA((nbuf,)),
                       pltpu.SemaphoreType.DMA((nbuf,))],
        name="gather",
    )(ybuf, dest3)


def _experts_kernel(be_ref, nv_ref, nu_ref, em_ref, x_ref, wg_hbm, wu_hbm, wd_hbm, o_ref,
                    wgs_ref, wus_ref, wds_ref, wgb_ref, wub_ref, wdb_ref, sems):
    planes, step_rows, lanes = x_ref.shape
    bm = EXPERT_ROWS
    blocks_per_step = step_rows // bm
    step = pl.program_id(0)
    last_step = (nu_ref[0] - 1) // blocks_per_step

    def weight_copies(expert):
        slot = em_ref[2, expert]
        return [pltpu.make_async_copy(src.at[expert], dst.at[slot], sems.at[slot, n])
                for n, (src, dst) in enumerate(((wg_hbm, wgs_ref), (wu_hbm, wus_ref), (wd_hbm, wds_ref)))]

    def start_if_other(expert, current):
        @pl.when(expert != current)
        def _():
            for cp in weight_copies(expert):
                cp.start()

    def swiglu_rows(row0, m, n_valid):
        rows = pl.ds(row0, m)
        valid = lax.broadcasted_iota(jnp.int32, (m, lanes), 0) < n_valid
        g = None
        u = None
        for j in range(planes):
            xj = _unpack_pairs_f32(jnp.where(valid, x_ref[j, rows, :], jnp.uint32(0))).astype(BF16)
            wrows = pl.ds(2 * lanes * j, 2 * lanes)
            gj = jnp.dot(xj, wgb_ref[wrows, :], preferred_element_type=F32)
            uj = jnp.dot(xj, wub_ref[wrows, :], preferred_element_type=F32)
            g = gj if g is None else g + gj
            u = uj if u is None else u + uj
        out = jnp.dot((_silu(g) * u).astype(BF16), wdb_ref[...], preferred_element_type=F32)
        for j in range(planes):
            o_ref[j, rows, :] = _pack_bf16_pairs(out[:, 2 * lanes * j:2 * lanes * (j + 1)])

    def block_step(sb):
        i = step * blocks_per_step + sb
        row0 = pl.multiple_of(sb * bm, bm)
        rows = pl.ds(row0, bm)
        active = i < nu_ref[0]
        e = be_ref[i]
        new_expert = jnp.logical_or(i == 0, e != be_ref[jnp.maximum(i - 1, 0)])
        nxt_i = jnp.minimum(i + 1, be_ref.shape[0] - 1)
        pair = jnp.logical_and(jnp.logical_and(active, sb + 1 < blocks_per_step),
                               jnp.logical_and(i + 1 < nu_ref[0], be_ref[nxt_i] == e))

        @pl.when(jnp.logical_and(active, new_expert))
        def _():
            @pl.when(i == 0)
            def _():
                for cp in weight_copies(e):
                    cp.start()
                start_if_other(em_ref[0, e], e)

            for cp in weight_copies(e):
                cp.wait()
            slot = em_ref[2, e]
            wgb_ref[...] = wgs_ref[slot].astype(BF16)
            wub_ref[...] = wus_ref[slot].astype(BF16)
            wdb_ref[...] = wds_ref[slot].astype(BF16)
            start_if_other(em_ref[1, e], e)

        @pl.when(pair)
        def _():
            swiglu_rows(row0, 2 * bm, bm + nv_ref[nxt_i])

        @pl.when(jnp.logical_and(active, jnp.logical_not(pair)))
        def _():
            swiglu_rows(row0, bm, nv_ref[i])

        @pl.when(jnp.logical_and(jnp.logical_not(active), step == last_step))
        def _():
            for j in range(planes):
                o_ref[j, rows, :] = jnp.zeros((bm, lanes), o_ref.dtype)

        return sb + jnp.where(pair, 2, 1)

    lax.while_loop(lambda sb: sb < blocks_per_step, block_step, jnp.int32(0))


def _experts(block_e, n_valid, n_used, expert_meta, xs, wg, wu, wd):
    planes, p, lanes = xs.shape
    e, d, f = wg.shape
    step_rows = EXPERT_ROWS * EXPERT_BLOCKS_PER_STEP
    assert p % step_rows == 0

    def row_map(i, be, nv, nu, nxt):
        return (0, jnp.minimum(i, (nu[0] - 1) // EXPERT_BLOCKS_PER_STEP), 0)

    return pl.pallas_call(
        _experts_kernel,
        grid_spec=pltpu.PrefetchScalarGridSpec(
            num_scalar_prefetch=4,
            grid=(p // step_rows,),
            in_specs=[pl.BlockSpec((planes, step_rows, lanes), row_map),
                      pl.BlockSpec(memory_space=pl.ANY),
                      pl.BlockSpec(memory_space=pl.ANY),
                      pl.BlockSpec(memory_space=pl.ANY)],
            out_specs=pl.BlockSpec((planes, step_rows, lanes), row_map),
            scratch_shapes=[pltpu.VMEM((2, d, f), F32), pltpu.VMEM((2, d, f), F32), pltpu.VMEM((2, f, d), F32),
                            pltpu.VMEM((d, f), BF16), pltpu.VMEM((d, f), BF16), pltpu.VMEM((f, d), BF16),
                            pltpu.SemaphoreType.DMA((2, 3))]),
        out_shape=jax.ShapeDtypeStruct((planes, p, lanes), jnp.uint32),
        compiler_params=pltpu.CompilerParams(dimension_semantics=("arbitrary",),
                                             vmem_limit_bytes=VMEM_LIMIT_BYTES),
        name="experts",
    )(block_e, n_valid, n_used, expert_meta, xs, wg, wu, wd)


def _combine_kernel(w_ref, x2_ref, g2_ref, gfin_ref, yg_ref, *maybe_prev_and_out):
    o_ref = maybe_prev_and_out[-1]
    w = w_ref[...]
    planes = yg_ref.shape[1]
    routed = None
    for k in range(TOP_K):
        rows = jnp.concatenate([_unpack_pairs_f32(yg_ref[k, j]) for j in range(planes)], axis=1) * w[:, k:k + 1]
        routed = rows if routed is None else routed + rows
    o_ref[0] = _rms(x2_ref[0] + g2_ref[0] * routed) * gfin_ref[...]


def _combine(wtok, x2, g2, gfin, yg, seg, prev_out):
    b, s, d = x2.shape
    _, planes, seg_tokens, lanes = yg.shape
    tc = COMBINE_TOKENS
    nt = seg_tokens // tc
    per_seq = s // seg_tokens
    assert nt * tc == seg_tokens and per_seq * seg_tokens == s
    bi, t0 = seg // per_seq, (seg % per_seq) * nt
    in_specs = [pl.BlockSpec((tc, TOP_K), lambda j: (seg * nt + j, 0)),
                pl.BlockSpec((1, tc, d), lambda j: (bi, t0 + j, 0)),
                pl.BlockSpec((1, 1, d), lambda j: (bi, 0, 0)),
                pl.BlockSpec((1, d), lambda j: (0, 0)),
                pl.BlockSpec((TOP_K, planes, tc, lanes), lambda j: (0, 0, j, 0))]
    args = [wtok, x2, g2, gfin, yg]
    aliases = {}
    if prev_out is not None:
        in_specs.append(pl.BlockSpec(memory_space=pl.ANY))
        args.append(prev_out)
        aliases = {len(args) - 1: 0}
    return pl.pallas_call(
        _combine_kernel,
        grid=(nt,),
        in_specs=in_specs,
        out_specs=pl.BlockSpec((1, tc, d), lambda j: (bi, t0 + j, 0)),
        out_shape=jax.ShapeDtypeStruct((b, s, d), F32),
        input_output_aliases=aliases,
        compiler_params=pltpu.CompilerParams(dimension_semantics=("arbitrary",),
                                             vmem_limit_bytes=VMEM_LIMIT_BYTES),
        name="combine",
    )(*args)


def kernel(x, c, positions, w_ada, b_ada, norm_mix_g, w_in, ret_norm_g, conv_w, w_br_ret, w_br_conv, w_out,
           norm_ffn_g, w_router, router_bias, w_exp_gate, w_exp_up, w_exp_down, w_sh_gate, w_sh_up, w_sh_down,
           norm_final_g):
    b, s, d = x.shape
    t = b * s
    depth = w_in.shape[0]
    assert depth == 1, "the combine kernel applies the final norm, so exactly one layer is supported"
    posf = positions.astype(F32)[:, :, None]
    c8 = jnp.zeros((8, d), F32).at[:b].set(c)
    bm = EXPERT_ROWS
    n_blocks = t * TOP_K // bm + N_EXPERTS
    n_rows = n_blocks * bm

    for l in range(depth):
        mod = _ada(c8, w_ada[l], b_ada[l][None, :])[:b].reshape(b, 6, d)
        x2, h2p, eidx, wts, rank, counts = _mixer(
            x, posf, mod, norm_mix_g[l][None, :], w_in[l].astype(BF16), ret_norm_g[l][None, :], conv_w[l],
            w_br_ret[l].astype(BF16), w_br_conv[l].astype(BF16), w_out[l].astype(BF16), norm_ffn_g[l][None, :],
            w_router[l].T, jnp.concatenate([w_sh_gate[l], w_sh_up[l]], axis=1).astype(BF16),
            w_sh_down[l].astype(BF16), router_bias[l][:, None])

        cnt = counts[:, 0].astype(jnp.int32)
        pcnt = (cnt + bm - 1) // bm * bm
        pend = jnp.cumsum(pcnt)
        pstart = pend - pcnt
        n_used = jnp.maximum(pend[-1] // bm, 1).astype(jnp.int32)[None]
        block_start = jnp.arange(n_blocks, dtype=jnp.int32) * bm
        block_e = jnp.minimum(jnp.sum((pend[None, :] <= block_start[:, None]).astype(jnp.int32), axis=1),
                              N_EXPERTS - 1)
        eid = jnp.arange(N_EXPERTS, dtype=jnp.int32)
        seg_end = jnp.sum(jnp.where(block_e[:, None] == eid[None, :], (pstart + cnt)[None, :], 0), axis=1)
        n_valid = jnp.clip(seg_end - block_start, 0, bm).astype(jnp.int32)
        later_used = jnp.logical_and(eid[None, :] > eid[:, None], cnt[None, :] > 0)
        next_expert = jnp.min(jnp.where(later_used, eid[None, :], N_EXPERTS), axis=1)
        next_expert = jnp.where(next_expert == N_EXPERTS, eid, next_expert)
        after_next = jnp.sum(jnp.where(next_expert[:, None] == eid[None, :], next_expert[None, :], 0), axis=1)
        after_next = jnp.where(after_next == next_expert, eid, after_next)
        stage_slot = (jnp.cumsum((cnt > 0).astype(jnp.int32)) - 1) % 2
        expert_meta = jnp.stack([next_expert, after_next, stage_slot]).astype(jnp.int32)

        dest3 = _dest(eidx, rank, pstart[:, None])
        xs = _dispatch(h2p, dest3, n_rows)
        ybuf = _experts(block_e, n_valid, n_used, expert_meta, xs, w_exp_gate[l], w_exp_up[l], w_exp_down[l])
        chunks = t // LANES // COMBINE_SEGMENTS
        out = None
        for seg in range(COMBINE_SEGMENTS):
            yg = _gather(ybuf, dest3[seg * chunks:(seg + 1) * chunks], chunks * LANES)
            out = _combine(wts.T, x2, mod[:, 5:6, :], norm_final_g[None, :], yg, seg, out)
        x = out
    return x
```

```python
import functools

import numpy as np
import jax
import jax.numpy as jnp
from jax import lax
from jax.experimental import pallas as pl
from jax.experimental.pallas import tpu as pltpu
from jax.experimental.pallas import tpu_sc as plsc

RET_HEADS = 4
RET_DK = 128
RET_DV = 256
RET_CHUNK = 128
ROPE_THETA = 10000.0
CONV_K = 3
N_EXPERTS = 256
TOP_K = 8
N_GROUPS = 8
TOPK_GROUPS = 4
GROUP_SIZE = N_EXPERTS // N_GROUPS
ROUTED_SCALE = 2.5
NORM_EPS = 1e-6

MIXER_TOKENS = 512
DEST_TOKENS = 512
EXPERT_ROWS = 256
EXPERT_BLOCKS_PER_STEP = 8
COMBINE_TOKENS = 256
COMBINE_SEGMENTS = 4
LANES = 128
GATHER_BUFFERS = 4

VMEM_LIMIT_BYTES = 56 * 1024 * 1024

F32 = jnp.float32
BF16 = jnp.bfloat16
HIGHEST = lax.Precision.HIGHEST


def _sigmoid(v):
    return 0.5 * jnp.tanh(0.5 * v) + 0.5


def _silu(v):
    return v * _sigmoid(v)


def _rms(v):
    return v * lax.rsqrt(jnp.mean(v * v, axis=-1, keepdims=True) + NORM_EPS)


def _resident(shape):
    nd = len(shape)
    return pl.BlockSpec(shape, lambda *_: (0,) * nd, pipeline_mode=pl.Buffered(1))


def _ada_kernel(c_ref, w_ref, b_ref, o_ref):
    c = c_ref[...]
    o_ref[...] = jnp.dot(_silu(c), w_ref[...], precision=HIGHEST, preferred_element_type=F32) + b_ref[...]


def _ada(c8, w, b):
    d, n = w.shape
    tn = 1024
    return pl.pallas_call(
        _ada_kernel,
        grid=(n // tn,),
        in_specs=[pl.BlockSpec((8, d), lambda j: (0, 0)),
                  pl.BlockSpec((d, tn), lambda j: (0, j)),
                  pl.BlockSpec((1, tn), lambda j: (0, j))],
        out_specs=pl.BlockSpec((8, tn), lambda j: (0, j)),
        out_shape=jax.ShapeDtypeStruct((8, n), F32),
        name="ada",
    )(c8, w, b)


def _retention_constants():
    f32 = np.float32
    c = RET_CHUNK
    log_g = np.log1p(-(f32(2.0) ** (f32(-5.0) - np.arange(RET_HEADS, dtype=f32)))).astype(f32)
    idx = np.arange(c, dtype=f32)
    diff = idx[:, None] - idx[None, :]
    intra = np.where(diff >= 0, np.exp(log_g[:, None, None] * np.maximum(diff, f32(0.0))), f32(0.0))
    k_decay = np.exp(log_g[:, None] * (f32(c - 1) - idx))
    q_decay = np.exp(log_g[:, None] * (idx + f32(1.0)))
    chunk_decay = np.exp(log_g * f32(c))
    kd = np.broadcast_to(k_decay[:, :, None], (RET_HEADS, c, RET_DK))
    qd = np.broadcast_to(q_decay[:, :, None], (RET_HEADS, c, RET_DK))
    cd = np.broadcast_to(chunk_decay[:, None, None], (RET_HEADS, 1, RET_DV))
    inv_freq = f32(ROPE_THETA) ** (-np.arange(0, RET_DK, 2, dtype=f32) / f32(RET_DK))
    inv_freq = np.concatenate([inv_freq, inv_freq])[None, :]
    sign = np.concatenate([-np.ones((RET_DK // 2,), f32), np.ones((RET_DK // 2,), f32)])[None, :]
    return tuple(jnp.asarray(np.ascontiguousarray(v, dtype=f32)) for v in (intra, qd, kd, cd, inv_freq, sign))


def _pack_bf16_pairs(v):
    w = v.shape[1] // 2
    lo = lax.bitcast_convert_type(v[:, :w].astype(BF16).astype(F32), jnp.uint32)
    hi = lax.bitcast_convert_type(v[:, w:].astype(BF16).astype(F32), jnp.uint32)
    return (lo >> 16) | (hi & jnp.uint32(0xFFFF0000))


def _unpack_pairs_f32(word):
    lo = lax.bitcast_convert_type(word << 16, F32)
    hi = lax.bitcast_convert_type(word & jnp.uint32(0xFFFF0000), F32)
    return jnp.concatenate([lo, hi], axis=1)


def _mix_tile(x_ref, pos_ref, mod_ref, gmix_ref, win_ref, invf_ref, sign_ref, intra_ref, qd_ref, kd_ref, cd_ref,
              retg_ref, convw_ref, wbr_ref, wbc_ref, wout_ref, gffn_ref, wrh_ref, wrl_ref, wshgu_ref, wshd_ref,
              x2_ref, h2p_ref, lg_ref, state_ref, carry_ref, q_ref, k_ref, v_ref, ret_ref, hb_ref, *, background):
    def tick():
        next(background, None)

    tm, d = x_ref.shape[1], x_ref.shape[2]
    q_w = RET_HEADS * RET_DK
    v_w = RET_HEADS * RET_DV
    offs = np.cumsum([0, q_w, q_w, v_w, v_w, d, d, d, d, d])

    x = x_ref[0]
    mod = mod_ref[0]
    sh1, sc1, g1, sh2, sc2, g2 = [mod[i:i + 1] for i in range(6)]
    hb_ref[...] = ((_rms(x) * gmix_ref[...]) * (1.0 + sc1) + sh1).astype(BF16)

    def proj(i):
        return jnp.dot(hb_ref[...], win_ref[:, offs[i]:offs[i + 1]], preferred_element_type=F32)

    ang = pos_ref[0] * invf_ref[...]
    cosv = jnp.cos(ang)
    sinv = jnp.sin(ang) * sign_ref[...]

    def rope(t):
        return jnp.concatenate(
            [t[:, h * RET_DK:(h + 1) * RET_DK] * cosv
             + pltpu.roll(t[:, h * RET_DK:(h + 1) * RET_DK], RET_DK // 2, 1) * sinv
             for h in range(RET_HEADS)], axis=1)

    q_ref[...] = rope(proj(0)) * (RET_DK ** -0.5)
    tick()
    k_ref[...] = rope(proj(1))
    tick()
    v_ref[...] = proj(2).astype(BF16)
    tick()

    for c in range(tm // RET_CHUNK):
        rows = pl.ds(c * RET_CHUNK, RET_CHUNK)
        for h in range(RET_HEADS):
            qh = q_ref[rows, h * RET_DK:(h + 1) * RET_DK]
            kh = k_ref[rows, h * RET_DK:(h + 1) * RET_DK]
            vh = v_ref[rows, h * RET_DV:(h + 1) * RET_DV]
            scores = lax.dot_general(qh.astype(BF16), kh.astype(BF16), (((1,), (1,)), ((), ())),
                                     preferred_element_type=F32) * intra_ref[h]
            inner = jnp.dot(scores.astype(BF16), vh, preferred_element_type=F32)
            st = state_ref[h]
            cross = jnp.dot((qh * qd_ref[h]).astype(BF16), st.astype(BF16), preferred_element_type=F32)
            kv = lax.dot_general((kh * kd_ref[h]).astype(BF16), vh, (((0,), (0,)), ((), ())),
                                 preferred_element_type=F32)
            state_ref[h] = st * cd_ref[h] + kv
            ret_ref[rows, h * RET_DV:(h + 1) * RET_DV] = _rms(inner + cross)
        tick()

    y_ret = jnp.dot((_silu(proj(3)) * (ret_ref[...] * retg_ref[...])).astype(BF16), wbr_ref[...],
                    preferred_element_type=F32)
    tick()

    z = proj(5) * proj(4)
    row = lax.broadcasted_iota(jnp.int32, z.shape, 0)
    prev1 = carry_ref[7:8, :]
    prev2 = carry_ref[6:7, :]
    z1 = jnp.where(row == 0, prev1, pltpu.roll(z, 1, 0))
    z2 = jnp.where(row == 0, prev2, jnp.where(row == 1, prev1, pltpu.roll(z, 2, 0)))
    conv = convw_ref[0:1, :] * z2 + convw_ref[1:2, :] * z1 + convw_ref[2:3, :] * z
    carry_ref[...] = z[tm - 8:tm, :]
    tick()
    y_conv = jnp.dot((proj(6) * conv).astype(BF16), wbc_ref[...], preferred_element_type=F32)
    tick()

    mix = _sigmoid(proj(7)) * y_ret + _sigmoid(proj(8)) * y_conv
    tick()
    x1 = x + g1 * jnp.dot(mix.astype(BF16), wout_ref[...], preferred_element_type=F32)
    tick()

    h2 = (_rms(x1) * gffn_ref[...]) * (1.0 + sc2) + sh2
    hi_f32 = lax.bitcast_convert_type(lax.bitcast_convert_type(h2, jnp.uint32) & jnp.uint32(0xFFFF0000), F32)
    h2_hi = hi_f32.astype(BF16)
    h2_lo = (h2 - hi_f32).astype(BF16)

    def nt_dot(a, b):
        return lax.dot_general(a, b, (((1,), (1,)), ((), ())), preferred_element_type=F32)

    lg_ref[...] = nt_dot(wrh_ref[...], h2_hi) + (nt_dot(wrh_ref[...], h2_lo) + nt_dot(wrl_ref[...], h2_hi))
    gu = jnp.dot(h2.astype(BF16), wshgu_ref[...], preferred_element_type=F32)
    f = gu.shape[1] // 2
    shared = jnp.dot((_silu(gu[:, :f]) * gu[:, f:]).astype(BF16), wshd_ref[...], preferred_element_type=F32)
    x2_ref[0] = x1 + g2 * shared
    for j in range(h2p_ref.shape[0]):
        h2p_ref[j] = _pack_bf16_pairs(h2[:, 2 * LANES * j:2 * LANES * (j + 1)])
    for _ in background:
        pass


def _mixer_kernel(*refs, tiles_per_seq):
    mix_inputs, (bias_ref, tri_ref) = refs[:21], refs[21:23]
    x2_ref, h2p_ref, eidx_ref, w_ref, rank_ref, cnt_ref = refs[23:29]
    state_ref, carry_ref, q_ref, k_ref, v_ref, ret_ref, hb_ref, lg_ref, rcarry_ref = refs[29:]
    s = pl.program_id(0)
    n_tiles = pl.num_programs(0) - 1

    @pl.when(s == 0)
    def _():
        lg_ref[...] = jnp.zeros_like(lg_ref)
        rcarry_ref[...] = jnp.zeros_like(rcarry_ref)

    @pl.when(s % tiles_per_seq == 0)
    def _():
        state_ref[...] = jnp.zeros_like(state_ref)
        carry_ref[...] = jnp.zeros_like(carry_ref)

    def route_previous_tile():
        return _route_phases(lg_ref.at[(s + 1) % 2], bias_ref, tri_ref, eidx_ref, w_ref, rank_ref, cnt_ref,
                             rcarry_ref, (s > 0).astype(jnp.int32))

    @pl.when(s == n_tiles)
    def _():
        for _ in route_previous_tile():
            pass

    @pl.when(s < n_tiles)
    def _():
        _mix_tile(*mix_inputs, x2_ref, h2p_ref, lg_ref.at[s % 2], state_ref, carry_ref, q_ref, k_ref, v_ref,
                  ret_ref, hb_ref, background=route_previous_tile())


def _mixer(x, posf, mod, gmix, win, retg, convw, wbr, wbc, wout, gffn, wr, wshgu, wshd, bias):
    b, s, d = x.shape
    tm = MIXER_TOKENS
    nt = s // tm
    n_tiles = b * nt
    intra, qd, kd, cd, invf, sign = _retention_constants()
    q_w, v_w = RET_HEADS * RET_DK, RET_HEADS * RET_DV
    hi_f32 = lax.bitcast_convert_type(lax.bitcast_convert_type(wr, jnp.uint32) & jnp.uint32(0xFFFF0000), F32)
    wr_hi = hi_f32.astype(BF16)
    wr_lo = (wr - hi_f32).astype(BF16)
    tri = jnp.asarray(np.triu(np.ones((tm, tm), np.float32), 1), BF16)
    weights = (gmix, win, invf, sign, intra, qd, kd, cd, retg, convw, wbr, wbc, wout, gffn, wr_hi, wr_lo,
               wshgu, wshd, bias, tri)

    def tile(i):
        return jnp.minimum(i, n_tiles - 1)

    def routed(i):
        return (0, jnp.maximum(i - 1, 0))

    return pl.pallas_call(
        functools.partial(_mixer_kernel, tiles_per_seq=nt),
        grid=(n_tiles + 1,),
        in_specs=[pl.BlockSpec((1, tm, d), lambda i: (tile(i) // nt, tile(i) % nt, 0)),
                  pl.BlockSpec((1, tm, 1), lambda i: (tile(i) // nt, tile(i) % nt, 0)),
                  pl.BlockSpec((1, 6, d), lambda i: (tile(i) // nt, 0, 0))]
                 + [_resident(w.shape) for w in weights],
        out_specs=[pl.BlockSpec((1, tm, d), lambda i: (tile(i) // nt, tile(i) % nt, 0)),
                   pl.BlockSpec((d // (2 * LANES), tm, LANES), lambda i: (0, tile(i), 0)),
                   pl.BlockSpec((TOP_K, tm), routed),
                   pl.BlockSpec((TOP_K, tm), routed),
                   pl.BlockSpec((TOP_K, tm), routed),
                   pl.BlockSpec((N_EXPERTS, 1), lambda i: (0, 0))],
        out_shape=[jax.ShapeDtypeStruct((b, s, d), F32),
                   jax.ShapeDtypeStruct((d // (2 * LANES), b * s, LANES), jnp.uint32),
                   jax.ShapeDtypeStruct((TOP_K, b * s), jnp.int32),
                   jax.ShapeDtypeStruct((TOP_K, b * s), F32),
                   jax.ShapeDtypeStruct((TOP_K, b * s), jnp.int32),
                   jax.ShapeDtypeStruct((N_EXPERTS, 1), F32)],
        scratch_shapes=[pltpu.VMEM((RET_HEADS, RET_DK, RET_DV), F32),
                        pltpu.VMEM((8, d), F32),
                        pltpu.VMEM((tm, q_w), F32),
                        pltpu.VMEM((tm, q_w), F32),
                        pltpu.VMEM((tm, v_w), BF16),
                        pltpu.VMEM((tm, v_w), F32),
                        pltpu.VMEM((tm, d), BF16),
                        pltpu.VMEM((2, N_EXPERTS, tm), F32),
                        pltpu.VMEM((N_EXPERTS, 1), F32)],
        compiler_params=pltpu.CompilerParams(dimension_semantics=("arbitrary",),
                                             vmem_limit_bytes=VMEM_LIMIT_BYTES),
        name="mixer",
    )(x, posf, mod, *weights)


def _first_argmax(v, idx, n):
    m = jnp.max(v, axis=0, keepdims=True)
    return m, jnp.min(jnp.where(v == m, idx, n), axis=0, keepdims=True)


def _route_phases(logits_ref, bias_ref, tri_ref, eidx_ref, w_ref, rank_ref, cnt_ref, carry_ref, count_it):
    tn = logits_ref.shape[1]
    neg = F32(-jnp.inf)
    score = _sigmoid(logits_ref[...])
    choice = score + bias_ref[...]

    grow = lax.broadcasted_iota(jnp.int32, (GROUP_SIZE, tn), 0)
    gscores = []
    for g in range(N_GROUPS):
        cg = choice[g * GROUP_SIZE:(g + 1) * GROUP_SIZE]
        m1, i1 = _first_argmax(cg, grow, GROUP_SIZE)
        m2 = jnp.max(jnp.where(grow == i1, neg, cg), axis=0, keepdims=True)
        gscores.append(m1 + m2)
    cur = jnp.concatenate(gscores, axis=0)
    yield
    gidx = lax.broadcasted_iota(jnp.int32, (N_GROUPS, tn), 0)
    keep = jnp.zeros((N_GROUPS, tn), F32)
    for _ in range(TOPK_GROUPS):
        _, ig = _first_argmax(cur, gidx, N_GROUPS)
        hit = gidx == ig
        keep = jnp.where(hit, 1.0, keep)
        cur = jnp.where(hit, neg, cur)
    cur = jnp.concatenate(
        [jnp.where(keep[g:g + 1] > 0.0, choice[g * GROUP_SIZE:(g + 1) * GROUP_SIZE], neg)
         for g in range(N_GROUPS)], axis=0)
    yield

    erow = lax.broadcasted_iota(jnp.int32, (N_EXPERTS, tn), 0)
    eidx, wts = [], []
    member = jnp.zeros((N_EXPERTS, tn), F32)
    for _ in range(TOP_K):
        _, ie = _first_argmax(cur, erow, N_EXPERTS)
        hit = erow == ie
        eidx.append(ie)
        wts.append(jnp.sum(jnp.where(hit, score, 0.0), axis=0, keepdims=True))
        member = member + hit.astype(F32)
        cur = jnp.where(hit, neg, cur)
        yield
    wsum = wts[0]
    for k in range(1, TOP_K):
        wsum = wsum + wts[k]

    before = jnp.dot(member.astype(BF16), tri_ref[...], preferred_element_type=F32) + carry_ref[...]
    ranks = []
    for k in range(TOP_K):
        ranks.append(jnp.sum(jnp.where(erow == eidx[k], before, 0.0), axis=0, keepdims=True))
        if k % 2 == 1:
            yield
    carry_ref[...] = carry_ref[...] + count_it.astype(F32) * jnp.sum(member, axis=1, keepdims=True)

    eidx_ref[...] = jnp.concatenate(eidx, axis=0)
    w_ref[...] = jnp.concatenate([w / wsum * ROUTED_SCALE for w in wts], axis=0)
    rank_ref[...] = jnp.concatenate(ranks, axis=0).astype(jnp.int32)
    cnt_ref[...] = carry_ref[...]


def _dest_kernel(eidx_ref, rank_ref, pstart_ref, dest_ref):
    tn = eidx_ref.shape[1]
    erow = lax.broadcasted_iota(jnp.int32, (N_EXPERTS, tn), 0)
    start = pstart_ref[...]
    dest = rank_ref[...] + jnp.concatenate(
        [jnp.sum(jnp.where(erow == eidx_ref[k:k + 1, :], start, 0), axis=0, keepdims=True) for k in range(TOP_K)],
        axis=0)
    for c in range(dest_ref.shape[0]):
        dest_ref[c] = dest[:, c * LANES:(c + 1) * LANES]


def _dest(eidx, rank, pstart):
    k, t = eidx.shape
    tn = DEST_TOKENS
    return pl.pallas_call(
        _dest_kernel,
        grid=(t // tn,),
        in_specs=[pl.BlockSpec((k, tn), lambda i: (0, i)),
                  pl.BlockSpec((k, tn), lambda i: (0, i)),
                  pl.BlockSpec((N_EXPERTS, 1), lambda i: (0, 0))],
        out_specs=pl.BlockSpec((tn // LANES, k, LANES), lambda i: (i, 0, 0)),
        out_shape=jax.ShapeDtypeStruct((t // LANES, k, LANES), jnp.int32),
        name="dest",
    )(eidx, rank, pstart)


def _sc_mesh_and_workers():
    mesh = plsc.VectorSubcoreMesh(core_axis_name="c", subcore_axis_name="s")
    return mesh, mesh.num_cores, mesh.num_cores * mesh.num_subcores


def _worker_id(num_cores):
    return lax.axis_index("s") * num_cores + lax.axis_index("c")


def _dispatch(h2p, dest3, n_rows):
    planes, t, lanes = h2p.shape
    n_chunks = dest3.shape[0]
    mesh, num_cores, workers = _sc_mesh_and_workers()
    chunks_per_worker = n_chunks // workers
    assert chunks_per_worker * workers == n_chunks and n_chunks * lanes == t

    def body(h_hbm, d_hbm, xs_hbm, idx_v, rows_v, load_sem, scatter_sem):
        wid = _worker_id(num_cores)

        @pl.loop(0, chunks_per_worker)
        def _(c):
            chunk = wid * chunks_per_worker + c
            tok = pl.ds(chunk * lanes, lanes)
            pltpu.sync_copy(d_hbm.at[chunk], idx_v)
            loads = [None] * planes
            scatters = [None] * planes
            loads[0] = pltpu.async_copy(h_hbm.at[0, tok], rows_v.at[0], load_sem.at[0])
            for j in range(planes):
                loads[j].wait()
                if j + 1 < planes:
                    if j >= 1:
                        for cp in scatters[j - 1]:
                            cp.wait()
                    loads[j + 1] = pltpu.async_copy(h_hbm.at[j + 1, tok], rows_v.at[(j + 1) % 2],
                                                    load_sem.at[(j + 1) % 2])
                scatters[j] = [pltpu.async_copy(rows_v.at[j % 2], xs_hbm.at[j].at[idx_v.at[k]],
                                                scatter_sem.at[j % 2]) for k in range(TOP_K)]
            for j in range(max(planes - 2, 0), planes):
                for cp in scatters[j]:
                    cp.wait()

    return pl.kernel(
        body,
        out_type=jax.ShapeDtypeStruct((planes, n_rows, lanes), h2p.dtype),
        mesh=mesh,
        scratch_types=[pltpu.VMEM((TOP_K, lanes), jnp.int32),
                       pltpu.VMEM((2, lanes, lanes), h2p.dtype),
                       pltpu.SemaphoreType.DMA((2,)),
                       pltpu.SemaphoreType.DMA((2,))],
        name="dispatch",
    )(h2p, dest3)


def _gather(ybuf, dest3, t):
    planes, _, lanes = ybuf.shape
    n_chunks = dest3.shape[0]
    mesh, num_cores, workers = _sc_mesh_and_workers()
    chunks_per_worker = n_chunks // workers
    nbuf = GATHER_BUFFERS
    lag = nbuf // 2
    assert chunks_per_worker * workers == n_chunks and n_chunks * lanes == t

    def body(y_hbm, d_hbm, yg_hbm, idx_v, rows_v, gather_sem, store_sem):
        wid = _worker_id(num_cores)

        @pl.loop(0, chunks_per_worker)
        def _(c):
            chunk = wid * chunks_per_worker + c
            tok = pl.ds(chunk * lanes, lanes)
            pltpu.sync_copy(d_hbm.at[chunk], idx_v)

            @pl.loop(0, TOP_K)
            def _(k):
                gathers = [None] * planes
                stores = [None] * planes

                def store(j):
                    gathers[j].wait()
                    stores[j] = pltpu.async_copy(rows_v.at[j % nbuf], yg_hbm.at[k, j, tok], store_sem.at[j % nbuf])

                for j in range(planes):
                    if j >= nbuf:
                        stores[j - nbuf].wait()
                    gathers[j] = pltpu.async_copy(y_hbm.at[j].at[idx_v.at[k]], rows_v.at[j % nbuf],
                                                  gather_sem.at[j % nbuf])
                    if j >= lag:
                        store(j - lag)
                for j in range(max(planes - lag, 0), planes):
                    store(j)
                for j in range(max(planes - nbuf, 0), planes):
                    stores[j].wait()

    return pl.kernel(
        body,
        out_type=jax.ShapeDtypeStruct((TOP_K, planes, t, lanes), ybuf.dtype),
        mesh=mesh,
        scratch_types=[pltpu.VMEM((TOP_K, lanes), jnp.int32),
                       pltpu.VMEM((nbuf, lanes, lanes), ybuf.dtype),
                       pltpu.SemaphoreType.DMA((nbuf,)),
                       pltpu.SemaphoreType.DMA((nbuf,))],
        name="gather",
    )(ybuf, dest3)


def _experts_kernel(be_ref, nv_ref, nu_ref, em_ref, x_ref, wg_hbm, wu_hbm, wd_hbm, o_ref,
                    wgs_ref, wus_ref, wds_ref, wgb_ref, wub_ref, wdb_ref, sems):
    planes, step_rows, lanes = x_ref.shape
    bm = EXPERT_ROWS
    blocks_per_step = step_rows // bm
    step = pl.program_id(0)
    last_step = (nu_ref[0] - 1) // blocks_per_step

    def weight_copies(expert):
        slot = em_ref[2, expert]
        return [pltpu.make_async_copy(src.at[expert], dst.at[slot], sems.at[slot, n])
                for n, (src, dst) in enumerate(((wg_hbm, wgs_ref), (wu_hbm, wus_ref), (wd_hbm, wds_ref)))]

    def start_if_other(expert, current):
        @pl.when(expert != current)
        def _():
            for cp in weight_copies(expert):
                cp.start()

    def swiglu_rows(row0, m, n_valid):
        rows = pl.ds(row0, m)
        valid = lax.broadcasted_iota(jnp.int32, (m, lanes), 0) < n_valid
        g = None
        u = None
        for j in range(planes):
            xj = _unpack_pairs_f32(jnp.where(valid, x_ref[j, rows, :], jnp.uint32(0))).astype(BF16)
            wrows = pl.ds(2 * lanes * j, 2 * lanes)
            gj = jnp.dot(xj, wgb_ref[wrows, :], preferred_element_type=F32)
            uj = jnp.dot(xj, wub_ref[wrows, :], preferred_element_type=F32)
            g = gj if g is None else g + gj
            u = uj if u is None else u + uj
        out = jnp.dot((_silu(g) * u).astype(BF16), wdb_ref[...], preferred_element_type=F32)
        for j in range(planes):
            o_ref[j, rows, :] = _pack_bf16_pairs(out[:, 2 * lanes * j:2 * lanes * (j + 1)])

    def block_step(sb):
        i = step * blocks_per_step + sb
        row0 = pl.multiple_of(sb * bm, bm)
        rows = pl.ds(row0, bm)
        active = i < nu_ref[0]
        e = be_ref[i]
        new_expert = jnp.logical_or(i == 0, e != be_ref[jnp.maximum(i - 1, 0)])
        nxt_i = jnp.minimum(i + 1, be_ref.shape[0] - 1)
        pair = jnp.logical_and(jnp.logical_and(active, sb + 1 < blocks_per_step),
                               jnp.logical_and(i + 1 < nu_ref[0], be_ref[nxt_i] == e))

        @pl.when(jnp.logical_and(active, new_expert))
        def _():
            @pl.when(i == 0)
            def _():
                for cp in weight_copies(e):
                    cp.start()
                start_if_other(em_ref[0, e], e)

            for cp in weight_copies(e):
                cp.wait()
            slot = em_ref[2, e]
            wgb_ref[...] = wgs_ref[slot].astype(BF16)
            wub_ref[...] = wus_ref[slot].astype(BF16)
            wdb_ref[...] = wds_ref[slot].astype(BF16)
            start_if_other(em_ref[1, e], e)

        @pl.when(pair)
        def _():
            swiglu_rows(row0, 2 * bm, bm + nv_ref[nxt_i])

        @pl.when(jnp.logical_and(active, jnp.logical_not(pair)))
        def _():
            swiglu_rows(row0, bm, nv_ref[i])

        @pl.when(jnp.logical_and(jnp.logical_not(active), step == last_step))
        def _():
            for j in range(planes):
                o_ref[j, rows, :] = jnp.zeros((bm, lanes), o_ref.dtype)

        return sb + jnp.where(pair, 2, 1)

    lax.while_loop(lambda sb: sb < blocks_per_step, block_step, jnp.int32(0))


def _experts(block_e, n_valid, n_used, expert_meta, xs, wg, wu, wd):
    planes, p, lanes = xs.shape
    e, d, f = wg.shape
    step_rows = EXPERT_ROWS * EXPERT_BLOCKS_PER_STEP
    assert p % step_rows == 0

    def row_map(i, be, nv, nu, nxt):
        return (0, jnp.minimum(i, (nu[0] - 1) // EXPERT_BLOCKS_PER_STEP), 0)

    return pl.pallas_call(
        _experts_kernel,
        grid_spec=pltpu.PrefetchScalarGridSpec(
            num_scalar_prefetch=4,
            grid=(p // step_rows,),
            in_specs=[pl.BlockSpec((planes, step_rows, lanes), row_map),
                      pl.BlockSpec(memory_space=pl.ANY),
                      pl.BlockSpec(memory_space=pl.ANY),
                      pl.BlockSpec(memory_space=pl.ANY)],
            out_specs=pl.BlockSpec((planes, step_rows, lanes), row_map),
            scratch_shapes=[pltpu.VMEM((2, d, f), F32), pltpu.VMEM((2, d, f), F32), pltpu.VMEM((2, f, d), F32),
                            pltpu.VMEM((d, f), BF16), pltpu.VMEM((d, f), BF16), pltpu.VMEM((f, d), BF16),
                            pltpu.SemaphoreType.DMA((2, 3))]),
        out_shape=jax.ShapeDtypeStruct((planes, p, lanes), jnp.uint32),
        compiler_params=pltpu.CompilerParams(dimension_semantics=("arbitrary",),
                                             vmem_limit_bytes=VMEM_LIMIT_BYTES),
        name="experts",
    )(block_e, n_valid, n_used, expert_meta, xs, wg, wu, wd)


def _combine_kernel(w_ref, x2_ref, g2_ref, gfin_ref, yg_ref, *maybe_prev_and_out):
    o_ref = maybe_prev_and_out[-1]
    tc = x2_ref.shape[1]
    eye = (lax.broadcasted_iota(jnp.int32, (tc, tc), 0) == lax.broadcasted_iota(jnp.int32, (tc, tc), 1)).astype(F32)
    w = lax.dot_general(eye, w_ref[...], (((1,), (1,)), ((), ())), precision=HIGHEST, preferred_element_type=F32)
    planes = yg_ref.shape[1]
    routed = None
    for k in range(TOP_K):
        rows = jnp.concatenate([_unpack_pairs_f32(yg_ref[k, j]) for j in range(planes)], axis=1) * w[:, k:k + 1]
        routed = rows if routed is None else routed + rows
    o_ref[0] = _rms(x2_ref[0] + g2_ref[0] * routed) * gfin_ref[...]


def _combine(wtok, x2, g2, gfin, yg, seg, prev_out):
    b, s, d = x2.shape
    _, planes, seg_tokens, lanes = yg.shape
    tc = COMBINE_TOKENS
    nt = seg_tokens // tc
    per_seq = s // seg_tokens
    assert nt * tc == seg_tokens and per_seq * seg_tokens == s
    bi, t0 = seg // per_seq, (seg % per_seq) * nt
    in_specs = [pl.BlockSpec((TOP_K, tc), lambda j: (0, seg * nt + j)),
                pl.BlockSpec((1, tc, d), lambda j: (bi, t0 + j, 0)),
                pl.BlockSpec((1, 1, d), lambda j: (bi, 0, 0)),
                pl.BlockSpec((1, d), lambda j: (0, 0)),
                pl.BlockSpec((TOP_K, planes, tc, lanes), lambda j: (0, 0, j, 0))]
    args = [wtok, x2, g2, gfin, yg]
    aliases = {}
    if prev_out is not None:
        in_specs.append(pl.BlockSpec(memory_space=pl.ANY))
        args.append(prev_out)
        aliases = {len(args) - 1: 0}
    return pl.pallas_call(
        _combine_kernel,
        grid=(nt,),
        in_specs=in_specs,
        out_specs=pl.BlockSpec((1, tc, d), lambda j: (bi, t0 + j, 0)),
        out_shape=jax.ShapeDtypeStruct((b, s, d), F32),
        input_output_aliases=aliases,
        compiler_params=pltpu.CompilerParams(dimension_semantics=("arbitrary",),
                                             vmem_limit_bytes=VMEM_LIMIT_BYTES),
        name="combine",
    )(*args)


def kernel(x, c, positions, w_ada, b_ada, norm_mix_g, w_in, ret_norm_g, conv_w, w_br_ret, w_br_conv, w_out,
           norm_ffn_g, w_router, router_bias, w_exp_gate, w_exp_up, w_exp_down, w_sh_gate, w_sh_up, w_sh_down,
           norm_final_g):
    b, s, d = x.shape
    t = b * s
    depth = w_in.shape[0]
    assert depth == 1, "the combine kernel applies the final norm, so exactly one layer is supported"
    posf = positions.astype(F32)[:, :, None]
    c8 = jnp.zeros((8, d), F32).at[:b].set(c)
    bm = EXPERT_ROWS
    n_blocks = t * TOP_K // bm + N_EXPERTS
    n_rows = n_blocks * bm

    for l in range(depth):
        mod = _ada(c8, w_ada[l], b_ada[l][None, :])[:b].reshape(b, 6, d)
        x2, h2p, eidx, wts, rank, counts = _mixer(
            x, posf, mod, norm_mix_g[l][None, :], w_in[l].astype(BF16), ret_norm_g[l][None, :], conv_w[l],
            w_br_ret[l].astype(BF16), w_br_conv[l].astype(BF16), w_out[l].astype(BF16), norm_ffn_g[l][None, :],
            w_router[l].T, jnp.concatenate([w_sh_gate[l], w_sh_up[l]], axis=1).astype(BF16),
            w_sh_down[l].astype(BF16), router_bias[l][:, None])

        cnt = counts[:, 0].astype(jnp.int32)
        pcnt = (cnt + bm - 1) // bm * bm
        pend = jnp.cumsum(pcnt)
        pstart = pend - pcnt
        n_used = jnp.maximum(pend[-1] // bm, 1).astype(jnp.int32)[None]
        block_start = jnp.arange(n_blocks, dtype=jnp.int32) * bm
        block_e = jnp.minimum(jnp.sum((pend[None, :] <= block_start[:, None]).astype(jnp.int32), axis=1),
                              N_EXPERTS - 1)
        eid = jnp.arange(N_EXPERTS, dtype=jnp.int32)
        seg_end = jnp.sum(jnp.where(block_e[:, None] == eid[None, :], (pstart + cnt)[None, :], 0), axis=1)
        n_valid = jnp.clip(seg_end - block_start, 0, bm).astype(jnp.int32)
        later_used = jnp.logical_and(eid[None, :] > eid[:, None], cnt[None, :] > 0)
        next_expert = jnp.min(jnp.where(later_used, eid[None, :], N_EXPERTS), axis=1)
        next_expert = jnp.where(next_expert == N_EXPERTS, eid, next_expert)
        after_next = jnp.sum(jnp.where(next_expert[:, None] == eid[None, :], next_expert[None, :], 0), axis=1)
        after_next = jnp.where(after_next == next_expert, eid, after_next)
        stage_slot = (jnp.cumsum((cnt > 0).astype(jnp.int32)) - 1) % 2
        expert_meta = jnp.stack([next_expert, after_next, stage_slot]).astype(jnp.int32)

        dest3 = _dest(eidx, rank, pstart[:, None])
        xs = _dispatch(h2p, dest3, n_rows)
        ybuf = _experts(block_e, n_valid, n_used, expert_meta, xs, w_exp_gate[l], w_exp_up[l], w_exp_down[l])
        chunks = t // LANES // COMBINE_SEGMENTS
        out = None
        for seg in range(COMBINE_SEGMENTS):
            yg = _gather(ybuf, dest3[seg * chunks:(seg + 1) * chunks], chunks * LANES)
            out = _combine(wts, x2, mod[:, 5:6, :], norm_final_g[None, :], yg, seg, out)
        x = out
    return x
```

```python
import functools

import numpy as np
import jax
import jax.numpy as jnp
from jax import lax
from jax.experimental import pallas as pl
from jax.experimental.pallas import tpu as pltpu
from jax.experimental.pallas import tpu_sc as plsc

RET_HEADS = 4
RET_DK = 128
RET_DV = 256
RET_CHUNK = 128
ROPE_THETA = 10000.0
CONV_K = 3
N_EXPERTS = 256
TOP_K = 8
N_GROUPS = 8
TOPK_GROUPS = 4
GROUP_SIZE = N_EXPERTS // N_GROUPS
ROUTED_SCALE = 2.5
NORM_EPS = 1e-6

MIXER_TOKENS = 512
DEST_TOKENS = 512
EXPERT_ROWS = 256
EXPERT_BLOCKS_PER_STEP = 8
EXPERT_STAGES = 3
COMBINE_TOKENS = 256
COMBINE_SEGMENTS = 4
LANES = 128
GATHER_BUFFERS = 4

VMEM_LIMIT_BYTES = 56 * 1024 * 1024

F32 = jnp.float32
BF16 = jnp.bfloat16
HIGHEST = lax.Precision.HIGHEST


def _sigmoid(v):
    return 0.5 * jnp.tanh(0.5 * v) + 0.5


def _silu(v):
    return v * _sigmoid(v)


def _rms(v):
    return v * lax.rsqrt(jnp.mean(v * v, axis=-1, keepdims=True) + NORM_EPS)


def _resident(shape):
    nd = len(shape)
    return pl.BlockSpec(shape, lambda *_: (0,) * nd, pipeline_mode=pl.Buffered(1))


def _ada_kernel(c_ref, w_ref, b_ref, o_ref):
    c = c_ref[...]
    o_ref[...] = jnp.dot(_silu(c), w_ref[...], precision=HIGHEST, preferred_element_type=F32) + b_ref[...]


def _ada(c8, w, b):
    d, n = w.shape
    tn = 1024
    return pl.pallas_call(
        _ada_kernel,
        grid=(n // tn,),
        in_specs=[pl.BlockSpec((8, d), lambda j: (0, 0)),
                  pl.BlockSpec((d, tn), lambda j: (0, j)),
                  pl.BlockSpec((1, tn), lambda j: (0, j))],
        out_specs=pl.BlockSpec((8, tn), lambda j: (0, j)),
        out_shape=jax.ShapeDtypeStruct((8, n), F32),
        name="ada",
    )(c8, w, b)


def _retention_constants():
    f32 = np.float32
    c = RET_CHUNK
    log_g = np.log1p(-(f32(2.0) ** (f32(-5.0) - np.arange(RET_HEADS, dtype=f32)))).astype(f32)
    idx = np.arange(c, dtype=f32)
    diff = idx[:, None] - idx[None, :]
    intra = np.where(diff >= 0, np.exp(log_g[:, None, None] * np.maximum(diff, f32(0.0))), f32(0.0))
    k_decay = np.exp(log_g[:, None] * (f32(c - 1) - idx))
    q_decay = np.exp(log_g[:, None] * (idx + f32(1.0)))
    chunk_decay = np.exp(log_g * f32(c))
    kd = np.broadcast_to(k_decay[:, :, None], (RET_HEADS, c, RET_DK))
    qd = np.broadcast_to(q_decay[:, :, None], (RET_HEADS, c, RET_DK))
    cd = np.broadcast_to(chunk_decay[:, None, None], (RET_HEADS, 1, RET_DV))
    inv_freq = f32(ROPE_THETA) ** (-np.arange(0, RET_DK, 2, dtype=f32) / f32(RET_DK))
    inv_freq = np.concatenate([inv_freq, inv_freq])[None, :]
    sign = np.concatenate([-np.ones((RET_DK // 2,), f32), np.ones((RET_DK // 2,), f32)])[None, :]
    return tuple(jnp.asarray(np.ascontiguousarray(v, dtype=f32)) for v in (intra, qd, kd, cd, inv_freq, sign))


def _pack_bf16_pairs(v):
    w = v.shape[1] // 2
    lo = lax.bitcast_convert_type(v[:, :w].astype(BF16).astype(F32), jnp.uint32)
    hi = lax.bitcast_convert_type(v[:, w:].astype(BF16).astype(F32), jnp.uint32)
    return (lo >> 16) | (hi & jnp.uint32(0xFFFF0000))


def _unpack_pairs_f32(word):
    lo = lax.bitcast_convert_type(word << 16, F32)
    hi = lax.bitcast_convert_type(word & jnp.uint32(0xFFFF0000), F32)
    return jnp.concatenate([lo, hi], axis=1)


def _mix_tile(x_ref, pos_ref, mod_ref, gmix_ref, win_ref, invf_ref, sign_ref, intra_ref, qd_ref, kd_ref, cd_ref,
              retg_ref, convw_ref, wbr_ref, wbc_ref, wout_ref, gffn_ref, wrh_ref, wrl_ref, wshgu_ref, wshd_ref,
              x2_ref, h2p_ref, lg_ref, state_ref, carry_ref, q_ref, k_ref, v_ref, ret_ref, hb_ref, *, background):
    def tick():
        next(background, None)

    tm, d = x_ref.shape[1], x_ref.shape[2]
    q_w = RET_HEADS * RET_DK
    v_w = RET_HEADS * RET_DV
    offs = np.cumsum([0, q_w, q_w, v_w, v_w, d, d, d, d, d])

    x = x_ref[0]
    mod = mod_ref[0]
    sh1, sc1, g1, sh2, sc2, g2 = [mod[i:i + 1] for i in range(6)]
    hb_ref[...] = ((_rms(x) * gmix_ref[...]) * (1.0 + sc1) + sh1).astype(BF16)

    def proj(i):
        return jnp.dot(hb_ref[...], win_ref[:, offs[i]:offs[i + 1]], preferred_element_type=F32)

    ang = pos_ref[0] * invf_ref[...]
    cosv = jnp.cos(ang)
    sinv = jnp.sin(ang) * sign_ref[...]

    def rope(t):
        return jnp.concatenate(
            [t[:, h * RET_DK:(h + 1) * RET_DK] * cosv
             + pltpu.roll(t[:, h * RET_DK:(h + 1) * RET_DK], RET_DK // 2, 1) * sinv
             for h in range(RET_HEADS)], axis=1)

    q_ref[...] = rope(proj(0)) * (RET_DK ** -0.5)
    tick()
    k_ref[...] = rope(proj(1))
    tick()
    v_ref[...] = proj(2).astype(BF16)
    tick()

    for c in range(tm // RET_CHUNK):
        rows = pl.ds(c * RET_CHUNK, RET_CHUNK)
        for h in range(RET_HEADS):
            qh = q_ref[rows, h * RET_DK:(h + 1) * RET_DK]
            kh = k_ref[rows, h * RET_DK:(h + 1) * RET_DK]
            vh = v_ref[rows, h * RET_DV:(h + 1) * RET_DV]
            scores = lax.dot_general(qh.astype(BF16), kh.astype(BF16), (((1,), (1,)), ((), ())),
                                     preferred_element_type=F32) * intra_ref[h]
            inner = jnp.dot(scores.astype(BF16), vh, preferred_element_type=F32)
            st = state_ref[h]
            cross = jnp.dot((qh * qd_ref[h]).astype(BF16), st.astype(BF16), preferred_element_type=F32)
            kv = lax.dot_general((kh * kd_ref[h]).astype(BF16), vh, (((0,), (0,)), ((), ())),
                                 preferred_element_type=F32)
            state_ref[h] = st * cd_ref[h] + kv
            ret_ref[rows, h * RET_DV:(h + 1) * RET_DV] = _rms(inner + cross)
        tick()

    y_ret = jnp.dot((_silu(proj(3)) * (ret_ref[...] * retg_ref[...])).astype(BF16), wbr_ref[...],
                    preferred_element_type=F32)
    tick()

    z = proj(5) * proj(4)
    row = lax.broadcasted_iota(jnp.int32, z.shape, 0)
    prev1 = carry_ref[7:8, :]
    prev2 = carry_ref[6:7, :]
    z1 = jnp.where(row == 0, prev1, pltpu.roll(z, 1, 0))
    z2 = jnp.where(row == 0, prev2, jnp.where(row == 1, prev1, pltpu.roll(z, 2, 0)))
    conv = convw_ref[0:1, :] * z2 + convw_ref[1:2, :] * z1 + convw_ref[2:3, :] * z
    carry_ref[...] = z[tm - 8:tm, :]
    tick()
    y_conv = jnp.dot((proj(6) * conv).astype(BF16), wbc_ref[...], preferred_element_type=F32)
    tick()

    mix = _sigmoid(proj(7)) * y_ret + _sigmoid(proj(8)) * y_conv
    tick()
    x1 = x + g1 * jnp.dot(mix.astype(BF16), wout_ref[...], preferred_element_type=F32)
    tick()

    h2 = (_rms(x1) * gffn_ref[...]) * (1.0 + sc2) + sh2
    hi_f32 = lax.bitcast_convert_type(lax.bitcast_convert_type(h2, jnp.uint32) & jnp.uint32(0xFFFF0000), F32)
    h2_hi = hi_f32.astype(BF16)
    h2_lo = (h2 - hi_f32).astype(BF16)

    def nt_dot(a, b):
        return lax.dot_general(a, b, (((1,), (1,)), ((), ())), preferred_element_type=F32)

    lg_ref[...] = nt_dot(wrh_ref[...], h2_hi) + (nt_dot(wrh_ref[...], h2_lo) + nt_dot(wrl_ref[...], h2_hi))
    gu = jnp.dot(h2.astype(BF16), wshgu_ref[...], preferred_element_type=F32)
    f = gu.shape[1] // 2
    shared = jnp.dot((_silu(gu[:, :f]) * gu[:, f:]).astype(BF16), wshd_ref[...], preferred_element_type=F32)
    x2_ref[0] = x1 + g2 * shared
    for j in range(h2p_ref.shape[0]):
        h2p_ref[j] = _pack_bf16_pairs(h2[:, 2 * LANES * j:2 * LANES * (j + 1)])
    for _ in background:
        pass


def _mixer_kernel(*refs, tiles_per_seq):
    mix_inputs, (bias_ref, tri_ref) = refs[:21], refs[21:23]
    x2_ref, h2p_ref, eidx_ref, w_ref, rank_ref, cnt_ref = refs[23:29]
    state_ref, carry_ref, q_ref, k_ref, v_ref, ret_ref, hb_ref, lg_ref, rcarry_ref = refs[29:]
    s = pl.program_id(0)
    n_tiles = pl.num_programs(0) - 1

    @pl.when(s == 0)
    def _():
        lg_ref[...] = jnp.zeros_like(lg_ref)
        rcarry_ref[...] = jnp.zeros_like(rcarry_ref)

    @pl.when(s % tiles_per_seq == 0)
    def _():
        state_ref[...] = jnp.zeros_like(state_ref)
        carry_ref[...] = jnp.zeros_like(carry_ref)

    def route_previous_tile():
        return _route_phases(lg_ref.at[(s + 1) % 2], bias_ref, tri_ref, eidx_ref, w_ref, rank_ref, cnt_ref,
                             rcarry_ref, (s > 0).astype(jnp.int32))

    @pl.when(s == n_tiles)
    def _():
        for _ in route_previous_tile():
            pass

    @pl.when(s < n_tiles)
    def _():
        _mix_tile(*mix_inputs, x2_ref, h2p_ref, lg_ref.at[s % 2], state_ref, carry_ref, q_ref, k_ref, v_ref,
                  ret_ref, hb_ref, background=route_previous_tile())


def _mixer(x, posf, mod, gmix, win, retg, convw, wbr, wbc, wout, gffn, wr, wshgu, wshd, bias):
    b, s, d = x.shape
    tm = MIXER_TOKENS
    nt = s // tm
    n_tiles = b * nt
    intra, qd, kd, cd, invf, sign = _retention_constants()
    q_w, v_w = RET_HEADS * RET_DK, RET_HEADS * RET_DV
    hi_f32 = lax.bitcast_convert_type(lax.bitcast_convert_type(wr, jnp.uint32) & jnp.uint32(0xFFFF0000), F32)
    wr_hi = hi_f32.astype(BF16)
    wr_lo = (wr - hi_f32).astype(BF16)
    tri = jnp.asarray(np.triu(np.ones((tm, tm), np.float32), 1), BF16)
    weights = (gmix, win, invf, sign, intra, qd, kd, cd, retg, convw, wbr, wbc, wout, gffn, wr_hi, wr_lo,
               wshgu, wshd, bias, tri)

    def tile(i):
        return jnp.minimum(i, n_tiles - 1)

    def routed(i):
        return (0, jnp.maximum(i - 1, 0))

    return pl.pallas_call(
        functools.partial(_mixer_kernel, tiles_per_seq=nt),
        grid=(n_tiles + 1,),
        in_specs=[pl.BlockSpec((1, tm, d), lambda i: (tile(i) // nt, tile(i) % nt, 0)),
                  pl.BlockSpec((1, tm, 1), lambda i: (tile(i) // nt, tile(i) % nt, 0)),
                  pl.BlockSpec((1, 6, d), lambda i: (tile(i) // nt, 0, 0))]
                 + [_resident(w.shape) for w in weights],
        out_specs=[pl.BlockSpec((1, tm, d), lambda i: (tile(i) // nt, tile(i) % nt, 0)),
                   pl.BlockSpec((d // (2 * LANES), tm, LANES), lambda i: (0, tile(i), 0)),
                   pl.BlockSpec((TOP_K, tm), routed),
                   pl.BlockSpec((TOP_K, tm), routed),
                   pl.BlockSpec((TOP_K, tm), routed),
                   pl.BlockSpec((N_EXPERTS, 1), lambda i: (0, 0))],
        out_shape=[jax.ShapeDtypeStruct((b, s, d), F32),
                   jax.ShapeDtypeStruct((d // (2 * LANES), b * s, LANES), jnp.uint32),
                   jax.ShapeDtypeStruct((TOP_K, b * s), jnp.int32),
                   jax.ShapeDtypeStruct((TOP_K, b * s), F32),
                   jax.ShapeDtypeStruct((TOP_K, b * s), jnp.int32),
                   jax.ShapeDtypeStruct((N_EXPERTS, 1), F32)],
        scratch_shapes=[pltpu.VMEM((RET_HEADS, RET_DK, RET_DV), F32),
                        pltpu.VMEM((8, d), F32),
                        pltpu.VMEM((tm, q_w), F32),
                        pltpu.VMEM((tm, q_w), F32),
                        pltpu.VMEM((tm, v_w), BF16),
                        pltpu.VMEM((tm, v_w), F32),
                        pltpu.VMEM((tm, d), BF16),
                        pltpu.VMEM((2, N_EXPERTS, tm), F32),
                        pltpu.VMEM((N_EXPERTS, 1), F32)],
        compiler_params=pltpu.CompilerParams(dimension_semantics=("arbitrary",),
                                             vmem_limit_bytes=VMEM_LIMIT_BYTES),
        name="mixer",
    )(x, posf, mod, *weights)


def _first_argmax(v, idx, n):
    m = jnp.max(v, axis=0, keepdims=True)
    return m, jnp.min(jnp.where(v == m, idx, n), axis=0, keepdims=True)


def _route_phases(logits_ref, bias_ref, tri_ref, eidx_ref, w_ref, rank_ref, cnt_ref, carry_ref, count_it):
    tn = logits_ref.shape[1]
    neg = F32(-jnp.inf)
    score = _sigmoid(logits_ref[...])
    choice = score + bias_ref[...]

    grow = lax.broadcasted_iota(jnp.int32, (GROUP_SIZE, tn), 0)
    gscores = []
    for g in range(N_GROUPS):
        cg = choice[g * GROUP_SIZE:(g + 1) * GROUP_SIZE]
        m1, i1 = _first_argmax(cg, grow, GROUP_SIZE)
        m2 = jnp.max(jnp.where(grow == i1, neg, cg), axis=0, keepdims=True)
        gscores.append(m1 + m2)
    cur = jnp.concatenate(gscores, axis=0)
    yield
    gidx = lax.broadcasted_iota(jnp.int32, (N_GROUPS, tn), 0)
    keep = jnp.zeros((N_GROUPS, tn), F32)
    for _ in range(TOPK_GROUPS):
        _, ig = _first_argmax(cur, gidx, N_GROUPS)
        hit = gidx == ig
        keep = jnp.where(hit, 1.0, keep)
        cur = jnp.where(hit, neg, cur)
    cur = jnp.concatenate(
        [jnp.where(keep[g:g + 1] > 0.0, choice[g * GROUP_SIZE:(g + 1) * GROUP_SIZE], neg)
         for g in range(N_GROUPS)], axis=0)
    yield

    erow = lax.broadcasted_iota(jnp.int32, (N_EXPERTS, tn), 0)
    eidx, wts = [], []
    member = jnp.zeros((N_EXPERTS, tn), F32)
    for _ in range(TOP_K):
        _, ie = _first_argmax(cur, erow, N_EXPERTS)
        hit = erow == ie
        eidx.append(ie)
        wts.append(jnp.sum(jnp.where(hit, score, 0.0), axis=0, keepdims=True))
        member = member + hit.astype(F32)
        cur = jnp.where(hit, neg, cur)
        yield
    wsum = wts[0]
    for k in range(1, TOP_K):
        wsum = wsum + wts[k]

    before = jnp.dot(member.astype(BF16), tri_ref[...], preferred_element_type=F32) + carry_ref[...]
    ranks = []
    for k in range(TOP_K):
        ranks.append(jnp.sum(jnp.where(erow == eidx[k], before, 0.0), axis=0, keepdims=True))
        if k % 2 == 1:
            yield
    carry_ref[...] = carry_ref[...] + count_it.astype(F32) * jnp.sum(member, axis=1, keepdims=True)

    eidx_ref[...] = jnp.concatenate(eidx, axis=0)
    w_ref[...] = jnp.concatenate([w / wsum * ROUTED_SCALE for w in wts], axis=0)
    rank_ref[...] = jnp.concatenate(ranks, axis=0).astype(jnp.int32)
    cnt_ref[...] = carry_ref[...]


def _dest_kernel(eidx_ref, rank_ref, pstart_ref, dest_ref):
    tn = eidx_ref.shape[1]
    erow = lax.broadcasted_iota(jnp.int32, (N_EXPERTS, tn), 0)
    start = pstart_ref[...]
    dest = rank_ref[...] + jnp.concatenate(
        [jnp.sum(jnp.where(erow == eidx_ref[k:k + 1, :], start, 0), axis=0, keepdims=True) for k in range(TOP_K)],
        axis=0)
    for c in range(dest_ref.shape[0]):
        dest_ref[c] = dest[:, c * LANES:(c + 1) * LANES]


def _dest(eidx, rank, pstart):
    k, t = eidx.shape
    tn = DEST_TOKENS
    return pl.pallas_call(
        _dest_kernel,
        grid=(t // tn,),
        in_specs=[pl.BlockSpec((k, tn), lambda i: (0, i)),
                  pl.BlockSpec((k, tn), lambda i: (0, i)),
                  pl.BlockSpec((N_EXPERTS, 1), lambda i: (0, 0))],
        out_specs=pl.BlockSpec((tn // LANES, k, LANES), lambda i: (i, 0, 0)),
        out_shape=jax.ShapeDtypeStruct((t // LANES, k, LANES), jnp.int32),
        name="dest",
    )(eidx, rank, pstart)


def _sc_mesh_and_workers():
    mesh = plsc.VectorSubcoreMesh(core_axis_name="c", subcore_axis_name="s")
    return mesh, mesh.num_cores, mesh.num_cores * mesh.num_subcores


def _worker_id(num_cores):
    return lax.axis_index("s") * num_cores + lax.axis_index("c")


def _dispatch(h2p, dest3, n_rows):
    planes, t, lanes = h2p.shape
    n_chunks = dest3.shape[0]
    mesh, num_cores, workers = _sc_mesh_and_workers()
    chunks_per_worker = n_chunks // workers
    assert chunks_per_worker * workers == n_chunks and n_chunks * lanes == t

    def body(h_hbm, d_hbm, xs_hbm, idx_v, rows_v, load_sem, scatter_sem):
        wid = _worker_id(num_cores)

        @pl.loop(0, chunks_per_worker)
        def _(c):
            chunk = wid * chunks_per_worker + c
            tok = pl.ds(chunk * lanes, lanes)
            pltpu.sync_copy(d_hbm.at[chunk], idx_v)
            loads = [None] * planes
            scatters = [None] * planes
            loads[0] = pltpu.async_copy(h_hbm.at[0, tok], rows_v.at[0], load_sem.at[0])
            for j in range(planes):
                loads[j].wait()
                if j + 1 < planes:
                    if j >= 1:
                        for cp in scatters[j - 1]:
                            cp.wait()
                    loads[j + 1] = pltpu.async_copy(h_hbm.at[j + 1, tok], rows_v.at[(j + 1) % 2],
                                                    load_sem.at[(j + 1) % 2])
                scatters[j] = [pltpu.async_copy(rows_v.at[j % 2], xs_hbm.at[j].at[idx_v.at[k]],
                                                scatter_sem.at[j % 2]) for k in range(TOP_K)]
            for j in range(max(planes - 2, 0), planes):
                for cp in scatters[j]:
                    cp.wait()

    return pl.kernel(
        body,
        out_type=jax.ShapeDtypeStruct((planes, n_rows, lanes), h2p.dtype),
        mesh=mesh,
        scratch_types=[pltpu.VMEM((TOP_K, lanes), jnp.int32),
                       pltpu.VMEM((2, lanes, lanes), h2p.dtype),
                       pltpu.SemaphoreType.DMA((2,)),
                       pltpu.SemaphoreType.DMA((2,))],
        name="dispatch",
    )(h2p, dest3)


def _gather(ybuf, dest3, t):
    planes, _, lanes = ybuf.shape
    n_chunks = dest3.shape[0]
    mesh, num_cores, workers = _sc_mesh_and_workers()
    chunks_per_worker = n_chunks // workers
    nbuf = GATHER_BUFFERS
    lag = nbuf // 2
    assert chunks_per_worker * workers == n_chunks and n_chunks * lanes == t

    def body(y_hbm, d_hbm, yg_hbm, idx_v, rows_v, gather_sem, store_sem):
        wid = _worker_id(num_cores)

        @pl.loop(0, chunks_per_worker)
        def _(c):
            chunk = wid * chunks_per_worker + c
            tok = pl.ds(chunk * lanes, lanes)
            pltpu.sync_copy(d_hbm.at[chunk], idx_v)

            @pl.loop(0, TOP_K)
            def _(k):
                gathers = [None] * planes
                stores = [None] * planes

                def store(j):
                    gathers[j].wait()
                    stores[j] = pltpu.async_copy(rows_v.at[j % nbuf], yg_hbm.at[k, j, tok], store_sem.at[j % nbuf])

                for j in range(planes):
                    if j >= nbuf:
                        stores[j - nbuf].wait()
                    gathers[j] = pltpu.async_copy(y_hbm.at[j].at[idx_v.at[k]], rows_v.at[j % nbuf],
                                                  gather_sem.at[j % nbuf])
                    if j >= lag:
                        store(j - lag)
                for j in range(max(planes - lag, 0), planes):
                    store(j)
                for j in range(max(planes - nbuf, 0), planes):
                    stores[j].wait()

    return pl.kernel(
        body,
        out_type=jax.ShapeDtypeStruct((TOP_K, planes, t, lanes), ybuf.dtype),
        mesh=mesh,
        scratch_types=[pltpu.VMEM((TOP_K, lanes), jnp.int32),
                       pltpu.VMEM((nbuf, lanes, lanes), ybuf.dtype),
                       pltpu.SemaphoreType.DMA((nbuf,)),
                       pltpu.SemaphoreType.DMA((nbuf,))],
        name="gather",
    )(ybuf, dest3)


def _experts_kernel(be_ref, nv_ref, nu_ref, em_ref, x_ref, wg_hbm, wu_hbm, wd_hbm, o_ref,
                    wgs_ref, wus_ref, wds_ref, wgb_ref, wub_ref, wdb_ref, sems):
    planes, step_rows, lanes = x_ref.shape
    bm = EXPERT_ROWS
    blocks_per_step = step_rows // bm
    step = pl.program_id(0)
    last_step = (nu_ref[0] - 1) // blocks_per_step

    def weight_copies(expert):
        slot = em_ref[0, expert]
        return [pltpu.make_async_copy(src.at[expert], dst.at[slot], sems.at[slot, n])
                for n, (src, dst) in enumerate(((wg_hbm, wgs_ref), (wu_hbm, wus_ref), (wd_hbm, wds_ref)))]

    def start_if_any(expert):
        @pl.when(expert < N_EXPERTS)
        def _():
            for cp in weight_copies(expert):
                cp.start()

    def swiglu_rows(row0, m, n_valid):
        rows = pl.ds(row0, m)
        valid = lax.broadcasted_iota(jnp.int32, (m, lanes), 0) < n_valid
        g = None
        u = None
        for j in range(planes):
            xj = _unpack_pairs_f32(jnp.where(valid, x_ref[j, rows, :], jnp.uint32(0))).astype(BF16)
            wrows = pl.ds(2 * lanes * j, 2 * lanes)
            gj = jnp.dot(xj, wgb_ref[wrows, :], preferred_element_type=F32)
            uj = jnp.dot(xj, wub_ref[wrows, :], preferred_element_type=F32)
            g = gj if g is None else g + gj
            u = uj if u is None else u + uj
        out = jnp.dot((_silu(g) * u).astype(BF16), wdb_ref[...], preferred_element_type=F32)
        for j in range(planes):
            o_ref[j, rows, :] = _pack_bf16_pairs(out[:, 2 * lanes * j:2 * lanes * (j + 1)])

    def block_step(sb):
        i = step * blocks_per_step + sb
        row0 = pl.multiple_of(sb * bm, bm)
        rows = pl.ds(row0, bm)
        active = i < nu_ref[0]
        e = be_ref[i]
        new_expert = jnp.logical_or(i == 0, e != be_ref[jnp.maximum(i - 1, 0)])
        nxt_i = jnp.minimum(i + 1, be_ref.shape[0] - 1)
        pair = jnp.logical_and(jnp.logical_and(active, sb + 1 < blocks_per_step),
                               jnp.logical_and(i + 1 < nu_ref[0], be_ref[nxt_i] == e))

        @pl.when(jnp.logical_and(active, new_expert))
        def _():
            @pl.when(i == 0)
            def _():
                for cp in weight_copies(e):
                    cp.start()
                for ahead in range(1, EXPERT_STAGES):
                    start_if_any(em_ref[ahead, e])

            for cp in weight_copies(e):
                cp.wait()
            slot = em_ref[0, e]
            wgb_ref[...] = wgs_ref[slot].astype(BF16)
            wub_ref[...] = wus_ref[slot].astype(BF16)
            wdb_ref[...] = wds_ref[slot].astype(BF16)
            start_if_any(em_ref[EXPERT_STAGES, e])

        @pl.when(pair)
        def _():
            swiglu_rows(row0, 2 * bm, bm + nv_ref[nxt_i])

        @pl.when(jnp.logical_and(active, jnp.logical_not(pair)))
        def _():
            swiglu_rows(row0, bm, nv_ref[i])

        @pl.when(jnp.logical_and(jnp.logical_not(active), step == last_step))
        def _():
            for j in range(planes):
                o_ref[j, rows, :] = jnp.zeros((bm, lanes), o_ref.dtype)

        return sb + jnp.where(pair, 2, 1)

    lax.while_loop(lambda sb: sb < blocks_per_step, block_step, jnp.int32(0))


def _experts(block_e, n_valid, n_used, expert_meta, xs, wg, wu, wd):
    planes, p, lanes = xs.shape
    e, d, f = wg.shape
    step_rows = EXPERT_ROWS * EXPERT_BLOCKS_PER_STEP
    assert p % step_rows == 0

    def row_map(i, be, nv, nu, nxt):
        return (0, jnp.minimum(i, (nu[0] - 1) // EXPERT_BLOCKS_PER_STEP), 0)

    return pl.pallas_call(
        _experts_kernel,
        grid_spec=pltpu.PrefetchScalarGridSpec(
            num_scalar_prefetch=4,
            grid=(p // step_rows,),
            in_specs=[pl.BlockSpec((planes, step_rows, lanes), row_map),
                      pl.BlockSpec(memory_space=pl.ANY),
                      pl.BlockSpec(memory_space=pl.ANY),
                      pl.BlockSpec(memory_space=pl.ANY)],
            out_specs=pl.BlockSpec((planes, step_rows, lanes), row_map),
            scratch_shapes=[pltpu.VMEM((EXPERT_STAGES, d, f), F32), pltpu.VMEM((EXPERT_STAGES, d, f), F32),
                            pltpu.VMEM((EXPERT_STAGES, f, d), F32),
                            pltpu.VMEM((d, f), BF16), pltpu.VMEM((d, f), BF16), pltpu.VMEM((f, d), BF16),
                            pltpu.SemaphoreType.DMA((EXPERT_STAGES, 3))]),
        out_shape=jax.ShapeDtypeStruct((planes, p, lanes), jnp.uint32),
        compiler_params=pltpu.CompilerParams(dimension_semantics=("arbitrary",),
                                             vmem_limit_bytes=VMEM_LIMIT_BYTES),
        name="experts",
    )(block_e, n_valid, n_used, expert_meta, xs, wg, wu, wd)


def _combine_kernel(w_ref, x2_ref, g2_ref, gfin_ref, yg_ref, *maybe_prev_and_out):
    o_ref = maybe_prev_and_out[-1]
    tc = x2_ref.shape[1]
    eye = (lax.broadcasted_iota(jnp.int32, (tc, tc), 0) == lax.broadcasted_iota(jnp.int32, (tc, tc), 1)).astype(F32)
    w = lax.dot_general(eye, w_ref[...], (((1,), (1,)), ((), ())), precision=HIGHEST, preferred_element_type=F32)
    planes = yg_ref.shape[1]
    routed = None
    for k in range(TOP_K):
        rows = jnp.concatenate([_unpack_pairs_f32(yg_ref[k, j]) for j in range(planes)], axis=1) * w[:, k:k + 1]
        routed = rows if routed is None else routed + rows
    o_ref[0] = _rms(x2_ref[0] + g2_ref[0] * routed) * gfin_ref[...]


def _combine(wtok, x2, g2, gfin, yg, seg, prev_out):
    b, s, d = x2.shape
    _, planes, seg_tokens, lanes = yg.shape
    tc = COMBINE_TOKENS
    nt = seg_tokens // tc
    per_seq = s // seg_tokens
    assert nt * tc == seg_tokens and per_seq * seg_tokens == s
    bi, t0 = seg // per_seq, (seg % per_seq) * nt
    in_specs = [pl.BlockSpec((TOP_K, tc), lambda j: (0, seg * nt + j)),
                pl.BlockSpec((1, tc, d), lambda j: (bi, t0 + j, 0)),
                pl.BlockSpec((1, 1, d), lambda j: (bi, 0, 0)),
                pl.BlockSpec((1, d), lambda j: (0, 0)),
                pl.BlockSpec((TOP_K, planes, tc, lanes), lambda j: (0, 0, j, 0))]
    args = [wtok, x2, g2, gfin, yg]
    aliases = {}
    if prev_out is not None:
        in_specs.append(pl.BlockSpec(memory_space=pl.ANY))
        args.append(prev_out)
        aliases = {len(args) - 1: 0}
    return pl.pallas_call(
        _combine_kernel,
        grid=(nt,),
        in_specs=in_specs,
        out_specs=pl.BlockSpec((1, tc, d), lambda j: (bi, t0 + j, 0)),
        out_shape=jax.ShapeDtypeStruct((b, s, d), F32),
        input_output_aliases=aliases,
        compiler_params=pltpu.CompilerParams(dimension_semantics=("arbitrary",),
                                             vmem_limit_bytes=VMEM_LIMIT_BYTES),
        name="combine",
    )(*args)


def kernel(x, c, positions, w_ada, b_ada, norm_mix_g, w_in, ret_norm_g, conv_w, w_br_ret, w_br_conv, w_out,
           norm_ffn_g, w_router, router_bias, w_exp_gate, w_exp_up, w_exp_down, w_sh_gate, w_sh_up, w_sh_down,
           norm_final_g):
    b, s, d = x.shape
    t = b * s
    depth = w_in.shape[0]
    assert depth == 1, "the combine kernel applies the final norm, so exactly one layer is supported"
    posf = positions.astype(F32)[:, :, None]
    c8 = jnp.zeros((8, d), F32).at[:b].set(c)
    bm = EXPERT_ROWS
    n_blocks = t * TOP_K // bm + N_EXPERTS
    n_rows = n_blocks * bm

    for l in range(depth):
        mod = _ada(c8, w_ada[l], b_ada[l][None, :])[:b].reshape(b, 6, d)
        x2, h2p, eidx, wts, rank, counts = _mixer(
            x, posf, mod, norm_mix_g[l][None, :], w_in[l].astype(BF16), ret_norm_g[l][None, :], conv_w[l],
            w_br_ret[l].astype(BF16), w_br_conv[l].astype(BF16), w_out[l].astype(BF16), norm_ffn_g[l][None, :],
            w_router[l].T, jnp.concatenate([w_sh_gate[l], w_sh_up[l]], axis=1).astype(BF16),
            w_sh_down[l].astype(BF16), router_bias[l][:, None])

        cnt = counts[:, 0].astype(jnp.int32)
        pcnt = (cnt + bm - 1) // bm * bm
        pend = jnp.cumsum(pcnt)
        pstart = pend - pcnt
        n_used = jnp.maximum(pend[-1] // bm, 1).astype(jnp.int32)[None]
        block_start = jnp.arange(n_blocks, dtype=jnp.int32) * bm
        block_e = jnp.minimum(jnp.sum((pend[None, :] <= block_start[:, None]).astype(jnp.int32), axis=1),
                              N_EXPERTS - 1)
        eid = jnp.arange(N_EXPERTS, dtype=jnp.int32)
        seg_end = jnp.sum(jnp.where(block_e[:, None] == eid[None, :], (pstart + cnt)[None, :], 0), axis=1)
        n_valid = jnp.clip(seg_end - block_start, 0, bm).astype(jnp.int32)
        later_used = jnp.logical_and(eid[None, :] > eid[:, None], cnt[None, :] > 0)
        next_expert = jnp.min(jnp.where(later_used, eid[None, :], N_EXPERTS), axis=1)
        stage_slot = (jnp.cumsum((cnt > 0).astype(jnp.int32)) - 1) % EXPERT_STAGES
        ahead = [stage_slot, next_expert]
        for _ in range(EXPERT_STAGES - 1):
            hop = jnp.sum(jnp.where(ahead[-1][:, None] == eid[None, :], next_expert[None, :], 0), axis=1)
            ahead.append(jnp.where(ahead[-1] == N_EXPERTS, N_EXPERTS, hop))
        expert_meta = jnp.stack(ahead).astype(jnp.int32)

        dest3 = _dest(eidx, rank, pstart[:, None])
        xs = _dispatch(h2p, dest3, n_rows)
        ybuf = _experts(block_e, n_valid, n_used, expert_meta, xs, w_exp_gate[l], w_exp_up[l], w_exp_down[l])
        chunks = t // LANES // COMBINE_SEGMENTS
        out = None
        for seg in range(COMBINE_SEGMENTS):
            yg = _gather(ybuf, dest3[seg * chunks:(seg + 1) * chunks], chunks * LANES)
            out = _combine(wts, x2, mod[:, 5:6, :], norm_final_g[None, :], yg, seg, out)
        x = out
    return x
```

```python
import functools

import numpy as np
import jax
import jax.numpy as jnp
from jax import lax
from jax.experimental import pallas as pl
from jax.experimental.pallas import tpu as pltpu
from jax.experimental.pallas import tpu_sc as plsc

RET_HEADS = 4
RET_DK = 128
RET_DV = 256
RET_CHUNK = 128
ROPE_THETA = 10000.0
CONV_K = 3
N_EXPERTS = 256
TOP_K = 8
N_GROUPS = 8
TOPK_GROUPS = 4
GROUP_SIZE = N_EXPERTS // N_GROUPS
ROUTED_SCALE = 2.5
NORM_EPS = 1e-6

MIXER_TOKENS = 512
DEST_TOKENS = 512
EXPERT_ROWS = 256
EXPERT_BLOCKS_PER_STEP = 8
EXPERT_STAGES = 5
COMBINE_TOKENS = 256
COMBINE_SEGMENTS = 4
LANES = 128
GATHER_BUFFERS = 4

VMEM_LIMIT_BYTES = 56 * 1024 * 1024

F32 = jnp.float32
BF16 = jnp.bfloat16
HIGHEST = lax.Precision.HIGHEST


def _sigmoid(v):
    return 0.5 * jnp.tanh(0.5 * v) + 0.5


def _silu(v):
    return v * _sigmoid(v)


def _rms(v):
    return v * lax.rsqrt(jnp.mean(v * v, axis=-1, keepdims=True) + NORM_EPS)


def _resident(shape):
    nd = len(shape)
    return pl.BlockSpec(shape, lambda *_: (0,) * nd, pipeline_mode=pl.Buffered(1))


def _ada_kernel(c_ref, w_ref, b_ref, o_ref):
    c = c_ref[...]
    o_ref[...] = jnp.dot(_silu(c), w_ref[...], precision=HIGHEST, preferred_element_type=F32) + b_ref[...]


def _ada(c8, w, b):
    d, n = w.shape
    tn = 1024
    return pl.pallas_call(
        _ada_kernel,
        grid=(n // tn,),
        in_specs=[pl.BlockSpec((8, d), lambda j: (0, 0)),
                  pl.BlockSpec((d, tn), lambda j: (0, j)),
                  pl.BlockSpec((1, tn), lambda j: (0, j))],
        out_specs=pl.BlockSpec((8, tn), lambda j: (0, j)),
        out_shape=jax.ShapeDtypeStruct((8, n), F32),
        name="ada",
    )(c8, w, b)


def _retention_constants():
    f32 = np.float32
    c = RET_CHUNK
    log_g = np.log1p(-(f32(2.0) ** (f32(-5.0) - np.arange(RET_HEADS, dtype=f32)))).astype(f32)
    idx = np.arange(c, dtype=f32)
    diff = idx[:, None] - idx[None, :]
    intra = np.where(diff >= 0, np.exp(log_g[:, None, None] * np.maximum(diff, f32(0.0))), f32(0.0))
    k_decay = np.exp(log_g[:, None] * (f32(c - 1) - idx))
    q_decay = np.exp(log_g[:, None] * (idx + f32(1.0)))
    chunk_decay = np.exp(log_g * f32(c))
    kd = np.broadcast_to(k_decay[:, :, None], (RET_HEADS, c, RET_DK))
    qd = np.broadcast_to(q_decay[:, :, None], (RET_HEADS, c, RET_DK))
    cd = np.broadcast_to(chunk_decay[:, None, None], (RET_HEADS, 1, RET_DV))
    inv_freq = f32(ROPE_THETA) ** (-np.arange(0, RET_DK, 2, dtype=f32) / f32(RET_DK))
    inv_freq = np.concatenate([inv_freq, inv_freq])[None, :]
    sign = np.concatenate([-np.ones((RET_DK // 2,), f32), np.ones((RET_DK // 2,), f32)])[None, :]
    return tuple(jnp.asarray(np.ascontiguousarray(v, dtype=f32)) for v in (intra, qd, kd, cd, inv_freq, sign))


def _pack_bf16_pairs(v):
    w = v.shape[1] // 2
    lo = lax.bitcast_convert_type(v[:, :w].astype(BF16).astype(F32), jnp.uint32)
    hi = lax.bitcast_convert_type(v[:, w:].astype(BF16).astype(F32), jnp.uint32)
    return (lo >> 16) | (hi & jnp.uint32(0xFFFF0000))


def _unpack_pairs_f32(word):
    lo = lax.bitcast_convert_type(word << 16, F32)
    hi = lax.bitcast_convert_type(word & jnp.uint32(0xFFFF0000), F32)
    return jnp.concatenate([lo, hi], axis=1)


def _mix_tile(x_ref, pos_ref, mod_ref, gmix_ref, win_ref, invf_ref, sign_ref, intra_ref, qd_ref, kd_ref, cd_ref,
              retg_ref, convw_ref, wbr_ref, wbc_ref, wout_ref, gffn_ref, wrh_ref, wrl_ref, wshgu_ref, wshd_ref,
              x2_ref, h2p_ref, lg_ref, state_ref, carry_ref, q_ref, k_ref, v_ref, ret_ref, hb_ref, *, background):
    def tick():
        next(background, None)

    tm, d = x_ref.shape[1], x_ref.shape[2]
    q_w = RET_HEADS * RET_DK
    v_w = RET_HEADS * RET_DV
    offs = np.cumsum([0, q_w, q_w, v_w, v_w, d, d, d, d, d])

    x = x_ref[0]
    mod = mod_ref[0]
    sh1, sc1, g1, sh2, sc2, g2 = [mod[i:i + 1] for i in range(6)]
    hb_ref[...] = ((_rms(x) * gmix_ref[...]) * (1.0 + sc1) + sh1).astype(BF16)

    def proj(i):
        return jnp.dot(hb_ref[...], win_ref[:, offs[i]:offs[i + 1]], preferred_element_type=F32)

    ang = pos_ref[0] * invf_ref[...]
    cosv = jnp.cos(ang)
    sinv = jnp.sin(ang) * sign_ref[...]

    def rope(t):
        return jnp.concatenate(
            [t[:, h * RET_DK:(h + 1) * RET_DK] * cosv
             + pltpu.roll(t[:, h * RET_DK:(h + 1) * RET_DK], RET_DK // 2, 1) * sinv
             for h in range(RET_HEADS)], axis=1)

    q_ref[...] = rope(proj(0)) * (RET_DK ** -0.5)
    tick()
    k_ref[...] = rope(proj(1))
    tick()
    v_ref[...] = proj(2).astype(BF16)
    tick()

    for c in range(tm // RET_CHUNK):
        rows = pl.ds(c * RET_CHUNK, RET_CHUNK)
        for h in range(RET_HEADS):
            qh = q_ref[rows, h * RET_DK:(h + 1) * RET_DK]
            kh = k_ref[rows, h * RET_DK:(h + 1) * RET_DK]
            vh = v_ref[rows, h * RET_DV:(h + 1) * RET_DV]
            scores = lax.dot_general(qh.astype(BF16), kh.astype(BF16), (((1,), (1,)), ((), ())),
                                     preferred_element_type=F32) * intra_ref[h]
            inner = jnp.dot(scores.astype(BF16), vh, preferred_element_type=F32)
            st = state_ref[h]
            cross = jnp.dot((qh * qd_ref[h]).astype(BF16), st.astype(BF16), preferred_element_type=F32)
            kv = lax.dot_general((kh * kd_ref[h]).astype(BF16), vh, (((0,), (0,)), ((), ())),
                                 preferred_element_type=F32)
            state_ref[h] = st * cd_ref[h] + kv
            ret_ref[rows, h * RET_DV:(h + 1) * RET_DV] = _rms(inner + cross)
        tick()

    y_ret = jnp.dot((_silu(proj(3)) * (ret_ref[...] * retg_ref[...])).astype(BF16), wbr_ref[...],
                    preferred_element_type=F32)
    tick()

    z = proj(5) * proj(4)
    row = lax.broadcasted_iota(jnp.int32, z.shape, 0)
    prev1 = carry_ref[7:8, :]
    prev2 = carry_ref[6:7, :]
    z1 = jnp.where(row == 0, prev1, pltpu.roll(z, 1, 0))
    z2 = jnp.where(row == 0, prev2, jnp.where(row == 1, prev1, pltpu.roll(z, 2, 0)))
    conv = convw_ref[0:1, :] * z2 + convw_ref[1:2, :] * z1 + convw_ref[2:3, :] * z
    carry_ref[...] = z[tm - 8:tm, :]
    tick()
    y_conv = jnp.dot((proj(6) * conv).astype(BF16), wbc_ref[...], preferred_element_type=F32)
    tick()

    mix = _sigmoid(proj(7)) * y_ret + _sigmoid(proj(8)) * y_conv
    tick()
    x1 = x + g1 * jnp.dot(mix.astype(BF16), wout_ref[...], preferred_element_type=F32)
    tick()

    h2 = (_rms(x1) * gffn_ref[...]) * (1.0 + sc2) + sh2
    hi_f32 = lax.bitcast_convert_type(lax.bitcast_convert_type(h2, jnp.uint32) & jnp.uint32(0xFFFF0000), F32)
    h2_hi = hi_f32.astype(BF16)
    h2_lo = (h2 - hi_f32).astype(BF16)

    def nt_dot(a, b):
        return lax.dot_general(a, b, (((1,), (1,)), ((), ())), preferred_element_type=F32)

    lg_ref[...] = nt_dot(wrh_ref[...], h2_hi) + (nt_dot(wrh_ref[...], h2_lo) + nt_dot(wrl_ref[...], h2_hi))
    gu = jnp.dot(h2.astype(BF16), wshgu_ref[...], preferred_element_type=F32)
    f = gu.shape[1] // 2
    shared = jnp.dot((_silu(gu[:, :f]) * gu[:, f:]).astype(BF16), wshd_ref[...], preferred_element_type=F32)
    x2_ref[0] = x1 + g2 * shared
    for j in range(h2p_ref.shape[0]):
        h2p_ref[j] = _pack_bf16_pairs(h2[:, 2 * LANES * j:2 * LANES * (j + 1)])
    for _ in background:
        pass


def _mixer_kernel(*refs, tiles_per_seq):
    mix_inputs, (bias_ref, tri_ref) = refs[:21], refs[21:23]
    x2_ref, h2p_ref, eidx_ref, w_ref, rank_ref, cnt_ref = refs[23:29]
    state_ref, carry_ref, q_ref, k_ref, v_ref, ret_ref, hb_ref, lg_ref, rcarry_ref = refs[29:]
    s = pl.program_id(0)
    n_tiles = pl.num_programs(0) - 1

    @pl.when(s == 0)
    def _():
        lg_ref[...] = jnp.zeros_like(lg_ref)
        rcarry_ref[...] = jnp.zeros_like(rcarry_ref)

    @pl.when(s % tiles_per_seq == 0)
    def _():
        state_ref[...] = jnp.zeros_like(state_ref)
        carry_ref[...] = jnp.zeros_like(carry_ref)

    def route_previous_tile():
        return _route_phases(lg_ref.at[(s + 1) % 2], bias_ref, tri_ref, eidx_ref, w_ref, rank_ref, cnt_ref,
                             rcarry_ref, (s > 0).astype(jnp.int32))

    @pl.when(s == n_tiles)
    def _():
        for _ in route_previous_tile():
            pass

    @pl.when(s < n_tiles)
    def _():
        _mix_tile(*mix_inputs, x2_ref, h2p_ref, lg_ref.at[s % 2], state_ref, carry_ref, q_ref, k_ref, v_ref,
                  ret_ref, hb_ref, background=route_previous_tile())


def _mixer(x, posf, mod, gmix, win, retg, convw, wbr, wbc, wout, gffn, wr, wshgu, wshd, bias):
    b, s, d = x.shape
    tm = MIXER_TOKENS
    nt = s // tm
    n_tiles = b * nt
    intra, qd, kd, cd, invf, sign = _retention_constants()
    q_w, v_w = RET_HEADS * RET_DK, RET_HEADS * RET_DV
    hi_f32 = lax.bitcast_convert_type(lax.bitcast_convert_type(wr, jnp.uint32) & jnp.uint32(0xFFFF0000), F32)
    wr_hi = hi_f32.astype(BF16)
    wr_lo = (wr - hi_f32).astype(BF16)
    tri = jnp.asarray(np.triu(np.ones((tm, tm), np.float32), 1), BF16)
    weights = (gmix, win, invf, sign, intra, qd, kd, cd, retg, convw, wbr, wbc, wout, gffn, wr_hi, wr_lo,
               wshgu, wshd, bias, tri)

    def tile(i):
        return jnp.minimum(i, n_tiles - 1)

    def routed(i):
        return (0, jnp.maximum(i - 1, 0))

    return pl.pallas_call(
        functools.partial(_mixer_kernel, tiles_per_seq=nt),
        grid=(n_tiles + 1,),
        in_specs=[pl.BlockSpec((1, tm, d), lambda i: (tile(i) // nt, tile(i) % nt, 0)),
                  pl.BlockSpec((1, tm, 1), lambda i: (tile(i) // nt, tile(i) % nt, 0)),
                  pl.BlockSpec((1, 6, d), lambda i: (tile(i) // nt, 0, 0))]
                 + [_resident(w.shape) for w in weights],
        out_specs=[pl.BlockSpec((1, tm, d), lambda i: (tile(i) // nt, tile(i) % nt, 0)),
                   pl.BlockSpec((d // (2 * LANES), tm, LANES), lambda i: (0, tile(i), 0)),
                   pl.BlockSpec((TOP_K, tm), routed),
                   pl.BlockSpec((TOP_K, tm), routed),
                   pl.BlockSpec((TOP_K, tm), routed),
                   pl.BlockSpec((N_EXPERTS, 1), lambda i: (0, 0))],
        out_shape=[jax.ShapeDtypeStruct((b, s, d), F32),
                   jax.ShapeDtypeStruct((d // (2 * LANES), b * s, LANES), jnp.uint32),
                   jax.ShapeDtypeStruct((TOP_K, b * s), jnp.int32),
                   jax.ShapeDtypeStruct((TOP_K, b * s), F32),
                   jax.ShapeDtypeStruct((TOP_K, b * s), jnp.int32),
                   jax.ShapeDtypeStruct((N_EXPERTS, 1), F32)],
        scratch_shapes=[pltpu.VMEM((RET_HEADS, RET_DK, RET_DV), F32),
                        pltpu.VMEM((8, d), F32),
                        pltpu.VMEM((tm, q_w), F32),
                        pltpu.VMEM((tm, q_w), F32),
                        pltpu.VMEM((tm, v_w), BF16),
                        pltpu.VMEM((tm, v_w), F32),
                        pltpu.VMEM((tm, d), BF16),
                        pltpu.VMEM((2, N_EXPERTS, tm), F32),
                        pltpu.VMEM((N_EXPERTS, 1), F32)],
        compiler_params=pltpu.CompilerParams(dimension_semantics=("arbitrary",),
                                             vmem_limit_bytes=VMEM_LIMIT_BYTES),
        name="mixer",
    )(x, posf, mod, *weights)


def _first_argmax(v, idx, n):
    m = jnp.max(v, axis=0, keepdims=True)
    return m, jnp.min(jnp.where(v == m, idx, n), axis=0, keepdims=True)


def _route_phases(logits_ref, bias_ref, tri_ref, eidx_ref, w_ref, rank_ref, cnt_ref, carry_ref, count_it):
    tn = logits_ref.shape[1]
    neg = F32(-jnp.inf)
    score = _sigmoid(logits_ref[...])
    choice = score + bias_ref[...]

    grow = lax.broadcasted_iota(jnp.int32, (GROUP_SIZE, tn), 0)
    gscores = []
    for g in range(N_GROUPS):
        cg = choice[g * GROUP_SIZE:(g + 1) * GROUP_SIZE]
        m1, i1 = _first_argmax(cg, grow, GROUP_SIZE)
        m2 = jnp.max(jnp.where(grow == i1, neg, cg), axis=0, keepdims=True)
        gscores.append(m1 + m2)
    cur = jnp.concatenate(gscores, axis=0)
    yield
    gidx = lax.broadcasted_iota(jnp.int32, (N_GROUPS, tn), 0)
    keep = jnp.zeros((N_GROUPS, tn), F32)
    for _ in range(TOPK_GROUPS):
        _, ig = _first_argmax(cur, gidx, N_GROUPS)
        hit = gidx == ig
        keep = jnp.where(hit, 1.0, keep)
        cur = jnp.where(hit, neg, cur)
    cur = jnp.concatenate(
        [jnp.where(keep[g:g + 1] > 0.0, choice[g * GROUP_SIZE:(g + 1) * GROUP_SIZE], neg)
         for g in range(N_GROUPS)], axis=0)
    yield

    erow = lax.broadcasted_iota(jnp.int32, (N_EXPERTS, tn), 0)
    eidx, wts = [], []
    member = jnp.zeros((N_EXPERTS, tn), F32)
    for _ in range(TOP_K):
        _, ie = _first_argmax(cur, erow, N_EXPERTS)
        hit = erow == ie
        eidx.append(ie)
        wts.append(jnp.sum(jnp.where(hit, score, 0.0), axis=0, keepdims=True))
        member = member + hit.astype(F32)
        cur = jnp.where(hit, neg, cur)
        yield
    wsum = wts[0]
    for k in range(1, TOP_K):
        wsum = wsum + wts[k]

    before = jnp.dot(member.astype(BF16), tri_ref[...], preferred_element_type=F32) + carry_ref[...]
    ranks = []
    for k in range(TOP_K):
        ranks.append(jnp.sum(jnp.where(erow == eidx[k], before, 0.0), axis=0, keepdims=True))
        if k % 2 == 1:
            yield
    carry_ref[...] = carry_ref[...] + count_it.astype(F32) * jnp.sum(member, axis=1, keepdims=True)

    eidx_ref[...] = jnp.concatenate(eidx, axis=0)
    w_ref[...] = jnp.concatenate([w / wsum * ROUTED_SCALE for w in wts], axis=0)
    rank_ref[...] = jnp.concatenate(ranks, axis=0).astype(jnp.int32)
    cnt_ref[...] = carry_ref[...]


def _dest_kernel(eidx_ref, rank_ref, pstart_ref, dest_ref):
    tn = eidx_ref.shape[1]
    erow = lax.broadcasted_iota(jnp.int32, (N_EXPERTS, tn), 0)
    start = pstart_ref[...]
    dest = rank_ref[...] + jnp.concatenate(
        [jnp.sum(jnp.where(erow == eidx_ref[k:k + 1, :], start, 0), axis=0, keepdims=True) for k in range(TOP_K)],
        axis=0)
    for c in range(dest_ref.shape[0]):
        dest_ref[c] = dest[:, c * LANES:(c + 1) * LANES]


def _dest(eidx, rank, pstart):
    k, t = eidx.shape
    tn = DEST_TOKENS
    return pl.pallas_call(
        _dest_kernel,
        grid=(t // tn,),
        in_specs=[pl.BlockSpec((k, tn), lambda i: (0, i)),
                  pl.BlockSpec((k, tn), lambda i: (0, i)),
                  pl.BlockSpec((N_EXPERTS, 1), lambda i: (0, 0))],
        out_specs=pl.BlockSpec((tn // LANES, k, LANES), lambda i: (i, 0, 0)),
        out_shape=jax.ShapeDtypeStruct((t // LANES, k, LANES), jnp.int32),
        name="dest",
    )(eidx, rank, pstart)


def _sc_mesh_and_workers():
    mesh = plsc.VectorSubcoreMesh(core_axis_name="c", subcore_axis_name="s")
    return mesh, mesh.num_cores, mesh.num_cores * mesh.num_subcores


def _worker_id(num_cores):
    return lax.axis_index("s") * num_cores + lax.axis_index("c")


def _dispatch(h2p, dest3, n_rows):
    planes, t, lanes = h2p.shape
    n_chunks = dest3.shape[0]
    mesh, num_cores, workers = _sc_mesh_and_workers()
    chunks_per_worker = n_chunks // workers
    assert chunks_per_worker * workers == n_chunks and n_chunks * lanes == t

    def body(h_hbm, d_hbm, xs_hbm, idx_v, rows_v, load_sem, scatter_sem):
        wid = _worker_id(num_cores)

        @pl.loop(0, chunks_per_worker)
        def _(c):
            chunk = wid * chunks_per_worker + c
            tok = pl.ds(chunk * lanes, lanes)
            pltpu.sync_copy(d_hbm.at[chunk], idx_v)
            loads = [None] * planes
            scatters = [None] * planes
            loads[0] = pltpu.async_copy(h_hbm.at[0, tok], rows_v.at[0], load_sem.at[0])
            for j in range(planes):
                loads[j].wait()
                if j + 1 < planes:
                    if j >= 1:
                        for cp in scatters[j - 1]:
                            cp.wait()
                    loads[j + 1] = pltpu.async_copy(h_hbm.at[j + 1, tok], rows_v.at[(j + 1) % 2],
                                                    load_sem.at[(j + 1) % 2])
                scatters[j] = [pltpu.async_copy(rows_v.at[j % 2], xs_hbm.at[j].at[idx_v.at[k]],
                                                scatter_sem.at[j % 2]) for k in range(TOP_K)]
            for j in range(max(planes - 2, 0), planes):
                for cp in scatters[j]:
                    cp.wait()

    return pl.kernel(
        body,
        out_type=jax.ShapeDtypeStruct((planes, n_rows, lanes), h2p.dtype),
        mesh=mesh,
        scratch_types=[pltpu.VMEM((TOP_K, lanes), jnp.int32),
                       pltpu.VMEM((2, lanes, lanes), h2p.dtype),
                       pltpu.SemaphoreType.DMA((2,)),
                       pltpu.SemaphoreType.DMA((2,))],
        name="dispatch",
    )(h2p, dest3)


def _gather(ybuf, dest3, t):
    planes, _, lanes = ybuf.shape
    n_chunks = dest3.shape[0]
    mesh, num_cores, workers = _sc_mesh_and_workers()
    chunks_per_worker = n_chunks // workers
    nbuf = GATHER_BUFFERS
    lag = nbuf // 2
    assert chunks_per_worker * workers == n_chunks and n_chunks * lanes == t

    def body(y_hbm, d_hbm, yg_hbm, idx_v, rows_v, gather_sem, store_sem):
        wid = _worker_id(num_cores)

        @pl.loop(0, chunks_per_worker)
        def _(c):
            chunk = wid * chunks_per_worker + c
            tok = pl.ds(chunk * lanes, lanes)
            pltpu.sync_copy(d_hbm.at[chunk], idx_v)

            @pl.loop(0, TOP_K)
            def _(k):
                gathers = [None] * planes
                stores = [None] * planes

                def store(j):
                    gathers[j].wait()
                    stores[j] = pltpu.async_copy(rows_v.at[j % nbuf], yg_hbm.at[k, j, tok], store_sem.at[j % nbuf])

                for j in range(planes):
                    if j >= nbuf:
                        stores[j - nbuf].wait()
                    gathers[j] = pltpu.async_copy(y_hbm.at[j].at[idx_v.at[k]], rows_v.at[j % nbuf],
                                                  gather_sem.at[j % nbuf])
                    if j >= lag:
                        store(j - lag)
                for j in range(max(planes - lag, 0), planes):
                    store(j)
                for j in range(max(planes - nbuf, 0), planes):
                    stores[j].wait()

    return pl.kernel(
        body,
        out_type=jax.ShapeDtypeStruct((TOP_K, planes, t, lanes), ybuf.dtype),
        mesh=mesh,
        scratch_types=[pltpu.VMEM((TOP_K, lanes), jnp.int32),
                       pltpu.VMEM((nbuf, lanes, lanes), ybuf.dtype),
                       pltpu.SemaphoreType.DMA((nbuf,)),
                       pltpu.SemaphoreType.DMA((nbuf,))],
        name="gather",
    )(ybuf, dest3)


def _experts_kernel(be_ref, nv_ref, nu_ref, em_ref, x_ref, wg_hbm, wu_hbm, wd_hbm, o_ref,
                    wgs_ref, wus_ref, wds_ref, wgb_ref, wub_ref, wdb_ref, sems):
    planes, step_rows, lanes = x_ref.shape
    bm = EXPERT_ROWS
    blocks_per_step = step_rows // bm
    step = pl.program_id(0)
    last_step = (nu_ref[0] - 1) // blocks_per_step

    def weight_copies(expert):
        slot = em_ref[0, expert]
        return [pltpu.make_async_copy(src.at[expert], dst.at[slot], sems.at[slot, n])
                for n, (src, dst) in enumerate(((wg_hbm, wgs_ref), (wu_hbm, wus_ref), (wd_hbm, wds_ref)))]

    def start_if_any(expert):
        @pl.when(expert < N_EXPERTS)
        def _():
            for cp in weight_copies(expert):
                cp.start()

    def swiglu_rows(row0, m, n_valid):
        rows = pl.ds(row0, m)
        valid = lax.broadcasted_iota(jnp.int32, (m, lanes), 0) < n_valid
        g = None
        u = None
        for j in range(planes):
            xj = _unpack_pairs_f32(jnp.where(valid, x_ref[j, rows, :], jnp.uint32(0))).astype(BF16)
            wrows = pl.ds(2 * lanes * j, 2 * lanes)
            gj = jnp.dot(xj, wgb_ref[wrows, :], preferred_element_type=F32)
            uj = jnp.dot(xj, wub_ref[wrows, :], preferred_element_type=F32)
            g = gj if g is None else g + gj
            u = uj if u is None else u + uj
        out = jnp.dot((_silu(g) * u).astype(BF16), wdb_ref[...], preferred_element_type=F32)
        for j in range(planes):
            o_ref[j, rows, :] = _pack_bf16_pairs(out[:, 2 * lanes * j:2 * lanes * (j + 1)])

    def block_step(sb):
        i = step * blocks_per_step + sb
        row0 = pl.multiple_of(sb * bm, bm)
        rows = pl.ds(row0, bm)
        active = i < nu_ref[0]
        e = be_ref[i]
        new_expert = jnp.logical_or(i == 0, e != be_ref[jnp.maximum(i - 1, 0)])
        nxt_i = jnp.minimum(i + 1, be_ref.shape[0] - 1)
        pair = jnp.logical_and(jnp.logical_and(active, sb + 1 < blocks_per_step),
                               jnp.logical_and(i + 1 < nu_ref[0], be_ref[nxt_i] == e))

        @pl.when(jnp.logical_and(active, new_expert))
        def _():
            @pl.when(i == 0)
            def _():
                for cp in weight_copies(e):
                    cp.start()
                for ahead in range(1, EXPERT_STAGES):
                    start_if_any(em_ref[ahead, e])

            for cp in weight_copies(e):
                cp.wait()
            slot = em_ref[0, e]
            wgb_ref[...] = wgs_ref[slot].astype(BF16)
            wub_ref[...] = wus_ref[slot].astype(BF16)
            wdb_ref[...] = wds_ref[slot].astype(BF16)
            start_if_any(em_ref[EXPERT_STAGES, e])

        @pl.when(pair)
        def _():
            swiglu_rows(row0, 2 * bm, bm + nv_ref[nxt_i])

        @pl.when(jnp.logical_and(active, jnp.logical_not(pair)))
        def _():
            swiglu_rows(row0, bm, nv_ref[i])

        @pl.when(jnp.logical_and(jnp.logical_not(active), step == last_step))
        def _():
            for j in range(planes):
                o_ref[j, rows, :] = jnp.zeros((bm, lanes), o_ref.dtype)

        return sb + jnp.where(pair, 2, 1)

    lax.while_loop(lambda sb: sb < blocks_per_step, block_step, jnp.int32(0))


def _experts(block_e, n_valid, n_used, expert_meta, xs, wg, wu, wd):
    planes, p, lanes = xs.shape
    e, d, f = wg.shape
    step_rows = EXPERT_ROWS * EXPERT_BLOCKS_PER_STEP
    assert p % step_rows == 0

    def row_map(i, be, nv, nu, nxt):
        return (0, jnp.minimum(i, (nu[0] - 1) // EXPERT_BLOCKS_PER_STEP), 0)

    return pl.pallas_call(
        _experts_kernel,
        grid_spec=pltpu.PrefetchScalarGridSpec(
            num_scalar_prefetch=4,
            grid=(p // step_rows,),
            in_specs=[pl.BlockSpec((planes, step_rows, lanes), row_map),
                      pl.BlockSpec(memory_space=pl.ANY),
                      pl.BlockSpec(memory_space=pl.ANY),
                      pl.BlockSpec(memory_space=pl.ANY)],
            out_specs=pl.BlockSpec((planes, step_rows, lanes), row_map),
            scratch_shapes=[pltpu.VMEM((EXPERT_STAGES, d, f), F32), pltpu.VMEM((EXPERT_STAGES, d, f), F32),
                            pltpu.VMEM((EXPERT_STAGES, f, d), F32),
                            pltpu.VMEM((d, f), BF16), pltpu.VMEM((d, f), BF16), pltpu.VMEM((f, d), BF16),
                            pltpu.SemaphoreType.DMA((EXPERT_STAGES, 3))]),
        out_shape=jax.ShapeDtypeStruct((planes, p, lanes), jnp.uint32),
        compiler_params=pltpu.CompilerParams(dimension_semantics=("arbitrary",),
                                             vmem_limit_bytes=VMEM_LIMIT_BYTES),
        name="experts",
    )(block_e, n_valid, n_used, expert_meta, xs, wg, wu, wd)


def _combine_kernel(w_ref, x2_ref, g2_ref, gfin_ref, yg_ref, *maybe_prev_and_out):
    o_ref = maybe_prev_and_out[-1]
    tc = x2_ref.shape[1]
    eye = (lax.broadcasted_iota(jnp.int32, (tc, tc), 0) == lax.broadcasted_iota(jnp.int32, (tc, tc), 1)).astype(F32)
    w = lax.dot_general(eye, w_ref[...], (((1,), (1,)), ((), ())), precision=HIGHEST, preferred_element_type=F32)
    planes = yg_ref.shape[1]
    routed = None
    for k in range(TOP_K):
        rows = jnp.concatenate([_unpack_pairs_f32(yg_ref[k, j]) for j in range(planes)], axis=1) * w[:, k:k + 1]
        routed = rows if routed is None else routed + rows
    o_ref[0] = _rms(x2_ref[0] + g2_ref[0] * routed) * gfin_ref[...]


def _combine(wtok, x2, g2, gfin, yg, seg, prev_out):
    b, s, d = x2.shape
    _, planes, seg_tokens, lanes = yg.shape
    tc = COMBINE_TOKENS
    nt = seg_tokens // tc
    per_seq = s // seg_tokens
    assert nt * tc == seg_tokens and per_seq * seg_tokens == s
    bi, t0 = seg // per_seq, (seg % per_seq) * nt
    in_specs = [pl.BlockSpec((TOP_K, tc), lambda j: (0, seg * nt + j)),
                pl.BlockSpec((1, tc, d), lambda j: (bi, t0 + j, 0)),
                pl.BlockSpec((1, 1, d), lambda j: (bi, 0, 0)),
                pl.BlockSpec((1, d), lambda j: (0, 0)),
                pl.BlockSpec((TOP_K, planes, tc, lanes), lambda j: (0, 0, j, 0))]
    args = [wtok, x2, g2, gfin, yg]
    aliases = {}
    if prev_out is not None:
        in_specs.append(pl.BlockSpec(memory_space=pl.ANY))
        args.append(prev_out)
        aliases = {len(args) - 1: 0}
    return pl.pallas_call(
        _combine_kernel,
        grid=(nt,),
        in_specs=in_specs,
        out_specs=pl.BlockSpec((1, tc, d), lambda j: (bi, t0 + j, 0)),
        out_shape=jax.ShapeDtypeStruct((b, s, d), F32),
        input_output_aliases=aliases,
        compiler_params=pltpu.CompilerParams(dimension_semantics=("arbitrary",),
                                             vmem_limit_bytes=VMEM_LIMIT_BYTES),
        name="combine",
    )(*args)


def kernel(x, c, positions, w_ada, b_ada, norm_mix_g, w_in, ret_norm_g, conv_w, w_br_ret, w_br_conv, w_out,
           norm_ffn_g, w_router, router_bias, w_exp_gate, w_exp_up, w_exp_down, w_sh_gate, w_sh_up, w_sh_down,
           norm_final_g):
    b, s, d = x.shape
    t = b * s
    depth = w_in.shape[0]
    assert depth == 1, "the combine kernel applies the final norm, so exactly one layer is supported"
    posf = positions.astype(F32)[:, :, None]
    c8 = jnp.zeros((8, d), F32).at[:b].set(c)
    bm = EXPERT_ROWS
    n_blocks = t * TOP_K // bm + N_EXPERTS
    n_rows = n_blocks * bm

    for l in range(depth):
        mod = _ada(c8, w_ada[l], b_ada[l][None, :])[:b].reshape(b, 6, d)
        x2, h2p, eidx, wts, rank, counts = _mixer(
            x, posf, mod, norm_mix_g[l][None, :], w_in[l].astype(BF16), ret_norm_g[l][None, :], conv_w[l],
            w_br_ret[l].astype(BF16), w_br_conv[l].astype(BF16), w_out[l].astype(BF16), norm_ffn_g[l][None, :],
            w_router[l].T, jnp.concatenate([w_sh_gate[l], w_sh_up[l]], axis=1).astype(BF16),
            w_sh_down[l].astype(BF16), router_bias[l][:, None])

        cnt = counts[:, 0].astype(jnp.int32)
        pcnt = (cnt + bm - 1) // bm * bm
        pend = jnp.cumsum(pcnt)
        pstart = pend - pcnt
        n_used = jnp.maximum(pend[-1] // bm, 1).astype(jnp.int32)[None]
        block_start = jnp.arange(n_blocks, dtype=jnp.int32) * bm
        block_e = jnp.minimum(jnp.sum((pend[None, :] <= block_start[:, None]).astype(jnp.int32), axis=1),
                              N_EXPERTS - 1)
        eid = jnp.arange(N_EXPERTS, dtype=jnp.int32)
        seg_end = jnp.sum(jnp.where(block_e[:, None] == eid[None, :], (pstart + cnt)[None, :], 0), axis=1)
        n_valid = jnp.clip(seg_end - block_start, 0, bm).astype(jnp.int32)
        later_used = jnp.logical_and(eid[None, :] > eid[:, None], cnt[None, :] > 0)
        next_expert = jnp.min(jnp.where(later_used, eid[None, :], N_EXPERTS), axis=1)
        stage_slot = (jnp.cumsum((cnt > 0).astype(jnp.int32)) - 1) % EXPERT_STAGES
        ahead = [stage_slot, next_expert]
        for _ in range(EXPERT_STAGES - 1):
            hop = jnp.sum(jnp.where(ahead[-1][:, None] == eid[None, :], next_expert[None, :], 0), axis=1)
            ahead.append(jnp.where(ahead[-1] == N_EXPERTS, N_EXPERTS, hop))
        expert_meta = jnp.stack(ahead).astype(jnp.int32)

        dest3 = _dest(eidx, rank, pstart[:, None])
        xs = _dispatch(h2p, dest3, n_rows)
        ybuf = _experts(block_e, n_valid, n_used, expert_meta, xs, w_exp_gate[l], w_exp_up[l], w_exp_down[l])
        chunks = t // LANES // COMBINE_SEGMENTS
        out = None
        for seg in range(COMBINE_SEGMENTS):
            yg = _gather(ybuf, dest3[seg * chunks:(seg + 1) * chunks], chunks * LANES)
            out = _combine(wts, x2, mod[:, 5:6, :], norm_final_g[None, :], yg, seg, out)
        x = out
    return x
```

```python
import functools

import numpy as np
import jax
import jax.numpy as jnp
from jax import lax
from jax.experimental import pallas as pl
from jax.experimental.pallas import tpu as pltpu
from jax.experimental.pallas import tpu_sc as plsc

RET_HEADS = 4
RET_DK = 128
RET_DV = 256
RET_CHUNK = 128
ROPE_THETA = 10000.0
CONV_K = 3
N_EXPERTS = 256
TOP_K = 8
N_GROUPS = 8
TOPK_GROUPS = 4
GROUP_SIZE = N_EXPERTS // N_GROUPS
ROUTED_SCALE = 2.5
NORM_EPS = 1e-6

MIXER_TOKENS = 512
EXPERT_ROWS = 256
EXPERT_BLOCKS_PER_STEP = 8
EXPERT_STAGES = 5
COMBINE_TOKENS = 256
COMBINE_SEGMENTS = 4
LANES = 128
GATHER_BUFFERS = 4

VMEM_LIMIT_BYTES = 56 * 1024 * 1024

F32 = jnp.float32
BF16 = jnp.bfloat16
HIGHEST = lax.Precision.HIGHEST


def _sigmoid(v):
    return 0.5 * jnp.tanh(0.5 * v) + 0.5


def _silu(v):
    return v * _sigmoid(v)


def _rms(v):
    return v * lax.rsqrt(jnp.mean(v * v, axis=-1, keepdims=True) + NORM_EPS)


def _resident(shape):
    nd = len(shape)
    return pl.BlockSpec(shape, lambda *_: (0,) * nd, pipeline_mode=pl.Buffered(1))


def _ada_kernel(c_ref, w_ref, b_ref, o_ref):
    c = c_ref[...]
    o_ref[...] = jnp.dot(_silu(c), w_ref[...], precision=HIGHEST, preferred_element_type=F32) + b_ref[...]


def _ada(c8, w, b):
    d, n = w.shape
    tn = 1024
    return pl.pallas_call(
        _ada_kernel,
        grid=(n // tn,),
        in_specs=[pl.BlockSpec((8, d), lambda j: (0, 0)),
                  pl.BlockSpec((d, tn), lambda j: (0, j)),
                  pl.BlockSpec((1, tn), lambda j: (0, j))],
        out_specs=pl.BlockSpec((8, tn), lambda j: (0, j)),
        out_shape=jax.ShapeDtypeStruct((8, n), F32),
        name="ada",
    )(c8, w, b)


def _retention_constants():
    f32 = np.float32
    c = RET_CHUNK
    log_g = np.log1p(-(f32(2.0) ** (f32(-5.0) - np.arange(RET_HEADS, dtype=f32)))).astype(f32)
    idx = np.arange(c, dtype=f32)
    diff = idx[:, None] - idx[None, :]
    intra = np.where(diff >= 0, np.exp(log_g[:, None, None] * np.maximum(diff, f32(0.0))), f32(0.0))
    k_decay = np.exp(log_g[:, None] * (f32(c - 1) - idx))
    q_decay = np.exp(log_g[:, None] * (idx + f32(1.0)))
    chunk_decay = np.exp(log_g * f32(c))
    kd = np.broadcast_to(k_decay[:, :, None], (RET_HEADS, c, RET_DK))
    qd = np.broadcast_to(q_decay[:, :, None], (RET_HEADS, c, RET_DK))
    cd = np.broadcast_to(chunk_decay[:, None, None], (RET_HEADS, 1, RET_DV))
    inv_freq = f32(ROPE_THETA) ** (-np.arange(0, RET_DK, 2, dtype=f32) / f32(RET_DK))
    inv_freq = np.concatenate([inv_freq, inv_freq])[None, :]
    sign = np.concatenate([-np.ones((RET_DK // 2,), f32), np.ones((RET_DK // 2,), f32)])[None, :]
    return tuple(jnp.asarray(np.ascontiguousarray(v, dtype=f32)) for v in (intra, qd, kd, cd, inv_freq, sign))


def _pack_bf16_pairs(v):
    w = v.shape[1] // 2
    lo = lax.bitcast_convert_type(v[:, :w].astype(BF16).astype(F32), jnp.uint32)
    hi = lax.bitcast_convert_type(v[:, w:].astype(BF16).astype(F32), jnp.uint32)
    return (lo >> 16) | (hi & jnp.uint32(0xFFFF0000))


def _unpack_pairs_f32(word):
    lo = lax.bitcast_convert_type(word << 16, F32)
    hi = lax.bitcast_convert_type(word & jnp.uint32(0xFFFF0000), F32)
    return jnp.concatenate([lo, hi], axis=1)


def _mix_tile(x_ref, pos_ref, mod_ref, gmix_ref, win_ref, invf_ref, sign_ref, intra_ref, qd_ref, kd_ref, cd_ref,
              retg_ref, convw_ref, wbr_ref, wbc_ref, wout_ref, gffn_ref, wrh_ref, wrl_ref, wshgu_ref, wshd_ref,
              x2_ref, h2p_ref, lg_ref, state_ref, carry_ref, q_ref, k_ref, v_ref, ret_ref, hb_ref, *, background):
    def tick():
        next(background, None)

    tm, d = x_ref.shape[1], x_ref.shape[2]
    q_w = RET_HEADS * RET_DK
    v_w = RET_HEADS * RET_DV
    offs = np.cumsum([0, q_w, q_w, v_w, v_w, d, d, d, d, d])

    x = x_ref[0]
    mod = mod_ref[0]
    sh1, sc1, g1, sh2, sc2, g2 = [mod[i:i + 1] for i in range(6)]
    hb_ref[...] = ((_rms(x) * gmix_ref[...]) * (1.0 + sc1) + sh1).astype(BF16)

    def proj(i):
        return jnp.dot(hb_ref[...], win_ref[:, offs[i]:offs[i + 1]], preferred_element_type=F32)

    ang = pos_ref[0] * invf_ref[...]
    cosv = jnp.cos(ang)
    sinv = jnp.sin(ang) * sign_ref[...]

    def rope(t):
        return jnp.concatenate(
            [t[:, h * RET_DK:(h + 1) * RET_DK] * cosv
             + pltpu.roll(t[:, h * RET_DK:(h + 1) * RET_DK], RET_DK // 2, 1) * sinv
             for h in range(RET_HEADS)], axis=1)

    q_ref[...] = rope(proj(0)) * (RET_DK ** -0.5)
    tick()
    k_ref[...] = rope(proj(1))
    tick()
    v_ref[...] = proj(2).astype(BF16)
    tick()

    for c in range(tm // RET_CHUNK):
        rows = pl.ds(c * RET_CHUNK, RET_CHUNK)
        for h in range(RET_HEADS):
            qh = q_ref[rows, h * RET_DK:(h + 1) * RET_DK]
            kh = k_ref[rows, h * RET_DK:(h + 1) * RET_DK]
            vh = v_ref[rows, h * RET_DV:(h + 1) * RET_DV]
            scores = lax.dot_general(qh.astype(BF16), kh.astype(BF16), (((1,), (1,)), ((), ())),
                                     preferred_element_type=F32) * intra_ref[h]
            inner = jnp.dot(scores.astype(BF16), vh, preferred_element_type=F32)
            st = state_ref[h]
            cross = jnp.dot((qh * qd_ref[h]).astype(BF16), st.astype(BF16), preferred_element_type=F32)
            kv = lax.dot_general((kh * kd_ref[h]).astype(BF16), vh, (((0,), (0,)), ((), ())),
                                 preferred_element_type=F32)
            state_ref[h] = st * cd_ref[h] + kv
            ret_ref[rows, h * RET_DV:(h + 1) * RET_DV] = _rms(inner + cross)
        tick()

    y_ret = jnp.dot((_silu(proj(3)) * (ret_ref[...] * retg_ref[...])).astype(BF16), wbr_ref[...],
                    preferred_element_type=F32)
    tick()

    z = proj(5) * proj(4)
    row = lax.broadcasted_iota(jnp.int32, z.shape, 0)
    prev1 = carry_ref[7:8, :]
    prev2 = carry_ref[6:7, :]
    z1 = jnp.where(row == 0, prev1, pltpu.roll(z, 1, 0))
    z2 = jnp.where(row == 0, prev2, jnp.where(row == 1, prev1, pltpu.roll(z, 2, 0)))
    conv = convw_ref[0:1, :] * z2 + convw_ref[1:2, :] * z1 + convw_ref[2:3, :] * z
    carry_ref[...] = z[tm - 8:tm, :]
    tick()
    y_conv = jnp.dot((proj(6) * conv).astype(BF16), wbc_ref[...], preferred_element_type=F32)
    tick()

    mix = _sigmoid(proj(7)) * y_ret + _sigmoid(proj(8)) * y_conv
    tick()
    x1 = x + g1 * jnp.dot(mix.astype(BF16), wout_ref[...], preferred_element_type=F32)
    tick()

    h2 = (_rms(x1) * gffn_ref[...]) * (1.0 + sc2) + sh2
    hi_f32 = lax.bitcast_convert_type(lax.bitcast_convert_type(h2, jnp.uint32) & jnp.uint32(0xFFFF0000), F32)
    h2_hi = hi_f32.astype(BF16)
    h2_lo = (h2 - hi_f32).astype(BF16)

    def nt_dot(a, b):
        return lax.dot_general(a, b, (((1,), (1,)), ((), ())), preferred_element_type=F32)

    lg_ref[...] = nt_dot(wrh_ref[...], h2_hi) + (nt_dot(wrh_ref[...], h2_lo) + nt_dot(wrl_ref[...], h2_hi))
    gu = jnp.dot(h2.astype(BF16), wshgu_ref[...], preferred_element_type=F32)
    f = gu.shape[1] // 2
    shared = jnp.dot((_silu(gu[:, :f]) * gu[:, f:]).astype(BF16), wshd_ref[...], preferred_element_type=F32)
    x2_ref[0] = x1 + g2 * shared
    for j in range(h2p_ref.shape[0]):
        h2p_ref[j] = _pack_bf16_pairs(h2[:, 2 * LANES * j:2 * LANES * (j + 1)])
    for _ in background:
        pass


def _mixer_kernel(*refs, tiles_per_seq):
    mix_inputs, (bias_ref, tri_ref) = refs[:21], refs[21:23]
    x2_ref, h2p_ref, eidx_ref, w_ref, rank_ref, cnt_ref = refs[23:29]
    state_ref, carry_ref, q_ref, k_ref, v_ref, ret_ref, hb_ref, lg_ref, rcarry_ref = refs[29:]
    s = pl.program_id(0)
    n_tiles = pl.num_programs(0) - 1

    @pl.when(s == 0)
    def _():
        lg_ref[...] = jnp.zeros_like(lg_ref)
        rcarry_ref[...] = jnp.zeros_like(rcarry_ref)

    @pl.when(s % tiles_per_seq == 0)
    def _():
        state_ref[...] = jnp.zeros_like(state_ref)
        carry_ref[...] = jnp.zeros_like(carry_ref)

    def route_previous_tile():
        return _route_phases(lg_ref.at[(s + 1) % 2], bias_ref, tri_ref, eidx_ref, w_ref, rank_ref, cnt_ref,
                             rcarry_ref, (s > 0).astype(jnp.int32))

    @pl.when(s == n_tiles)
    def _():
        for _ in route_previous_tile():
            pass

    @pl.when(s < n_tiles)
    def _():
        _mix_tile(*mix_inputs, x2_ref, h2p_ref, lg_ref.at[s % 2], state_ref, carry_ref, q_ref, k_ref, v_ref,
                  ret_ref, hb_ref, background=route_previous_tile())


def _mixer(x, posf, mod, gmix, win, retg, convw, wbr, wbc, wout, gffn, wr, wshgu, wshd, bias):
    b, s, d = x.shape
    tm = MIXER_TOKENS
    nt = s // tm
    n_tiles = b * nt
    intra, qd, kd, cd, invf, sign = _retention_constants()
    q_w, v_w = RET_HEADS * RET_DK, RET_HEADS * RET_DV
    hi_f32 = lax.bitcast_convert_type(lax.bitcast_convert_type(wr, jnp.uint32) & jnp.uint32(0xFFFF0000), F32)
    wr_hi = hi_f32.astype(BF16)
    wr_lo = (wr - hi_f32).astype(BF16)
    tri = jnp.asarray(np.triu(np.ones((tm, tm), np.float32), 1), BF16)
    weights = (gmix, win, invf, sign, intra, qd, kd, cd, retg, convw, wbr, wbc, wout, gffn, wr_hi, wr_lo,
               wshgu, wshd, bias, tri)

    def tile(i):
        return jnp.minimum(i, n_tiles - 1)

    def routed(i):
        return (0, jnp.maximum(i - 1, 0))

    def routed_chunks(i):
        return (jnp.maximum(i - 1, 0), 0, 0)

    return pl.pallas_call(
        functools.partial(_mixer_kernel, tiles_per_seq=nt),
        grid=(n_tiles + 1,),
        in_specs=[pl.BlockSpec((1, tm, d), lambda i: (tile(i) // nt, tile(i) % nt, 0)),
                  pl.BlockSpec((1, tm, 1), lambda i: (tile(i) // nt, tile(i) % nt, 0)),
                  pl.BlockSpec((1, 6, d), lambda i: (tile(i) // nt, 0, 0))]
                 + [_resident(w.shape) for w in weights],
        out_specs=[pl.BlockSpec((1, tm, d), lambda i: (tile(i) // nt, tile(i) % nt, 0)),
                   pl.BlockSpec((d // (2 * LANES), tm, LANES), lambda i: (0, tile(i), 0)),
                   pl.BlockSpec((tm // LANES, TOP_K, LANES), routed_chunks),
                   pl.BlockSpec((TOP_K, tm), routed),
                   pl.BlockSpec((tm // LANES, TOP_K, LANES), routed_chunks),
                   pl.BlockSpec((N_EXPERTS, 1), lambda i: (0, 0))],
        out_shape=[jax.ShapeDtypeStruct((b, s, d), F32),
                   jax.ShapeDtypeStruct((d // (2 * LANES), b * s, LANES), jnp.uint32),
                   jax.ShapeDtypeStruct((b * s // LANES, TOP_K, LANES), jnp.int32),
                   jax.ShapeDtypeStruct((TOP_K, b * s), F32),
                   jax.ShapeDtypeStruct((b * s // LANES, TOP_K, LANES), jnp.int32),
                   jax.ShapeDtypeStruct((N_EXPERTS, 1), F32)],
        scratch_shapes=[pltpu.VMEM((RET_HEADS, RET_DK, RET_DV), F32),
                        pltpu.VMEM((8, d), F32),
                        pltpu.VMEM((tm, q_w), F32),
                        pltpu.VMEM((tm, q_w), F32),
                        pltpu.VMEM((tm, v_w), BF16),
                        pltpu.VMEM((tm, v_w), F32),
                        pltpu.VMEM((tm, d), BF16),
                        pltpu.VMEM((2, N_EXPERTS, tm), F32),
                        pltpu.VMEM((N_EXPERTS, 1), F32)],
        compiler_params=pltpu.CompilerParams(dimension_semantics=("arbitrary",),
                                             vmem_limit_bytes=VMEM_LIMIT_BYTES),
        name="mixer",
    )(x, posf, mod, *weights)


def _first_argmax(v, idx, n):
    m = jnp.max(v, axis=0, keepdims=True)
    return m, jnp.min(jnp.where(v == m, idx, n), axis=0, keepdims=True)


def _route_phases(logits_ref, bias_ref, tri_ref, eidx_ref, w_ref, rank_ref, cnt_ref, carry_ref, count_it):
    tn = logits_ref.shape[1]
    neg = F32(-jnp.inf)
    score = _sigmoid(logits_ref[...])
    choice = score + bias_ref[...]

    grow = lax.broadcasted_iota(jnp.int32, (GROUP_SIZE, tn), 0)
    gscores = []
    for g in range(N_GROUPS):
        cg = choice[g * GROUP_SIZE:(g + 1) * GROUP_SIZE]
        m1, i1 = _first_argmax(cg, grow, GROUP_SIZE)
        m2 = jnp.max(jnp.where(grow == i1, neg, cg), axis=0, keepdims=True)
        gscores.append(m1 + m2)
    cur = jnp.concatenate(gscores, axis=0)
    yield
    gidx = lax.broadcasted_iota(jnp.int32, (N_GROUPS, tn), 0)
    keep = jnp.zeros((N_GROUPS, tn), F32)
    for _ in range(TOPK_GROUPS):
        _, ig = _first_argmax(cur, gidx, N_GROUPS)
        hit = gidx == ig
        keep = jnp.where(hit, 1.0, keep)
        cur = jnp.where(hit, neg, cur)
    cur = jnp.concatenate(
        [jnp.where(keep[g:g + 1] > 0.0, choice[g * GROUP_SIZE:(g + 1) * GROUP_SIZE], neg)
         for g in range(N_GROUPS)], axis=0)
    yield

    erow = lax.broadcasted_iota(jnp.int32, (N_EXPERTS, tn), 0)
    eidx, wts = [], []
    member = jnp.zeros((N_EXPERTS, tn), F32)
    for _ in range(TOP_K):
        _, ie = _first_argmax(cur, erow, N_EXPERTS)
        hit = erow == ie
        eidx.append(ie)
        wts.append(jnp.sum(jnp.where(hit, score, 0.0), axis=0, keepdims=True))
        member = member + hit.astype(F32)
        cur = jnp.where(hit, neg, cur)
        yield
    wsum = wts[0]
    for k in range(1, TOP_K):
        wsum = wsum + wts[k]

    before = jnp.dot(member.astype(BF16), tri_ref[...], preferred_element_type=F32) + carry_ref[...]
    ranks = []
    for k in range(TOP_K):
        ranks.append(jnp.sum(jnp.where(erow == eidx[k], before, 0.0), axis=0, keepdims=True))
        if k % 2 == 1:
            yield
    carry_ref[...] = carry_ref[...] + count_it.astype(F32) * jnp.sum(member, axis=1, keepdims=True)

    eidx_rows = jnp.concatenate(eidx, axis=0)
    rank_rows = jnp.concatenate(ranks, axis=0).astype(jnp.int32)
    for c in range(tn // LANES):
        eidx_ref[c] = eidx_rows[:, c * LANES:(c + 1) * LANES]
        rank_ref[c] = rank_rows[:, c * LANES:(c + 1) * LANES]
    w_ref[...] = jnp.concatenate([w / wsum * ROUTED_SCALE for w in wts], axis=0)
    cnt_ref[...] = carry_ref[...]


def _sc_mesh_and_workers():
    mesh = plsc.VectorSubcoreMesh(core_axis_name="c", subcore_axis_name="s")
    return mesh, mesh.num_cores, mesh.num_cores * mesh.num_subcores


def _worker_id(num_cores):
    return lax.axis_index("s") * num_cores + lax.axis_index("c")


def _dispatch(h2p, eidx3, rank3, pstart, n_rows):
    planes, t, lanes = h2p.shape
    n_chunks = eidx3.shape[0]
    mesh, num_cores, workers = _sc_mesh_and_workers()
    chunks_per_worker = n_chunks // workers
    vec = plsc.get_sparse_core_info().num_lanes
    assert chunks_per_worker * workers == n_chunks and n_chunks * lanes == t and lanes % vec == 0

    def body(h_hbm, e_hbm, r_hbm, p_hbm, xs_hbm, d_hbm, e_v, r_v, p_v, idx_v, rows_v, load_sem, scatter_sem):
        wid = _worker_id(num_cores)
        pltpu.sync_copy(p_hbm, p_v)

        @pl.loop(0, chunks_per_worker)
        def _(c):
            chunk = wid * chunks_per_worker + c
            tok = pl.ds(chunk * lanes, lanes)
            pltpu.sync_copy(e_hbm.at[chunk], e_v)
            pltpu.sync_copy(r_hbm.at[chunk], r_v)
            for k in range(TOP_K):
                for g in range(lanes // vec):
                    part = pl.ds(g * vec, vec)
                    idx_v[k, part] = plsc.load_gather(p_v, [e_v[k, part]]) + r_v[k, part]
            pltpu.sync_copy(idx_v, d_hbm.at[chunk])
            loads = [None] * planes
            scatters = [None] * planes
            loads[0] = pltpu.async_copy(h_hbm.at[0, tok], rows_v.at[0], load_sem.at[0])
            for j in range(planes):
                loads[j].wait()
                if j + 1 < planes:
                    if j >= 1:
                        for cp in scatters[j - 1]:
                            cp.wait()
                    loads[j + 1] = pltpu.async_copy(h_hbm.at[j + 1, tok], rows_v.at[(j + 1) % 2],
                                                    load_sem.at[(j + 1) % 2])
                scatters[j] = [pltpu.async_copy(rows_v.at[j % 2], xs_hbm.at[j].at[idx_v.at[k]],
                                                scatter_sem.at[j % 2]) for k in range(TOP_K)]
            for j in range(max(planes - 2, 0), planes):
                for cp in scatters[j]:
                    cp.wait()

    return pl.kernel(
        body,
        out_type=(jax.ShapeDtypeStruct((planes, n_rows, lanes), h2p.dtype),
                  jax.ShapeDtypeStruct(eidx3.shape, jnp.int32)),
        mesh=mesh,
        scratch_types=[pltpu.VMEM((TOP_K, lanes), jnp.int32),
                       pltpu.VMEM((TOP_K, lanes), jnp.int32),
                       pltpu.VMEM((N_EXPERTS,), jnp.int32),
                       pltpu.VMEM((TOP_K, lanes), jnp.int32),
                       pltpu.VMEM((2, lanes, lanes), h2p.dtype),
                       pltpu.SemaphoreType.DMA((2,)),
                       pltpu.SemaphoreType.DMA((2,))],
        compiler_params=pltpu.CompilerParams(needs_layout_passes=False),
        name="dispatch",
    )(h2p, eidx3, rank3, pstart)


def _gather(ybuf, dest3, t):
    planes, _, lanes = ybuf.shape
    n_chunks = dest3.shape[0]
    mesh, num_cores, workers = _sc_mesh_and_workers()
    chunks_per_worker = n_chunks // workers
    nbuf = GATHER_BUFFERS
    lag = nbuf // 2
    assert chunks_per_worker * workers == n_chunks and n_chunks * lanes == t

    def body(y_hbm, d_hbm, yg_hbm, idx_v, rows_v, gather_sem, store_sem):
        wid = _worker_id(num_cores)

        @pl.loop(0, chunks_per_worker)
        def _(c):
            chunk = wid * chunks_per_worker + c
            tok = pl.ds(chunk * lanes, lanes)
            pltpu.sync_copy(d_hbm.at[chunk], idx_v)

            @pl.loop(0, TOP_K)
            def _(k):
                gathers = [None] * planes
                stores = [None] * planes

                def store(j):
                    gathers[j].wait()
                    stores[j] = pltpu.async_copy(rows_v.at[j % nbuf], yg_hbm.at[k, j, tok], store_sem.at[j % nbuf])

                for j in range(planes):
                    if j >= nbuf:
                        stores[j - nbuf].wait()
                    gathers[j] = pltpu.async_copy(y_hbm.at[j].at[idx_v.at[k]], rows_v.at[j % nbuf],
                                                  gather_sem.at[j % nbuf])
                    if j >= lag:
                        store(j - lag)
                for j in range(max(planes - lag, 0), planes):
                    store(j)
                for j in range(max(planes - nbuf, 0), planes):
                    stores[j].wait()

    return pl.kernel(
        body,
        out_type=jax.ShapeDtypeStruct((TOP_K, planes, t, lanes), ybuf.dtype),
        mesh=mesh,
        scratch_types=[pltpu.VMEM((TOP_K, lanes), jnp.int32),
                       pltpu.VMEM((nbuf, lanes, lanes), ybuf.dtype),
                       pltpu.SemaphoreType.DMA((nbuf,)),
                       pltpu.SemaphoreType.DMA((nbuf,))],
        name="gather",
    )(ybuf, dest3)


def _experts_kernel(be_ref, nv_ref, nu_ref, em_ref, x_ref, wg_hbm, wu_hbm, wd_hbm, o_ref,
                    wgs_ref, wus_ref, wds_ref, wgb_ref, wub_ref, wdb_ref, sems):
    planes, step_rows, lanes = x_ref.shape
    bm = EXPERT_ROWS
    blocks_per_step = step_rows // bm
    step = pl.program_id(0)
    last_step = (nu_ref[0] - 1) // blocks_per_step

    def weight_copies(expert):
        slot = em_ref[0, expert]
        return [pltpu.make_async_copy(src.at[expert], dst.at[slot], sems.at[slot, n])
                for n, (src, dst) in enumerate(((wg_hbm, wgs_ref), (wu_hbm, wus_ref), (wd_hbm, wds_ref)))]

    def start_if_any(expert):
        @pl.when(expert < N_EXPERTS)
        def _():
            for cp in weight_copies(expert):
                cp.start()

    def swiglu_rows(row0, m, n_valid):
        rows = pl.ds(row0, m)
        valid = lax.broadcasted_iota(jnp.int32, (m, lanes), 0) < n_valid
        g = None
        u = None
        for j in range(planes):
            xj = _unpack_pairs_f32(jnp.where(valid, x_ref[j, rows, :], jnp.uint32(0))).astype(BF16)
            wrows = pl.ds(2 * lanes * j, 2 * lanes)
            gj = jnp.dot(xj, wgb_ref[wrows, :], preferred_element_type=F32)
            uj = jnp.dot(xj, wub_ref[wrows, :], preferred_element_type=F32)
            g = gj if g is None else g + gj
            u = uj if u is None else u + uj
        out = jnp.dot((_silu(g) * u).astype(BF16), wdb_ref[...], preferred_element_type=F32)
        for j in range(planes):
            o_ref[j, rows, :] = _pack_bf16_pairs(out[:, 2 * lanes * j:2 * lanes * (j + 1)])

    def block_step(sb):
        i = step * blocks_per_step + sb
        row0 = pl.multiple_of(sb * bm, bm)
        rows = pl.ds(row0, bm)
        active = i < nu_ref[0]
        e = be_ref[i]
        new_expert = jnp.logical_or(i == 0, e != be_ref[jnp.maximum(i - 1, 0)])
        nxt_i = jnp.minimum(i + 1, be_ref.shape[0] - 1)
        pair = jnp.logical_and(jnp.logical_and(active, sb + 1 < blocks_per_step),
                               jnp.logical_and(i + 1 < nu_ref[0], be_ref[nxt_i] == e))

        @pl.when(jnp.logical_and(active, new_expert))
        def _():
            @pl.when(i == 0)
            def _():
                for cp in weight_copies(e):
                    cp.start()
                for ahead in range(1, EXPERT_STAGES):
                    start_if_any(em_ref[ahead, e])

            for cp in weight_copies(e):
                cp.wait()
            slot = em_ref[0, e]
            wgb_ref[...] = wgs_ref[slot].astype(BF16)
            wub_ref[...] = wus_ref[slot].astype(BF16)
            wdb_ref[...] = wds_ref[slot].astype(BF16)
            start_if_any(em_ref[EXPERT_STAGES, e])

        @pl.when(pair)
        def _():
            swiglu_rows(row0, 2 * bm, bm + nv_ref[nxt_i])

        @pl.when(jnp.logical_and(active, jnp.logical_not(pair)))
        def _():
            swiglu_rows(row0, bm, nv_ref[i])

        @pl.when(jnp.logical_and(jnp.logical_not(active), step == last_step))
        def _():
            for j in range(planes):
                o_ref[j, rows, :] = jnp.zeros((bm, lanes), o_ref.dtype)

        return sb + jnp.where(pair, 2, 1)

    lax.while_loop(lambda sb: sb < blocks_per_step, block_step, jnp.int32(0))


def _experts(block_e, n_valid, n_used, expert_meta, xs, wg, wu, wd):
    planes, p, lanes = xs.shape
    e, d, f = wg.shape
    step_rows = EXPERT_ROWS * EXPERT_BLOCKS_PER_STEP
    assert p % step_rows == 0

    def row_map(i, be, nv, nu, nxt):
        return (0, jnp.minimum(i, (nu[0] - 1) // EXPERT_BLOCKS_PER_STEP), 0)

    return pl.pallas_call(
        _experts_kernel,
        grid_spec=pltpu.PrefetchScalarGridSpec(
            num_scalar_prefetch=4,
            grid=(p // step_rows,),
            in_specs=[pl.BlockSpec((planes, step_rows, lanes), row_map),
                      pl.BlockSpec(memory_space=pl.ANY),
                      pl.BlockSpec(memory_space=pl.ANY),
                      pl.BlockSpec(memory_space=pl.ANY)],
            out_specs=pl.BlockSpec((planes, step_rows, lanes), row_map),
            scratch_shapes=[pltpu.VMEM((EXPERT_STAGES, d, f), F32), pltpu.VMEM((EXPERT_STAGES, d, f), F32),
                            pltpu.VMEM((EXPERT_STAGES, f, d), F32),
                            pltpu.VMEM((d, f), BF16), pltpu.VMEM((d, f), BF16), pltpu.VMEM((f, d), BF16),
                            pltpu.SemaphoreType.DMA((EXPERT_STAGES, 3))]),
        out_shape=jax.ShapeDtypeStruct((planes, p, lanes), jnp.uint32),
        compiler_params=pltpu.CompilerParams(dimension_semantics=("arbitrary",),
                                             vmem_limit_bytes=VMEM_LIMIT_BYTES),
        name="experts",
    )(block_e, n_valid, n_used, expert_meta, xs, wg, wu, wd)


def _combine_kernel(w_ref, x2_ref, g2_ref, gfin_ref, yg_ref, *maybe_prev_and_out):
    o_ref = maybe_prev_and_out[-1]
    tc = x2_ref.shape[1]
    eye = (lax.broadcasted_iota(jnp.int32, (tc, tc), 0) == lax.broadcasted_iota(jnp.int32, (tc, tc), 1)).astype(F32)
    w = lax.dot_general(eye, w_ref[...], (((1,), (1,)), ((), ())), precision=HIGHEST, preferred_element_type=F32)
    planes = yg_ref.shape[1]
    routed = None
    for k in range(TOP_K):
        rows = jnp.concatenate([_unpack_pairs_f32(yg_ref[k, j]) for j in range(planes)], axis=1) * w[:, k:k + 1]
        routed = rows if routed is None else routed + rows
    o_ref[0] = _rms(x2_ref[0] + g2_ref[0] * routed) * gfin_ref[...]


def _combine(wtok, x2, g2, gfin, yg, seg, prev_out):
    b, s, d = x2.shape
    _, planes, seg_tokens, lanes = yg.shape
    tc = COMBINE_TOKENS
    nt = seg_tokens // tc
    per_seq = s // seg_tokens
    assert nt * tc == seg_tokens and per_seq * seg_tokens == s
    bi, t0 = seg // per_seq, (seg % per_seq) * nt
    in_specs = [pl.BlockSpec((TOP_K, tc), lambda j: (0, seg * nt + j)),
                pl.BlockSpec((1, tc, d), lambda j: (bi, t0 + j, 0)),
                pl.BlockSpec((1, 1, d), lambda j: (bi, 0, 0)),
                pl.BlockSpec((1, d), lambda j: (0, 0)),
                pl.BlockSpec((TOP_K, planes, tc, lanes), lambda j: (0, 0, j, 0))]
    args = [wtok, x2, g2, gfin, yg]
    aliases = {}
    if prev_out is not None:
        in_specs.append(pl.BlockSpec(memory_space=pl.ANY))
        args.append(prev_out)
        aliases = {len(args) - 1: 0}
    return pl.pallas_call(
        _combine_kernel,
        grid=(nt,),
        in_specs=in_specs,
        out_specs=pl.BlockSpec((1, tc, d), lambda j: (bi, t0 + j, 0)),
        out_shape=jax.ShapeDtypeStruct((b, s, d), F32),
        input_output_aliases=aliases,
        compiler_params=pltpu.CompilerParams(dimension_semantics=("arbitrary",),
                                             vmem_limit_bytes=VMEM_LIMIT_BYTES),
        name="combine",
    )(*args)


def kernel(x, c, positions, w_ada, b_ada, norm_mix_g, w_in, ret_norm_g, conv_w, w_br_ret, w_br_conv, w_out,
           norm_ffn_g, w_router, router_bias, w_exp_gate, w_exp_up, w_exp_down, w_sh_gate, w_sh_up, w_sh_down,
           norm_final_g):
    b, s, d = x.shape
    t = b * s
    depth = w_in.shape[0]
    assert depth == 1, "the combine kernel applies the final norm, so exactly one layer is supported"
    posf = positions.astype(F32)[:, :, None]
    c8 = jnp.zeros((8, d), F32).at[:b].set(c)
    bm = EXPERT_ROWS
    n_blocks = t * TOP_K // bm + N_EXPERTS
    n_rows = n_blocks * bm

    for l in range(depth):
        mod = _ada(c8, w_ada[l], b_ada[l][None, :])[:b].reshape(b, 6, d)
        x2, h2p, eidx, wts, rank, counts = _mixer(
            x, posf, mod, norm_mix_g[l][None, :], w_in[l].astype(BF16), ret_norm_g[l][None, :], conv_w[l],
            w_br_ret[l].astype(BF16), w_br_conv[l].astype(BF16), w_out[l].astype(BF16), norm_ffn_g[l][None, :],
            w_router[l].T, jnp.concatenate([w_sh_gate[l], w_sh_up[l]], axis=1).astype(BF16),
            w_sh_down[l].astype(BF16), router_bias[l][:, None])

        cnt = counts[:, 0].astype(jnp.int32)
        pcnt = (cnt + bm - 1) // bm * bm
        pend = jnp.cumsum(pcnt)
        pstart = pend - pcnt
        n_used = jnp.maximum(pend[-1] // bm, 1).astype(jnp.int32)[None]
        block_start = jnp.arange(n_blocks, dtype=jnp.int32) * bm
        block_e = jnp.minimum(jnp.sum((pend[None, :] <= block_start[:, None]).astype(jnp.int32), axis=1),
                              N_EXPERTS - 1)
        eid = jnp.arange(N_EXPERTS, dtype=jnp.int32)
        seg_end = jnp.sum(jnp.where(block_e[:, None] == eid[None, :], (pstart + cnt)[None, :], 0), axis=1)
        n_valid = jnp.clip(seg_end - block_start, 0, bm).astype(jnp.int32)
        later_used = jnp.logical_and(eid[None, :] > eid[:, None], cnt[None, :] > 0)
        next_expert = jnp.min(jnp.where(later_used, eid[None, :], N_EXPERTS), axis=1)
        stage_slot = (jnp.cumsum((cnt > 0).astype(jnp.int32)) - 1) % EXPERT_STAGES
        ahead = [stage_slot, next_expert]
        for _ in range(EXPERT_STAGES - 1):
            hop = jnp.sum(jnp.where(ahead[-1][:, None] == eid[None, :], next_expert[None, :], 0), axis=1)
            ahead.append(jnp.where(ahead[-1] == N_EXPERTS, N_EXPERTS, hop))
        expert_meta = jnp.stack(ahead).astype(jnp.int32)

        xs, dest3 = _dispatch(h2p, eidx, rank, pstart.astype(jnp.int32), n_rows)
        ybuf = _experts(block_e, n_valid, n_used, expert_meta, xs, w_exp_gate[l], w_exp_up[l], w_exp_down[l])
        chunks = t // LANES // COMBINE_SEGMENTS
        out = None
        for seg in range(COMBINE_SEGMENTS):
            yg = _gather(ybuf, dest3[seg * chunks:(seg + 1) * chunks], chunks * LANES)
            out = _combine(wts, x2, mod[:, 5:6, :], norm_final_g[None, :], yg, seg, out)
        x = out
    return x
```

```python
import functools

import numpy as np
import jax
import jax.numpy as jnp
from jax import lax
from jax.experimental import pallas as pl
from jax.experimental.pallas import tpu as pltpu
from jax.experimental.pallas import tpu_sc as plsc

RET_HEADS = 4
RET_DK = 128
RET_DV = 256
RET_CHUNK = 128
ROPE_THETA = 10000.0
CONV_K = 3
N_EXPERTS = 256
TOP_K = 8
N_GROUPS = 8
TOPK_GROUPS = 4
GROUP_SIZE = N_EXPERTS // N_GROUPS
ROUTED_SCALE = 2.5
NORM_EPS = 1e-6

MIXER_TOKENS = 512
EXPERT_ROWS = 64
EXPERT_MAX_RUN = 8
EXPERT_BLOCKS_PER_STEP = 32
EXPERT_STAGES = 5
COMBINE_TOKENS = 512
COMBINE_SEGMENTS = 4
LANES = 128
GATHER_BUFFERS = 4

VMEM_LIMIT_BYTES = 56 * 1024 * 1024

F32 = jnp.float32
BF16 = jnp.bfloat16
HIGHEST = lax.Precision.HIGHEST


def _sigmoid(v):
    return 0.5 * jnp.tanh(0.5 * v) + 0.5


def _silu(v):
    return v * _sigmoid(v)


def _rms(v):
    return v * lax.rsqrt(jnp.mean(v * v, axis=-1, keepdims=True) + NORM_EPS)


def _resident(shape):
    nd = len(shape)
    return pl.BlockSpec(shape, lambda *_: (0,) * nd, pipeline_mode=pl.Buffered(1))


def _ada_kernel(c_ref, w_ref, b_ref, o_ref):
    c = c_ref[...]
    o_ref[...] = jnp.dot(_silu(c), w_ref[...], precision=HIGHEST, preferred_element_type=F32) + b_ref[...]


def _ada(c8, w, b):
    d, n = w.shape
    tn = 1024
    return pl.pallas_call(
        _ada_kernel,
        grid=(n // tn,),
        in_specs=[pl.BlockSpec((8, d), lambda j: (0, 0)),
                  pl.BlockSpec((d, tn), lambda j: (0, j)),
                  pl.BlockSpec((1, tn), lambda j: (0, j))],
        out_specs=pl.BlockSpec((8, tn), lambda j: (0, j)),
        out_shape=jax.ShapeDtypeStruct((8, n), F32),
        name="ada",
    )(c8, w, b)


def _retention_constants():
    f32 = np.float32
    c = RET_CHUNK
    log_g = np.log1p(-(f32(2.0) ** (f32(-5.0) - np.arange(RET_HEADS, dtype=f32)))).astype(f32)
    idx = np.arange(c, dtype=f32)
    diff = idx[:, None] - idx[None, :]
    intra = np.where(diff >= 0, np.exp(log_g[:, None, None] * np.maximum(diff, f32(0.0))), f32(0.0))
    k_decay = np.exp(log_g[:, None] * (f32(c - 1) - idx))
    q_decay = np.exp(log_g[:, None] * (idx + f32(1.0)))
    chunk_decay = np.exp(log_g * f32(c))
    kd = np.broadcast_to(k_decay[:, :, None], (RET_HEADS, c, RET_DK))
    qd = np.broadcast_to(q_decay[:, :, None], (RET_HEADS, c, RET_DK))
    cd = np.broadcast_to(chunk_decay[:, None, None], (RET_HEADS, 1, RET_DV))
    inv_freq = f32(ROPE_THETA) ** (-np.arange(0, RET_DK, 2, dtype=f32) / f32(RET_DK))
    inv_freq = np.concatenate([inv_freq, inv_freq])[None, :]
    sign = np.concatenate([-np.ones((RET_DK // 2,), f32), np.ones((RET_DK // 2,), f32)])[None, :]
    return tuple(jnp.asarray(np.ascontiguousarray(v, dtype=f32)) for v in (intra, qd, kd, cd, inv_freq, sign))


def _pack_bf16_pairs(v):
    w = v.shape[1] // 2
    lo = lax.bitcast_convert_type(v[:, :w].astype(BF16).astype(F32), jnp.uint32)
    hi = lax.bitcast_convert_type(v[:, w:].astype(BF16).astype(F32), jnp.uint32)
    return (lo >> 16) | (hi & jnp.uint32(0xFFFF0000))


def _unpack_pairs_f32(word):
    lo = lax.bitcast_convert_type(word << 16, F32)
    hi = lax.bitcast_convert_type(word & jnp.uint32(0xFFFF0000), F32)
    return jnp.concatenate([lo, hi], axis=1)


def _mix_tile(x_ref, pos_ref, mod_ref, gmix_ref, win_ref, invf_ref, sign_ref, intra_ref, qd_ref, kd_ref, cd_ref,
              retg_ref, convw_ref, wbr_ref, wbc_ref, wout_ref, gffn_ref, wrh_ref, wrl_ref, wshgu_ref, wshd_ref,
              x2_ref, h2p_ref, lg_ref, state_ref, carry_ref, q_ref, k_ref, v_ref, ret_ref, hb_ref, *, background):
    def tick():
        next(background, None)

    tm, d = x_ref.shape[1], x_ref.shape[2]
    q_w = RET_HEADS * RET_DK
    v_w = RET_HEADS * RET_DV
    offs = np.cumsum([0, q_w, q_w, v_w, v_w, d, d, d, d, d])

    x = x_ref[0]
    mod = mod_ref[0]
    sh1, sc1, g1, sh2, sc2, g2 = [mod[i:i + 1] for i in range(6)]
    hb_ref[...] = ((_rms(x) * gmix_ref[...]) * (1.0 + sc1) + sh1).astype(BF16)

    def proj(i):
        return jnp.dot(hb_ref[...], win_ref[:, offs[i]:offs[i + 1]], preferred_element_type=F32)

    ang = pos_ref[0] * invf_ref[...]
    cosv = jnp.cos(ang)
    sinv = jnp.sin(ang) * sign_ref[...]

    def rope(t):
        return jnp.concatenate(
            [t[:, h * RET_DK:(h + 1) * RET_DK] * cosv
             + pltpu.roll(t[:, h * RET_DK:(h + 1) * RET_DK], RET_DK // 2, 1) * sinv
             for h in range(RET_HEADS)], axis=1)

    q_ref[...] = rope(proj(0)) * (RET_DK ** -0.5)
    tick()
    k_ref[...] = rope(proj(1))
    tick()
    v_ref[...] = proj(2).astype(BF16)
    tick()

    for c in range(tm // RET_CHUNK):
        rows = pl.ds(c * RET_CHUNK, RET_CHUNK)
        for h in range(RET_HEADS):
            qh = q_ref[rows, h * RET_DK:(h + 1) * RET_DK]
            kh = k_ref[rows, h * RET_DK:(h + 1) * RET_DK]
            vh = v_ref[rows, h * RET_DV:(h + 1) * RET_DV]
            scores = lax.dot_general(qh.astype(BF16), kh.astype(BF16), (((1,), (1,)), ((), ())),
                                     preferred_element_type=F32) * intra_ref[h]
            inner = jnp.dot(scores.astype(BF16), vh, preferred_element_type=F32)
            st = state_ref[h]
            cross = jnp.dot((qh * qd_ref[h]).astype(BF16), st.astype(BF16), preferred_element_type=F32)
            kv = lax.dot_general((kh * kd_ref[h]).astype(BF16), vh, (((0,), (0,)), ((), ())),
                                 preferred_element_type=F32)
            state_ref[h] = st * cd_ref[h] + kv
            ret_ref[rows, h * RET_DV:(h + 1) * RET_DV] = _rms(inner + cross)
        tick()

    y_ret = jnp.dot((_silu(proj(3)) * (ret_ref[...] * retg_ref[...])).astype(BF16), wbr_ref[...],
                    preferred_element_type=F32)
    tick()

    z = proj(5) * proj(4)
    row = lax.broadcasted_iota(jnp.int32, z.shape, 0)
    prev1 = carry_ref[7:8, :]
    prev2 = carry_ref[6:7, :]
    z1 = jnp.where(row == 0, prev1, pltpu.roll(z, 1, 0))
    z2 = jnp.where(row == 0, prev2, jnp.where(row == 1, prev1, pltpu.roll(z, 2, 0)))
    conv = convw_ref[0:1, :] * z2 + convw_ref[1:2, :] * z1 + convw_ref[2:3, :] * z
    carry_ref[...] = z[tm - 8:tm, :]
    tick()
    y_conv = jnp.dot((proj(6) * conv).astype(BF16), wbc_ref[...], preferred_element_type=F32)
    tick()

    mix = _sigmoid(proj(7)) * y_ret + _sigmoid(proj(8)) * y_conv
    tick()
    x1 = x + g1 * jnp.dot(mix.astype(BF16), wout_ref[...], preferred_element_type=F32)
    tick()

    h2 = (_rms(x1) * gffn_ref[...]) * (1.0 + sc2) + sh2
    hi_f32 = lax.bitcast_convert_type(lax.bitcast_convert_type(h2, jnp.uint32) & jnp.uint32(0xFFFF0000), F32)
    h2_hi = hi_f32.astype(BF16)
    h2_lo = (h2 - hi_f32).astype(BF16)

    def nt_dot(a, b):
        return lax.dot_general(a, b, (((1,), (1,)), ((), ())), preferred_element_type=F32)

    lg_ref[...] = nt_dot(wrh_ref[...], h2_hi) + (nt_dot(wrh_ref[...], h2_lo) + nt_dot(wrl_ref[...], h2_hi))
    gu = jnp.dot(h2.astype(BF16), wshgu_ref[...], preferred_element_type=F32)
    f = gu.shape[1] // 2
    shared = jnp.dot((_silu(gu[:, :f]) * gu[:, f:]).astype(BF16), wshd_ref[...], preferred_element_type=F32)
    x2_ref[0] = x1 + g2 * shared
    for j in range(h2p_ref.shape[0]):
        h2p_ref[j] = _pack_bf16_pairs(h2[:, 2 * LANES * j:2 * LANES * (j + 1)])
    for _ in background:
        pass


def _mixer_kernel(*refs, tiles_per_seq):
    mix_inputs, (bias_ref, tri_ref) = refs[:21], refs[21:23]
    x2_ref, h2p_ref, eidx_ref, w_ref, rank_ref, cnt_ref = refs[23:29]
    state_ref, carry_ref, q_ref, k_ref, v_ref, ret_ref, hb_ref, lg_ref, rcarry_ref = refs[29:]
    s = pl.program_id(0)
    n_tiles = pl.num_programs(0) - 1

    @pl.when(s == 0)
    def _():
        lg_ref[...] = jnp.zeros_like(lg_ref)
        rcarry_ref[...] = jnp.zeros_like(rcarry_ref)

    @pl.when(s % tiles_per_seq == 0)
    def _():
        state_ref[...] = jnp.zeros_like(state_ref)
        carry_ref[...] = jnp.zeros_like(carry_ref)

    def route_previous_tile():
        return _route_phases(lg_ref.at[(s + 1) % 2], bias_ref, tri_ref, eidx_ref, w_ref, rank_ref, cnt_ref,
                             rcarry_ref, (s > 0).astype(jnp.int32))

    @pl.when(s == n_tiles)
    def _():
        for _ in route_previous_tile():
            pass

    @pl.when(s < n_tiles)
    def _():
        _mix_tile(*mix_inputs, x2_ref, h2p_ref, lg_ref.at[s % 2], state_ref, carry_ref, q_ref, k_ref, v_ref,
                  ret_ref, hb_ref, background=route_previous_tile())


def _mixer(x, posf, mod, gmix, win, retg, convw, wbr, wbc, wout, gffn, wr, wshgu, wshd, bias):
    b, s, d = x.shape
    tm = MIXER_TOKENS
    nt = s // tm
    n_tiles = b * nt
    intra, qd, kd, cd, invf, sign = _retention_constants()
    q_w, v_w = RET_HEADS * RET_DK, RET_HEADS * RET_DV
    hi_f32 = lax.bitcast_convert_type(lax.bitcast_convert_type(wr, jnp.uint32) & jnp.uint32(0xFFFF0000), F32)
    wr_hi = hi_f32.astype(BF16)
    wr_lo = (wr - hi_f32).astype(BF16)
    tri = jnp.asarray(np.triu(np.ones((tm, tm), np.float32), 1), BF16)
    weights = (gmix, win, invf, sign, intra, qd, kd, cd, retg, convw, wbr, wbc, wout, gffn, wr_hi, wr_lo,
               wshgu, wshd, bias, tri)

    def tile(i):
        return jnp.minimum(i, n_tiles - 1)

    def routed(i):
        return (0, jnp.maximum(i - 1, 0))

    def routed_chunks(i):
        return (jnp.maximum(i - 1, 0), 0, 0)

    return pl.pallas_call(
        functools.partial(_mixer_kernel, tiles_per_seq=nt),
        grid=(n_tiles + 1,),
        in_specs=[pl.BlockSpec((1, tm, d), lambda i: (tile(i) // nt, tile(i) % nt, 0)),
                  pl.BlockSpec((1, tm, 1), lambda i: (tile(i) // nt, tile(i) % nt, 0)),
                  pl.BlockSpec((1, 6, d), lambda i: (tile(i) // nt, 0, 0))]
                 + [_resident(w.shape) for w in weights],
        out_specs=[pl.BlockSpec((1, tm, d), lambda i: (tile(i) // nt, tile(i) % nt, 0)),
                   pl.BlockSpec((d // (2 * LANES), tm, LANES), lambda i: (0, tile(i), 0)),
                   pl.BlockSpec((tm // LANES, TOP_K, LANES), routed_chunks),
                   pl.BlockSpec((TOP_K, tm), routed),
                   pl.BlockSpec((tm // LANES, TOP_K, LANES), routed_chunks),
                   pl.BlockSpec((N_EXPERTS, 1), lambda i: (0, 0))],
        out_shape=[jax.ShapeDtypeStruct((b, s, d), F32),
                   jax.ShapeDtypeStruct((d // (2 * LANES), b * s, LANES), jnp.uint32),
                   jax.ShapeDtypeStruct((b * s // LANES, TOP_K, LANES), jnp.int32),
                   jax.ShapeDtypeStruct((TOP_K, b * s), F32),
                   jax.ShapeDtypeStruct((b * s // LANES, TOP_K, LANES), jnp.int32),
                   jax.ShapeDtypeStruct((N_EXPERTS, 1), F32)],
        scratch_shapes=[pltpu.VMEM((RET_HEADS, RET_DK, RET_DV), F32),
                        pltpu.VMEM((8, d), F32),
                        pltpu.VMEM((tm, q_w), F32),
                        pltpu.VMEM((tm, q_w), F32),
                        pltpu.VMEM((tm, v_w), BF16),
                        pltpu.VMEM((tm, v_w), F32),
                        pltpu.VMEM((tm, d), BF16),
                        pltpu.VMEM((2, N_EXPERTS, tm), F32),
                        pltpu.VMEM((N_EXPERTS, 1), F32)],
        compiler_params=pltpu.CompilerParams(dimension_semantics=("arbitrary",),
                                             vmem_limit_bytes=VMEM_LIMIT_BYTES),
        name="mixer",
    )(x, posf, mod, *weights)


def _first_argmax(v, idx, n):
    m = jnp.max(v, axis=0, keepdims=True)
    return m, jnp.min(jnp.where(v == m, idx, n), axis=0, keepdims=True)


def _route_phases(logits_ref, bias_ref, tri_ref, eidx_ref, w_ref, rank_ref, cnt_ref, carry_ref, count_it):
    tn = logits_ref.shape[1]
    neg = F32(-jnp.inf)
    score = _sigmoid(logits_ref[...])
    choice = score + bias_ref[...]

    grow = lax.broadcasted_iota(jnp.int32, (GROUP_SIZE, tn), 0)
    gscores = []
    for g in range(N_GROUPS):
        cg = choice[g * GROUP_SIZE:(g + 1) * GROUP_SIZE]
        m1, i1 = _first_argmax(cg, grow, GROUP_SIZE)
        m2 = jnp.max(jnp.where(grow == i1, neg, cg), axis=0, keepdims=True)
        gscores.append(m1 + m2)
    cur = jnp.concatenate(gscores, axis=0)
    yield
    gidx = lax.broadcasted_iota(jnp.int32, (N_GROUPS, tn), 0)
    keep = jnp.zeros((N_GROUPS, tn), F32)
    for _ in range(TOPK_GROUPS):
        _, ig = _first_argmax(cur, gidx, N_GROUPS)
        hit = gidx == ig
        keep = jnp.where(hit, 1.0, keep)
        cur = jnp.where(hit, neg, cur)
    cur = jnp.concatenate(
        [jnp.where(keep[g:g + 1] > 0.0, choice[g * GROUP_SIZE:(g + 1) * GROUP_SIZE], neg)
         for g in range(N_GROUPS)], axis=0)
    yield

    erow = lax.broadcasted_iota(jnp.int32, (N_EXPERTS, tn), 0)
    eidx, wts = [], []
    member = jnp.zeros((N_EXPERTS, tn), F32)
    for _ in range(TOP_K):
        _, ie = _first_argmax(cur, erow, N_EXPERTS)
        hit = erow == ie
        eidx.append(ie)
        wts.append(jnp.sum(jnp.where(hit, score, 0.0), axis=0, keepdims=True))
        member = member + hit.astype(F32)
        cur = jnp.where(hit, neg, cur)
        yield
    wsum = wts[0]
    for k in range(1, TOP_K):
        wsum = wsum + wts[k]

    before = jnp.dot(member.astype(BF16), tri_ref[...], preferred_element_type=F32) + carry_ref[...]
    ranks = []
    for k in range(TOP_K):
        ranks.append(jnp.sum(jnp.where(erow == eidx[k], before, 0.0), axis=0, keepdims=True))
        if k % 2 == 1:
            yield
    carry_ref[...] = carry_ref[...] + count_it.astype(F32) * jnp.sum(member, axis=1, keepdims=True)

    eidx_rows = jnp.concatenate(eidx, axis=0)
    rank_rows = jnp.concatenate(ranks, axis=0).astype(jnp.int32)
    for c in range(tn // LANES):
        eidx_ref[c] = eidx_rows[:, c * LANES:(c + 1) * LANES]
        rank_ref[c] = rank_rows[:, c * LANES:(c + 1) * LANES]
    w_ref[...] = jnp.concatenate([w / wsum * ROUTED_SCALE for w in wts], axis=0)
    cnt_ref[...] = carry_ref[...]


def _sc_mesh_and_workers():
    mesh = plsc.VectorSubcoreMesh(core_axis_name="c", subcore_axis_name="s")
    return mesh, mesh.num_cores, mesh.num_cores * mesh.num_subcores


def _worker_id(num_cores):
    return lax.axis_index("s") * num_cores + lax.axis_index("c")


def _dispatch(h2p, eidx3, rank3, pstart, n_rows):
    planes, t, lanes = h2p.shape
    n_chunks = eidx3.shape[0]
    mesh, num_cores, workers = _sc_mesh_and_workers()
    chunks_per_worker = n_chunks // workers
    vec = plsc.get_sparse_core_info().num_lanes
    assert chunks_per_worker * workers == n_chunks and n_chunks * lanes == t and lanes % vec == 0

    def body(h_hbm, e_hbm, r_hbm, p_hbm, xs_hbm, d_hbm, e_v, r_v, p_v, idx_v, rows_v, load_sem, scatter_sem):
        wid = _worker_id(num_cores)
        mine = pl.ds(wid * chunks_per_worker, chunks_per_worker)
        pltpu.sync_copy(p_hbm, p_v)
        pltpu.sync_copy(e_hbm.at[mine], e_v)
        pltpu.sync_copy(r_hbm.at[mine], r_v)

        @pl.loop(0, chunks_per_worker)
        def _(c):
            for k in range(TOP_K):
                for g in range(lanes // vec):
                    part = pl.ds(g * vec, vec)
                    idx_v[c, k, part] = plsc.load_gather(p_v, [e_v[c, k, part]]) + r_v[c, k, part]

        pltpu.sync_copy(idx_v, d_hbm.at[mine])

        @pl.loop(0, chunks_per_worker)
        def _(c):
            tok = pl.ds((wid * chunks_per_worker + c) * lanes, lanes)
            loads = [None] * planes
            scatters = [None] * planes
            loads[0] = pltpu.async_copy(h_hbm.at[0, tok], rows_v.at[0], load_sem.at[0])
            for j in range(planes):
                loads[j].wait()
                if j + 1 < planes:
                    if j >= 1:
                        for cp in scatters[j - 1]:
                            cp.wait()
                    loads[j + 1] = pltpu.async_copy(h_hbm.at[j + 1, tok], rows_v.at[(j + 1) % 2],
                                                    load_sem.at[(j + 1) % 2])
                scatters[j] = [pltpu.async_copy(rows_v.at[j % 2], xs_hbm.at[j].at[idx_v.at[c, k]],
                                                scatter_sem.at[j % 2]) for k in range(TOP_K)]
            for j in range(max(planes - 2, 0), planes):
                for cp in scatters[j]:
                    cp.wait()

    return pl.kernel(
        body,
        out_type=(jax.ShapeDtypeStruct((planes, n_rows, lanes), h2p.dtype),
                  jax.ShapeDtypeStruct(eidx3.shape, jnp.int32)),
        mesh=mesh,
        scratch_types=[pltpu.VMEM((chunks_per_worker, TOP_K, lanes), jnp.int32),
                       pltpu.VMEM((chunks_per_worker, TOP_K, lanes), jnp.int32),
                       pltpu.VMEM((N_EXPERTS,), jnp.int32),
                       pltpu.VMEM((chunks_per_worker, TOP_K, lanes), jnp.int32),
                       pltpu.VMEM((2, lanes, lanes), h2p.dtype),
                       pltpu.SemaphoreType.DMA((2,)),
                       pltpu.SemaphoreType.DMA((2,))],
        compiler_params=pltpu.CompilerParams(needs_layout_passes=False),
        name="dispatch",
    )(h2p, eidx3, rank3, pstart)


def _gather(ybuf, dest3, t):
    planes, _, lanes = ybuf.shape
    n_chunks = dest3.shape[0]
    mesh, num_cores, workers = _sc_mesh_and_workers()
    chunks_per_worker = n_chunks // workers
    nbuf = GATHER_BUFFERS
    lag = nbuf // 2
    assert chunks_per_worker * workers == n_chunks and n_chunks * lanes == t

    def body(y_hbm, d_hbm, yg_hbm, idx_v, rows_v, gather_sem, store_sem):
        wid = _worker_id(num_cores)

        @pl.loop(0, chunks_per_worker)
        def _(c):
            chunk = wid * chunks_per_worker + c
            tok = pl.ds(chunk * lanes, lanes)
            pltpu.sync_copy(d_hbm.at[chunk], idx_v)

            @pl.loop(0, TOP_K)
            def _(k):
                gathers = [None] * planes
                stores = [None] * planes

                def store(j):
                    gathers[j].wait()
                    stores[j] = pltpu.async_copy(rows_v.at[j % nbuf], yg_hbm.at[k, j, tok], store_sem.at[j % nbuf])

                for j in range(planes):
                    if j >= nbuf:
                        stores[j - nbuf].wait()
                    gathers[j] = pltpu.async_copy(y_hbm.at[j].at[idx_v.at[k]], rows_v.at[j % nbuf],
                                                  gather_sem.at[j % nbuf])
                    if j >= lag:
                        store(j - lag)
                for j in range(max(planes - lag, 0), planes):
                    store(j)
                for j in range(max(planes - nbuf, 0), planes):
                    stores[j].wait()

    return pl.kernel(
        body,
        out_type=jax.ShapeDtypeStruct((TOP_K, planes, t, lanes), ybuf.dtype),
        mesh=mesh,
        scratch_types=[pltpu.VMEM((TOP_K, lanes), jnp.int32),
                       pltpu.VMEM((nbuf, lanes, lanes), ybuf.dtype),
                       pltpu.SemaphoreType.DMA((nbuf,)),
                       pltpu.SemaphoreType.DMA((nbuf,))],
        name="gather",
    )(ybuf, dest3)


def _experts_kernel(be_ref, nv_ref, nu_ref, em_ref, x_ref, wg_hbm, wu_hbm, wd_hbm, o_ref,
                    wgs_ref, wus_ref, wds_ref, wgb_ref, wub_ref, wdb_ref, sems):
    planes, step_rows, lanes = x_ref.shape
    bm = EXPERT_ROWS
    blocks_per_step = step_rows // bm
    step = pl.program_id(0)
    last_step = (nu_ref[0] - 1) // blocks_per_step

    def weight_copies(expert):
        slot = em_ref[0, expert]
        return [pltpu.make_async_copy(src.at[expert], dst.at[slot], sems.at[slot, n])
                for n, (src, dst) in enumerate(((wg_hbm, wgs_ref), (wu_hbm, wus_ref), (wd_hbm, wds_ref)))]

    def start_if_any(expert):
        @pl.when(expert < N_EXPERTS)
        def _():
            for cp in weight_copies(expert):
                cp.start()

    def swiglu_rows(row0, m, n_valid):
        rows = pl.ds(row0, m)
        valid = lax.broadcasted_iota(jnp.int32, (m, lanes), 0) < n_valid
        g = None
        u = None
        for j in range(planes):
            xj = _unpack_pairs_f32(jnp.where(valid, x_ref[j, rows, :], jnp.uint32(0))).astype(BF16)
            wrows = pl.ds(2 * lanes * j, 2 * lanes)
            gj = jnp.dot(xj, wgb_ref[wrows, :], preferred_element_type=F32)
            uj = jnp.dot(xj, wub_ref[wrows, :], preferred_element_type=F32)
            g = gj if g is None else g + gj
            u = uj if u is None else u + uj
        out = jnp.dot((_silu(g) * u).astype(BF16), wdb_ref[...], preferred_element_type=F32)
        for j in range(planes):
            o_ref[j, rows, :] = _pack_bf16_pairs(out[:, 2 * lanes * j:2 * lanes * (j + 1)])

    def block_step(sb):
        i = step * blocks_per_step + sb
        row0 = pl.multiple_of(sb * bm, bm)
        rows = pl.ds(row0, bm)
        active = i < nu_ref[0]
        e = be_ref[i]
        new_expert = jnp.logical_or(i == 0, e != be_ref[jnp.maximum(i - 1, 0)])
        run = active.astype(jnp.int32)
        same = active
        for ahead in range(1, EXPERT_MAX_RUN):
            j = jnp.minimum(i + ahead, be_ref.shape[0] - 1)
            same = jnp.logical_and(jnp.logical_and(same, sb + ahead < blocks_per_step),
                                   jnp.logical_and(i + ahead < nu_ref[0], be_ref[j] == e))
            run = run + same.astype(jnp.int32)

        @pl.when(jnp.logical_and(active, new_expert))
        def _():
            @pl.when(i == 0)
            def _():
                for cp in weight_copies(e):
                    cp.start()
                for ahead in range(1, EXPERT_STAGES):
                    start_if_any(em_ref[ahead, e])

            for cp in weight_copies(e):
                cp.wait()
            slot = em_ref[0, e]
            wgb_ref[...] = wgs_ref[slot].astype(BF16)
            wub_ref[...] = wus_ref[slot].astype(BF16)
            wdb_ref[...] = wds_ref[slot].astype(BF16)
            start_if_any(em_ref[EXPERT_STAGES, e])

        for m in range(1, EXPERT_MAX_RUN + 1):
            @pl.when(run == m)
            def _():
                last = jnp.minimum(i + (m - 1), be_ref.shape[0] - 1)
                swiglu_rows(row0, m * bm, (m - 1) * bm + nv_ref[last])

        @pl.when(jnp.logical_and(jnp.logical_not(active), step == last_step))
        def _():
            for j in range(planes):
                o_ref[j, rows, :] = jnp.zeros((bm, lanes), o_ref.dtype)

        return sb + jnp.maximum(run, 1)

    lax.while_loop(lambda sb: sb < blocks_per_step, block_step, jnp.int32(0))


def _experts(block_e, n_valid, n_used, expert_meta, xs, wg, wu, wd):
    planes, p, lanes = xs.shape
    e, d, f = wg.shape
    step_rows = EXPERT_ROWS * EXPERT_BLOCKS_PER_STEP
    assert p % step_rows == 0

    def row_map(i, be, nv, nu, nxt):
        return (0, jnp.minimum(i, (nu[0] - 1) // EXPERT_BLOCKS_PER_STEP), 0)

    return pl.pallas_call(
        _experts_kernel,
        grid_spec=pltpu.PrefetchScalarGridSpec(
            num_scalar_prefetch=4,
            grid=(p // step_rows,),
            in_specs=[pl.BlockSpec((planes, step_rows, lanes), row_map),
                      pl.BlockSpec(memory_space=pl.ANY),
                      pl.BlockSpec(memory_space=pl.ANY),
                      pl.BlockSpec(memory_space=pl.ANY)],
            out_specs=pl.BlockSpec((planes, step_rows, lanes), row_map),
            scratch_shapes=[pltpu.VMEM((EXPERT_STAGES, d, f), F32), pltpu.VMEM((EXPERT_STAGES, d, f), F32),
                            pltpu.VMEM((EXPERT_STAGES, f, d), F32),
                            pltpu.VMEM((d, f), BF16), pltpu.VMEM((d, f), BF16), pltpu.VMEM((f, d), BF16),
                            pltpu.SemaphoreType.DMA((EXPERT_STAGES, 3))]),
        out_shape=jax.ShapeDtypeStruct((planes, p, lanes), jnp.uint32),
        compiler_params=pltpu.CompilerParams(dimension_semantics=("arbitrary",),
                                             vmem_limit_bytes=VMEM_LIMIT_BYTES),
        name="experts",
    )(block_e, n_valid, n_used, expert_meta, xs, wg, wu, wd)


def _combine_kernel(w_ref, x2_ref, g2_ref, gfin_ref, yg_ref, *maybe_prev_and_out):
    o_ref = maybe_prev_and_out[-1]
    tc = x2_ref.shape[1]
    eye = (lax.broadcasted_iota(jnp.int32, (tc, tc), 0) == lax.broadcasted_iota(jnp.int32, (tc, tc), 1)).astype(F32)
    w = lax.dot_general(eye, w_ref[...], (((1,), (1,)), ((), ())), precision=HIGHEST, preferred_element_type=F32)
    planes = yg_ref.shape[1]
    routed = None
    for k in range(TOP_K):
        rows = jnp.concatenate([_unpack_pairs_f32(yg_ref[k, j]) for j in range(planes)], axis=1) * w[:, k:k + 1]
        routed = rows if routed is None else routed + rows
    o_ref[0] = _rms(x2_ref[0] + g2_ref[0] * routed) * gfin_ref[...]


def _combine(wtok, x2, g2, gfin, yg, seg, prev_out):
    b, s, d = x2.shape
    _, planes, seg_tokens, lanes = yg.shape
    tc = COMBINE_TOKENS
    nt = seg_tokens // tc
    per_seq = s // seg_tokens
    assert nt * tc == seg_tokens and per_seq * seg_tokens == s
    bi, t0 = seg // per_seq, (seg % per_seq) * nt
    in_specs = [pl.BlockSpec((TOP_K, tc), lambda j: (0, seg * nt + j)),
                pl.BlockSpec((1, tc, d), lambda j: (bi, t0 + j, 0)),
                pl.BlockSpec((1, 1, d), lambda j: (bi, 0, 0)),
                pl.BlockSpec((1, d), lambda j: (0, 0)),
                pl.BlockSpec((TOP_K, planes, tc, lanes), lambda j: (0, 0, j, 0))]
    args = [wtok, x2, g2, gfin, yg]
    aliases = {}
    if prev_out is not None:
        in_specs.append(pl.BlockSpec(memory_space=pl.ANY))
        args.append(prev_out)
        aliases = {len(args) - 1: 0}
    return pl.pallas_call(
        _combine_kernel,
        grid=(nt,),
        in_specs=in_specs,
        out_specs=pl.BlockSpec((1, tc, d), lambda j: (bi, t0 + j, 0)),
        out_shape=jax.ShapeDtypeStruct((b, s, d), F32),
        input_output_aliases=aliases,
        compiler_params=pltpu.CompilerParams(dimension_semantics=("arbitrary",),
                                             vmem_limit_bytes=VMEM_LIMIT_BYTES),
        name="combine",
    )(*args)


def kernel(x, c, positions, w_ada, b_ada, norm_mix_g, w_in, ret_norm_g, conv_w, w_br_ret, w_br_conv, w_out,
           norm_ffn_g, w_router, router_bias, w_exp_gate, w_exp_up, w_exp_down, w_sh_gate, w_sh_up, w_sh_down,
           norm_final_g):
    b, s, d = x.shape
    t = b * s
    depth = w_in.shape[0]
    assert depth == 1, "the combine kernel applies the final norm, so exactly one layer is supported"
    posf = positions.astype(F32)[:, :, None]
    c8 = jnp.zeros((8, d), F32).at[:b].set(c)
    bm = EXPERT_ROWS
    n_blocks = t * TOP_K // bm + N_EXPERTS
    n_rows = n_blocks * bm

    for l in range(depth):
        mod = _ada(c8, w_ada[l], b_ada[l][None, :])[:b].reshape(b, 6, d)
        x2, h2p, eidx, wts, rank, counts = _mixer(
            x, posf, mod, norm_mix_g[l][None, :], w_in[l].astype(BF16), ret_norm_g[l][None, :], conv_w[l],
            w_br_ret[l].astype(BF16), w_br_conv[l].astype(BF16), w_out[l].astype(BF16), norm_ffn_g[l][None, :],
            w_router[l].T, jnp.concatenate([w_sh_gate[l], w_sh_up[l]], axis=1).astype(BF16),
            w_sh_down[l].astype(BF16), router_bias[l][:, None])

        cnt = counts[:, 0].astype(jnp.int32)
        pcnt = (cnt + bm - 1) // bm * bm
        pend = jnp.cumsum(pcnt)
        pstart = pend - pcnt
        n_used = jnp.maximum(pend[-1] // bm, 1).astype(jnp.int32)[None]
        block_start = jnp.arange(n_blocks, dtype=jnp.int32) * bm
        block_e = jnp.minimum(jnp.sum((pend[None, :] <= block_start[:, None]).astype(jnp.int32), axis=1),
                              N_EXPERTS - 1)
        eid = jnp.arange(N_EXPERTS, dtype=jnp.int32)
        seg_end = jnp.sum(jnp.where(block_e[:, None] == eid[None, :], (pstart + cnt)[None, :], 0), axis=1)
        n_valid = jnp.clip(seg_end - block_start, 0, bm).astype(jnp.int32)
        later_used = jnp.logical_and(eid[None, :] > eid[:, None], cnt[None, :] > 0)
        next_expert = jnp.min(jnp.where(later_used, eid[None, :], N_EXPERTS), axis=1)
        stage_slot = (jnp.cumsum((cnt > 0).astype(jnp.int32)) - 1) % EXPERT_STAGES
        ahead = [stage_slot, next_expert]
        for _ in range(EXPERT_STAGES - 1):
            hop = jnp.sum(jnp.where(ahead[-1][:, None] == eid[None, :], next_expert[None, :], 0), axis=1)
            ahead.append(jnp.where(ahead[-1] == N_EXPERTS, N_EXPERTS, hop))
        expert_meta = jnp.stack(ahead).astype(jnp.int32)

        xs, dest3 = _dispatch(h2p, eidx, rank, pstart.astype(jnp.int32), n_rows)
        ybuf = _experts(block_e, n_valid, n_used, expert_meta, xs, w_exp_gate[l], w_exp_up[l], w_exp_down[l])
        chunks = t // LANES // COMBINE_SEGMENTS
        out = None
        for seg in range(COMBINE_SEGMENTS):
            yg = _gather(ybuf, dest3[seg * chunks:(seg + 1) * chunks], chunks * LANES)
            out = _combine(wts, x2, mod[:, 5:6, :], norm_final_g[None, :], yg, seg, out)
        x = out
    return x
```

```python
import functools

import numpy as np
import jax
import jax.numpy as jnp
from jax import lax
from jax.experimental import pallas as pl
from jax.experimental.pallas import tpu as pltpu
from jax.experimental.pallas import tpu_sc as plsc

RET_HEADS = 4
RET_DK = 128
RET_DV = 256
RET_CHUNK = 128
ROPE_THETA = 10000.0
CONV_K = 3
N_EXPERTS = 256
TOP_K = 8
N_GROUPS = 8
TOPK_GROUPS = 4
GROUP_SIZE = N_EXPERTS // N_GROUPS
ROUTED_SCALE = 2.5
NORM_EPS = 1e-6

MIXER_TOKENS = 512
EXPERT_ROWS = 128
EXPERT_MAX_RUN = 4
EXPERT_BLOCKS_PER_STEP = 16
EXPERT_STAGES = 5
COMBINE_TOKENS = 512
COMBINE_SEGMENTS = 4
LANES = 128
GATHER_BUFFERS = 4

VMEM_LIMIT_BYTES = 56 * 1024 * 1024

F32 = jnp.float32
BF16 = jnp.bfloat16
HIGHEST = lax.Precision.HIGHEST


def _sigmoid(v):
    return 0.5 * jnp.tanh(0.5 * v) + 0.5


def _silu(v):
    return v * _sigmoid(v)


def _rms(v):
    return v * lax.rsqrt(jnp.mean(v * v, axis=-1, keepdims=True) + NORM_EPS)


def _resident(shape):
    nd = len(shape)
    return pl.BlockSpec(shape, lambda *_: (0,) * nd, pipeline_mode=pl.Buffered(1))


def _ada_kernel(c_ref, w_ref, b_ref, o_ref):
    c = c_ref[...]
    o_ref[...] = jnp.dot(_silu(c).astype(BF16), w_ref[...].astype(BF16), preferred_element_type=F32) + b_ref[...]


def _ada(c8, w, b):
    d, n = w.shape
    tn = 1024
    return pl.pallas_call(
        _ada_kernel,
        grid=(n // tn,),
        in_specs=[pl.BlockSpec((8, d), lambda j: (0, 0)),
                  pl.BlockSpec((d, tn), lambda j: (0, j)),
                  pl.BlockSpec((1, tn), lambda j: (0, j))],
        out_specs=pl.BlockSpec((8, tn), lambda j: (0, j)),
        out_shape=jax.ShapeDtypeStruct((8, n), F32),
        name="ada",
    )(c8, w, b)


def _retention_constants():
    f32 = np.float32
    c = RET_CHUNK
    log_g = np.log1p(-(f32(2.0) ** (f32(-5.0) - np.arange(RET_HEADS, dtype=f32)))).astype(f32)
    idx = np.arange(c, dtype=f32)
    diff = idx[:, None] - idx[None, :]
    intra = np.where(diff >= 0, np.exp(log_g[:, None, None] * np.maximum(diff, f32(0.0))), f32(0.0))
    k_decay = np.exp(log_g[:, None] * (f32(c - 1) - idx))
    q_decay = np.exp(log_g[:, None] * (idx + f32(1.0)))
    chunk_decay = np.exp(log_g * f32(c))
    kd = np.broadcast_to(k_decay[:, :, None], (RET_HEADS, c, RET_DK))
    qd = np.broadcast_to(q_decay[:, :, None], (RET_HEADS, c, RET_DK))
    cd = np.broadcast_to(chunk_decay[:, None, None], (RET_HEADS, 1, RET_DV))
    inv_freq = f32(ROPE_THETA) ** (-np.arange(0, RET_DK, 2, dtype=f32) / f32(RET_DK))
    inv_freq = np.concatenate([inv_freq, inv_freq])[None, :]
    sign = np.concatenate([-np.ones((RET_DK // 2,), f32), np.ones((RET_DK // 2,), f32)])[None, :]
    return tuple(jnp.asarray(np.ascontiguousarray(v, dtype=f32)) for v in (intra, qd, kd, cd, inv_freq, sign))


def _pack_bf16_pairs(v):
    w = v.shape[1] // 2
    lo = lax.bitcast_convert_type(v[:, :w].astype(BF16).astype(F32), jnp.uint32)
    hi = lax.bitcast_convert_type(v[:, w:].astype(BF16).astype(F32), jnp.uint32)
    return (lo >> 16) | (hi & jnp.uint32(0xFFFF0000))


def _unpack_pairs_f32(word):
    lo = lax.bitcast_convert_type(word << 16, F32)
    hi = lax.bitcast_convert_type(word & jnp.uint32(0xFFFF0000), F32)
    return jnp.concatenate([lo, hi], axis=1)


def _mix_tile(x_ref, pos_ref, mod_ref, gmix_ref, win_ref, invf_ref, sign_ref, intra_ref, qd_ref, kd_ref, cd_ref,
              retg_ref, convw_ref, wbr_ref, wbc_ref, wout_ref, gffn_ref, wrh_ref, wrl_ref, wshgu_ref, wshd_ref,
              x2_ref, h2p_ref, lg_ref, state_ref, carry_ref, q_ref, k_ref, v_ref, ret_ref, hb_ref, *, background):
    def tick():
        next(background, None)

    tm, d = x_ref.shape[1], x_ref.shape[2]
    q_w = RET_HEADS * RET_DK
    v_w = RET_HEADS * RET_DV
    offs = np.cumsum([0, q_w, q_w, v_w, v_w, d, d, d, d, d])

    x = x_ref[0]
    mod = mod_ref[0]
    sh1, sc1, g1, sh2, sc2, g2 = [mod[i:i + 1] for i in range(6)]
    hb_ref[...] = ((_rms(x) * gmix_ref[...]) * (1.0 + sc1) + sh1).astype(BF16)

    def proj(i):
        return jnp.dot(hb_ref[...], win_ref[:, offs[i]:offs[i + 1]], preferred_element_type=F32)

    ang = pos_ref[0] * invf_ref[...]
    cosv = jnp.cos(ang)
    sinv = jnp.sin(ang) * sign_ref[...]

    def rope(t):
        return jnp.concatenate(
            [t[:, h * RET_DK:(h + 1) * RET_DK] * cosv
             + pltpu.roll(t[:, h * RET_DK:(h + 1) * RET_DK], RET_DK // 2, 1) * sinv
             for h in range(RET_HEADS)], axis=1)

    q_ref[...] = rope(proj(0)) * (RET_DK ** -0.5)
    tick()
    k_ref[...] = rope(proj(1))
    tick()
    v_ref[...] = proj(2).astype(BF16)
    tick()

    for c in range(tm // RET_CHUNK):
        rows = pl.ds(c * RET_CHUNK, RET_CHUNK)
        for h in range(RET_HEADS):
            qh = q_ref[rows, h * RET_DK:(h + 1) * RET_DK]
            kh = k_ref[rows, h * RET_DK:(h + 1) * RET_DK]
            vh = v_ref[rows, h * RET_DV:(h + 1) * RET_DV]
            scores = lax.dot_general(qh.astype(BF16), kh.astype(BF16), (((1,), (1,)), ((), ())),
                                     preferred_element_type=F32) * intra_ref[h]
            inner = jnp.dot(scores.astype(BF16), vh, preferred_element_type=F32)
            st = state_ref[h]
            cross = jnp.dot((qh * qd_ref[h]).astype(BF16), st.astype(BF16), preferred_element_type=F32)
            kv = lax.dot_general((kh * kd_ref[h]).astype(BF16), vh, (((0,), (0,)), ((), ())),
                                 preferred_element_type=F32)
            state_ref[h] = st * cd_ref[h] + kv
            ret_ref[rows, h * RET_DV:(h + 1) * RET_DV] = _rms(inner + cross)
        tick()

    y_ret = jnp.dot((_silu(proj(3)) * (ret_ref[...] * retg_ref[...])).astype(BF16), wbr_ref[...],
                    preferred_element_type=F32)
    tick()

    z = proj(5) * proj(4)
    row = lax.broadcasted_iota(jnp.int32, z.shape, 0)
    prev1 = carry_ref[7:8, :]
    prev2 = carry_ref[6:7, :]
    z1 = jnp.where(row == 0, prev1, pltpu.roll(z, 1, 0))
    z2 = jnp.where(row == 0, prev2, jnp.where(row == 1, prev1, pltpu.roll(z, 2, 0)))
    conv = convw_ref[0:1, :] * z2 + convw_ref[1:2, :] * z1 + convw_ref[2:3, :] * z
    carry_ref[...] = z[tm - 8:tm, :]
    tick()
    y_conv = jnp.dot((proj(6) * conv).astype(BF16), wbc_ref[...], preferred_element_type=F32)
    tick()

    mix = _sigmoid(proj(7)) * y_ret + _sigmoid(proj(8)) * y_conv
    tick()
    x1 = x + g1 * jnp.dot(mix.astype(BF16), wout_ref[...], preferred_element_type=F32)
    tick()

    h2 = (_rms(x1) * gffn_ref[...]) * (1.0 + sc2) + sh2
    hi_f32 = lax.bitcast_convert_type(lax.bitcast_convert_type(h2, jnp.uint32) & jnp.uint32(0xFFFF0000), F32)
    h2_hi = hi_f32.astype(BF16)
    h2_lo = (h2 - hi_f32).astype(BF16)

    def nt_dot(a, b):
        return lax.dot_general(a, b, (((1,), (1,)), ((), ())), preferred_element_type=F32)

    lg_ref[...] = nt_dot(wrh_ref[...], h2_hi) + (nt_dot(wrh_ref[...], h2_lo) + nt_dot(wrl_ref[...], h2_hi))
    gu = jnp.dot(h2.astype(BF16), wshgu_ref[...], preferred_element_type=F32)
    f = gu.shape[1] // 2
    shared = jnp.dot((_silu(gu[:, :f]) * gu[:, f:]).astype(BF16), wshd_ref[...], preferred_element_type=F32)
    x2_ref[0] = x1 + g2 * shared
    for j in range(h2p_ref.shape[0]):
        h2p_ref[j] = _pack_bf16_pairs(h2[:, 2 * LANES * j:2 * LANES * (j + 1)])
    for _ in background:
        pass


def _mixer_kernel(*refs, tiles_per_seq):
    mix_inputs, (bias_ref, tri_ref) = refs[:21], refs[21:23]
    x2_ref, h2p_ref, eidx_ref, w_ref, rank_ref, cnt_ref = refs[23:29]
    state_ref, carry_ref, q_ref, k_ref, v_ref, ret_ref, hb_ref, lg_ref, rcarry_ref = refs[29:]
    s = pl.program_id(0)
    n_tiles = pl.num_programs(0) - 1

    @pl.when(s == 0)
    def _():
        lg_ref[...] = jnp.zeros_like(lg_ref)
        rcarry_ref[...] = jnp.zeros_like(rcarry_ref)

    @pl.when(s % tiles_per_seq == 0)
    def _():
        state_ref[...] = jnp.zeros_like(state_ref)
        carry_ref[...] = jnp.zeros_like(carry_ref)

    def route_previous_tile():
        return _route_phases(lg_ref.at[(s + 1) % 2], bias_ref, tri_ref, eidx_ref, w_ref, rank_ref, cnt_ref,
                             rcarry_ref, (s > 0).astype(jnp.int32))

    @pl.when(s == n_tiles)
    def _():
        for _ in route_previous_tile():
            pass

    @pl.when(s < n_tiles)
    def _():
        _mix_tile(*mix_inputs, x2_ref, h2p_ref, lg_ref.at[s % 2], state_ref, carry_ref, q_ref, k_ref, v_ref,
                  ret_ref, hb_ref, background=route_previous_tile())


def _mixer(x, posf, mod, gmix, win, retg, convw, wbr, wbc, wout, gffn, wr, wshgu, wshd, bias):
    b, s, d = x.shape
    tm = MIXER_TOKENS
    nt = s // tm
    n_tiles = b * nt
    intra, qd, kd, cd, invf, sign = _retention_constants()
    q_w, v_w = RET_HEADS * RET_DK, RET_HEADS * RET_DV
    hi_f32 = lax.bitcast_convert_type(lax.bitcast_convert_type(wr, jnp.uint32) & jnp.uint32(0xFFFF0000), F32)
    wr_hi = hi_f32.astype(BF16)
    wr_lo = (wr - hi_f32).astype(BF16)
    tri = jnp.asarray(np.triu(np.ones((tm, tm), np.float32), 1), BF16)
    weights = (gmix, win, invf, sign, intra, qd, kd, cd, retg, convw, wbr, wbc, wout, gffn, wr_hi, wr_lo,
               wshgu, wshd, bias, tri)

    def tile(i):
        return jnp.minimum(i, n_tiles - 1)

    def routed(i):
        return (0, jnp.maximum(i - 1, 0))

    def routed_chunks(i):
        return (jnp.maximum(i - 1, 0), 0, 0)

    return pl.pallas_call(
        functools.partial(_mixer_kernel, tiles_per_seq=nt),
        grid=(n_tiles + 1,),
        in_specs=[pl.BlockSpec((1, tm, d), lambda i: (tile(i) // nt, tile(i) % nt, 0)),
                  pl.BlockSpec((1, tm, 1), lambda i: (tile(i) // nt, tile(i) % nt, 0)),
                  pl.BlockSpec((1, 6, d), lambda i: (tile(i) // nt, 0, 0))]
                 + [_resident(w.shape) for w in weights],
        out_specs=[pl.BlockSpec((1, tm, d), lambda i: (tile(i) // nt, tile(i) % nt, 0)),
                   pl.BlockSpec((d // (2 * LANES), tm, LANES), lambda i: (0, tile(i), 0)),
                   pl.BlockSpec((tm // LANES, TOP_K, LANES), routed_chunks),
                   pl.BlockSpec((TOP_K, tm), routed),
                   pl.BlockSpec((tm // LANES, TOP_K, LANES), routed_chunks),
                   pl.BlockSpec((N_EXPERTS, 1), lambda i: (0, 0))],
        out_shape=[jax.ShapeDtypeStruct((b, s, d), F32),
                   jax.ShapeDtypeStruct((d // (2 * LANES), b * s, LANES), jnp.uint32),
                   jax.ShapeDtypeStruct((b * s // LANES, TOP_K, LANES), jnp.int32),
                   jax.ShapeDtypeStruct((TOP_K, b * s), F32),
                   jax.ShapeDtypeStruct((b * s // LANES, TOP_K, LANES), jnp.int32),
                   jax.ShapeDtypeStruct((N_EXPERTS, 1), F32)],
        scratch_shapes=[pltpu.VMEM((RET_HEADS, RET_DK, RET_DV), F32),
                        pltpu.VMEM((8, d), F32),
                        pltpu.VMEM((tm, q_w), F32),
                        pltpu.VMEM((tm, q_w), F32),
                        pltpu.VMEM((tm, v_w), BF16),
                        pltpu.VMEM((tm, v_w), F32),
                        pltpu.VMEM((tm, d), BF16),
                        pltpu.VMEM((2, N_EXPERTS, tm), F32),
                        pltpu.VMEM((N_EXPERTS, 1), F32)],
        compiler_params=pltpu.CompilerParams(dimension_semantics=("arbitrary",),
                                             vmem_limit_bytes=VMEM_LIMIT_BYTES),
        name="mixer",
    )(x, posf, mod, *weights)


def _first_argmax(v, idx, n):
    m = jnp.max(v, axis=0, keepdims=True)
    return m, jnp.min(jnp.where(v == m, idx, n), axis=0, keepdims=True)


def _route_phases(logits_ref, bias_ref, tri_ref, eidx_ref, w_ref, rank_ref, cnt_ref, carry_ref, count_it):
    tn = logits_ref.shape[1]
    neg = F32(-jnp.inf)
    score = _sigmoid(logits_ref[...])
    choice = score + bias_ref[...]

    grow = lax.broadcasted_iota(jnp.int32, (GROUP_SIZE, tn), 0)
    gscores = []
    for g in range(N_GROUPS):
        cg = choice[g * GROUP_SIZE:(g + 1) * GROUP_SIZE]
        m1, i1 = _first_argmax(cg, grow, GROUP_SIZE)
        m2 = jnp.max(jnp.where(grow == i1, neg, cg), axis=0, keepdims=True)
        gscores.append(m1 + m2)
    cur = jnp.concatenate(gscores, axis=0)
    yield
    gidx = lax.broadcasted_iota(jnp.int32, (N_GROUPS, tn), 0)
    keep = jnp.zeros((N_GROUPS, tn), F32)
    for _ in range(TOPK_GROUPS):
        _, ig = _first_argmax(cur, gidx, N_GROUPS)
        hit = gidx == ig
        keep = jnp.where(hit, 1.0, keep)
        cur = jnp.where(hit, neg, cur)
    cur = jnp.concatenate(
        [jnp.where(keep[g:g + 1] > 0.0, choice[g * GROUP_SIZE:(g + 1) * GROUP_SIZE], neg)
         for g in range(N_GROUPS)], axis=0)
    yield

    erow = lax.broadcasted_iota(jnp.int32, (N_EXPERTS, tn), 0)
    eidx, wts = [], []
    member = jnp.zeros((N_EXPERTS, tn), F32)
    for _ in range(TOP_K):
        _, ie = _first_argmax(cur, erow, N_EXPERTS)
        hit = erow == ie
        eidx.append(ie)
        wts.append(jnp.sum(jnp.where(hit, score, 0.0), axis=0, keepdims=True))
        member = member + hit.astype(F32)
        cur = jnp.where(hit, neg, cur)
        yield
    wsum = wts[0]
    for k in range(1, TOP_K):
        wsum = wsum + wts[k]

    before = jnp.dot(member.astype(BF16), tri_ref[...], preferred_element_type=F32) + carry_ref[...]
    ranks = []
    for k in range(TOP_K):
        ranks.append(jnp.sum(jnp.where(erow == eidx[k], before, 0.0), axis=0, keepdims=True))
        if k % 2 == 1:
            yield
    carry_ref[...] = carry_ref[...] + count_it.astype(F32) * jnp.sum(member, axis=1, keepdims=True)

    eidx_rows = jnp.concatenate(eidx, axis=0)
    rank_rows = jnp.concatenate(ranks, axis=0).astype(jnp.int32)
    for c in range(tn // LANES):
        eidx_ref[c] = eidx_rows[:, c * LANES:(c + 1) * LANES]
        rank_ref[c] = rank_rows[:, c * LANES:(c + 1) * LANES]
    w_ref[...] = jnp.concatenate([w / wsum * ROUTED_SCALE for w in wts], axis=0)
    cnt_ref[...] = carry_ref[...]


def _sc_mesh_and_workers():
    mesh = plsc.VectorSubcoreMesh(core_axis_name="c", subcore_axis_name="s")
    return mesh, mesh.num_cores, mesh.num_cores * mesh.num_subcores


def _worker_id(num_cores):
    return lax.axis_index("s") * num_cores + lax.axis_index("c")


def _dispatch(h2p, eidx3, rank3, pstart, n_rows):
    planes, t, lanes = h2p.shape
    n_chunks = eidx3.shape[0]
    mesh, num_cores, workers = _sc_mesh_and_workers()
    chunks_per_worker = n_chunks // workers
    vec = plsc.get_sparse_core_info().num_lanes
    assert chunks_per_worker * workers == n_chunks and n_chunks * lanes == t and lanes % vec == 0

    def body(h_hbm, e_hbm, r_hbm, p_hbm, xs_hbm, d_hbm, e_v, r_v, p_v, idx_v, rows_v, load_sem, scatter_sem):
        wid = _worker_id(num_cores)
        mine = pl.ds(wid * chunks_per_worker, chunks_per_worker)
        pltpu.sync_copy(p_hbm, p_v)
        pltpu.sync_copy(e_hbm.at[mine], e_v)
        pltpu.sync_copy(r_hbm.at[mine], r_v)

        @pl.loop(0, chunks_per_worker)
        def _(c):
            for k in range(TOP_K):
                for g in range(lanes // vec):
                    part = pl.ds(g * vec, vec)
                    idx_v[c, k, part] = plsc.load_gather(p_v, [e_v[c, k, part]]) + r_v[c, k, part]

        pltpu.sync_copy(idx_v, d_hbm.at[mine])

        @pl.loop(0, chunks_per_worker)
        def _(c):
            tok = pl.ds((wid * chunks_per_worker + c) * lanes, lanes)
            loads = [None] * planes
            scatters = [None] * planes
            loads[0] = pltpu.async_copy(h_hbm.at[0, tok], rows_v.at[0], load_sem.at[0])
            for j in range(planes):
                loads[j].wait()
                if j + 1 < planes:
                    if j >= 1:
                        for cp in scatters[j - 1]:
                            cp.wait()
                    loads[j + 1] = pltpu.async_copy(h_hbm.at[j + 1, tok], rows_v.at[(j + 1) % 2],
                                                    load_sem.at[(j + 1) % 2])
                scatters[j] = [pltpu.async_copy(rows_v.at[j % 2], xs_hbm.at[j].at[idx_v.at[c, k]],
                                                scatter_sem.at[j % 2]) for k in range(TOP_K)]
            for j in range(max(planes - 2, 0), planes):
                for cp in scatters[j]:
                    cp.wait()

    return pl.kernel(
        body,
        out_type=(jax.ShapeDtypeStruct((planes, n_rows, lanes), h2p.dtype),
                  jax.ShapeDtypeStruct(eidx3.shape, jnp.int32)),
        mesh=mesh,
        scratch_types=[pltpu.VMEM((chunks_per_worker, TOP_K, lanes), jnp.int32),
                       pltpu.VMEM((chunks_per_worker, TOP_K, lanes), jnp.int32),
                       pltpu.VMEM((N_EXPERTS,), jnp.int32),
                       pltpu.VMEM((chunks_per_worker, TOP_K, lanes), jnp.int32),
                       pltpu.VMEM((2, lanes, lanes), h2p.dtype),
                       pltpu.SemaphoreType.DMA((2,)),
                       pltpu.SemaphoreType.DMA((2,))],
        compiler_params=pltpu.CompilerParams(needs_layout_passes=False),
        name="dispatch",
    )(h2p, eidx3, rank3, pstart)


def _gather(ybuf, dest3, t):
    planes, _, lanes = ybuf.shape
    n_chunks = dest3.shape[0]
    mesh, num_cores, workers = _sc_mesh_and_workers()
    chunks_per_worker = n_chunks // workers
    nbuf = GATHER_BUFFERS
    lag = nbuf // 2
    assert chunks_per_worker * workers == n_chunks and n_chunks * lanes == t

    def body(y_hbm, d_hbm, yg_hbm, idx_v, rows_v, gather_sem, store_sem):
        wid = _worker_id(num_cores)

        @pl.loop(0, chunks_per_worker)
        def _(c):
            chunk = wid * chunks_per_worker + c
            tok = pl.ds(chunk * lanes, lanes)
            pltpu.sync_copy(d_hbm.at[chunk], idx_v)

            @pl.loop(0, TOP_K)
            def _(k):
                gathers = [None] * planes
                stores = [None] * planes

                def store(j):
                    gathers[j].wait()
                    stores[j] = pltpu.async_copy(rows_v.at[j % nbuf], yg_hbm.at[k, j, tok], store_sem.at[j % nbuf])

                for j in range(planes):
                    if j >= nbuf:
                        stores[j - nbuf].wait()
                    gathers[j] = pltpu.async_copy(y_hbm.at[j].at[idx_v.at[k]], rows_v.at[j % nbuf],
                                                  gather_sem.at[j % nbuf])
                    if j >= lag:
                        store(j - lag)
                for j in range(max(planes - lag, 0), planes):
                    store(j)
                for j in range(max(planes - nbuf, 0), planes):
                    stores[j].wait()

    return pl.kernel(
        body,
        out_type=jax.ShapeDtypeStruct((TOP_K, planes, t, lanes), ybuf.dtype),
        mesh=mesh,
        scratch_types=[pltpu.VMEM((TOP_K, lanes), jnp.int32),
                       pltpu.VMEM((nbuf, lanes, lanes), ybuf.dtype),
                       pltpu.SemaphoreType.DMA((nbuf,)),
                       pltpu.SemaphoreType.DMA((nbuf,))],
        name="gather",
    )(ybuf, dest3)


def _experts_kernel(be_ref, nv_ref, nu_ref, em_ref, x_ref, wg_hbm, wu_hbm, wd_hbm, o_ref,
                    wgs_ref, wus_ref, wds_ref, wgb_ref, wub_ref, wdb_ref, sems):
    planes, step_rows, lanes = x_ref.shape
    bm = EXPERT_ROWS
    blocks_per_step = step_rows // bm
    step = pl.program_id(0)
    last_step = (nu_ref[0] - 1) // blocks_per_step

    def weight_copies(expert):
        slot = em_ref[0, expert]
        return [pltpu.make_async_copy(src.at[expert], dst.at[slot], sems.at[slot, n])
                for n, (src, dst) in enumerate(((wg_hbm, wgs_ref), (wu_hbm, wus_ref), (wd_hbm, wds_ref)))]

    def start_if_any(expert):
        @pl.when(expert < N_EXPERTS)
        def _():
            for cp in weight_copies(expert):
                cp.start()

    def swiglu_rows(row0, m, n_valid):
        rows = pl.ds(row0, m)
        valid = lax.broadcasted_iota(jnp.int32, (m, lanes), 0) < n_valid
        g = None
        u = None
        for j in range(planes):
            xj = _unpack_pairs_f32(jnp.where(valid, x_ref[j, rows, :], jnp.uint32(0))).astype(BF16)
            wrows = pl.ds(2 * lanes * j, 2 * lanes)
            gj = jnp.dot(xj, wgb_ref[wrows, :], preferred_element_type=F32)
            uj = jnp.dot(xj, wub_ref[wrows, :], preferred_element_type=F32)
            g = gj if g is None else g + gj
            u = uj if u is None else u + uj
        out = jnp.dot((_silu(g) * u).astype(BF16), wdb_ref[...], preferred_element_type=F32)
        for j in range(planes):
            o_ref[j, rows, :] = _pack_bf16_pairs(out[:, 2 * lanes * j:2 * lanes * (j + 1)])

    def block_step(sb):
        i = step * blocks_per_step + sb
        row0 = pl.multiple_of(sb * bm, bm)
        rows = pl.ds(row0, bm)
        active = i < nu_ref[0]
        e = be_ref[i]
        new_expert = jnp.logical_or(i == 0, e != be_ref[jnp.maximum(i - 1, 0)])
        run = active.astype(jnp.int32)
        same = active
        for ahead in range(1, EXPERT_MAX_RUN):
            j = jnp.minimum(i + ahead, be_ref.shape[0] - 1)
            same = jnp.logical_and(jnp.logical_and(same, sb + ahead < blocks_per_step),
                                   jnp.logical_and(i + ahead < nu_ref[0], be_ref[j] == e))
            run = run + same.astype(jnp.int32)

        @pl.when(jnp.logical_and(active, new_expert))
        def _():
            @pl.when(i == 0)
            def _():
                for cp in weight_copies(e):
                    cp.start()
                for ahead in range(1, EXPERT_STAGES):
                    start_if_any(em_ref[ahead, e])

            for cp in weight_copies(e):
                cp.wait()
            slot = em_ref[0, e]
            wgb_ref[...] = wgs_ref[slot].astype(BF16)
            wub_ref[...] = wus_ref[slot].astype(BF16)
            wdb_ref[...] = wds_ref[slot].astype(BF16)
            start_if_any(em_ref[EXPERT_STAGES, e])

        for m in range(1, EXPERT_MAX_RUN + 1):
            @pl.when(run == m)
            def _():
                last = jnp.minimum(i + (m - 1), be_ref.shape[0] - 1)
                swiglu_rows(row0, m * bm, (m - 1) * bm + nv_ref[last])

        @pl.when(jnp.logical_and(jnp.logical_not(active), step == last_step))
        def _():
            for j in range(planes):
                o_ref[j, rows, :] = jnp.zeros((bm, lanes), o_ref.dtype)

        return sb + jnp.maximum(run, 1)

    lax.while_loop(lambda sb: sb < blocks_per_step, block_step, jnp.int32(0))


def _experts(block_e, n_valid, n_used, expert_meta, xs, wg, wu, wd):
    planes, p, lanes = xs.shape
    e, d, f = wg.shape
    step_rows = EXPERT_ROWS * EXPERT_BLOCKS_PER_STEP
    assert p % step_rows == 0

    def row_map(i, be, nv, nu, nxt):
        return (0, jnp.minimum(i, (nu[0] - 1) // EXPERT_BLOCKS_PER_STEP), 0)

    return pl.pallas_call(
        _experts_kernel,
        grid_spec=pltpu.PrefetchScalarGridSpec(
            num_scalar_prefetch=4,
            grid=(p // step_rows,),
            in_specs=[pl.BlockSpec((planes, step_rows, lanes), row_map),
                      pl.BlockSpec(memory_space=pl.ANY),
                      pl.BlockSpec(memory_space=pl.ANY),
                      pl.BlockSpec(memory_space=pl.ANY)],
            out_specs=pl.BlockSpec((planes, step_rows, lanes), row_map),
            scratch_shapes=[pltpu.VMEM((EXPERT_STAGES, d, f), F32), pltpu.VMEM((EXPERT_STAGES, d, f), F32),
                            pltpu.VMEM((EXPERT_STAGES, f, d), F32),
                            pltpu.VMEM((d, f), BF16), pltpu.VMEM((d, f), BF16), pltpu.VMEM((f, d), BF16),
                            pltpu.SemaphoreType.DMA((EXPERT_STAGES, 3))]),
        out_shape=jax.ShapeDtypeStruct((planes, p, lanes), jnp.uint32),
        compiler_params=pltpu.CompilerParams(dimension_semantics=("arbitrary",),
                                             vmem_limit_bytes=VMEM_LIMIT_BYTES),
        name="experts",
    )(block_e, n_valid, n_used, expert_meta, xs, wg, wu, wd)


def _combine_kernel(w_ref, x2_ref, g2_ref, gfin_ref, yg_ref, *maybe_prev_and_out):
    o_ref = maybe_prev_and_out[-1]
    tc = x2_ref.shape[1]
    eye = (lax.broadcasted_iota(jnp.int32, (tc, tc), 0) == lax.broadcasted_iota(jnp.int32, (tc, tc), 1)).astype(F32)
    w = lax.dot_general(eye, w_ref[...], (((1,), (1,)), ((), ())), precision=HIGHEST, preferred_element_type=F32)
    planes = yg_ref.shape[1]
    routed = None
    for k in range(TOP_K):
        rows = jnp.concatenate([_unpack_pairs_f32(yg_ref[k, j]) for j in range(planes)], axis=1) * w[:, k:k + 1]
        routed = rows if routed is None else routed + rows
    o_ref[0] = _rms(x2_ref[0] + g2_ref[0] * routed) * gfin_ref[...]


def _combine(wtok, x2, g2, gfin, yg, seg, prev_out):
    b, s, d = x2.shape
    _, planes, seg_tokens, lanes = yg.shape
    tc = COMBINE_TOKENS
    nt = seg_tokens // tc
    per_seq = s // seg_tokens
    assert nt * tc == seg_tokens and per_seq * seg_tokens == s
    bi, t0 = seg // per_seq, (seg % per_seq) * nt
    in_specs = [pl.BlockSpec((TOP_K, tc), lambda j: (0, seg * nt + j)),
                pl.BlockSpec((1, tc, d), lambda j: (bi, t0 + j, 0)),
                pl.BlockSpec((1, 1, d), lambda j: (bi, 0, 0)),
                pl.BlockSpec((1, d), lambda j: (0, 0)),
                pl.BlockSpec((TOP_K, planes, tc, lanes), lambda j: (0, 0, j, 0))]
    args = [wtok, x2, g2, gfin, yg]
    aliases = {}
    if prev_out is not None:
        in_specs.append(pl.BlockSpec(memory_space=pl.ANY))
        args.append(prev_out)
        aliases = {len(args) - 1: 0}
    return pl.pallas_call(
        _combine_kernel,
        grid=(nt,),
        in_specs=in_specs,
        out_specs=pl.BlockSpec((1, tc, d), lambda j: (bi, t0 + j, 0)),
        out_shape=jax.ShapeDtypeStruct((b, s, d), F32),
        input_output_aliases=aliases,
        compiler_params=pltpu.CompilerParams(dimension_semantics=("arbitrary",),
                                             vmem_limit_bytes=VMEM_LIMIT_BYTES),
        name="combine",
    )(*args)


def kernel(x, c, positions, w_ada, b_ada, norm_mix_g, w_in, ret_norm_g, conv_w, w_br_ret, w_br_conv, w_out,
           norm_ffn_g, w_router, router_bias, w_exp_gate, w_exp_up, w_exp_down, w_sh_gate, w_sh_up, w_sh_down,
           norm_final_g):
    b, s, d = x.shape
    t = b * s
    depth = w_in.shape[0]
    assert depth == 1, "the combine kernel applies the final norm, so exactly one layer is supported"
    posf = positions.astype(F32)[:, :, None]
    c8 = jnp.zeros((8, d), F32).at[:b].set(c)
    bm = EXPERT_ROWS
    n_blocks = t * TOP_K // bm + N_EXPERTS
    n_rows = n_blocks * bm

    for l in range(depth):
        mod = _ada(c8, w_ada[l], b_ada[l][None, :])[:b].reshape(b, 6, d)
        x2, h2p, eidx, wts, rank, counts = _mixer(
            x, posf, mod, norm_mix_g[l][None, :], w_in[l].astype(BF16), ret_norm_g[l][None, :], conv_w[l],
            w_br_ret[l].astype(BF16), w_br_conv[l].astype(BF16), w_out[l].astype(BF16), norm_ffn_g[l][None, :],
            w_router[l].T, jnp.concatenate([w_sh_gate[l], w_sh_up[l]], axis=1).astype(BF16),
            w_sh_down[l].astype(BF16), router_bias[l][:, None])

        cnt = counts[:, 0].astype(jnp.int32)
        pcnt = (cnt + bm - 1) // bm * bm
        pend = jnp.cumsum(pcnt)
        pstart = pend - pcnt
        n_used = jnp.maximum(pend[-1] // bm, 1).astype(jnp.int32)[None]
        block_start = jnp.arange(n_blocks, dtype=jnp.int32) * bm
        block_e = jnp.minimum(jnp.sum((pend[None, :] <= block_start[:, None]).astype(jnp.int32), axis=1),
                              N_EXPERTS - 1)
        eid = jnp.arange(N_EXPERTS, dtype=jnp.int32)
        seg_end = jnp.sum(jnp.where(block_e[:, None] == eid[None, :], (pstart + cnt)[None, :], 0), axis=1)
        n_valid = jnp.clip(seg_end - block_start, 0, bm).astype(jnp.int32)
        later_used = jnp.logical_and(eid[None, :] > eid[:, None], cnt[None, :] > 0)
        next_expert = jnp.min(jnp.where(later_used, eid[None, :], N_EXPERTS), axis=1)
        stage_slot = (jnp.cumsum((cnt > 0).astype(jnp.int32)) - 1) % EXPERT_STAGES
        ahead = [stage_slot, next_expert]
        for _ in range(EXPERT_STAGES - 1):
            hop = jnp.sum(jnp.where(ahead[-1][:, None] == eid[None, :], next_expert[None, :], 0), axis=1)
            ahead.append(jnp.where(ahead[-1] == N_EXPERTS, N_EXPERTS, hop))
        expert_meta = jnp.stack(ahead).astype(jnp.int32)

        xs, dest3 = _dispatch(h2p, eidx, rank, pstart.astype(jnp.int32), n_rows)
        ybuf = _experts(block_e, n_valid, n_used, expert_meta, xs, w_exp_gate[l], w_exp_up[l], w_exp_down[l])
        chunks = t // LANES // COMBINE_SEGMENTS
        out = None
        for seg in range(COMBINE_SEGMENTS):
            yg = _gather(ybuf, dest3[seg * chunks:(seg + 1) * chunks], chunks * LANES)
            out = _combine(wts, x2, mod[:, 5:6, :], norm_final_g[None, :], yg, seg, out)
        x = out
    return x
```

```python
import functools

import numpy as np
import jax
import jax.numpy as jnp
from jax import lax
from jax.experimental import pallas as pl
from jax.experimental.pallas import tpu as pltpu
from jax.experimental.pallas import tpu_sc as plsc

RET_HEADS = 4
RET_DK = 128
RET_DV = 256
RET_CHUNK = 128
ROPE_THETA = 10000.0
N_EXPERTS = 256
TOP_K = 8
N_GROUPS = 8
TOPK_GROUPS = 4
GROUP_SIZE = N_EXPERTS // N_GROUPS
ROUTED_SCALE = 2.5
NORM_EPS = 1e-6

MIXER_TOKENS = 512
EXPERT_ROWS = 128
EXPERT_MAX_RUN = 4
EXPERT_BLOCKS_PER_STEP = 16
EXPERT_STAGES = 5
COMBINE_TOKENS = 512
COMBINE_SEGMENTS = 4
LANES = 128
GATHER_BUFFERS = 4

VMEM_LIMIT_BYTES = 56 * 1024 * 1024

F32 = jnp.float32
BF16 = jnp.bfloat16
HIGHEST = lax.Precision.HIGHEST


def _sigmoid(v):
    return 0.5 * jnp.tanh(0.5 * v) + 0.5


def _silu(v):
    return v * _sigmoid(v)


def _rms(v):
    return v * lax.rsqrt(jnp.mean(v * v, axis=-1, keepdims=True) + NORM_EPS)


def _resident(shape):
    nd = len(shape)
    return pl.BlockSpec(shape, lambda *_: (0,) * nd, pipeline_mode=pl.Buffered(1))


def _ada_kernel(c_ref, w_ref, b_ref, o_ref):
    c = c_ref[...]
    o_ref[...] = jnp.dot(_silu(c), w_ref[...], precision=HIGHEST, preferred_element_type=F32) + b_ref[...]


def _ada(c8, w, b):
    d, n = w.shape
    tn = 1024
    return pl.pallas_call(
        _ada_kernel,
        grid=(n // tn,),
        in_specs=[pl.BlockSpec((8, d), lambda j: (0, 0)),
                  pl.BlockSpec((d, tn), lambda j: (0, j)),
                  pl.BlockSpec((1, tn), lambda j: (0, j))],
        out_specs=pl.BlockSpec((8, tn), lambda j: (0, j)),
        out_shape=jax.ShapeDtypeStruct((8, n), F32),
        name="ada",
    )(c8, w, b)


def _retention_constants():
    f32 = np.float32
    c = RET_CHUNK
    log_g = np.log1p(-(f32(2.0) ** (f32(-5.0) - np.arange(RET_HEADS, dtype=f32)))).astype(f32)
    idx = np.arange(c, dtype=f32)
    diff = idx[:, None] - idx[None, :]
    intra = np.where(diff >= 0, np.exp(log_g[:, None, None] * np.maximum(diff, f32(0.0))), f32(0.0))
    k_decay = np.exp(log_g[:, None] * (f32(c - 1) - idx))
    q_decay = np.exp(log_g[:, None] * (idx + f32(1.0)))
    chunk_decay = np.exp(log_g * f32(c))
    kd = np.broadcast_to(k_decay[:, :, None], (RET_HEADS, c, RET_DK))
    qd = np.broadcast_to(q_decay[:, :, None], (RET_HEADS, c, RET_DK))
    cd = np.broadcast_to(chunk_decay[:, None, None], (RET_HEADS, 1, RET_DV))
    inv_freq = f32(ROPE_THETA) ** (-np.arange(0, RET_DK, 2, dtype=f32) / f32(RET_DK))
    inv_freq = np.concatenate([inv_freq, inv_freq])[None, :]
    sign = np.concatenate([-np.ones((RET_DK // 2,), f32), np.ones((RET_DK // 2,), f32)])[None, :]
    return tuple(jnp.asarray(np.ascontiguousarray(v, dtype=f32)) for v in (intra, qd, kd, cd, inv_freq, sign))


def _pack_bf16_pairs(v):
    w = v.shape[1] // 2
    lo = lax.bitcast_convert_type(v[:, :w].astype(BF16).astype(F32), jnp.uint32)
    hi = lax.bitcast_convert_type(v[:, w:].astype(BF16).astype(F32), jnp.uint32)
    return (lo >> 16) | (hi & jnp.uint32(0xFFFF0000))


def _unpack_pairs_f32(word):
    lo = lax.bitcast_convert_type(word << 16, F32)
    hi = lax.bitcast_convert_type(word & jnp.uint32(0xFFFF0000), F32)
    return jnp.concatenate([lo, hi], axis=1)


def _mix_tile(x_ref, pos_ref, mod_ref, gmix_ref, win_ref, invf_ref, sign_ref, intra_ref, qd_ref, kd_ref, cd_ref,
              retg_ref, convw_ref, wbr_ref, wbc_ref, wout_ref, gffn_ref, wrh_ref, wrl_ref, wshgu_ref, wshd_ref,
              x2_ref, h2p_ref, lg_ref, state_ref, carry_ref, q_ref, k_ref, v_ref, ret_ref, hb_ref, *, background):
    def tick():
        next(background, None)

    tm, d = x_ref.shape[1], x_ref.shape[2]
    q_w = RET_HEADS * RET_DK
    v_w = RET_HEADS * RET_DV
    offs = np.cumsum([0, q_w, q_w, v_w, v_w, d, d, d, d, d])

    x = x_ref[0]
    mod = mod_ref[0]
    sh1, sc1, g1, sh2, sc2, g2 = [mod[i:i + 1] for i in range(6)]
    hb_ref[...] = ((_rms(x) * gmix_ref[...]) * (1.0 + sc1) + sh1).astype(BF16)

    def proj(i):
        return jnp.dot(hb_ref[...], win_ref[:, offs[i]:offs[i + 1]], preferred_element_type=F32)

    ang = pos_ref[0] * invf_ref[...]
    cosv = jnp.cos(ang)
    sinv = jnp.sin(ang) * sign_ref[...]

    def rope(t):
        return jnp.concatenate(
            [t[:, h * RET_DK:(h + 1) * RET_DK] * cosv
             + pltpu.roll(t[:, h * RET_DK:(h + 1) * RET_DK], RET_DK // 2, 1) * sinv
             for h in range(RET_HEADS)], axis=1)

    q_ref[...] = rope(proj(0)) * (RET_DK ** -0.5)
    tick()
    k_ref[...] = rope(proj(1))
    tick()
    v_ref[...] = proj(2).astype(BF16)
    tick()

    for c in range(tm // RET_CHUNK):
        rows = pl.ds(c * RET_CHUNK, RET_CHUNK)
        for h in range(RET_HEADS):
            qh = q_ref[rows, h * RET_DK:(h + 1) * RET_DK]
            kh = k_ref[rows, h * RET_DK:(h + 1) * RET_DK]
            vh = v_ref[rows, h * RET_DV:(h + 1) * RET_DV]
            scores = lax.dot_general(qh.astype(BF16), kh.astype(BF16), (((1,), (1,)), ((), ())),
                                     preferred_element_type=F32) * intra_ref[h]
            inner = jnp.dot(scores.astype(BF16), vh, preferred_element_type=F32)
            st = state_ref[h]
            cross = jnp.dot((qh * qd_ref[h]).astype(BF16), st.astype(BF16), preferred_element_type=F32)
            kv = lax.dot_general((kh * kd_ref[h]).astype(BF16), vh, (((0,), (0,)), ((), ())),
                                 preferred_element_type=F32)
            state_ref[h] = st * cd_ref[h] + kv
            ret_ref[rows, h * RET_DV:(h + 1) * RET_DV] = _rms(inner + cross)
        tick()

    y_ret = jnp.dot((_silu(proj(3)) * (ret_ref[...] * retg_ref[...])).astype(BF16), wbr_ref[...],
                    preferred_element_type=F32)
    tick()

    z = proj(5) * proj(4)
    row = lax.broadcasted_iota(jnp.int32, z.shape, 0)
    prev1 = carry_ref[7:8, :]
    prev2 = carry_ref[6:7, :]
    z1 = jnp.where(row == 0, prev1, pltpu.roll(z, 1, 0))
    z2 = jnp.where(row == 0, prev2, jnp.where(row == 1, prev1, pltpu.roll(z, 2, 0)))
    conv = convw_ref[0:1, :] * z2 + convw_ref[1:2, :] * z1 + convw_ref[2:3, :] * z
    carry_ref[...] = z[tm - 8:tm, :]
    tick()
    y_conv = jnp.dot((proj(6) * conv).astype(BF16), wbc_ref[...], preferred_element_type=F32)
    tick()

    mix = _sigmoid(proj(7)) * y_ret + _sigmoid(proj(8)) * y_conv
    tick()
    x1 = x + g1 * jnp.dot(mix.astype(BF16), wout_ref[...], preferred_element_type=F32)
    tick()

    h2 = (_rms(x1) * gffn_ref[...]) * (1.0 + sc2) + sh2
    hi_f32 = lax.bitcast_convert_type(lax.bitcast_convert_type(h2, jnp.uint32) & jnp.uint32(0xFFFF0000), F32)
    h2_hi = hi_f32.astype(BF16)
    h2_lo = (h2 - hi_f32).astype(BF16)

    def nt_dot(a, b):
        return lax.dot_general(a, b, (((1,), (1,)), ((), ())), preferred_element_type=F32)

    lg_ref[...] = nt_dot(wrh_ref[...], h2_hi) + (nt_dot(wrh_ref[...], h2_lo) + nt_dot(wrl_ref[...], h2_hi))
    gu = jnp.dot(h2.astype(BF16), wshgu_ref[...], preferred_element_type=F32)
    f = gu.shape[1] // 2
    shared = jnp.dot((_silu(gu[:, :f]) * gu[:, f:]).astype(BF16), wshd_ref[...], preferred_element_type=F32)
    x2_ref[0] = x1 + g2 * shared
    for j in range(h2p_ref.shape[0]):
        h2p_ref[j] = _pack_bf16_pairs(h2[:, 2 * LANES * j:2 * LANES * (j + 1)])
    for _ in background:
        pass


def _mixer_kernel(*refs, tiles_per_seq):
    mix_inputs, (bias_ref, tri_ref) = refs[:21], refs[21:23]
    x2_ref, h2p_ref, eidx_ref, w_ref, rank_ref, cnt_ref = refs[23:29]
    state_ref, carry_ref, q_ref, k_ref, v_ref, ret_ref, hb_ref, lg_ref, rcarry_ref = refs[29:]
    s = pl.program_id(0)
    n_tiles = pl.num_programs(0) - 1

    @pl.when(s == 0)
    def _():
        lg_ref[...] = jnp.zeros_like(lg_ref)
        rcarry_ref[...] = jnp.zeros_like(rcarry_ref)

    @pl.when(s % tiles_per_seq == 0)
    def _():
        state_ref[...] = jnp.zeros_like(state_ref)
        carry_ref[...] = jnp.zeros_like(carry_ref)

    def route_previous_tile():
        return _route_phases(lg_ref.at[(s + 1) % 2], bias_ref, tri_ref, eidx_ref, w_ref, rank_ref, cnt_ref,
                             rcarry_ref, (s > 0).astype(jnp.int32))

    @pl.when(s == n_tiles)
    def _():
        for _ in route_previous_tile():
            pass

    @pl.when(s < n_tiles)
    def _():
        _mix_tile(*mix_inputs, x2_ref, h2p_ref, lg_ref.at[s % 2], state_ref, carry_ref, q_ref, k_ref, v_ref,
                  ret_ref, hb_ref, background=route_previous_tile())


def _mixer(x, posf, mod, gmix, win, retg, convw, wbr, wbc, wout, gffn, wr, wshgu, wshd, bias):
    b, s, d = x.shape
    tm = MIXER_TOKENS
    nt = s // tm
    n_tiles = b * nt
    intra, qd, kd, cd, invf, sign = _retention_constants()
    q_w, v_w = RET_HEADS * RET_DK, RET_HEADS * RET_DV
    hi_f32 = lax.bitcast_convert_type(lax.bitcast_convert_type(wr, jnp.uint32) & jnp.uint32(0xFFFF0000), F32)
    wr_hi = hi_f32.astype(BF16)
    wr_lo = (wr - hi_f32).astype(BF16)
    tri = jnp.asarray(np.triu(np.ones((tm, tm), np.float32), 1), BF16)
    weights = (gmix, win, invf, sign, intra, qd, kd, cd, retg, convw, wbr, wbc, wout, gffn, wr_hi, wr_lo,
               wshgu, wshd, bias, tri)

    def tile(i):
        return jnp.minimum(i, n_tiles - 1)

    def routed(i):
        return (0, jnp.maximum(i - 1, 0))

    def routed_chunks(i):
        return (jnp.maximum(i - 1, 0), 0, 0)

    return pl.pallas_call(
        functools.partial(_mixer_kernel, tiles_per_seq=nt),
        grid=(n_tiles + 1,),
        in_specs=[pl.BlockSpec((1, tm, d), lambda i: (tile(i) // nt, tile(i) % nt, 0)),
                  pl.BlockSpec((1, tm, 1), lambda i: (tile(i) // nt, tile(i) % nt, 0)),
                  pl.BlockSpec((1, 6, d), lambda i: (tile(i) // nt, 0, 0))]
                 + [_resident(w.shape) for w in weights],
        out_specs=[pl.BlockSpec((1, tm, d), lambda i: (tile(i) // nt, tile(i) % nt, 0)),
                   pl.BlockSpec((d // (2 * LANES), tm, LANES), lambda i: (0, tile(i), 0)),
                   pl.BlockSpec((tm // LANES, TOP_K, LANES), routed_chunks),
                   pl.BlockSpec((TOP_K, tm), routed),
                   pl.BlockSpec((tm // LANES, TOP_K, LANES), routed_chunks),
                   pl.BlockSpec((N_EXPERTS, 1), lambda i: (0, 0))],
        out_shape=[jax.ShapeDtypeStruct((b, s, d), F32),
                   jax.ShapeDtypeStruct((d // (2 * LANES), b * s, LANES), jnp.uint32),
                   jax.ShapeDtypeStruct((b * s // LANES, TOP_K, LANES), jnp.int32),
                   jax.ShapeDtypeStruct((TOP_K, b * s), F32),
                   jax.ShapeDtypeStruct((b * s // LANES, TOP_K, LANES), jnp.int32),
                   jax.ShapeDtypeStruct((N_EXPERTS, 1), F32)],
        scratch_shapes=[pltpu.VMEM((RET_HEADS, RET_DK, RET_DV), F32),
                        pltpu.VMEM((8, d), F32),
                        pltpu.VMEM((tm, q_w), F32),
                        pltpu.VMEM((tm, q_w), F32),
                        pltpu.VMEM((tm, v_w), BF16),
                        pltpu.VMEM((tm, v_w), F32),
                        pltpu.VMEM((tm, d), BF16),
                        pltpu.VMEM((2, N_EXPERTS, tm), F32),
                        pltpu.VMEM((N_EXPERTS, 1), F32)],
        compiler_params=pltpu.CompilerParams(dimension_semantics=("arbitrary",),
                                             vmem_limit_bytes=VMEM_LIMIT_BYTES),
        name="mixer",
    )(x, posf, mod, *weights)


def _first_argmax(v, idx, n):
    m = jnp.max(v, axis=0, keepdims=True)
    return m, jnp.min(jnp.where(v == m, idx, n), axis=0, keepdims=True)


def _route_phases(logits_ref, bias_ref, tri_ref, eidx_ref, w_ref, rank_ref, cnt_ref, carry_ref, count_it):
    tn = logits_ref.shape[1]
    neg = F32(-jnp.inf)
    score = _sigmoid(logits_ref[...])
    choice = score + bias_ref[...]

    grow = lax.broadcasted_iota(jnp.int32, (GROUP_SIZE, tn), 0)
    gscores = []
    for g in range(N_GROUPS):
        cg = choice[g * GROUP_SIZE:(g + 1) * GROUP_SIZE]
        m1, i1 = _first_argmax(cg, grow, GROUP_SIZE)
        m2 = jnp.max(jnp.where(grow == i1, neg, cg), axis=0, keepdims=True)
        gscores.append(m1 + m2)
    cur = jnp.concatenate(gscores, axis=0)
    yield
    gidx = lax.broadcasted_iota(jnp.int32, (N_GROUPS, tn), 0)
    keep = jnp.zeros((N_GROUPS, tn), F32)
    for _ in range(TOPK_GROUPS):
        _, ig = _first_argmax(cur, gidx, N_GROUPS)
        hit = gidx == ig
        keep = jnp.where(hit, 1.0, keep)
        cur = jnp.where(hit, neg, cur)
    cur = jnp.concatenate(
        [jnp.where(keep[g:g + 1] > 0.0, choice[g * GROUP_SIZE:(g + 1) * GROUP_SIZE], neg)
         for g in range(N_GROUPS)], axis=0)
    yield

    erow = lax.broadcasted_iota(jnp.int32, (N_EXPERTS, tn), 0)
    eidx, wts = [], []
    member = jnp.zeros((N_EXPERTS, tn), F32)
    for _ in range(TOP_K):
        _, ie = _first_argmax(cur, erow, N_EXPERTS)
        hit = erow == ie
        eidx.append(ie)
        wts.append(jnp.sum(jnp.where(hit, score, 0.0), axis=0, keepdims=True))
        member = member + hit.astype(F32)
        cur = jnp.where(hit, neg, cur)
        yield
    wsum = wts[0]
    for k in range(1, TOP_K):
        wsum = wsum + wts[k]

    before = jnp.dot(member.astype(BF16), tri_ref[...], preferred_element_type=F32) + carry_ref[...]
    ranks = []
    for k in range(TOP_K):
        ranks.append(jnp.sum(jnp.where(erow == eidx[k], before, 0.0), axis=0, keepdims=True))
        if k % 2 == 1:
            yield
    carry_ref[...] = carry_ref[...] + count_it.astype(F32) * jnp.sum(member, axis=1, keepdims=True)

    eidx_rows = jnp.concatenate(eidx, axis=0)
    rank_rows = jnp.concatenate(ranks, axis=0).astype(jnp.int32)
    for c in range(tn // LANES):
        eidx_ref[c] = eidx_rows[:, c * LANES:(c + 1) * LANES]
        rank_ref[c] = rank_rows[:, c * LANES:(c + 1) * LANES]
    w_ref[...] = jnp.concatenate([w / wsum * ROUTED_SCALE for w in wts], axis=0)
    cnt_ref[...] = carry_ref[...]


def _sc_mesh_and_workers():
    mesh = plsc.VectorSubcoreMesh(core_axis_name="c", subcore_axis_name="s")
    return mesh, mesh.num_cores, mesh.num_cores * mesh.num_subcores


def _worker_id(num_cores):
    return lax.axis_index("s") * num_cores + lax.axis_index("c")


def _dispatch(h2p, eidx3, rank3, pstart, n_rows):
    planes, t, lanes = h2p.shape
    n_chunks = eidx3.shape[0]
    mesh, num_cores, workers = _sc_mesh_and_workers()
    chunks_per_worker = n_chunks // workers
    vec = plsc.get_sparse_core_info().num_lanes
    assert chunks_per_worker * workers == n_chunks and n_chunks * lanes == t and lanes % vec == 0

    def body(h_hbm, e_hbm, r_hbm, p_hbm, xs_hbm, d_hbm, e_v, r_v, p_v, idx_v, rows_v, load_sem, scatter_sem):
        wid = _worker_id(num_cores)
        mine = pl.ds(wid * chunks_per_worker, chunks_per_worker)
        pltpu.sync_copy(p_hbm, p_v)
        pltpu.sync_copy(e_hbm.at[mine], e_v)
        pltpu.sync_copy(r_hbm.at[mine], r_v)

        @pl.loop(0, chunks_per_worker)
        def _(c):
            for k in range(TOP_K):
                for g in range(lanes // vec):
                    part = pl.ds(g * vec, vec)
                    idx_v[c, k, part] = plsc.load_gather(p_v, [e_v[c, k, part]]) + r_v[c, k, part]

        pltpu.sync_copy(idx_v, d_hbm.at[mine])

        @pl.loop(0, chunks_per_worker)
        def _(c):
            tok = pl.ds((wid * chunks_per_worker + c) * lanes, lanes)
            loads = [None] * planes
            scatters = [None] * planes
            loads[0] = pltpu.async_copy(h_hbm.at[0, tok], rows_v.at[0], load_sem.at[0])
            for j in range(planes):
                loads[j].wait()
                if j + 1 < planes:
                    if j >= 1:
                        for cp in scatters[j - 1]:
                            cp.wait()
                    loads[j + 1] = pltpu.async_copy(h_hbm.at[j + 1, tok], rows_v.at[(j + 1) % 2],
                                                    load_sem.at[(j + 1) % 2])
                scatters[j] = [pltpu.async_copy(rows_v.at[j % 2], xs_hbm.at[j].at[idx_v.at[c, k]],
                                                scatter_sem.at[j % 2]) for k in range(TOP_K)]
            for j in range(max(planes - 2, 0), planes):
                for cp in scatters[j]:
                    cp.wait()

    return pl.kernel(
        body,
        out_type=(jax.ShapeDtypeStruct((planes, n_rows, lanes), h2p.dtype),
                  jax.ShapeDtypeStruct(eidx3.shape, jnp.int32)),
        mesh=mesh,
        scratch_types=[pltpu.VMEM((chunks_per_worker, TOP_K, lanes), jnp.int32),
                       pltpu.VMEM((chunks_per_worker, TOP_K, lanes), jnp.int32),
                       pltpu.VMEM((N_EXPERTS,), jnp.int32),
                       pltpu.VMEM((chunks_per_worker, TOP_K, lanes), jnp.int32),
                       pltpu.VMEM((2, lanes, lanes), h2p.dtype),
                       pltpu.SemaphoreType.DMA((2,)),
                       pltpu.SemaphoreType.DMA((2,))],
        compiler_params=pltpu.CompilerParams(needs_layout_passes=False),
        name="dispatch",
    )(h2p, eidx3, rank3, pstart)


def _gather(ybuf, dest3, t):
    planes, _, lanes = ybuf.shape
    n_chunks = dest3.shape[0]
    mesh, num_cores, workers = _sc_mesh_and_workers()
    chunks_per_worker = n_chunks // workers
    nbuf = GATHER_BUFFERS
    lag = nbuf // 2
    assert chunks_per_worker * workers == n_chunks and n_chunks * lanes == t

    def body(y_hbm, d_hbm, yg_hbm, idx_v, rows_v, gather_sem, store_sem):
        wid = _worker_id(num_cores)

        @pl.loop(0, chunks_per_worker)
        def _(c):
            chunk = wid * chunks_per_worker + c
            tok = pl.ds(chunk * lanes, lanes)
            pltpu.sync_copy(d_hbm.at[chunk], idx_v)

            @pl.loop(0, TOP_K)
            def _(k):
                gathers = [None] * planes
                stores = [None] * planes

                def store(j):
                    gathers[j].wait()
                    stores[j] = pltpu.async_copy(rows_v.at[j % nbuf], yg_hbm.at[k, j, tok], store_sem.at[j % nbuf])

                for j in range(planes):
                    if j >= nbuf:
                        stores[j - nbuf].wait()
                    gathers[j] = pltpu.async_copy(y_hbm.at[j].at[idx_v.at[k]], rows_v.at[j % nbuf],
                                                  gather_sem.at[j % nbuf])
                    if j >= lag:
                        store(j - lag)
                for j in range(max(planes - lag, 0), planes):
                    store(j)
                for j in range(max(planes - nbuf, 0), planes):
                    stores[j].wait()

    return pl.kernel(
        body,
        out_type=jax.ShapeDtypeStruct((TOP_K, planes, t, lanes), ybuf.dtype),
        mesh=mesh,
        scratch_types=[pltpu.VMEM((TOP_K, lanes), jnp.int32),
                       pltpu.VMEM((nbuf, lanes, lanes), ybuf.dtype),
                       pltpu.SemaphoreType.DMA((nbuf,)),
                       pltpu.SemaphoreType.DMA((nbuf,))],
        name="gather",
    )(ybuf, dest3)


def _experts_kernel(be_ref, nv_ref, nu_ref, em_ref, x_ref, wg_hbm, wu_hbm, wd_hbm, o_ref,
                    wgs_ref, wus_ref, wds_ref, wgb_ref, wub_ref, wdb_ref, sems):
    planes, step_rows, lanes = x_ref.shape
    bm = EXPERT_ROWS
    blocks_per_step = step_rows // bm
    step = pl.program_id(0)
    last_step = (nu_ref[0] - 1) // blocks_per_step

    def weight_copies(expert):
        slot = em_ref[0, expert]
        return [pltpu.make_async_copy(src.at[expert], dst.at[slot], sems.at[slot, n])
                for n, (src, dst) in enumerate(((wg_hbm, wgs_ref), (wu_hbm, wus_ref), (wd_hbm, wds_ref)))]

    def start_if_any(expert):
        @pl.when(expert < N_EXPERTS)
        def _():
            for cp in weight_copies(expert):
                cp.start()

    def swiglu_rows(row0, m, n_valid):
        rows = pl.ds(row0, m)
        valid = lax.broadcasted_iota(jnp.int32, (m, lanes), 0) < n_valid
        g = None
        u = None
        for j in range(planes):
            xj = _unpack_pairs_f32(jnp.where(valid, x_ref[j, rows, :], jnp.uint32(0))).astype(BF16)
            wrows = pl.ds(2 * lanes * j, 2 * lanes)
            gj = jnp.dot(xj, wgb_ref[wrows, :], preferred_element_type=F32)
            uj = jnp.dot(xj, wub_ref[wrows, :], preferred_element_type=F32)
            g = gj if g is None else g + gj
            u = uj if u is None else u + uj
        out = jnp.dot((_silu(g) * u).astype(BF16), wdb_ref[...], preferred_element_type=F32)
        for j in range(planes):
            o_ref[j, rows, :] = _pack_bf16_pairs(out[:, 2 * lanes * j:2 * lanes * (j + 1)])

    def block_step(sb):
        i = step * blocks_per_step + sb
        row0 = pl.multiple_of(sb * bm, bm)
        rows = pl.ds(row0, bm)
        active = i < nu_ref[0]
        e = be_ref[i]
        new_expert = jnp.logical_or(i == 0, e != be_ref[jnp.maximum(i - 1, 0)])
        run = active.astype(jnp.int32)
        same = active
        for ahead in range(1, EXPERT_MAX_RUN):
            j = jnp.minimum(i + ahead, be_ref.shape[0] - 1)
            same = jnp.logical_and(jnp.logical_and(same, sb + ahead < blocks_per_step),
                                   jnp.logical_and(i + ahead < nu_ref[0], be_ref[j] == e))
            run = run + same.astype(jnp.int32)

        @pl.when(jnp.logical_and(active, new_expert))
        def _():
            @pl.when(i == 0)
            def _():
                for cp in weight_copies(e):
                    cp.start()
                for ahead in range(1, EXPERT_STAGES):
                    start_if_any(em_ref[ahead, e])

            for cp in weight_copies(e):
                cp.wait()
            slot = em_ref[0, e]
            wgb_ref[...] = wgs_ref[slot].astype(BF16)
            wub_ref[...] = wus_ref[slot].astype(BF16)
            wdb_ref[...] = wds_ref[slot].astype(BF16)
            start_if_any(em_ref[EXPERT_STAGES, e])

        for m in range(1, EXPERT_MAX_RUN + 1):
            @pl.when(run == m)
            def _():
                last = jnp.minimum(i + (m - 1), be_ref.shape[0] - 1)
                swiglu_rows(row0, m * bm, (m - 1) * bm + nv_ref[last])

        @pl.when(jnp.logical_and(jnp.logical_not(active), step == last_step))
        def _():
            for j in range(planes):
                o_ref[j, rows, :] = jnp.zeros((bm, lanes), o_ref.dtype)

        return sb + jnp.maximum(run, 1)

    lax.while_loop(lambda sb: sb < blocks_per_step, block_step, jnp.int32(0))


def _experts(block_e, n_valid, n_used, expert_meta, xs, wg, wu, wd):
    planes, p, lanes = xs.shape
    e, d, f = wg.shape
    step_rows = EXPERT_ROWS * EXPERT_BLOCKS_PER_STEP
    assert p % step_rows == 0

    def row_map(i, be, nv, nu, nxt):
        return (0, jnp.minimum(i, (nu[0] - 1) // EXPERT_BLOCKS_PER_STEP), 0)

    return pl.pallas_call(
        _experts_kernel,
        grid_spec=pltpu.PrefetchScalarGridSpec(
            num_scalar_prefetch=4,
            grid=(p // step_rows,),
            in_specs=[pl.BlockSpec((planes, step_rows, lanes), row_map),
                      pl.BlockSpec(memory_space=pl.ANY),
                      pl.BlockSpec(memory_space=pl.ANY),
                      pl.BlockSpec(memory_space=pl.ANY)],
            out_specs=pl.BlockSpec((planes, step_rows, lanes), row_map),
            scratch_shapes=[pltpu.VMEM((EXPERT_STAGES, d, f), F32), pltpu.VMEM((EXPERT_STAGES, d, f), F32),
                            pltpu.VMEM((EXPERT_STAGES, f, d), F32),
                            pltpu.VMEM((d, f), BF16), pltpu.VMEM((d, f), BF16), pltpu.VMEM((f, d), BF16),
                            pltpu.SemaphoreType.DMA((EXPERT_STAGES, 3))]),
        out_shape=jax.ShapeDtypeStruct((planes, p, lanes), jnp.uint32),
        compiler_params=pltpu.CompilerParams(dimension_semantics=("arbitrary",),
                                             vmem_limit_bytes=VMEM_LIMIT_BYTES),
        name="experts",
    )(block_e, n_valid, n_used, expert_meta, xs, wg, wu, wd)


def _combine_kernel(w_ref, x2_ref, g2_ref, gfin_ref, yg_ref, *maybe_prev_and_out):
    o_ref = maybe_prev_and_out[-1]
    tc = x2_ref.shape[1]
    eye = (lax.broadcasted_iota(jnp.int32, (tc, tc), 0) == lax.broadcasted_iota(jnp.int32, (tc, tc), 1)).astype(F32)
    w = lax.dot_general(eye, w_ref[...], (((1,), (1,)), ((), ())), precision=HIGHEST, preferred_element_type=F32)
    planes = yg_ref.shape[1]
    routed = None
    for k in range(TOP_K):
        rows = jnp.concatenate([_unpack_pairs_f32(yg_ref[k, j]) for j in range(planes)], axis=1) * w[:, k:k + 1]
        routed = rows if routed is None else routed + rows
    o_ref[0] = _rms(x2_ref[0] + g2_ref[0] * routed) * gfin_ref[...]


def _combine(wtok, x2, g2, gfin, yg, seg, prev_out):
    b, s, d = x2.shape
    _, planes, seg_tokens, lanes = yg.shape
    tc = COMBINE_TOKENS
    nt = seg_tokens // tc
    per_seq = s // seg_tokens
    assert nt * tc == seg_tokens and per_seq * seg_tokens == s
    bi, t0 = seg // per_seq, (seg % per_seq) * nt
    in_specs = [pl.BlockSpec((TOP_K, tc), lambda j: (0, seg * nt + j)),
                pl.BlockSpec((1, tc, d), lambda j: (bi, t0 + j, 0)),
                pl.BlockSpec((1, 1, d), lambda j: (bi, 0, 0)),
                pl.BlockSpec((1, d), lambda j: (0, 0)),
                pl.BlockSpec((TOP_K, planes, tc, lanes), lambda j: (0, 0, j, 0))]
    args = [wtok, x2, g2, gfin, yg]
    aliases = {}
    if prev_out is not None:
        in_specs.append(pl.BlockSpec(memory_space=pl.ANY))
        args.append(prev_out)
        aliases = {len(args) - 1: 0}
    return pl.pallas_call(
        _combine_kernel,
        grid=(nt,),
        in_specs=in_specs,
        out_specs=pl.BlockSpec((1, tc, d), lambda j: (bi, t0 + j, 0)),
        out_shape=jax.ShapeDtypeStruct((b, s, d), F32),
        input_output_aliases=aliases,
        compiler_params=pltpu.CompilerParams(dimension_semantics=("arbitrary",),
                                             vmem_limit_bytes=VMEM_LIMIT_BYTES),
        name="combine",
    )(*args)


def kernel(x, c, positions, w_ada, b_ada, norm_mix_g, w_in, ret_norm_g, conv_w, w_br_ret, w_br_conv, w_out,
           norm_ffn_g, w_router, router_bias, w_exp_gate, w_exp_up, w_exp_down, w_sh_gate, w_sh_up, w_sh_down,
           norm_final_g):
    b, s, d = x.shape
    t = b * s
    depth = w_in.shape[0]
    assert depth == 1, "the combine kernel applies the final norm, so exactly one layer is supported"
    posf = positions.astype(F32)[:, :, None]
    c8 = jnp.zeros((8, d), F32).at[:b].set(c)
    bm = EXPERT_ROWS
    n_blocks = t * TOP_K // bm + N_EXPERTS
    n_rows = n_blocks * bm

    for l in range(depth):
        mod = _ada(c8, w_ada[l], b_ada[l][None, :])[:b].reshape(b, 6, d)
        x2, h2p, eidx, wts, rank, counts = _mixer(
            x, posf, mod, norm_mix_g[l][None, :], w_in[l].astype(BF16), ret_norm_g[l][None, :], conv_w[l],
            w_br_ret[l].astype(BF16), w_br_conv[l].astype(BF16), w_out[l].astype(BF16), norm_ffn_g[l][None, :],
            w_router[l].T, jnp.concatenate([w_sh_gate[l], w_sh_up[l]], axis=1).astype(BF16),
            w_sh_down[l].astype(BF16), router_bias[l][:, None])

        cnt = counts[:, 0].astype(jnp.int32)
        pcnt = (cnt + bm - 1) // bm * bm
        pend = jnp.cumsum(pcnt)
        pstart = pend - pcnt
        n_used = jnp.maximum(pend[-1] // bm, 1).astype(jnp.int32)[None]
        block_start = jnp.arange(n_blocks, dtype=jnp.int32) * bm
        block_e = jnp.minimum(jnp.sum((pend[None, :] <= block_start[:, None]).astype(jnp.int32), axis=1),
                              N_EXPERTS - 1)
        eid = jnp.arange(N_EXPERTS, dtype=jnp.int32)
        seg_end = jnp.sum(jnp.where(block_e[:, None] == eid[None, :], (pstart + cnt)[None, :], 0), axis=1)
        n_valid = jnp.clip(seg_end - block_start, 0, bm).astype(jnp.int32)
        later_used = jnp.logical_and(eid[None, :] > eid[:, None], cnt[None, :] > 0)
        next_expert = jnp.min(jnp.where(later_used, eid[None, :], N_EXPERTS), axis=1)
        stage_slot = (jnp.cumsum((cnt > 0).astype(jnp.int32)) - 1) % EXPERT_STAGES
        ahead = [stage_slot, next_expert]
        for _ in range(EXPERT_STAGES - 1):
            hop = jnp.sum(jnp.where(ahead[-1][:, None] == eid[None, :], next_expert[None, :], 0), axis=1)
            ahead.append(jnp.where(ahead[-1] == N_EXPERTS, N_EXPERTS, hop))
        expert_meta = jnp.stack(ahead).astype(jnp.int32)

        xs, dest3 = _dispatch(h2p, eidx, rank, pstart.astype(jnp.int32), n_rows)
        ybuf = _experts(block_e, n_valid, n_used, expert_meta, xs, w_exp_gate[l], w_exp_up[l], w_exp_down[l])
        chunks = t // LANES // COMBINE_SEGMENTS
        out = None
        for seg in range(COMBINE_SEGMENTS):
            yg = _gather(ybuf, dest3[seg * chunks:(seg + 1) * chunks], chunks * LANES)
            out = _combine(wts, x2, mod[:, 5:6, :], norm_final_g[None, :], yg, seg, out)
        x = out
    return x
```

```python
import functools

import numpy as np
import jax
import jax.numpy as jnp
from jax import lax
from jax.experimental import pallas as pl
from jax.experimental.pallas import tpu as pltpu
from jax.experimental.pallas import tpu_sc as plsc

RET_HEADS = 4
RET_DK = 128
RET_DV = 256
RET_CHUNK = 128
ROPE_THETA = 10000.0
N_EXPERTS = 256
TOP_K = 8
N_GROUPS = 8
TOPK_GROUPS = 4
GROUP_SIZE = N_EXPERTS // N_GROUPS
ROUTED_SCALE = 2.5
NORM_EPS = 1e-6

MIXER_TOKENS = 512
WIN_INPUT = 4
WIN_STAGE_COLUMNS = 512
EXPERT_ROWS = 128
EXPERT_MAX_RUN = 4
EXPERT_BLOCKS_PER_STEP = 16
EXPERT_STAGES = 5
COMBINE_TOKENS = 512
COMBINE_SEGMENTS = 4
LANES = 128
GATHER_BUFFERS = 4

VMEM_LIMIT_BYTES = 56 * 1024 * 1024

F32 = jnp.float32
BF16 = jnp.bfloat16
HIGHEST = lax.Precision.HIGHEST


def _sigmoid(v):
    return 0.5 * jnp.tanh(0.5 * v) + 0.5


def _silu(v):
    return v * _sigmoid(v)


def _rms(v):
    return v * lax.rsqrt(jnp.mean(v * v, axis=-1, keepdims=True) + NORM_EPS)


def _resident(shape):
    nd = len(shape)
    return pl.BlockSpec(shape, lambda *_: (0,) * nd, pipeline_mode=pl.Buffered(1))


def _ada_kernel(c_ref, w_ref, b_ref, o_ref):
    c = c_ref[...]
    o_ref[...] = jnp.dot(_silu(c), w_ref[...], precision=HIGHEST, preferred_element_type=F32) + b_ref[...]


def _ada(c8, w, b):
    d, n = w.shape
    tn = 1024
    return pl.pallas_call(
        _ada_kernel,
        grid=(n // tn,),
        in_specs=[pl.BlockSpec((8, d), lambda j: (0, 0)),
                  pl.BlockSpec((d, tn), lambda j: (0, j)),
                  pl.BlockSpec((1, tn), lambda j: (0, j))],
        out_specs=pl.BlockSpec((8, tn), lambda j: (0, j)),
        out_shape=jax.ShapeDtypeStruct((8, n), F32),
        name="ada",
    )(c8, w, b)


def _retention_constants():
    f32 = np.float32
    c = RET_CHUNK
    log_g = np.log1p(-(f32(2.0) ** (f32(-5.0) - np.arange(RET_HEADS, dtype=f32)))).astype(f32)
    idx = np.arange(c, dtype=f32)
    diff = idx[:, None] - idx[None, :]
    intra = np.where(diff >= 0, np.exp(log_g[:, None, None] * np.maximum(diff, f32(0.0))), f32(0.0))
    k_decay = np.exp(log_g[:, None] * (f32(c - 1) - idx))
    q_decay = np.exp(log_g[:, None] * (idx + f32(1.0)))
    chunk_decay = np.exp(log_g * f32(c))
    kd = np.broadcast_to(k_decay[:, :, None], (RET_HEADS, c, RET_DK))
    qd = np.broadcast_to(q_decay[:, :, None], (RET_HEADS, c, RET_DK))
    cd = np.broadcast_to(chunk_decay[:, None, None], (RET_HEADS, 1, RET_DV))
    inv_freq = f32(ROPE_THETA) ** (-np.arange(0, RET_DK, 2, dtype=f32) / f32(RET_DK))
    inv_freq = np.concatenate([inv_freq, inv_freq])[None, :]
    sign = np.concatenate([-np.ones((RET_DK // 2,), f32), np.ones((RET_DK // 2,), f32)])[None, :]
    return tuple(jnp.asarray(np.ascontiguousarray(v, dtype=f32)) for v in (intra, qd, kd, cd, inv_freq, sign))


def _pack_bf16_pairs(v):
    w = v.shape[1] // 2
    lo = lax.bitcast_convert_type(v[:, :w].astype(BF16).astype(F32), jnp.uint32)
    hi = lax.bitcast_convert_type(v[:, w:].astype(BF16).astype(F32), jnp.uint32)
    return (lo >> 16) | (hi & jnp.uint32(0xFFFF0000))


def _unpack_pairs_f32(word):
    lo = lax.bitcast_convert_type(word << 16, F32)
    hi = lax.bitcast_convert_type(word & jnp.uint32(0xFFFF0000), F32)
    return jnp.concatenate([lo, hi], axis=1)


def _mix_tile(x_ref, pos_ref, mod_ref, gmix_ref, win_ref, invf_ref, sign_ref, intra_ref, qd_ref, kd_ref, cd_ref,
              retg_ref, convw_ref, wbr_ref, wbc_ref, wout_ref, gffn_ref, wrh_ref, wrl_ref, wshgu_ref, wshd_ref,
              x2_ref, h2p_ref, lg_ref, state_ref, carry_ref, q_ref, k_ref, v_ref, ret_ref, hb_ref, *, background):
    def tick():
        next(background, None)

    tm, d = x_ref.shape[1], x_ref.shape[2]
    q_w = RET_HEADS * RET_DK
    v_w = RET_HEADS * RET_DV
    offs = np.cumsum([0, q_w, q_w, v_w, v_w, d, d, d, d, d])

    x = x_ref[0]
    mod = mod_ref[0]
    sh1, sc1, g1, sh2, sc2, g2 = [mod[i:i + 1] for i in range(6)]
    hb_ref[...] = ((_rms(x) * gmix_ref[...]) * (1.0 + sc1) + sh1).astype(BF16)

    def proj(i):
        return jnp.dot(hb_ref[...], win_ref[:, offs[i]:offs[i + 1]], preferred_element_type=F32)

    ang = pos_ref[0] * invf_ref[...]
    cosv = jnp.cos(ang)
    sinv = jnp.sin(ang) * sign_ref[...]

    def rope(t):
        return jnp.concatenate(
            [t[:, h * RET_DK:(h + 1) * RET_DK] * cosv
             + pltpu.roll(t[:, h * RET_DK:(h + 1) * RET_DK], RET_DK // 2, 1) * sinv
             for h in range(RET_HEADS)], axis=1)

    q_ref[...] = rope(proj(0)) * (RET_DK ** -0.5)
    tick()
    k_ref[...] = rope(proj(1))
    tick()
    v_ref[...] = proj(2).astype(BF16)
    tick()

    for c in range(tm // RET_CHUNK):
        rows = pl.ds(c * RET_CHUNK, RET_CHUNK)
        for h in range(RET_HEADS):
            qh = q_ref[rows, h * RET_DK:(h + 1) * RET_DK]
            kh = k_ref[rows, h * RET_DK:(h + 1) * RET_DK]
            vh = v_ref[rows, h * RET_DV:(h + 1) * RET_DV]
            scores = lax.dot_general(qh.astype(BF16), kh.astype(BF16), (((1,), (1,)), ((), ())),
                                     preferred_element_type=F32) * intra_ref[h]
            inner = jnp.dot(scores.astype(BF16), vh, preferred_element_type=F32)
            st = state_ref[h]
            cross = jnp.dot((qh * qd_ref[h]).astype(BF16), st.astype(BF16), preferred_element_type=F32)
            kv = lax.dot_general((kh * kd_ref[h]).astype(BF16), vh, (((0,), (0,)), ((), ())),
                                 preferred_element_type=F32)
            state_ref[h] = st * cd_ref[h] + kv
            ret_ref[rows, h * RET_DV:(h + 1) * RET_DV] = _rms(inner + cross)
        tick()

    y_ret = jnp.dot((_silu(proj(3)) * (ret_ref[...] * retg_ref[...])).astype(BF16), wbr_ref[...],
                    preferred_element_type=F32)
    tick()

    z = proj(5) * proj(4)
    row = lax.broadcasted_iota(jnp.int32, z.shape, 0)
    prev1 = carry_ref[7:8, :]
    prev2 = carry_ref[6:7, :]
    z1 = jnp.where(row == 0, prev1, pltpu.roll(z, 1, 0))
    z2 = jnp.where(row == 0, prev2, jnp.where(row == 1, prev1, pltpu.roll(z, 2, 0)))
    conv = convw_ref[0:1, :] * z2 + convw_ref[1:2, :] * z1 + convw_ref[2:3, :] * z
    carry_ref[...] = z[tm - 8:tm, :]
    tick()
    y_conv = jnp.dot((proj(6) * conv).astype(BF16), wbc_ref[...], preferred_element_type=F32)
    tick()

    mix = _sigmoid(proj(7)) * y_ret + _sigmoid(proj(8)) * y_conv
    tick()
    x1 = x + g1 * jnp.dot(mix.astype(BF16), wout_ref[...], preferred_element_type=F32)
    tick()

    h2 = (_rms(x1) * gffn_ref[...]) * (1.0 + sc2) + sh2
    hi_f32 = lax.bitcast_convert_type(lax.bitcast_convert_type(h2, jnp.uint32) & jnp.uint32(0xFFFF0000), F32)
    h2_hi = hi_f32.astype(BF16)
    h2_lo = (h2 - hi_f32).astype(BF16)

    def nt_dot(a, b):
        return lax.dot_general(a, b, (((1,), (1,)), ((), ())), preferred_element_type=F32)

    lg_ref[...] = nt_dot(wrh_ref[...], h2_hi) + (nt_dot(wrh_ref[...], h2_lo) + nt_dot(wrl_ref[...], h2_hi))
    gu = jnp.dot(h2.astype(BF16), wshgu_ref[...], preferred_element_type=F32)
    f = gu.shape[1] // 2
    shared = jnp.dot((_silu(gu[:, :f]) * gu[:, f:]).astype(BF16), wshd_ref[...], preferred_element_type=F32)
    x2_ref[0] = x1 + g2 * shared
    for j in range(h2p_ref.shape[0]):
        h2p_ref[j] = _pack_bf16_pairs(h2[:, 2 * LANES * j:2 * LANES * (j + 1)])
    for _ in background:
        pass


def _stage_as_bf16(w_hbm, w_bf16_ref, stage_ref, sems):
    chunk = stage_ref.shape[2]
    n = w_hbm.shape[1] // chunk

    def copy(i):
        return pltpu.make_async_copy(w_hbm.at[:, pl.ds(i * chunk, chunk)], stage_ref.at[i % 2], sems.at[i % 2])

    copy(0).start()
    for i in range(n):
        if i + 1 < n:
            copy(i + 1).start()
        copy(i).wait()
        w_bf16_ref[:, i * chunk:(i + 1) * chunk] = stage_ref[i % 2].astype(BF16)


def _mixer_kernel(*refs, tiles_per_seq):
    mix_inputs, (bias_ref, tri_ref) = list(refs[:21]), refs[21:23]
    x2_ref, h2p_ref, eidx_ref, w_ref, rank_ref, cnt_ref = refs[23:29]
    (state_ref, carry_ref, q_ref, k_ref, v_ref, ret_ref, hb_ref, lg_ref, rcarry_ref,
     win_bf16_ref, stage_ref, stage_sems) = refs[29:]
    win_hbm = mix_inputs[WIN_INPUT]
    mix_inputs[WIN_INPUT] = win_bf16_ref
    s = pl.program_id(0)
    n_tiles = pl.num_programs(0) - 1

    @pl.when(s == 0)
    def _():
        lg_ref[...] = jnp.zeros_like(lg_ref)
        rcarry_ref[...] = jnp.zeros_like(rcarry_ref)
        _stage_as_bf16(win_hbm, win_bf16_ref, stage_ref, stage_sems)

    @pl.when(s % tiles_per_seq == 0)
    def _():
        state_ref[...] = jnp.zeros_like(state_ref)
        carry_ref[...] = jnp.zeros_like(carry_ref)

    def route_previous_tile():
        return _route_phases(lg_ref.at[(s + 1) % 2], bias_ref, tri_ref, eidx_ref, w_ref, rank_ref, cnt_ref,
                             rcarry_ref, (s > 0).astype(jnp.int32))

    @pl.when(s == n_tiles)
    def _():
        for _ in route_previous_tile():
            pass

    @pl.when(s < n_tiles)
    def _():
        _mix_tile(*mix_inputs, x2_ref, h2p_ref, lg_ref.at[s % 2], state_ref, carry_ref, q_ref, k_ref, v_ref,
                  ret_ref, hb_ref, background=route_previous_tile())


def _mixer(x, posf, mod, gmix, win, retg, convw, wbr, wbc, wout, gffn, wr, wshgu, wshd, bias):
    b, s, d = x.shape
    tm = MIXER_TOKENS
    nt = s // tm
    n_tiles = b * nt
    intra, qd, kd, cd, invf, sign = _retention_constants()
    q_w, v_w = RET_HEADS * RET_DK, RET_HEADS * RET_DV
    hi_f32 = lax.bitcast_convert_type(lax.bitcast_convert_type(wr, jnp.uint32) & jnp.uint32(0xFFFF0000), F32)
    wr_hi = hi_f32.astype(BF16)
    wr_lo = (wr - hi_f32).astype(BF16)
    tri = jnp.asarray(np.triu(np.ones((tm, tm), np.float32), 1), BF16)
    weights = (gmix, win, invf, sign, intra, qd, kd, cd, retg, convw, wbr, wbc, wout, gffn, wr_hi, wr_lo,
               wshgu, wshd, bias, tri)

    def tile(i):
        return jnp.minimum(i, n_tiles - 1)

    def routed(i):
        return (0, jnp.maximum(i - 1, 0))

    def routed_chunks(i):
        return (jnp.maximum(i - 1, 0), 0, 0)

    return pl.pallas_call(
        functools.partial(_mixer_kernel, tiles_per_seq=nt),
        grid=(n_tiles + 1,),
        in_specs=[pl.BlockSpec((1, tm, d), lambda i: (tile(i) // nt, tile(i) % nt, 0)),
                  pl.BlockSpec((1, tm, 1), lambda i: (tile(i) // nt, tile(i) % nt, 0)),
                  pl.BlockSpec((1, 6, d), lambda i: (tile(i) // nt, 0, 0))]
                 + [pl.BlockSpec(memory_space=pl.ANY) if n + 3 == WIN_INPUT else _resident(w.shape)
                    for n, w in enumerate(weights)],
        out_specs=[pl.BlockSpec((1, tm, d), lambda i: (tile(i) // nt, tile(i) % nt, 0)),
                   pl.BlockSpec((d // (2 * LANES), tm, LANES), lambda i: (0, tile(i), 0)),
                   pl.BlockSpec((tm // LANES, TOP_K, LANES), routed_chunks),
                   pl.BlockSpec((TOP_K, tm), routed),
                   pl.BlockSpec((tm // LANES, TOP_K, LANES), routed_chunks),
                   pl.BlockSpec((N_EXPERTS, 1), lambda i: (0, 0))],
        out_shape=[jax.ShapeDtypeStruct((b, s, d), F32),
                   jax.ShapeDtypeStruct((d // (2 * LANES), b * s, LANES), jnp.uint32),
                   jax.ShapeDtypeStruct((b * s // LANES, TOP_K, LANES), jnp.int32),
                   jax.ShapeDtypeStruct((TOP_K, b * s), F32),
                   jax.ShapeDtypeStruct((b * s // LANES, TOP_K, LANES), jnp.int32),
                   jax.ShapeDtypeStruct((N_EXPERTS, 1), F32)],
        scratch_shapes=[pltpu.VMEM((RET_HEADS, RET_DK, RET_DV), F32),
                        pltpu.VMEM((8, d), F32),
                        pltpu.VMEM((tm, q_w), F32),
                        pltpu.VMEM((tm, q_w), F32),
                        pltpu.VMEM((tm, v_w), BF16),
                        pltpu.VMEM((tm, v_w), F32),
                        pltpu.VMEM((tm, d), BF16),
                        pltpu.VMEM((2, N_EXPERTS, tm), F32),
                        pltpu.VMEM((N_EXPERTS, 1), F32),
                        pltpu.VMEM(win.shape, BF16),
                        pltpu.VMEM((2, win.shape[0], WIN_STAGE_COLUMNS), F32),
                        pltpu.SemaphoreType.DMA((2,))],
        compiler_params=pltpu.CompilerParams(dimension_semantics=("arbitrary",),
                                             vmem_limit_bytes=VMEM_LIMIT_BYTES),
        name="mixer",
    )(x, posf, mod, *weights)


def _first_argmax(v, idx, n):
    m = jnp.max(v, axis=0, keepdims=True)
    return m, jnp.min(jnp.where(v == m, idx, n), axis=0, keepdims=True)


def _route_phases(logits_ref, bias_ref, tri_ref, eidx_ref, w_ref, rank_ref, cnt_ref, carry_ref, count_it):
    tn = logits_ref.shape[1]
    neg = F32(-jnp.inf)
    score = _sigmoid(logits_ref[...])
    choice = score + bias_ref[...]

    grow = lax.broadcasted_iota(jnp.int32, (GROUP_SIZE, tn), 0)
    gscores = []
    for g in range(N_GROUPS):
        cg = choice[g * GROUP_SIZE:(g + 1) * GROUP_SIZE]
        m1, i1 = _first_argmax(cg, grow, GROUP_SIZE)
        m2 = jnp.max(jnp.where(grow == i1, neg, cg), axis=0, keepdims=True)
        gscores.append(m1 + m2)
    cur = jnp.concatenate(gscores, axis=0)
    yield
    gidx = lax.broadcasted_iota(jnp.int32, (N_GROUPS, tn), 0)
    keep = jnp.zeros((N_GROUPS, tn), F32)
    for _ in range(TOPK_GROUPS):
        _, ig = _first_argmax(cur, gidx, N_GROUPS)
        hit = gidx == ig
        keep = jnp.where(hit, 1.0, keep)
        cur = jnp.where(hit, neg, cur)
    cur = jnp.concatenate(
        [jnp.where(keep[g:g + 1] > 0.0, choice[g * GROUP_SIZE:(g + 1) * GROUP_SIZE], neg)
         for g in range(N_GROUPS)], axis=0)
    yield

    erow = lax.broadcasted_iota(jnp.int32, (N_EXPERTS, tn), 0)
    eidx, wts = [], []
    member = jnp.zeros((N_EXPERTS, tn), F32)
    for _ in range(TOP_K):
        _, ie = _first_argmax(cur, erow, N_EXPERTS)
        hit = erow == ie
        eidx.append(ie)
        wts.append(jnp.sum(jnp.where(hit, score, 0.0), axis=0, keepdims=True))
        member = member + hit.astype(F32)
        cur = jnp.where(hit, neg, cur)
        yield
    wsum = wts[0]
    for k in range(1, TOP_K):
        wsum = wsum + wts[k]

    before = jnp.dot(member.astype(BF16), tri_ref[...], preferred_element_type=F32) + carry_ref[...]
    ranks = []
    for k in range(TOP_K):
        ranks.append(jnp.sum(jnp.where(erow == eidx[k], before, 0.0), axis=0, keepdims=True))
        if k % 2 == 1:
            yield
    carry_ref[...] = carry_ref[...] + count_it.astype(F32) * jnp.sum(member, axis=1, keepdims=True)

    eidx_rows = jnp.concatenate(eidx, axis=0)
    rank_rows = jnp.concatenate(ranks, axis=0).astype(jnp.int32)
    for c in range(tn // LANES):
        eidx_ref[c] = eidx_rows[:, c * LANES:(c + 1) * LANES]
        rank_ref[c] = rank_rows[:, c * LANES:(c + 1) * LANES]
    w_ref[...] = jnp.concatenate([w / wsum * ROUTED_SCALE for w in wts], axis=0)
    cnt_ref[...] = carry_ref[...]


def _sc_mesh_and_workers():
    mesh = plsc.VectorSubcoreMesh(core_axis_name="c", subcore_axis_name="s")
    return mesh, mesh.num_cores, mesh.num_cores * mesh.num_subcores


def _worker_id(num_cores):
    return lax.axis_index("s") * num_cores + lax.axis_index("c")


def _dispatch(h2p, eidx3, rank3, pstart, n_rows):
    planes, t, lanes = h2p.shape
    n_chunks = eidx3.shape[0]
    mesh, num_cores, workers = _sc_mesh_and_workers()
    chunks_per_worker = n_chunks // workers
    vec = plsc.get_sparse_core_info().num_lanes
    assert chunks_per_worker * workers == n_chunks and n_chunks * lanes == t and lanes % vec == 0

    def body(h_hbm, e_hbm, r_hbm, p_hbm, xs_hbm, d_hbm, e_v, r_v, p_v, idx_v, rows_v, load_sem, scatter_sem):
        wid = _worker_id(num_cores)
        mine = pl.ds(wid * chunks_per_worker, chunks_per_worker)
        pltpu.sync_copy(p_hbm, p_v)
        pltpu.sync_copy(e_hbm.at[mine], e_v)
        pltpu.sync_copy(r_hbm.at[mine], r_v)

        @pl.loop(0, chunks_per_worker)
        def _(c):
            for k in range(TOP_K):
                for g in range(lanes // vec):
                    part = pl.ds(g * vec, vec)
                    idx_v[c, k, part] = plsc.load_gather(p_v, [e_v[c, k, part]]) + r_v[c, k, part]

        pltpu.sync_copy(idx_v, d_hbm.at[mine])

        @pl.loop(0, chunks_per_worker)
        def _(c):
            tok = pl.ds((wid * chunks_per_worker + c) * lanes, lanes)
            loads = [None] * planes
            scatters = [None] * planes
            loads[0] = pltpu.async_copy(h_hbm.at[0, tok], rows_v.at[0], load_sem.at[0])
            for j in range(planes):
                loads[j].wait()
                if j + 1 < planes:
                    if j >= 1:
                        for cp in scatters[j - 1]:
                            cp.wait()
                    loads[j + 1] = pltpu.async_copy(h_hbm.at[j + 1, tok], rows_v.at[(j + 1) % 2],
                                                    load_sem.at[(j + 1) % 2])
                scatters[j] = [pltpu.async_copy(rows_v.at[j % 2], xs_hbm.at[j].at[idx_v.at[c, k]],
                                                scatter_sem.at[j % 2]) for k in range(TOP_K)]
            for j in range(max(planes - 2, 0), planes):
                for cp in scatters[j]:
                    cp.wait()

    return pl.kernel(
        body,
        out_type=(jax.ShapeDtypeStruct((planes, n_rows, lanes), h2p.dtype),
                  jax.ShapeDtypeStruct(eidx3.shape, jnp.int32)),
        mesh=mesh,
        scratch_types=[pltpu.VMEM((chunks_per_worker, TOP_K, lanes), jnp.int32),
                       pltpu.VMEM((chunks_per_worker, TOP_K, lanes), jnp.int32),
                       pltpu.VMEM((N_EXPERTS,), jnp.int32),
                       pltpu.VMEM((chunks_per_worker, TOP_K, lanes), jnp.int32),
                       pltpu.VMEM((2, lanes, lanes), h2p.dtype),
                       pltpu.SemaphoreType.DMA((2,)),
                       pltpu.SemaphoreType.DMA((2,))],
        compiler_params=pltpu.CompilerParams(needs_layout_passes=False),
        name="dispatch",
    )(h2p, eidx3, rank3, pstart)


def _gather(ybuf, dest3, t):
    planes, _, lanes = ybuf.shape
    n_chunks = dest3.shape[0]
    mesh, num_cores, workers = _sc_mesh_and_workers()
    chunks_per_worker = n_chunks // workers
    nbuf = GATHER_BUFFERS
    lag = nbuf // 2
    assert chunks_per_worker * workers == n_chunks and n_chunks * lanes == t

    def body(y_hbm, d_hbm, yg_hbm, idx_v, rows_v, gather_sem, store_sem):
        wid = _worker_id(num_cores)

        @pl.loop(0, chunks_per_worker)
        def _(c):
            chunk = wid * chunks_per_worker + c
            tok = pl.ds(chunk * lanes, lanes)
            pltpu.sync_copy(d_hbm.at[chunk], idx_v)

            @pl.loop(0, TOP_K)
            def _(k):
                gathers = [None] * planes
                stores = [None] * planes

                def store(j):
                    gathers[j].wait()
                    stores[j] = pltpu.async_copy(rows_v.at[j % nbuf], yg_hbm.at[k, j, tok], store_sem.at[j % nbuf])

                for j in range(planes):
                    if j >= nbuf:
                        stores[j - nbuf].wait()
                    gathers[j] = pltpu.async_copy(y_hbm.at[j].at[idx_v.at[k]], rows_v.at[j % nbuf],
                                                  gather_sem.at[j % nbuf])
                    if j >= lag:
                        store(j - lag)
                for j in range(max(planes - lag, 0), planes):
                    store(j)
                for j in range(max(planes - nbuf, 0), planes):
                    stores[j].wait()

    return pl.kernel(
        body,
        out_type=jax.ShapeDtypeStruct((TOP_K, planes, t, lanes), ybuf.dtype),
        mesh=mesh,
        scratch_types=[pltpu.VMEM((TOP_K, lanes), jnp.int32),
                       pltpu.VMEM((nbuf, lanes, lanes), ybuf.dtype),
                       pltpu.SemaphoreType.DMA((nbuf,)),
                       pltpu.SemaphoreType.DMA((nbuf,))],
        name="gather",
    )(ybuf, dest3)


def _experts_kernel(be_ref, nv_ref, nu_ref, em_ref, x_ref, wg_hbm, wu_hbm, wd_hbm, o_ref,
                    wgs_ref, wus_ref, wds_ref, wgb_ref, wub_ref, wdb_ref, sems):
    planes, step_rows, lanes = x_ref.shape
    bm = EXPERT_ROWS
    blocks_per_step = step_rows // bm
    step = pl.program_id(0)
    last_step = (nu_ref[0] - 1) // blocks_per_step

    def weight_copies(expert):
        slot = em_ref[0, expert]
        return [pltpu.make_async_copy(src.at[expert], dst.at[slot], sems.at[slot, n])
                for n, (src, dst) in enumerate(((wg_hbm, wgs_ref), (wu_hbm, wus_ref), (wd_hbm, wds_ref)))]

    def start_if_any(expert):
        @pl.when(expert < N_EXPERTS)
        def _():
            for cp in weight_copies(expert):
                cp.start()

    def swiglu_rows(row0, m, n_valid):
        rows = pl.ds(row0, m)
        valid = lax.broadcasted_iota(jnp.int32, (m, lanes), 0) < n_valid
        g = None
        u = None
        for j in range(planes):
            xj = _unpack_pairs_f32(jnp.where(valid, x_ref[j, rows, :], jnp.uint32(0))).astype(BF16)
            wrows = pl.ds(2 * lanes * j, 2 * lanes)
            gj = jnp.dot(xj, wgb_ref[wrows, :], preferred_element_type=F32)
            uj = jnp.dot(xj, wub_ref[wrows, :], preferred_element_type=F32)
            g = gj if g is None else g + gj
            u = uj if u is None else u + uj
        out = jnp.dot((_silu(g) * u).astype(BF16), wdb_ref[...], preferred_element_type=F32)
        for j in range(planes):
            o_ref[j, rows, :] = _pack_bf16_pairs(out[:, 2 * lanes * j:2 * lanes * (j + 1)])

    def block_step(sb):
        i = step * blocks_per_step + sb
        row0 = pl.multiple_of(sb * bm, bm)
        rows = pl.ds(row0, bm)
        active = i < nu_ref[0]
        e = be_ref[i]
        new_expert = jnp.logical_or(i == 0, e != be_ref[jnp.maximum(i - 1, 0)])
        run = active.astype(jnp.int32)
        same = active
        for ahead in range(1, EXPERT_MAX_RUN):
            j = jnp.minimum(i + ahead, be_ref.shape[0] - 1)
            same = jnp.logical_and(jnp.logical_and(same, sb + ahead < blocks_per_step),
                                   jnp.logical_and(i + ahead < nu_ref[0], be_ref[j] == e))
            run = run + same.astype(jnp.int32)

        @pl.when(jnp.logical_and(active, new_expert))
        def _():
            @pl.when(i == 0)
            def _():
                for cp in weight_copies(e):
                    cp.start()
                for ahead in range(1, EXPERT_STAGES):
                    start_if_any(em_ref[ahead, e])

            for cp in weight_copies(e):
                cp.wait()
            slot = em_ref[0, e]
            wgb_ref[...] = wgs_ref[slot].astype(BF16)
            wub_ref[...] = wus_ref[slot].astype(BF16)
            wdb_ref[...] = wds_ref[slot].astype(BF16)
            start_if_any(em_ref[EXPERT_STAGES, e])

        for m in range(1, EXPERT_MAX_RUN + 1):
            @pl.when(run == m)
            def _():
                last = jnp.minimum(i + (m - 1), be_ref.shape[0] - 1)
                swiglu_rows(row0, m * bm, (m - 1) * bm + nv_ref[last])

        @pl.when(jnp.logical_and(jnp.logical_not(active), step == last_step))
        def _():
            for j in range(planes):
                o_ref[j, rows, :] = jnp.zeros((bm, lanes), o_ref.dtype)

        return sb + jnp.maximum(run, 1)

    lax.while_loop(lambda sb: sb < blocks_per_step, block_step, jnp.int32(0))


def _experts(block_e, n_valid, n_used, expert_meta, xs, wg, wu, wd):
    planes, p, lanes = xs.shape
    e, d, f = wg.shape
    step_rows = EXPERT_ROWS * EXPERT_BLOCKS_PER_STEP
    assert p % step_rows == 0

    def row_map(i, be, nv, nu, nxt):
        return (0, jnp.minimum(i, (nu[0] - 1) // EXPERT_BLOCKS_PER_STEP), 0)

    return pl.pallas_call(
        _experts_kernel,
        grid_spec=pltpu.PrefetchScalarGridSpec(
            num_scalar_prefetch=4,
            grid=(p // step_rows,),
            in_specs=[pl.BlockSpec((planes, step_rows, lanes), row_map),
                      pl.BlockSpec(memory_space=pl.ANY),
                      pl.BlockSpec(memory_space=pl.ANY),
                      pl.BlockSpec(memory_space=pl.ANY)],
            out_specs=pl.BlockSpec((planes, step_rows, lanes), row_map),
            scratch_shapes=[pltpu.VMEM((EXPERT_STAGES, d, f), F32), pltpu.VMEM((EXPERT_STAGES, d, f), F32),
                            pltpu.VMEM((EXPERT_STAGES, f, d), F32),
                            pltpu.VMEM((d, f), BF16), pltpu.VMEM((d, f), BF16), pltpu.VMEM((f, d), BF16),
                            pltpu.SemaphoreType.DMA((EXPERT_STAGES, 3))]),
        out_shape=jax.ShapeDtypeStruct((planes, p, lanes), jnp.uint32),
        compiler_params=pltpu.CompilerParams(dimension_semantics=("arbitrary",),
                                             vmem_limit_bytes=VMEM_LIMIT_BYTES),
        name="experts",
    )(block_e, n_valid, n_used, expert_meta, xs, wg, wu, wd)


def _combine_kernel(w_ref, x2_ref, g2_ref, gfin_ref, yg_ref, *maybe_prev_and_out):
    o_ref = maybe_prev_and_out[-1]
    tc = x2_ref.shape[1]
    eye = (lax.broadcasted_iota(jnp.int32, (tc, tc), 0) == lax.broadcasted_iota(jnp.int32, (tc, tc), 1)).astype(F32)
    w = lax.dot_general(eye, w_ref[...], (((1,), (1,)), ((), ())), precision=HIGHEST, preferred_element_type=F32)
    planes = yg_ref.shape[1]
    routed = None
    for k in range(TOP_K):
        rows = jnp.concatenate([_unpack_pairs_f32(yg_ref[k, j]) for j in range(planes)], axis=1) * w[:, k:k + 1]
        routed = rows if routed is None else routed + rows
    o_ref[0] = _rms(x2_ref[0] + g2_ref[0] * routed) * gfin_ref[...]


def _combine(wtok, x2, g2, gfin, yg, seg, prev_out):
    b, s, d = x2.shape
    _, planes, seg_tokens, lanes = yg.shape
    tc = COMBINE_TOKENS
    nt = seg_tokens // tc
    per_seq = s // seg_tokens
    assert nt * tc == seg_tokens and per_seq * seg_tokens == s
    bi, t0 = seg // per_seq, (seg % per_seq) * nt
    in_specs = [pl.BlockSpec((TOP_K, tc), lambda j: (0, seg * nt + j)),
                pl.BlockSpec((1, tc, d), lambda j: (bi, t0 + j, 0)),
                pl.BlockSpec((1, 1, d), lambda j: (bi, 0, 0)),
                pl.BlockSpec((1, d), lambda j: (0, 0)),
                pl.BlockSpec((TOP_K, planes, tc, lanes), lambda j: (0, 0, j, 0))]
    args = [wtok, x2, g2, gfin, yg]
    aliases = {}
    if prev_out is not None:
        in_specs.append(pl.BlockSpec(memory_space=pl.ANY))
        args.append(prev_out)
        aliases = {len(args) - 1: 0}
    return pl.pallas_call(
        _combine_kernel,
        grid=(nt,),
        in_specs=in_specs,
        out_specs=pl.BlockSpec((1, tc, d), lambda j: (bi, t0 + j, 0)),
        out_shape=jax.ShapeDtypeStruct((b, s, d), F32),
        input_output_aliases=aliases,
        compiler_params=pltpu.CompilerParams(dimension_semantics=("arbitrary",),
                                             vmem_limit_bytes=VMEM_LIMIT_BYTES),
        name="combine",
    )(*args)


def kernel(x, c, positions, w_ada, b_ada, norm_mix_g, w_in, ret_norm_g, conv_w, w_br_ret, w_br_conv, w_out,
           norm_ffn_g, w_router, router_bias, w_exp_gate, w_exp_up, w_exp_down, w_sh_gate, w_sh_up, w_sh_down,
           norm_final_g):
    b, s, d = x.shape
    t = b * s
    depth = w_in.shape[0]
    assert depth == 1, "the combine kernel applies the final norm, so exactly one layer is supported"
    posf = positions.astype(F32)[:, :, None]
    c8 = jnp.zeros((8, d), F32).at[:b].set(c)
    bm = EXPERT_ROWS
    n_blocks = t * TOP_K // bm + N_EXPERTS
    n_rows = n_blocks * bm

    for l in range(depth):
        mod = _ada(c8, w_ada[l], b_ada[l][None, :])[:b].reshape(b, 6, d)
        x2, h2p, eidx, wts, rank, counts = _mixer(
            x, posf, mod, norm_mix_g[l][None, :], w_in[l], ret_norm_g[l][None, :], conv_w[l],
            w_br_ret[l].astype(BF16), w_br_conv[l].astype(BF16), w_out[l].astype(BF16), norm_ffn_g[l][None, :],
            w_router[l].T, jnp.concatenate([w_sh_gate[l], w_sh_up[l]], axis=1).astype(BF16),
            w_sh_down[l].astype(BF16), router_bias[l][:, None])

        cnt = counts[:, 0].astype(jnp.int32)
        pcnt = (cnt + bm - 1) // bm * bm
        pend = jnp.cumsum(pcnt)
        pstart = pend - pcnt
        n_used = jnp.maximum(pend[-1] // bm, 1).astype(jnp.int32)[None]
        block_start = jnp.arange(n_blocks, dtype=jnp.int32) * bm
        block_e = jnp.minimum(jnp.sum((pend[None, :] <= block_start[:, None]).astype(jnp.int32), axis=1),
                              N_EXPERTS - 1)
        eid = jnp.arange(N_EXPERTS, dtype=jnp.int32)
        seg_end = jnp.sum(jnp.where(block_e[:, None] == eid[None, :], (pstart + cnt)[None, :], 0), axis=1)
        n_valid = jnp.clip(seg_end - block_start, 0, bm).astype(jnp.int32)
        later_used = jnp.logical_and(eid[None, :] > eid[:, None], cnt[None, :] > 0)
        next_expert = jnp.min(jnp.where(later_used, eid[None, :], N_EXPERTS), axis=1)
        stage_slot = (jnp.cumsum((cnt > 0).astype(jnp.int32)) - 1) % EXPERT_STAGES
        ahead = [stage_slot, next_expert]
        for _ in range(EXPERT_STAGES - 1):
            hop = jnp.sum(jnp.where(ahead[-1][:, None] == eid[None, :], next_expert[None, :], 0), axis=1)
            ahead.append(jnp.where(ahead[-1] == N_EXPERTS, N_EXPERTS, hop))
        expert_meta = jnp.stack(ahead).astype(jnp.int32)

        xs, dest3 = _dispatch(h2p, eidx, rank, pstart.astype(jnp.int32), n_rows)
        ybuf = _experts(block_e, n_valid, n_used, expert_meta, xs, w_exp_gate[l], w_exp_up[l], w_exp_down[l])
        chunks = t // LANES // COMBINE_SEGMENTS
        out = None
        for seg in range(COMBINE_SEGMENTS):
            yg = _gather(ybuf, dest3[seg * chunks:(seg + 1) * chunks], chunks * LANES)
            out = _combine(wts, x2, mod[:, 5:6, :], norm_final_g[None, :], yg, seg, out)
        x = out
    return x
```

```python
import functools

import numpy as np
import jax
import jax.numpy as jnp
from jax import lax
from jax.experimental import pallas as pl
from jax.experimental.pallas import tpu as pltpu
from jax.experimental.pallas import tpu_sc as plsc

RET_HEADS = 4
RET_DK = 128
RET_DV = 256
RET_CHUNK = 128
ROPE_THETA = 10000.0
N_EXPERTS = 256
TOP_K = 8
N_GROUPS = 8
TOPK_GROUPS = 4
GROUP_SIZE = N_EXPERTS // N_GROUPS
ROUTED_SCALE = 2.5
NORM_EPS = 1e-6

MIXER_TOKENS = 512
EXPERT_ROWS = 128
EXPERT_MAX_RUN = 4
EXPERT_BLOCKS_PER_STEP = 16
EXPERT_STAGES = 5
COMBINE_TOKENS = 512
COMBINE_SEGMENTS = 4
LANES = 128
GATHER_BUFFERS = 6

VMEM_LIMIT_BYTES = 56 * 1024 * 1024

F32 = jnp.float32
BF16 = jnp.bfloat16
HIGHEST = lax.Precision.HIGHEST


def _sigmoid(v):
    return 0.5 * jnp.tanh(0.5 * v) + 0.5


def _silu(v):
    return v * _sigmoid(v)


def _rms(v):
    return v * lax.rsqrt(jnp.mean(v * v, axis=-1, keepdims=True) + NORM_EPS)


def _resident(shape):
    nd = len(shape)
    return pl.BlockSpec(shape, lambda *_: (0,) * nd, pipeline_mode=pl.Buffered(1))


def _ada_kernel(c_ref, w_ref, b_ref, o_ref):
    c = c_ref[...]
    o_ref[...] = jnp.dot(_silu(c), w_ref[...], precision=HIGHEST, preferred_element_type=F32) + b_ref[...]


def _ada(c8, w, b):
    d, n = w.shape
    tn = 1024
    return pl.pallas_call(
        _ada_kernel,
        grid=(n // tn,),
        in_specs=[pl.BlockSpec((8, d), lambda j: (0, 0)),
                  pl.BlockSpec((d, tn), lambda j: (0, j)),
                  pl.BlockSpec((1, tn), lambda j: (0, j))],
        out_specs=pl.BlockSpec((8, tn), lambda j: (0, j)),
        out_shape=jax.ShapeDtypeStruct((8, n), F32),
        name="ada",
    )(c8, w, b)


def _retention_constants():
    f32 = np.float32
    c = RET_CHUNK
    log_g = np.log1p(-(f32(2.0) ** (f32(-5.0) - np.arange(RET_HEADS, dtype=f32)))).astype(f32)
    idx = np.arange(c, dtype=f32)
    diff = idx[:, None] - idx[None, :]
    intra = np.where(diff >= 0, np.exp(log_g[:, None, None] * np.maximum(diff, f32(0.0))), f32(0.0))
    k_decay = np.exp(log_g[:, None] * (f32(c - 1) - idx))
    q_decay = np.exp(log_g[:, None] * (idx + f32(1.0)))
    chunk_decay = np.exp(log_g * f32(c))
    kd = np.broadcast_to(k_decay[:, :, None], (RET_HEADS, c, RET_DK))
    qd = np.broadcast_to(q_decay[:, :, None], (RET_HEADS, c, RET_DK))
    cd = np.broadcast_to(chunk_decay[:, None, None], (RET_HEADS, 1, RET_DV))
    inv_freq = f32(ROPE_THETA) ** (-np.arange(0, RET_DK, 2, dtype=f32) / f32(RET_DK))
    inv_freq = np.concatenate([inv_freq, inv_freq])[None, :]
    sign = np.concatenate([-np.ones((RET_DK // 2,), f32), np.ones((RET_DK // 2,), f32)])[None, :]
    return tuple(jnp.asarray(np.ascontiguousarray(v, dtype=f32)) for v in (intra, qd, kd, cd, inv_freq, sign))


def _pack_bf16_pairs(v):
    w = v.shape[1] // 2
    lo = lax.bitcast_convert_type(v[:, :w].astype(BF16).astype(F32), jnp.uint32)
    hi = lax.bitcast_convert_type(v[:, w:].astype(BF16).astype(F32), jnp.uint32)
    return (lo >> 16) | (hi & jnp.uint32(0xFFFF0000))


def _unpack_pairs_f32(word):
    lo = lax.bitcast_convert_type(word << 16, F32)
    hi = lax.bitcast_convert_type(word & jnp.uint32(0xFFFF0000), F32)
    return jnp.concatenate([lo, hi], axis=1)


def _mix_tile(x_ref, pos_ref, mod_ref, gmix_ref, win_ref, invf_ref, sign_ref, intra_ref, qd_ref, kd_ref, cd_ref,
              retg_ref, convw_ref, wbr_ref, wbc_ref, wout_ref, gffn_ref, wrh_ref, wrl_ref, wshgu_ref, wshd_ref,
              x2_ref, h2p_ref, lg_ref, state_ref, carry_ref, q_ref, k_ref, v_ref, ret_ref, hb_ref, *, background):
    def tick():
        next(background, None)

    tm, d = x_ref.shape[1], x_ref.shape[2]
    q_w = RET_HEADS * RET_DK
    v_w = RET_HEADS * RET_DV
    offs = np.cumsum([0, q_w, q_w, v_w, v_w, d, d, d, d, d])

    x = x_ref[0]
    mod = mod_ref[0]
    sh1, sc1, g1, sh2, sc2, g2 = [mod[i:i + 1] for i in range(6)]
    hb_ref[...] = ((_rms(x) * gmix_ref[...]) * (1.0 + sc1) + sh1).astype(BF16)

    def proj(i):
        return jnp.dot(hb_ref[...], win_ref[:, offs[i]:offs[i + 1]], preferred_element_type=F32)

    ang = pos_ref[0] * invf_ref[...]
    cosv = jnp.cos(ang)
    sinv = jnp.sin(ang) * sign_ref[...]

    def rope(t):
        return jnp.concatenate(
            [t[:, h * RET_DK:(h + 1) * RET_DK] * cosv
             + pltpu.roll(t[:, h * RET_DK:(h + 1) * RET_DK], RET_DK // 2, 1) * sinv
             for h in range(RET_HEADS)], axis=1)

    q_ref[...] = rope(proj(0)) * (RET_DK ** -0.5)
    tick()
    k_ref[...] = rope(proj(1))
    tick()
    v_ref[...] = proj(2).astype(BF16)
    tick()

    for c in range(tm // RET_CHUNK):
        rows = pl.ds(c * RET_CHUNK, RET_CHUNK)
        for h in range(RET_HEADS):
            qh = q_ref[rows, h * RET_DK:(h + 1) * RET_DK]
            kh = k_ref[rows, h * RET_DK:(h + 1) * RET_DK]
            vh = v_ref[rows, h * RET_DV:(h + 1) * RET_DV]
            scores = lax.dot_general(qh.astype(BF16), kh.astype(BF16), (((1,), (1,)), ((), ())),
                                     preferred_element_type=F32) * intra_ref[h]
            inner = jnp.dot(scores.astype(BF16), vh, preferred_element_type=F32)
            st = state_ref[h]
            cross = jnp.dot((qh * qd_ref[h]).astype(BF16), st.astype(BF16), preferred_element_type=F32)
            kv = lax.dot_general((kh * kd_ref[h]).astype(BF16), vh, (((0,), (0,)), ((), ())),
                                 preferred_element_type=F32)
            state_ref[h] = st * cd_ref[h] + kv
            ret_ref[rows, h * RET_DV:(h + 1) * RET_DV] = _rms(inner + cross)
        tick()

    y_ret = jnp.dot((_silu(proj(3)) * (ret_ref[...] * retg_ref[...])).astype(BF16), wbr_ref[...],
                    preferred_element_type=F32)
    tick()

    z = proj(5) * proj(4)
    row = lax.broadcasted_iota(jnp.int32, z.shape, 0)
    prev1 = carry_ref[7:8, :]
    prev2 = carry_ref[6:7, :]
    z1 = jnp.where(row == 0, prev1, pltpu.roll(z, 1, 0))
    z2 = jnp.where(row == 0, prev2, jnp.where(row == 1, prev1, pltpu.roll(z, 2, 0)))
    conv = convw_ref[0:1, :] * z2 + convw_ref[1:2, :] * z1 + convw_ref[2:3, :] * z
    carry_ref[...] = z[tm - 8:tm, :]
    tick()
    y_conv = jnp.dot((proj(6) * conv).astype(BF16), wbc_ref[...], preferred_element_type=F32)
    tick()

    mix = _sigmoid(proj(7)) * y_ret + _sigmoid(proj(8)) * y_conv
    tick()
    x1 = x + g1 * jnp.dot(mix.astype(BF16), wout_ref[...], preferred_element_type=F32)
    tick()

    h2 = (_rms(x1) * gffn_ref[...]) * (1.0 + sc2) + sh2
    hi_f32 = lax.bitcast_convert_type(lax.bitcast_convert_type(h2, jnp.uint32) & jnp.uint32(0xFFFF0000), F32)
    h2_hi = hi_f32.astype(BF16)
    h2_lo = (h2 - hi_f32).astype(BF16)

    def nt_dot(a, b):
        return lax.dot_general(a, b, (((1,), (1,)), ((), ())), preferred_element_type=F32)

    lg_ref[...] = nt_dot(wrh_ref[...], h2_hi) + (nt_dot(wrh_ref[...], h2_lo) + nt_dot(wrl_ref[...], h2_hi))
    gu = jnp.dot(h2.astype(BF16), wshgu_ref[...], preferred_element_type=F32)
    f = gu.shape[1] // 2
    shared = jnp.dot((_silu(gu[:, :f]) * gu[:, f:]).astype(BF16), wshd_ref[...], preferred_element_type=F32)
    x2_ref[0] = x1 + g2 * shared
    for j in range(h2p_ref.shape[0]):
        h2p_ref[j] = _pack_bf16_pairs(h2[:, 2 * LANES * j:2 * LANES * (j + 1)])
    for _ in background:
        pass


def _mixer_kernel(*refs, tiles_per_seq):
    mix_inputs, (bias_ref, tri_ref) = refs[:21], refs[21:23]
    x2_ref, h2p_ref, eidx_ref, w_ref, rank_ref, cnt_ref = refs[23:29]
    state_ref, carry_ref, q_ref, k_ref, v_ref, ret_ref, hb_ref, lg_ref, rcarry_ref = refs[29:]
    s = pl.program_id(0)
    n_tiles = pl.num_programs(0) - 1

    @pl.when(s == 0)
    def _():
        lg_ref[...] = jnp.zeros_like(lg_ref)
        rcarry_ref[...] = jnp.zeros_like(rcarry_ref)

    @pl.when(s % tiles_per_seq == 0)
    def _():
        state_ref[...] = jnp.zeros_like(state_ref)
        carry_ref[...] = jnp.zeros_like(carry_ref)

    def route_previous_tile():
        return _route_phases(lg_ref.at[(s + 1) % 2], bias_ref, tri_ref, eidx_ref, w_ref, rank_ref, cnt_ref,
                             rcarry_ref, (s > 0).astype(jnp.int32))

    @pl.when(s == n_tiles)
    def _():
        for _ in route_previous_tile():
            pass

    @pl.when(s < n_tiles)
    def _():
        _mix_tile(*mix_inputs, x2_ref, h2p_ref, lg_ref.at[s % 2], state_ref, carry_ref, q_ref, k_ref, v_ref,
                  ret_ref, hb_ref, background=route_previous_tile())


def _mixer(x, posf, mod, gmix, win, retg, convw, wbr, wbc, wout, gffn, wr, wshgu, wshd, bias):
    b, s, d = x.shape
    tm = MIXER_TOKENS
    nt = s // tm
    n_tiles = b * nt
    intra, qd, kd, cd, invf, sign = _retention_constants()
    q_w, v_w = RET_HEADS * RET_DK, RET_HEADS * RET_DV
    hi_f32 = lax.bitcast_convert_type(lax.bitcast_convert_type(wr, jnp.uint32) & jnp.uint32(0xFFFF0000), F32)
    wr_hi = hi_f32.astype(BF16)
    wr_lo = (wr - hi_f32).astype(BF16)
    tri = jnp.asarray(np.triu(np.ones((tm, tm), np.float32), 1), BF16)
    weights = (gmix, win, invf, sign, intra, qd, kd, cd, retg, convw, wbr, wbc, wout, gffn, wr_hi, wr_lo,
               wshgu, wshd, bias, tri)

    def tile(i):
        return jnp.minimum(i, n_tiles - 1)

    def routed(i):
        return (0, jnp.maximum(i - 1, 0))

    def routed_chunks(i):
        return (jnp.maximum(i - 1, 0), 0, 0)

    return pl.pallas_call(
        functools.partial(_mixer_kernel, tiles_per_seq=nt),
        grid=(n_tiles + 1,),
        in_specs=[pl.BlockSpec((1, tm, d), lambda i: (tile(i) // nt, tile(i) % nt, 0)),
                  pl.BlockSpec((1, tm, 1), lambda i: (tile(i) // nt, tile(i) % nt, 0)),
                  pl.BlockSpec((1, 6, d), lambda i: (tile(i) // nt, 0, 0))]
                 + [_resident(w.shape) for w in weights],
        out_specs=[pl.BlockSpec((1, tm, d), lambda i: (tile(i) // nt, tile(i) % nt, 0)),
                   pl.BlockSpec((d // (2 * LANES), tm, LANES), lambda i: (0, tile(i), 0)),
                   pl.BlockSpec((tm // LANES, TOP_K, LANES), routed_chunks),
                   pl.BlockSpec((TOP_K, tm), routed),
                   pl.BlockSpec((tm // LANES, TOP_K, LANES), routed_chunks),
                   pl.BlockSpec((N_EXPERTS, 1), lambda i: (0, 0))],
        out_shape=[jax.ShapeDtypeStruct((b, s, d), F32),
                   jax.ShapeDtypeStruct((d // (2 * LANES), b * s, LANES), jnp.uint32),
                   jax.ShapeDtypeStruct((b * s // LANES, TOP_K, LANES), jnp.int32),
                   jax.ShapeDtypeStruct((TOP_K, b * s), F32),
                   jax.ShapeDtypeStruct((b * s // LANES, TOP_K, LANES), jnp.int32),
                   jax.ShapeDtypeStruct((N_EXPERTS, 1), F32)],
        scratch_shapes=[pltpu.VMEM((RET_HEADS, RET_DK, RET_DV), F32),
                        pltpu.VMEM((8, d), F32),
                        pltpu.VMEM((tm, q_w), F32),
                        pltpu.VMEM((tm, q_w), F32),
                        pltpu.VMEM((tm, v_w), BF16),
                        pltpu.VMEM((tm, v_w), F32),
                        pltpu.VMEM((tm, d), BF16),
                        pltpu.VMEM((2, N_EXPERTS, tm), F32),
                        pltpu.VMEM((N_EXPERTS, 1), F32)],
        compiler_params=pltpu.CompilerParams(dimension_semantics=("arbitrary",),
                                             vmem_limit_bytes=VMEM_LIMIT_BYTES),
        name="mixer",
    )(x, posf, mod, *weights)


def _first_argmax(v, idx, n):
    m = jnp.max(v, axis=0, keepdims=True)
    return m, jnp.min(jnp.where(v == m, idx, n), axis=0, keepdims=True)


def _route_phases(logits_ref, bias_ref, tri_ref, eidx_ref, w_ref, rank_ref, cnt_ref, carry_ref, count_it):
    tn = logits_ref.shape[1]
    neg = F32(-jnp.inf)
    score = _sigmoid(logits_ref[...])
    choice = score + bias_ref[...]

    grow = lax.broadcasted_iota(jnp.int32, (GROUP_SIZE, tn), 0)
    gscores = []
    for g in range(N_GROUPS):
        cg = choice[g * GROUP_SIZE:(g + 1) * GROUP_SIZE]
        m1, i1 = _first_argmax(cg, grow, GROUP_SIZE)
        m2 = jnp.max(jnp.where(grow == i1, neg, cg), axis=0, keepdims=True)
        gscores.append(m1 + m2)
    cur = jnp.concatenate(gscores, axis=0)
    yield
    gidx = lax.broadcasted_iota(jnp.int32, (N_GROUPS, tn), 0)
    keep = jnp.zeros((N_GROUPS, tn), F32)
    for _ in range(TOPK_GROUPS):
        _, ig = _first_argmax(cur, gidx, N_GROUPS)
        hit = gidx == ig
        keep = jnp.where(hit, 1.0, keep)
        cur = jnp.where(hit, neg, cur)
    cur = jnp.concatenate(
        [jnp.where(keep[g:g + 1] > 0.0, choice[g * GROUP_SIZE:(g + 1) * GROUP_SIZE], neg)
         for g in range(N_GROUPS)], axis=0)
    yield

    erow = lax.broadcasted_iota(jnp.int32, (N_EXPERTS, tn), 0)
    eidx, wts = [], []
    member = jnp.zeros((N_EXPERTS, tn), F32)
    for _ in range(TOP_K):
        _, ie = _first_argmax(cur, erow, N_EXPERTS)
        hit = erow == ie
        eidx.append(ie)
        wts.append(jnp.sum(jnp.where(hit, score, 0.0), axis=0, keepdims=True))
        member = member + hit.astype(F32)
        cur = jnp.where(hit, neg, cur)
        yield
    wsum = wts[0]
    for k in range(1, TOP_K):
        wsum = wsum + wts[k]

    before = jnp.dot(member.astype(BF16), tri_ref[...], preferred_element_type=F32) + carry_ref[...]
    ranks = []
    for k in range(TOP_K):
        ranks.append(jnp.sum(jnp.where(erow == eidx[k], before, 0.0), axis=0, keepdims=True))
        if k % 2 == 1:
            yield
    carry_ref[...] = carry_ref[...] + count_it.astype(F32) * jnp.sum(member, axis=1, keepdims=True)

    eidx_rows = jnp.concatenate(eidx, axis=0)
    rank_rows = jnp.concatenate(ranks, axis=0).astype(jnp.int32)
    for c in range(tn // LANES):
        eidx_ref[c] = eidx_rows[:, c * LANES:(c + 1) * LANES]
        rank_ref[c] = rank_rows[:, c * LANES:(c + 1) * LANES]
    w_ref[...] = jnp.concatenate([w / wsum * ROUTED_SCALE for w in wts], axis=0)
    cnt_ref[...] = carry_ref[...]


def _sc_mesh_and_workers():
    mesh = plsc.VectorSubcoreMesh(core_axis_name="c", subcore_axis_name="s")
    return mesh, mesh.num_cores, mesh.num_cores * mesh.num_subcores


def _worker_id(num_cores):
    return lax.axis_index("s") * num_cores + lax.axis_index("c")


def _dispatch(h2p, eidx3, rank3, pstart, n_rows):
    planes, t, lanes = h2p.shape
    n_chunks = eidx3.shape[0]
    mesh, num_cores, workers = _sc_mesh_and_workers()
    chunks_per_worker = n_chunks // workers
    vec = plsc.get_sparse_core_info().num_lanes
    assert chunks_per_worker * workers == n_chunks and n_chunks * lanes == t and lanes % vec == 0

    def body(h_hbm, e_hbm, r_hbm, p_hbm, xs_hbm, d_hbm, e_v, r_v, p_v, idx_v, rows_v, load_sem, scatter_sem):
        wid = _worker_id(num_cores)
        mine = pl.ds(wid * chunks_per_worker, chunks_per_worker)
        pltpu.sync_copy(p_hbm, p_v)
        pltpu.sync_copy(e_hbm.at[mine], e_v)
        pltpu.sync_copy(r_hbm.at[mine], r_v)

        @pl.loop(0, chunks_per_worker)
        def _(c):
            for k in range(TOP_K):
                for g in range(lanes // vec):
                    part = pl.ds(g * vec, vec)
                    idx_v[c, k, part] = plsc.load_gather(p_v, [e_v[c, k, part]]) + r_v[c, k, part]

        pltpu.sync_copy(idx_v, d_hbm.at[mine])

        @pl.loop(0, chunks_per_worker)
        def _(c):
            tok = pl.ds((wid * chunks_per_worker + c) * lanes, lanes)
            loads = [None] * planes
            scatters = [None] * planes
            loads[0] = pltpu.async_copy(h_hbm.at[0, tok], rows_v.at[0], load_sem.at[0])
            for j in range(planes):
                loads[j].wait()
                if j + 1 < planes:
                    if j >= 1:
                        for cp in scatters[j - 1]:
                            cp.wait()
                    loads[j + 1] = pltpu.async_copy(h_hbm.at[j + 1, tok], rows_v.at[(j + 1) % 2],
                                                    load_sem.at[(j + 1) % 2])
                scatters[j] = [pltpu.async_copy(rows_v.at[j % 2], xs_hbm.at[j].at[idx_v.at[c, k]],
                                                scatter_sem.at[j % 2]) for k in range(TOP_K)]
            for j in range(max(planes - 2, 0), planes):
                for cp in scatters[j]:
                    cp.wait()

    return pl.kernel(
        body,
        out_type=(jax.ShapeDtypeStruct((planes, n_rows, lanes), h2p.dtype),
                  jax.ShapeDtypeStruct(eidx3.shape, jnp.int32)),
        mesh=mesh,
        scratch_types=[pltpu.VMEM((chunks_per_worker, TOP_K, lanes), jnp.int32),
                       pltpu.VMEM((chunks_per_worker, TOP_K, lanes), jnp.int32),
                       pltpu.VMEM((N_EXPERTS,), jnp.int32),
                       pltpu.VMEM((chunks_per_worker, TOP_K, lanes), jnp.int32),
                       pltpu.VMEM((2, lanes, lanes), h2p.dtype),
                       pltpu.SemaphoreType.DMA((2,)),
                       pltpu.SemaphoreType.DMA((2,))],
        compiler_params=pltpu.CompilerParams(needs_layout_passes=False),
        name="dispatch",
    )(h2p, eidx3, rank3, pstart)


def _gather(ybuf, dest3, t):
    planes, _, lanes = ybuf.shape
    n_chunks = dest3.shape[0]
    mesh, num_cores, workers = _sc_mesh_and_workers()
    chunks_per_worker = n_chunks // workers
    nbuf = GATHER_BUFFERS
    lag = nbuf // 2
    assert chunks_per_worker * workers == n_chunks and n_chunks * lanes == t

    def body(y_hbm, d_hbm, yg_hbm, idx_v, rows_v, gather_sem, store_sem):
        wid = _worker_id(num_cores)

        @pl.loop(0, chunks_per_worker)
        def _(c):
            chunk = wid * chunks_per_worker + c
            tok = pl.ds(chunk * lanes, lanes)
            pltpu.sync_copy(d_hbm.at[chunk], idx_v)

            items = [(k, j) for k in range(TOP_K) for j in range(planes)]
            gathers = [None] * len(items)
            stores = [None] * len(items)

            def store(n):
                k, j = items[n]
                gathers[n].wait()
                stores[n] = pltpu.async_copy(rows_v.at[n % nbuf], yg_hbm.at[k, j, tok], store_sem.at[n % nbuf])

            for n, (k, j) in enumerate(items):
                if n >= nbuf:
                    stores[n - nbuf].wait()
                gathers[n] = pltpu.async_copy(y_hbm.at[j].at[idx_v.at[k]], rows_v.at[n % nbuf],
                                              gather_sem.at[n % nbuf])
                if n >= lag:
                    store(n - lag)
            for n in range(len(items) - lag, len(items)):
                store(n)
            for n in range(len(items) - nbuf, len(items)):
                stores[n].wait()

    return pl.kernel(
        body,
        out_type=jax.ShapeDtypeStruct((TOP_K, planes, t, lanes), ybuf.dtype),
        mesh=mesh,
        scratch_types=[pltpu.VMEM((TOP_K, lanes), jnp.int32),
                       pltpu.VMEM((nbuf, lanes, lanes), ybuf.dtype),
                       pltpu.SemaphoreType.DMA((nbuf,)),
                       pltpu.SemaphoreType.DMA((nbuf,))],
        name="gather",
    )(ybuf, dest3)


def _experts_kernel(be_ref, nv_ref, nu_ref, em_ref, x_ref, wg_hbm, wu_hbm, wd_hbm, o_ref,
                    wgs_ref, wus_ref, wds_ref, wgb_ref, wub_ref, wdb_ref, sems):
    planes, step_rows, lanes = x_ref.shape
    bm = EXPERT_ROWS
    blocks_per_step = step_rows // bm
    step = pl.program_id(0)
    last_step = (nu_ref[0] - 1) // blocks_per_step

    def weight_copies(expert):
        slot = em_ref[0, expert]
        return [pltpu.make_async_copy(src.at[expert], dst.at[slot], sems.at[slot, n])
                for n, (src, dst) in enumerate(((wg_hbm, wgs_ref), (wu_hbm, wus_ref), (wd_hbm, wds_ref)))]

    def start_if_any(expert):
        @pl.when(expert < N_EXPERTS)
        def _():
            for cp in weight_copies(expert):
                cp.start()

    def swiglu_rows(row0, m, n_valid):
        rows = pl.ds(row0, m)
        valid = lax.broadcasted_iota(jnp.int32, (m, lanes), 0) < n_valid
        g = None
        u = None
        for j in range(planes):
            xj = _unpack_pairs_f32(jnp.where(valid, x_ref[j, rows, :], jnp.uint32(0))).astype(BF16)
            wrows = pl.ds(2 * lanes * j, 2 * lanes)
            gj = jnp.dot(xj, wgb_ref[wrows, :], preferred_element_type=F32)
            uj = jnp.dot(xj, wub_ref[wrows, :], preferred_element_type=F32)
            g = gj if g is None else g + gj
            u = uj if u is None else u + uj
        out = jnp.dot((_silu(g) * u).astype(BF16), wdb_ref[...], preferred_element_type=F32)
        for j in range(planes):
            o_ref[j, rows, :] = _pack_bf16_pairs(out[:, 2 * lanes * j:2 * lanes * (j + 1)])

    def block_step(sb):
        i = step * blocks_per_step + sb
        row0 = pl.multiple_of(sb * bm, bm)
        rows = pl.ds(row0, bm)
        active = i < nu_ref[0]
        e = be_ref[i]
        new_expert = jnp.logical_or(i == 0, e != be_ref[jnp.maximum(i - 1, 0)])
        run = active.astype(jnp.int32)
        same = active
        for ahead in range(1, EXPERT_MAX_RUN):
            j = jnp.minimum(i + ahead, be_ref.shape[0] - 1)
            same = jnp.logical_and(jnp.logical_and(same, sb + ahead < blocks_per_step),
                                   jnp.logical_and(i + ahead < nu_ref[0], be_ref[j] == e))
            run = run + same.astype(jnp.int32)

        @pl.when(jnp.logical_and(active, new_expert))
        def _():
            @pl.when(i == 0)
            def _():
                for cp in weight_copies(e):
                    cp.start()
                for ahead in range(1, EXPERT_STAGES):
                    start_if_any(em_ref[ahead, e])

            for cp in weight_copies(e):
                cp.wait()
            slot = em_ref[0, e]
            wgb_ref[...] = wgs_ref[slot].astype(BF16)
            wub_ref[...] = wus_ref[slot].astype(BF16)
            wdb_ref[...] = wds_ref[slot].astype(BF16)
            start_if_any(em_ref[EXPERT_STAGES, e])

        for m in range(1, EXPERT_MAX_RUN + 1):
            @pl.when(run == m)
            def _():
                last = jnp.minimum(i + (m - 1), be_ref.shape[0] - 1)
                swiglu_rows(row0, m * bm, (m - 1) * bm + nv_ref[last])

        @pl.when(jnp.logical_and(jnp.logical_not(active), step == last_step))
        def _():
            for j in range(planes):
                o_ref[j, rows, :] = jnp.zeros((bm, lanes), o_ref.dtype)

        return sb + jnp.maximum(run, 1)

    lax.while_loop(lambda sb: sb < blocks_per_step, block_step, jnp.int32(0))


def _experts(block_e, n_valid, n_used, expert_meta, xs, wg, wu, wd):
    planes, p, lanes = xs.shape
    e, d, f = wg.shape
    step_rows = EXPERT_ROWS * EXPERT_BLOCKS_PER_STEP
    assert p % step_rows == 0

    def row_map(i, be, nv, nu, nxt):
        return (0, jnp.minimum(i, (nu[0] - 1) // EXPERT_BLOCKS_PER_STEP), 0)

    return pl.pallas_call(
        _experts_kernel,
        grid_spec=pltpu.PrefetchScalarGridSpec(
            num_scalar_prefetch=4,
            grid=(p // step_rows,),
            in_specs=[pl.BlockSpec((planes, step_rows, lanes), row_map),
                      pl.BlockSpec(memory_space=pl.ANY),
                      pl.BlockSpec(memory_space=pl.ANY),
                      pl.BlockSpec(memory_space=pl.ANY)],
            out_specs=pl.BlockSpec((planes, step_rows, lanes), row_map),
            scratch_shapes=[pltpu.VMEM((EXPERT_STAGES, d, f), F32), pltpu.VMEM((EXPERT_STAGES, d, f), F32),
                            pltpu.VMEM((EXPERT_STAGES, f, d), F32),
                            pltpu.VMEM((d, f), BF16), pltpu.VMEM((d, f), BF16), pltpu.VMEM((f, d), BF16),
                            pltpu.SemaphoreType.DMA((EXPERT_STAGES, 3))]),
        out_shape=jax.ShapeDtypeStruct((planes, p, lanes), jnp.uint32),
        compiler_params=pltpu.CompilerParams(dimension_semantics=("arbitrary",),
                                             vmem_limit_bytes=VMEM_LIMIT_BYTES),
        name="experts",
    )(block_e, n_valid, n_used, expert_meta, xs, wg, wu, wd)


def _combine_kernel(w_ref, x2_ref, g2_ref, gfin_ref, yg_ref, *maybe_prev_and_out):
    o_ref = maybe_prev_and_out[-1]
    tc = x2_ref.shape[1]
    eye = (lax.broadcasted_iota(jnp.int32, (tc, tc), 0) == lax.broadcasted_iota(jnp.int32, (tc, tc), 1)).astype(F32)
    w = lax.dot_general(eye, w_ref[...], (((1,), (1,)), ((), ())), precision=HIGHEST, preferred_element_type=F32)
    planes = yg_ref.shape[1]
    routed = None
    for k in range(TOP_K):
        rows = jnp.concatenate([_unpack_pairs_f32(yg_ref[k, j]) for j in range(planes)], axis=1) * w[:, k:k + 1]
        routed = rows if routed is None else routed + rows
    o_ref[0] = _rms(x2_ref[0] + g2_ref[0] * routed) * gfin_ref[...]


def _combine(wtok, x2, g2, gfin, yg, seg, prev_out):
    b, s, d = x2.shape
    _, planes, seg_tokens, lanes = yg.shape
    tc = COMBINE_TOKENS
    nt = seg_tokens // tc
    per_seq = s // seg_tokens
    assert nt * tc == seg_tokens and per_seq * seg_tokens == s
    bi, t0 = seg // per_seq, (seg % per_seq) * nt
    in_specs = [pl.BlockSpec((TOP_K, tc), lambda j: (0, seg * nt + j)),
                pl.BlockSpec((1, tc, d), lambda j: (bi, t0 + j, 0)),
                pl.BlockSpec((1, 1, d), lambda j: (bi, 0, 0)),
                pl.BlockSpec((1, d), lambda j: (0, 0)),
                pl.BlockSpec((TOP_K, planes, tc, lanes), lambda j: (0, 0, j, 0))]
    args = [wtok, x2, g2, gfin, yg]
    aliases = {}
    if prev_out is not None:
        in_specs.append(pl.BlockSpec(memory_space=pl.ANY))
        args.append(prev_out)
        aliases = {len(args) - 1: 0}
    return pl.pallas_call(
        _combine_kernel,
        grid=(nt,),
        in_specs=in_specs,
        out_specs=pl.BlockSpec((1, tc, d), lambda j: (bi, t0 + j, 0)),
        out_shape=jax.ShapeDtypeStruct((b, s, d), F32),
        input_output_aliases=aliases,
        compiler_params=pltpu.CompilerParams(dimension_semantics=("arbitrary",),
                                             vmem_limit_bytes=VMEM_LIMIT_BYTES),
        name="combine",
    )(*args)


def kernel(x, c, positions, w_ada, b_ada, norm_mix_g, w_in, ret_norm_g, conv_w, w_br_ret, w_br_conv, w_out,
           norm_ffn_g, w_router, router_bias, w_exp_gate, w_exp_up, w_exp_down, w_sh_gate, w_sh_up, w_sh_down,
           norm_final_g):
    b, s, d = x.shape
    t = b * s
    depth = w_in.shape[0]
    assert depth == 1, "the combine kernel applies the final norm, so exactly one layer is supported"
    posf = positions.astype(F32)[:, :, None]
    c8 = jnp.zeros((8, d), F32).at[:b].set(c)
    bm = EXPERT_ROWS
    n_blocks = t * TOP_K // bm + N_EXPERTS
    n_rows = n_blocks * bm

    for l in range(depth):
        mod = _ada(c8, w_ada[l], b_ada[l][None, :])[:b].reshape(b, 6, d)
        x2, h2p, eidx, wts, rank, counts = _mixer(
            x, posf, mod, norm_mix_g[l][None, :], w_in[l].astype(BF16), ret_norm_g[l][None, :], conv_w[l],
            w_br_ret[l].astype(BF16), w_br_conv[l].astype(BF16), w_out[l].astype(BF16), norm_ffn_g[l][None, :],
            w_router[l].T, jnp.concatenate([w_sh_gate[l], w_sh_up[l]], axis=1).astype(BF16),
            w_sh_down[l].astype(BF16), router_bias[l][:, None])

        cnt = counts[:, 0].astype(jnp.int32)
        pcnt = (cnt + bm - 1) // bm * bm
        pend = jnp.cumsum(pcnt)
        pstart = pend - pcnt
        n_used = jnp.maximum(pend[-1] // bm, 1).astype(jnp.int32)[None]
        block_start = jnp.arange(n_blocks, dtype=jnp.int32) * bm
        block_e = jnp.minimum(jnp.sum((pend[None, :] <= block_start[:, None]).astype(jnp.int32), axis=1),
                              N_EXPERTS - 1)
        eid = jnp.arange(N_EXPERTS, dtype=jnp.int32)
        seg_end = jnp.sum(jnp.where(block_e[:, None] == eid[None, :], (pstart + cnt)[None, :], 0), axis=1)
        n_valid = jnp.clip(seg_end - block_start, 0, bm).astype(jnp.int32)
        later_used = jnp.logical_and(eid[None, :] > eid[:, None], cnt[None, :] > 0)
        next_expert = jnp.min(jnp.where(later_used, eid[None, :], N_EXPERTS), axis=1)
        stage_slot = (jnp.cumsum((cnt > 0).astype(jnp.int32)) - 1) % EXPERT_STAGES
        ahead = [stage_slot, next_expert]
        for _ in range(EXPERT_STAGES - 1):
            hop = jnp.sum(jnp.where(ahead[-1][:, None] == eid[None, :], next_expert[None, :], 0), axis=1)
            ahead.append(jnp.where(ahead[-1] == N_EXPERTS, N_EXPERTS, hop))
        expert_meta = jnp.stack(ahead).astype(jnp.int32)

        xs, dest3 = _dispatch(h2p, eidx, rank, pstart.astype(jnp.int32), n_rows)
        ybuf = _experts(block_e, n_valid, n_used, expert_meta, xs, w_exp_gate[l], w_exp_up[l], w_exp_down[l])
        chunks = t // LANES // COMBINE_SEGMENTS
        out = None
        for seg in range(COMBINE_SEGMENTS):
            yg = _gather(ybuf, dest3[seg * chunks:(seg + 1) * chunks], chunks * LANES)
            out = _combine(wts, x2, mod[:, 5:6, :], norm_final_g[None, :], yg, seg, out)
        x = out
    return x
```

```python
import functools

import numpy as np
import jax
import jax.numpy as jnp
from jax import lax
from jax.experimental import pallas as pl
from jax.experimental.pallas import tpu as pltpu
from jax.experimental.pallas import tpu_sc as plsc

RET_HEADS = 4
RET_DK = 128
RET_DV = 256
RET_CHUNK = 128
ROPE_THETA = 10000.0
N_EXPERTS = 256
TOP_K = 8
N_GROUPS = 8
TOPK_GROUPS = 4
GROUP_SIZE = N_EXPERTS // N_GROUPS
ROUTED_SCALE = 2.5
NORM_EPS = 1e-6

MIXER_TOKENS = 512
EXPERT_ROWS = 128
EXPERT_MAX_RUN = 4
EXPERT_BLOCKS_PER_STEP = 8
EXPERT_STAGES = 5
COMBINE_TOKENS = 512
COMBINE_SEGMENTS = 4
LANES = 128
GATHER_BUFFERS = 4

VMEM_LIMIT_BYTES = 56 * 1024 * 1024

F32 = jnp.float32
BF16 = jnp.bfloat16
HIGHEST = lax.Precision.HIGHEST


def _sigmoid(v):
    return 0.5 * jnp.tanh(0.5 * v) + 0.5


def _silu(v):
    return v * _sigmoid(v)


def _rms(v):
    return v * lax.rsqrt(jnp.mean(v * v, axis=-1, keepdims=True) + NORM_EPS)


def _resident(shape):
    nd = len(shape)
    return pl.BlockSpec(shape, lambda *_: (0,) * nd, pipeline_mode=pl.Buffered(1))


def _ada_kernel(c_ref, w_ref, b_ref, o_ref):
    c = c_ref[...]
    o_ref[...] = jnp.dot(_silu(c), w_ref[...], precision=HIGHEST, preferred_element_type=F32) + b_ref[...]


def _ada(c8, w, b):
    d, n = w.shape
    tn = 1024
    return pl.pallas_call(
        _ada_kernel,
        grid=(n // tn,),
        in_specs=[pl.BlockSpec((8, d), lambda j: (0, 0)),
                  pl.BlockSpec((d, tn), lambda j: (0, j)),
                  pl.BlockSpec((1, tn), lambda j: (0, j))],
        out_specs=pl.BlockSpec((8, tn), lambda j: (0, j)),
        out_shape=jax.ShapeDtypeStruct((8, n), F32),
        name="ada",
    )(c8, w, b)


def _retention_constants():
    f32 = np.float32
    c = RET_CHUNK
    log_g = np.log1p(-(f32(2.0) ** (f32(-5.0) - np.arange(RET_HEADS, dtype=f32)))).astype(f32)
    idx = np.arange(c, dtype=f32)
    diff = idx[:, None] - idx[None, :]
    intra = np.where(diff >= 0, np.exp(log_g[:, None, None] * np.maximum(diff, f32(0.0))), f32(0.0))
    k_decay = np.exp(log_g[:, None] * (f32(c - 1) - idx))
    q_decay = np.exp(log_g[:, None] * (idx + f32(1.0)))
    chunk_decay = np.exp(log_g * f32(c))
    kd = np.broadcast_to(k_decay[:, :, None], (RET_HEADS, c, RET_DK))
    qd = np.broadcast_to(q_decay[:, :, None], (RET_HEADS, c, RET_DK))
    cd = np.broadcast_to(chunk_decay[:, None, None], (RET_HEADS, 1, RET_DV))
    inv_freq = f32(ROPE_THETA) ** (-np.arange(0, RET_DK, 2, dtype=f32) / f32(RET_DK))
    inv_freq = np.concatenate([inv_freq, inv_freq])[None, :]
    sign = np.concatenate([-np.ones((RET_DK // 2,), f32), np.ones((RET_DK // 2,), f32)])[None, :]
    return tuple(jnp.asarray(np.ascontiguousarray(v, dtype=f32)) for v in (intra, qd, kd, cd, inv_freq, sign))


def _pack_bf16_pairs(v):
    w = v.shape[1] // 2
    lo = lax.bitcast_convert_type(v[:, :w].astype(BF16).astype(F32), jnp.uint32)
    hi = lax.bitcast_convert_type(v[:, w:].astype(BF16).astype(F32), jnp.uint32)
    return (lo >> 16) | (hi & jnp.uint32(0xFFFF0000))


def _unpack_pairs_f32(word):
    lo = lax.bitcast_convert_type(word << 16, F32)
    hi = lax.bitcast_convert_type(word & jnp.uint32(0xFFFF0000), F32)
    return jnp.concatenate([lo, hi], axis=1)


def _mix_tile(x_ref, pos_ref, mod_ref, gmix_ref, win_ref, invf_ref, sign_ref, intra_ref, qd_ref, kd_ref, cd_ref,
              retg_ref, convw_ref, wbr_ref, wbc_ref, wout_ref, gffn_ref, wrh_ref, wrl_ref, wshgu_ref, wshd_ref,
              x2_ref, h2p_ref, lg_ref, state_ref, carry_ref, q_ref, k_ref, v_ref, ret_ref, hb_ref, *, background):
    def tick():
        next(background, None)

    tm, d = x_ref.shape[1], x_ref.shape[2]
    q_w = RET_HEADS * RET_DK
    v_w = RET_HEADS * RET_DV
    offs = np.cumsum([0, q_w, q_w, v_w, v_w, d, d, d, d, d])

    x = x_ref[0]
    mod = mod_ref[0]
    sh1, sc1, g1, sh2, sc2, g2 = [mod[i:i + 1] for i in range(6)]
    hb_ref[...] = ((_rms(x) * gmix_ref[...]) * (1.0 + sc1) + sh1).astype(BF16)

    def proj(i):
        return jnp.dot(hb_ref[...], win_ref[:, offs[i]:offs[i + 1]], preferred_element_type=F32)

    ang = pos_ref[0] * invf_ref[...]
    cosv = jnp.cos(ang)
    sinv = jnp.sin(ang) * sign_ref[...]

    def rope(t):
        return jnp.concatenate(
            [t[:, h * RET_DK:(h + 1) * RET_DK] * cosv
             + pltpu.roll(t[:, h * RET_DK:(h + 1) * RET_DK], RET_DK // 2, 1) * sinv
             for h in range(RET_HEADS)], axis=1)

    q_ref[...] = rope(proj(0)) * (RET_DK ** -0.5)
    tick()
    k_ref[...] = rope(proj(1))
    tick()
    v_ref[...] = proj(2).astype(BF16)
    tick()

    for c in range(tm // RET_CHUNK):
        rows = pl.ds(c * RET_CHUNK, RET_CHUNK)
        for h in range(RET_HEADS):
            qh = q_ref[rows, h * RET_DK:(h + 1) * RET_DK]
            kh = k_ref[rows, h * RET_DK:(h + 1) * RET_DK]
            vh = v_ref[rows, h * RET_DV:(h + 1) * RET_DV]
            scores = lax.dot_general(qh.astype(BF16), kh.astype(BF16), (((1,), (1,)), ((), ())),
                                     preferred_element_type=F32) * intra_ref[h]
            inner = jnp.dot(scores.astype(BF16), vh, preferred_element_type=F32)
            st = state_ref[h]
            cross = jnp.dot((qh * qd_ref[h]).astype(BF16), st.astype(BF16), preferred_element_type=F32)
            kv = lax.dot_general((kh * kd_ref[h]).astype(BF16), vh, (((0,), (0,)), ((), ())),
                                 preferred_element_type=F32)
            state_ref[h] = st * cd_ref[h] + kv
            ret_ref[rows, h * RET_DV:(h + 1) * RET_DV] = _rms(inner + cross)
        tick()

    y_ret = jnp.dot((_silu(proj(3)) * (ret_ref[...] * retg_ref[...])).astype(BF16), wbr_ref[...],
                    preferred_element_type=F32)
    tick()

    z = proj(5) * proj(4)
    row = lax.broadcasted_iota(jnp.int32, z.shape, 0)
    prev1 = carry_ref[7:8, :]
    prev2 = carry_ref[6:7, :]
    z1 = jnp.where(row == 0, prev1, pltpu.roll(z, 1, 0))
    z2 = jnp.where(row == 0, prev2, jnp.where(row == 1, prev1, pltpu.roll(z, 2, 0)))
    conv = convw_ref[0:1, :] * z2 + convw_ref[1:2, :] * z1 + convw_ref[2:3, :] * z
    carry_ref[...] = z[tm - 8:tm, :]
    tick()
    y_conv = jnp.dot((proj(6) * conv).astype(BF16), wbc_ref[...], preferred_element_type=F32)
    tick()

    mix = _sigmoid(proj(7)) * y_ret + _sigmoid(proj(8)) * y_conv
    tick()
    x1 = x + g1 * jnp.dot(mix.astype(BF16), wout_ref[...], preferred_element_type=F32)
    tick()

    h2 = (_rms(x1) * gffn_ref[...]) * (1.0 + sc2) + sh2
    hi_f32 = lax.bitcast_convert_type(lax.bitcast_convert_type(h2, jnp.uint32) & jnp.uint32(0xFFFF0000), F32)
    h2_hi = hi_f32.astype(BF16)
    h2_lo = (h2 - hi_f32).astype(BF16)

    def nt_dot(a, b):
        return lax.dot_general(a, b, (((1,), (1,)), ((), ())), preferred_element_type=F32)

    lg_ref[...] = nt_dot(wrh_ref[...], h2_hi) + (nt_dot(wrh_ref[...], h2_lo) + nt_dot(wrl_ref[...], h2_hi))
    gu = jnp.dot(h2.astype(BF16), wshgu_ref[...], preferred_element_type=F32)
    f = gu.shape[1] // 2
    shared = jnp.dot((_silu(gu[:, :f]) * gu[:, f:]).astype(BF16), wshd_ref[...], preferred_element_type=F32)
    x2_ref[0] = x1 + g2 * shared
    for j in range(h2p_ref.shape[0]):
        h2p_ref[j] = _pack_bf16_pairs(h2[:, 2 * LANES * j:2 * LANES * (j + 1)])
    for _ in background:
        pass


def _mixer_kernel(*refs, tiles_per_seq):
    mix_inputs, (bias_ref, tri_ref) = refs[:21], refs[21:23]
    x2_ref, h2p_ref, eidx_ref, w_ref, rank_ref, cnt_ref = refs[23:29]
    state_ref, carry_ref, q_ref, k_ref, v_ref, ret_ref, hb_ref, lg_ref, rcarry_ref = refs[29:]
    s = pl.program_id(0)
    n_tiles = pl.num_programs(0) - 1

    @pl.when(s == 0)
    def _():
        lg_ref[...] = jnp.zeros_like(lg_ref)
        rcarry_ref[...] = jnp.zeros_like(rcarry_ref)

    @pl.when(s % tiles_per_seq == 0)
    def _():
        state_ref[...] = jnp.zeros_like(state_ref)
        carry_ref[...] = jnp.zeros_like(carry_ref)

    def route_previous_tile():
        return _route_phases(lg_ref.at[(s + 1) % 2], bias_ref, tri_ref, eidx_ref, w_ref, rank_ref, cnt_ref,
                             rcarry_ref, (s > 0).astype(jnp.int32))

    @pl.when(s == n_tiles)
    def _():
        for _ in route_previous_tile():
            pass

    @pl.when(s < n_tiles)
    def _():
        _mix_tile(*mix_inputs, x2_ref, h2p_ref, lg_ref.at[s % 2], state_ref, carry_ref, q_ref, k_ref, v_ref,
                  ret_ref, hb_ref, background=route_previous_tile())


def _mixer(x, posf, mod, gmix, win, retg, convw, wbr, wbc, wout, gffn, wr, wshgu, wshd, bias):
    b, s, d = x.shape
    tm = MIXER_TOKENS
    nt = s // tm
    n_tiles = b * nt
    intra, qd, kd, cd, invf, sign = _retention_constants()
    q_w, v_w = RET_HEADS * RET_DK, RET_HEADS * RET_DV
    hi_f32 = lax.bitcast_convert_type(lax.bitcast_convert_type(wr, jnp.uint32) & jnp.uint32(0xFFFF0000), F32)
    wr_hi = hi_f32.astype(BF16)
    wr_lo = (wr - hi_f32).astype(BF16)
    tri = jnp.asarray(np.triu(np.ones((tm, tm), np.float32), 1), BF16)
    weights = (gmix, win, invf, sign, intra, qd, kd, cd, retg, convw, wbr, wbc, wout, gffn, wr_hi, wr_lo,
               wshgu, wshd, bias, tri)

    def tile(i):
        return jnp.minimum(i, n_tiles - 1)

    def routed(i):
        return (0, jnp.maximum(i - 1, 0))

    def routed_chunks(i):
        return (jnp.maximum(i - 1, 0), 0, 0)

    return pl.pallas_call(
        functools.partial(_mixer_kernel, tiles_per_seq=nt),
        grid=(n_tiles + 1,),
        in_specs=[pl.BlockSpec((1, tm, d), lambda i: (tile(i) // nt, tile(i) % nt, 0)),
                  pl.BlockSpec((1, tm, 1), lambda i: (tile(i) // nt, tile(i) % nt, 0)),
                  pl.BlockSpec((1, 6, d), lambda i: (tile(i) // nt, 0, 0))]
                 + [_resident(w.shape) for w in weights],
        out_specs=[pl.BlockSpec((1, tm, d), lambda i: (tile(i) // nt, tile(i) % nt, 0)),
                   pl.BlockSpec((d // (2 * LANES), tm, LANES), lambda i: (0, tile(i), 0)),
                   pl.BlockSpec((tm // LANES, TOP_K, LANES), routed_chunks),
                   pl.BlockSpec((TOP_K, tm), routed),
                   pl.BlockSpec((tm // LANES, TOP_K, LANES), routed_chunks),
                   pl.BlockSpec((N_EXPERTS, 1), lambda i: (0, 0))],
        out_shape=[jax.ShapeDtypeStruct((b, s, d), F32),
                   jax.ShapeDtypeStruct((d // (2 * LANES), b * s, LANES), jnp.uint32),
                   jax.ShapeDtypeStruct((b * s // LANES, TOP_K, LANES), jnp.int32),
                   jax.ShapeDtypeStruct((TOP_K, b * s), F32),
                   jax.ShapeDtypeStruct((b * s // LANES, TOP_K, LANES), jnp.int32),
                   jax.ShapeDtypeStruct((N_EXPERTS, 1), F32)],
        scratch_shapes=[pltpu.VMEM((RET_HEADS, RET_DK, RET_DV), F32),
                        pltpu.VMEM((8, d), F32),
                        pltpu.VMEM((tm, q_w), F32),
                        pltpu.VMEM((tm, q_w), F32),
                        pltpu.VMEM((tm, v_w), BF16),
                        pltpu.VMEM((tm, v_w), F32),
                        pltpu.VMEM((tm, d), BF16),
                        pltpu.VMEM((2, N_EXPERTS, tm), F32),
                        pltpu.VMEM((N_EXPERTS, 1), F32)],
        compiler_params=pltpu.CompilerParams(dimension_semantics=("arbitrary",),
                                             vmem_limit_bytes=VMEM_LIMIT_BYTES),
        name="mixer",
    )(x, posf, mod, *weights)


def _first_argmax(v, idx, n):
    m = jnp.max(v, axis=0, keepdims=True)
    return m, jnp.min(jnp.where(v == m, idx, n), axis=0, keepdims=True)


def _route_phases(logits_ref, bias_ref, tri_ref, eidx_ref, w_ref, rank_ref, cnt_ref, carry_ref, count_it):
    tn = logits_ref.shape[1]
    neg = F32(-jnp.inf)
    score = _sigmoid(logits_ref[...])
    choice = score + bias_ref[...]

    grow = lax.broadcasted_iota(jnp.int32, (GROUP_SIZE, tn), 0)
    gscores = []
    for g in range(N_GROUPS):
        cg = choice[g * GROUP_SIZE:(g + 1) * GROUP_SIZE]
        m1, i1 = _first_argmax(cg, grow, GROUP_SIZE)
        m2 = jnp.max(jnp.where(grow == i1, neg, cg), axis=0, keepdims=True)
        gscores.append(m1 + m2)
    cur = jnp.concatenate(gscores, axis=0)
    yield
    gidx = lax.broadcasted_iota(jnp.int32, (N_GROUPS, tn), 0)
    keep = jnp.zeros((N_GROUPS, tn), F32)
    for _ in range(TOPK_GROUPS):
        _, ig = _first_argmax(cur, gidx, N_GROUPS)
        hit = gidx == ig
        keep = jnp.where(hit, 1.0, keep)
        cur = jnp.where(hit, neg, cur)
    cur = jnp.concatenate(
        [jnp.where(keep[g:g + 1] > 0.0, choice[g * GROUP_SIZE:(g + 1) * GROUP_SIZE], neg)
         for g in range(N_GROUPS)], axis=0)
    yield

    erow = lax.broadcasted_iota(jnp.int32, (N_EXPERTS, tn), 0)
    eidx, wts = [], []
    member = jnp.zeros((N_EXPERTS, tn), F32)
    for _ in range(TOP_K):
        _, ie = _first_argmax(cur, erow, N_EXPERTS)
        hit = erow == ie
        eidx.append(ie)
        wts.append(jnp.sum(jnp.where(hit, score, 0.0), axis=0, keepdims=True))
        member = member + hit.astype(F32)
        cur = jnp.where(hit, neg, cur)
        yield
    wsum = wts[0]
    for k in range(1, TOP_K):
        wsum = wsum + wts[k]

    before = jnp.dot(member.astype(BF16), tri_ref[...], preferred_element_type=F32) + carry_ref[...]
    ranks = []
    for k in range(TOP_K):
        ranks.append(jnp.sum(jnp.where(erow == eidx[k], before, 0.0), axis=0, keepdims=True))
        if k % 2 == 1:
            yield
    carry_ref[...] = carry_ref[...] + count_it.astype(F32) * jnp.sum(member, axis=1, keepdims=True)

    eidx_rows = jnp.concatenate(eidx, axis=0)
    rank_rows = jnp.concatenate(ranks, axis=0).astype(jnp.int32)
    for c in range(tn // LANES):
        eidx_ref[c] = eidx_rows[:, c * LANES:(c + 1) * LANES]
        rank_ref[c] = rank_rows[:, c * LANES:(c + 1) * LANES]
    w_ref[...] = jnp.concatenate([w / wsum * ROUTED_SCALE for w in wts], axis=0)
    cnt_ref[...] = carry_ref[...]


def _sc_mesh_and_workers():
    mesh = plsc.VectorSubcoreMesh(core_axis_name="c", subcore_axis_name="s")
    return mesh, mesh.num_cores, mesh.num_cores * mesh.num_subcores


def _worker_id(num_cores):
    return lax.axis_index("s") * num_cores + lax.axis_index("c")


def _dispatch(h2p, eidx3, rank3, pstart, n_rows):
    planes, t, lanes = h2p.shape
    n_chunks = eidx3.shape[0]
    mesh, num_cores, workers = _sc_mesh_and_workers()
    chunks_per_worker = n_chunks // workers
    vec = plsc.get_sparse_core_info().num_lanes
    assert chunks_per_worker * workers == n_chunks and n_chunks * lanes == t and lanes % vec == 0

    def body(h_hbm, e_hbm, r_hbm, p_hbm, xs_hbm, d_hbm, e_v, r_v, p_v, idx_v, rows_v, load_sem, scatter_sem):
        wid = _worker_id(num_cores)
        mine = pl.ds(wid * chunks_per_worker, chunks_per_worker)
        pltpu.sync_copy(p_hbm, p_v)
        pltpu.sync_copy(e_hbm.at[mine], e_v)
        pltpu.sync_copy(r_hbm.at[mine], r_v)

        @pl.loop(0, chunks_per_worker)
        def _(c):
            for k in range(TOP_K):
                for g in range(lanes // vec):
                    part = pl.ds(g * vec, vec)
                    idx_v[c, k, part] = plsc.load_gather(p_v, [e_v[c, k, part]]) + r_v[c, k, part]

        pltpu.sync_copy(idx_v, d_hbm.at[mine])

        @pl.loop(0, chunks_per_worker)
        def _(c):
            tok = pl.ds((wid * chunks_per_worker + c) * lanes, lanes)
            loads = [None] * planes
            scatters = [None] * planes
            loads[0] = pltpu.async_copy(h_hbm.at[0, tok], rows_v.at[0], load_sem.at[0])
            for j in range(planes):
                loads[j].wait()
                if j + 1 < planes:
                    if j >= 1:
                        for cp in scatters[j - 1]:
                            cp.wait()
                    loads[j + 1] = pltpu.async_copy(h_hbm.at[j + 1, tok], rows_v.at[(j + 1) % 2],
                                                    load_sem.at[(j + 1) % 2])
                scatters[j] = [pltpu.async_copy(rows_v.at[j % 2], xs_hbm.at[j].at[idx_v.at[c, k]],
                                                scatter_sem.at[j % 2]) for k in range(TOP_K)]
            for j in range(max(planes - 2, 0), planes):
                for cp in scatters[j]:
                    cp.wait()

    return pl.kernel(
        body,
        out_type=(jax.ShapeDtypeStruct((planes, n_rows, lanes), h2p.dtype),
                  jax.ShapeDtypeStruct(eidx3.shape, jnp.int32)),
        mesh=mesh,
        scratch_types=[pltpu.VMEM((chunks_per_worker, TOP_K, lanes), jnp.int32),
                       pltpu.VMEM((chunks_per_worker, TOP_K, lanes), jnp.int32),
                       pltpu.VMEM((N_EXPERTS,), jnp.int32),
                       pltpu.VMEM((chunks_per_worker, TOP_K, lanes), jnp.int32),
                       pltpu.VMEM((2, lanes, lanes), h2p.dtype),
                       pltpu.SemaphoreType.DMA((2,)),
                       pltpu.SemaphoreType.DMA((2,))],
        compiler_params=pltpu.CompilerParams(needs_layout_passes=False),
        name="dispatch",
    )(h2p, eidx3, rank3, pstart)


def _gather(ybuf, dest3, t):
    planes, _, lanes = ybuf.shape
    n_chunks = dest3.shape[0]
    mesh, num_cores, workers = _sc_mesh_and_workers()
    chunks_per_worker = n_chunks // workers
    nbuf = GATHER_BUFFERS
    lag = nbuf // 2
    assert chunks_per_worker * workers == n_chunks and n_chunks * lanes == t

    def body(y_hbm, d_hbm, yg_hbm, idx_v, rows_v, gather_sem, store_sem):
        wid = _worker_id(num_cores)

        @pl.loop(0, chunks_per_worker)
        def _(c):
            chunk = wid * chunks_per_worker + c
            tok = pl.ds(chunk * lanes, lanes)
            pltpu.sync_copy(d_hbm.at[chunk], idx_v)

            @pl.loop(0, TOP_K)
            def _(k):
                gathers = [None] * planes
                stores = [None] * planes

                def store(j):
                    gathers[j].wait()
                    stores[j] = pltpu.async_copy(rows_v.at[j % nbuf], yg_hbm.at[k, j, tok], store_sem.at[j % nbuf])

                for j in range(planes):
                    if j >= nbuf:
                        stores[j - nbuf].wait()
                    gathers[j] = pltpu.async_copy(y_hbm.at[j].at[idx_v.at[k]], rows_v.at[j % nbuf],
                                                  gather_sem.at[j % nbuf])
                    if j >= lag:
                        store(j - lag)
                for j in range(max(planes - lag, 0), planes):
                    store(j)
                for j in range(max(planes - nbuf, 0), planes):
                    stores[j].wait()

    return pl.kernel(
        body,
        out_type=jax.ShapeDtypeStruct((TOP_K, planes, t, lanes), ybuf.dtype),
        mesh=mesh,
        scratch_types=[pltpu.VMEM((TOP_K, lanes), jnp.int32),
                       pltpu.VMEM((nbuf, lanes, lanes), ybuf.dtype),
                       pltpu.SemaphoreType.DMA((nbuf,)),
                       pltpu.SemaphoreType.DMA((nbuf,))],
        name="gather",
    )(ybuf, dest3)


def _experts_kernel(be_ref, nv_ref, nu_ref, em_ref, x_ref, wg_hbm, wu_hbm, wd_hbm, o_ref,
                    wgs_ref, wus_ref, wds_ref, wgb_ref, wub_ref, wdb_ref, sems):
    planes, step_rows, lanes = x_ref.shape
    bm = EXPERT_ROWS
    blocks_per_step = step_rows // bm
    step = pl.program_id(0)
    last_step = (nu_ref[0] - 1) // blocks_per_step

    def weight_copies(expert):
        slot = em_ref[0, expert]
        return [pltpu.make_async_copy(src.at[expert], dst.at[slot], sems.at[slot, n])
                for n, (src, dst) in enumerate(((wg_hbm, wgs_ref), (wu_hbm, wus_ref), (wd_hbm, wds_ref)))]

    def start_if_any(expert):
        @pl.when(expert < N_EXPERTS)
        def _():
            for cp in weight_copies(expert):
                cp.start()

    def swiglu_rows(row0, m, n_valid):
        rows = pl.ds(row0, m)
        valid = lax.broadcasted_iota(jnp.int32, (m, lanes), 0) < n_valid
        g = None
        u = None
        for j in range(planes):
            xj = _unpack_pairs_f32(jnp.where(valid, x_ref[j, rows, :], jnp.uint32(0))).astype(BF16)
            wrows = pl.ds(2 * lanes * j, 2 * lanes)
            gj = jnp.dot(xj, wgb_ref[wrows, :], preferred_element_type=F32)
            uj = jnp.dot(xj, wub_ref[wrows, :], preferred_element_type=F32)
            g = gj if g is None else g + gj
            u = uj if u is None else u + uj
        out = jnp.dot((_silu(g) * u).astype(BF16), wdb_ref[...], preferred_element_type=F32)
        for j in range(planes):
            o_ref[j, rows, :] = _pack_bf16_pairs(out[:, 2 * lanes * j:2 * lanes * (j + 1)])

    def block_step(sb):
        i = step * blocks_per_step + sb
        row0 = pl.multiple_of(sb * bm, bm)
        rows = pl.ds(row0, bm)
        active = i < nu_ref[0]
        e = be_ref[i]
        new_expert = jnp.logical_or(i == 0, e != be_ref[jnp.maximum(i - 1, 0)])
        run = active.astype(jnp.int32)
        same = active
        for ahead in range(1, EXPERT_MAX_RUN):
            j = jnp.minimum(i + ahead, be_ref.shape[0] - 1)
            same = jnp.logical_and(jnp.logical_and(same, sb + ahead < blocks_per_step),
                                   jnp.logical_and(i + ahead < nu_ref[0], be_ref[j] == e))
            run = run + same.astype(jnp.int32)

        @pl.when(jnp.logical_and(active, new_expert))
        def _():
            @pl.when(i == 0)
            def _():
                for cp in weight_copies(e):
                    cp.start()
                for ahead in range(1, EXPERT_STAGES):
                    start_if_any(em_ref[ahead, e])

            for cp in weight_copies(e):
                cp.wait()
            slot = em_ref[0, e]
            wgb_ref[...] = wgs_ref[slot].astype(BF16)
            wub_ref[...] = wus_ref[slot].astype(BF16)
            wdb_ref[...] = wds_ref[slot].astype(BF16)
            start_if_any(em_ref[EXPERT_STAGES, e])

        for m in range(1, EXPERT_MAX_RUN + 1):
            @pl.when(run == m)
            def _():
                last = jnp.minimum(i + (m - 1), be_ref.shape[0] - 1)
                swiglu_rows(row0, m * bm, (m - 1) * bm + nv_ref[last])

        @pl.when(jnp.logical_and(jnp.logical_not(active), step == last_step))
        def _():
            for j in range(planes):
                o_ref[j, rows, :] = jnp.zeros((bm, lanes), o_ref.dtype)

        return sb + jnp.maximum(run, 1)

    lax.while_loop(lambda sb: sb < blocks_per_step, block_step, jnp.int32(0))


def _experts(block_e, n_valid, n_used, expert_meta, xs, wg, wu, wd):
    planes, p, lanes = xs.shape
    e, d, f = wg.shape
    step_rows = EXPERT_ROWS * EXPERT_BLOCKS_PER_STEP
    assert p % step_rows == 0

    def row_map(i, be, nv, nu, nxt):
        return (0, jnp.minimum(i, (nu[0] - 1) // EXPERT_BLOCKS_PER_STEP), 0)

    return pl.pallas_call(
        _experts_kernel,
        grid_spec=pltpu.PrefetchScalarGridSpec(
            num_scalar_prefetch=4,
            grid=(p // step_rows,),
            in_specs=[pl.BlockSpec((planes, step_rows, lanes), row_map),
                      pl.BlockSpec(memory_space=pl.ANY),
                      pl.BlockSpec(memory_space=pl.ANY),
                      pl.BlockSpec(memory_space=pl.ANY)],
            out_specs=pl.BlockSpec((planes, step_rows, lanes), row_map),
            scratch_shapes=[pltpu.VMEM((EXPERT_STAGES, d, f), F32), pltpu.VMEM((EXPERT_STAGES, d, f), F32),
                            pltpu.VMEM((EXPERT_STAGES, f, d), F32),
                            pltpu.VMEM((d, f), BF16), pltpu.VMEM((d, f), BF16), pltpu.VMEM((f, d), BF16),
                            pltpu.SemaphoreType.DMA((EXPERT_STAGES, 3))]),
        out_shape=jax.ShapeDtypeStruct((planes, p, lanes), jnp.uint32),
        compiler_params=pltpu.CompilerParams(dimension_semantics=("arbitrary",),
                                             vmem_limit_bytes=VMEM_LIMIT_BYTES),
        name="experts",
    )(block_e, n_valid, n_used, expert_meta, xs, wg, wu, wd)


def _combine_kernel(w_ref, x2_ref, g2_ref, gfin_ref, yg_ref, *maybe_prev_and_out):
    o_ref = maybe_prev_and_out[-1]
    tc = x2_ref.shape[1]
    eye = (lax.broadcasted_iota(jnp.int32, (tc, tc), 0) == lax.broadcasted_iota(jnp.int32, (tc, tc), 1)).astype(F32)
    w = lax.dot_general(eye, w_ref[...], (((1,), (1,)), ((), ())), precision=HIGHEST, preferred_element_type=F32)
    planes = yg_ref.shape[1]
    routed = None
    for k in range(TOP_K):
        rows = jnp.concatenate([_unpack_pairs_f32(yg_ref[k, j]) for j in range(planes)], axis=1) * w[:, k:k + 1]
        routed = rows if routed is None else routed + rows
    o_ref[0] = _rms(x2_ref[0] + g2_ref[0] * routed) * gfin_ref[...]


def _combine(wtok, x2, g2, gfin, yg, seg, prev_out):
    b, s, d = x2.shape
    _, planes, seg_tokens, lanes = yg.shape
    tc = COMBINE_TOKENS
    nt = seg_tokens // tc
    per_seq = s // seg_tokens
    assert nt * tc == seg_tokens and per_seq * seg_tokens == s
    bi, t0 = seg // per_seq, (seg % per_seq) * nt
    in_specs = [pl.BlockSpec((TOP_K, tc), lambda j: (0, seg * nt + j)),
                pl.BlockSpec((1, tc, d), lambda j: (bi, t0 + j, 0)),
                pl.BlockSpec((1, 1, d), lambda j: (bi, 0, 0)),
                pl.BlockSpec((1, d), lambda j: (0, 0)),
                pl.BlockSpec((TOP_K, planes, tc, lanes), lambda j: (0, 0, j, 0))]
    args = [wtok, x2, g2, gfin, yg]
    aliases = {}
    if prev_out is not None:
        in_specs.append(pl.BlockSpec(memory_space=pl.ANY))
        args.append(prev_out)
        aliases = {len(args) - 1: 0}
    return pl.pallas_call(
        _combine_kernel,
        grid=(nt,),
        in_specs=in_specs,
        out_specs=pl.BlockSpec((1, tc, d), lambda j: (bi, t0 + j, 0)),
        out_shape=jax.ShapeDtypeStruct((b, s, d), F32),
        input_output_aliases=aliases,
        compiler_params=pltpu.CompilerParams(dimension_semantics=("arbitrary",),
                                             vmem_limit_bytes=VMEM_LIMIT_BYTES),
        name="combine",
    )(*args)


def kernel(x, c, positions, w_ada, b_ada, norm_mix_g, w_in, ret_norm_g, conv_w, w_br_ret, w_br_conv, w_out,
           norm_ffn_g, w_router, router_bias, w_exp_gate, w_exp_up, w_exp_down, w_sh_gate, w_sh_up, w_sh_down,
           norm_final_g):
    b, s, d = x.shape
    t = b * s
    depth = w_in.shape[0]
    assert depth == 1, "the combine kernel applies the final norm, so exactly one layer is supported"
    posf = positions.astype(F32)[:, :, None]
    c8 = jnp.zeros((8, d), F32).at[:b].set(c)
    bm = EXPERT_ROWS
    n_blocks = t * TOP_K // bm + N_EXPERTS
    n_rows = n_blocks * bm

    for l in range(depth):
        mod = _ada(c8, w_ada[l], b_ada[l][None, :])[:b].reshape(b, 6, d)
        x2, h2p, eidx, wts, rank, counts = _mixer(
            x, posf, mod, norm_mix_g[l][None, :], w_in[l].astype(BF16), ret_norm_g[l][None, :], conv_w[l],
            w_br_ret[l].astype(BF16), w_br_conv[l].astype(BF16), w_out[l].astype(BF16), norm_ffn_g[l][None, :],
            w_router[l].T, jnp.concatenate([w_sh_gate[l], w_sh_up[l]], axis=1).astype(BF16),
            w_sh_down[l].astype(BF16), router_bias[l][:, None])

        cnt = counts[:, 0].astype(jnp.int32)
        pcnt = (cnt + bm - 1) // bm * bm
        pend = jnp.cumsum(pcnt)
        pstart = pend - pcnt
        n_used = jnp.maximum(pend[-1] // bm, 1).astype(jnp.int32)[None]
        block_start = jnp.arange(n_blocks, dtype=jnp.int32) * bm
        block_e = jnp.minimum(jnp.sum((pend[None, :] <= block_start[:, None]).astype(jnp.int32), axis=1),
                              N_EXPERTS - 1)
        eid = jnp.arange(N_EXPERTS, dtype=jnp.int32)
        seg_end = jnp.sum(jnp.where(block_e[:, None] == eid[None, :], (pstart + cnt)[None, :], 0), axis=1)
        n_valid = jnp.clip(seg_end - block_start, 0, bm).astype(jnp.int32)
        later_used = jnp.logical_and(eid[None, :] > eid[:, None], cnt[None, :] > 0)
        next_expert = jnp.min(jnp.where(later_used, eid[None, :], N_EXPERTS), axis=1)
        stage_slot = (jnp.cumsum((cnt > 0).astype(jnp.int32)) - 1) % EXPERT_STAGES
        ahead = [stage_slot, next_expert]
        for _ in range(EXPERT_STAGES - 1):
            hop = jnp.sum(jnp.where(ahead[-1][:, None] == eid[None, :], next_expert[None, :], 0), axis=1)
            ahead.append(jnp.where(ahead[-1] == N_EXPERTS, N_EXPERTS, hop))
        expert_meta = jnp.stack(ahead).astype(jnp.int32)

        xs, dest3 = _dispatch(h2p, eidx, rank, pstart.astype(jnp.int32), n_rows)
        ybuf = _experts(block_e, n_valid, n_used, expert_meta, xs, w_exp_gate[l], w_exp_up[l], w_exp_down[l])
        chunks = t // LANES // COMBINE_SEGMENTS
        out = None
        for seg in range(COMBINE_SEGMENTS):
            yg = _gather(ybuf, dest3[seg * chunks:(seg + 1) * chunks], chunks * LANES)
            out = _combine(wts, x2, mod[:, 5:6, :], norm_final_g[None, :], yg, seg, out)
        x = out
    return x
```
